```python
import jax, jax.numpy as jnp
from jax import lax
import numpy as np

D_MODEL = 1024
BATCH = 16
SEQ = 4096
DEPTH = 2

N_MIXERS = 2
N_A = (DEPTH + 1) // 2
N_B = DEPTH // 2

CHUNK = 128
SGU_WIDTH = D_MODEL
SGU_GROUPS = 8
SGU_GROUP_DIM = SGU_WIDTH // SGU_GROUPS

RWKV_HEAD = 64
RWKV_HEADS = D_MODEL // RWKV_HEAD
DECAY_LORA = 64
AAA_LORA = 64
GATE_LORA = 160

D_FF = 4 * D_MODEL
N_MOD = 6

RMS_EPS = 1e-6
LN_EPS = 1e-5
GN_EPS = RWKV_HEAD * 1e-5
L2_EPS = 1e-12

kernel_name = "hybrid_sgu_rwkv7_adaln_trunk"


def rms_norm(x):
    xf = x.astype(jnp.float32)
    y = xf * lax.rsqrt(jnp.mean(xf * xf, axis=-1, keepdims=True) + RMS_EPS)
    return y.astype(x.dtype)


def layer_norm(x, g, b):
    xf = x.astype(jnp.float32)
    mu = jnp.mean(xf, axis=-1, keepdims=True)
    var = jnp.mean(jnp.square(xf - mu), axis=-1, keepdims=True)
    y = (xf - mu) * lax.rsqrt(var + LN_EPS)
    return y.astype(x.dtype) * g + b


def modulate(h, shift, scale):
    return h * (1 + scale[:, None, :]) + shift[:, None, :]


def token_shift(x):
    return jnp.pad(x[:, :-1], ((0, 0), (1, 0), (0, 0)))


def sgu_mixer(h, w_in, ln_g, ln_b, w_s, b_s, w_out):
    B, T, _ = h.shape
    uv = jax.nn.gelu(h @ w_in, approximate=False)
    u, v = jnp.split(uv, 2, axis=-1)
    v = layer_norm(v, ln_g, ln_b)
    vc = v.reshape(B, T // CHUNK, CHUNK, SGU_GROUPS, SGU_GROUP_DIM)
    mask = jnp.tril(jnp.ones((CHUNK, CHUNK), dtype=w_s.dtype))
    sv = jnp.einsum('gts,bcsgd->bctgd', w_s * mask, vc) + b_s.T[:, :, None]
    return (u * sv.reshape(B, T, SGU_WIDTH)) @ w_out


def wkv7_scan(r, w, k, v, a_, b_):
    B, T, H, N = r.shape
    seq = tuple(jnp.swapaxes(z.astype(jnp.float32), 0, 1) for z in (r, w, k, v, a_, b_))

    def step(S, inp):
        r_t, w_t, k_t, v_t, a_t, b_t = inp
        sa = jnp.einsum('bhij,bhj->bhi', S, a_t)
        S = S * w_t[:, :, None, :] + sa[..., None] * b_t[:, :, None, :] + v_t[..., None] * k_t[:, :, None, :]
        y = jnp.einsum('bhij,bhj->bhi', S, r_t)
        return S, y

    S0 = jnp.zeros((B, H, N, N), jnp.float32)
    _, ys = lax.scan(step, S0, seq)
    return jnp.swapaxes(ys, 0, 1).astype(r.dtype)


def rwkv7_mixer(h, mu, w_in, w0, w1, w2, a0, a1, a2, g1, g2, k_k, k_a, r_k, ln_g, ln_b, w_out):
    B, T, D = h.shape
    H, N = RWKV_HEADS, RWKV_HEAD
    xx = token_shift(h) - h
    xr, xw, xk, xv, xa, xg = [h + xx * mu[i] for i in range(6)]
    rkv = jnp.einsum('nbtd,dne->nbte', jnp.stack([xr, xk, xv]), w_in.reshape(D, 3, D))
    r, k, v = rkv[0], rkv[1], rkv[2]
    w_log = -jax.nn.softplus(-(w0 + jnp.tanh(xw @ w1) @ w2)) - 0.5
    decay = jnp.exp(-jnp.exp(w_log))
    a = jax.nn.sigmoid(a0 + (xa @ a1) @ a2)
    g = jax.nn.sigmoid(xg @ g1) @ g2
    kk = (k * k_k).reshape(B, T, H, N)
    kkf = kk.astype(jnp.float32)
    kk = (kkf / jnp.maximum(jnp.sqrt(jnp.sum(kkf * kkf, axis=-1, keepdims=True)), L2_EPS)).astype(h.dtype)
    k = k * (1 + (a - 1) * k_a)
    rh = r.reshape(B, T, H, N)
    kh = k.reshape(B, T, H, N)
    vh = v.reshape(B, T, H, N)
    ah = a.reshape(B, T, H, N)
    y = wkv7_scan(rh, decay.reshape(B, T, H, N), kh, vh, -kk, kk * ah)
    yf = y.astype(jnp.float32)
    m = jnp.mean(yf, axis=-1, keepdims=True)
    var = jnp.mean(jnp.square(yf - m), axis=-1, keepdims=True)
    y = ((yf - m) * lax.rsqrt(var + GN_EPS)).astype(h.dtype).reshape(B, T, D) * ln_g + ln_b
    bonus = jnp.sum(rh * kh * r_k, axis=-1, keepdims=True) * vh
    y = y + bonus.reshape(B, T, D)
    return (y * g) @ w_out


def _fwd_setup_inputs(seed: int = 0) -> dict:
    key = jax.random.key(seed)
    ks = iter(jax.random.split(key, 48))
    D = D_MODEL

    def nrm(shape, scale):
        return jax.random.normal(next(ks), shape, jnp.float32) * scale

    def unif(shape, lo, hi):
        return jax.random.uniform(next(ks), shape, jnp.float32, lo, hi)

    return {
        "x": nrm((BATCH, SEQ, D), 1.0),
        "c": nrm((BATCH, D), 1.0),
        "ada_w": nrm((DEPTH, D, N_MOD * D), 0.3 * D ** -0.5),
        "ada_b": nrm((DEPTH, N_MOD * D), 0.02),
        "mlp_w1": nrm((DEPTH, D, D_FF), D ** -0.5),
        "mlp_w2": nrm((DEPTH, D_FF, D), D_FF ** -0.5),
        "a_w_in": nrm((N_A, D, 2 * SGU_WIDTH), D ** -0.5),
        "a_ln_g": 1.0 + nrm((N_A, SGU_WIDTH), 0.02),
        "a_ln_b": nrm((N_A, SGU_WIDTH), 0.02),
        "a_w_s": nrm((N_A, SGU_GROUPS, CHUNK, CHUNK), CHUNK ** -0.5),
        "a_b_s": 1.0 + nrm((N_A, SGU_GROUPS, CHUNK), 0.02),
        "a_w_out": nrm((N_A, SGU_WIDTH, D), SGU_WIDTH ** -0.5),
        "b_mu": unif((N_B, 6, D), 0.0, 1.0),
        "b_w_in": nrm((N_B, D, 3 * D), D ** -0.5),
        "b_w0": unif((N_B, D), -7.0, -2.0),
        "b_w1": nrm((N_B, D, DECAY_LORA), D ** -0.5),
        "b_w2": nrm((N_B, DECAY_LORA, D), 0.1 * DECAY_LORA ** -0.5),
        "b_a0": nrm((N_B, D), 0.5),
        "b_a1": nrm((N_B, D, AAA_LORA), D ** -0.5),
        "b_a2": nrm((N_B, AAA_LORA, D), 0.5 * AAA_LORA ** -0.5),
        "b_g1": nrm((N_B, D, GATE_LORA), D ** -0.5),
        "b_g2": nrm((N_B, GATE_LORA, D), GATE_LORA ** -0.5),
        "b_k_k": 0.85 + nrm((N_B, D), 0.02),
        "b_k_a": 1.0 + nrm((N_B, D), 0.02),
        "b_r_k": -0.04 + nrm((N_B, RWKV_HEADS, RWKV_HEAD), 0.02),
        "b_ln_g": 1.0 + nrm((N_B, D), 0.02),
        "b_ln_b": nrm((N_B, D), 0.02),
        "b_w_out": nrm((N_B, D, D), D ** -0.5),
        "final_g": 1.0 + nrm((D,), 0.02),
    }


def _fwd_reference(x, c, ada_w, ada_b, mlp_w1, mlp_w2,
              a_w_in, a_ln_g, a_ln_b, a_w_s, a_b_s, a_w_out,
              b_mu, b_w_in, b_w0, b_w1, b_w2, b_a0, b_a1, b_a2, b_g1, b_g2,
              b_k_k, b_k_a, b_r_k, b_ln_g, b_ln_b, b_w_out, final_g):
    cond = jax.nn.silu(c)
    for i in range(DEPTH):
        mod = cond @ ada_w[i] + ada_b[i]
        shift1, scale1, gate1, shift2, scale2, gate2 = jnp.split(mod, N_MOD, axis=-1)
        h = modulate(rms_norm(x), shift1, scale1)
        j = i // N_MIXERS
        if i % N_MIXERS == 0:
            mix = sgu_mixer(h, a_w_in[j], a_ln_g[j], a_ln_b[j], a_w_s[j], a_b_s[j], a_w_out[j])
        else:
            mix = rwkv7_mixer(h, b_mu[j], b_w_in[j], b_w0[j], b_w1[j], b_w2[j],
                              b_a0[j], b_a1[j], b_a2[j], b_g1[j], b_g2[j],
                              b_k_k[j], b_k_a[j], b_r_k[j], b_ln_g[j], b_ln_b[j], b_w_out[j])
        x = x + gate1[:, None, :] * mix
        h = modulate(rms_norm(x), shift2, scale2)
        ff = jnp.square(jax.nn.relu(h @ mlp_w1[i])) @ mlp_w2[i]
        x = x + gate2[:, None, :] * ff
    return rms_norm(x) * final_g


import jax as _jax
import jax.numpy as _jnp

TWIN_FORMAT = 'train_step'
FWD_PARAMS = ['x', 'c', 'ada_w', 'ada_b', 'mlp_w1', 'mlp_w2', 'a_w_in', 'a_ln_g', 'a_ln_b', 'a_w_s', 'a_b_s', 'a_w_out', 'b_mu', 'b_w_in', 'b_w0', 'b_w1', 'b_w2', 'b_a0', 'b_a1', 'b_a2', 'b_g1', 'b_g2', 'b_k_k', 'b_k_a', 'b_r_k', 'b_ln_g', 'b_ln_b', 'b_w_out', 'final_g']
TWIN_WEIGHTS = ['ada_w', 'ada_b', 'mlp_w1', 'mlp_w2', 'a_w_in', 'a_ln_g', 'a_ln_b', 'a_w_s', 'a_b_s', 'a_w_out', 'b_mu', 'b_w_in', 'b_w0', 'b_w1', 'b_w2', 'b_a0', 'b_a1', 'b_a2', 'b_g1', 'b_g2', 'b_k_k', 'b_k_a', 'b_r_k', 'b_ln_g', 'b_ln_b', 'b_w_out', 'final_g']
TWIN_DIFF_INPUT = 'x'
TWIN_INPUTS = ['x', 'c', 'ada_w', 'ada_b', 'mlp_w1', 'mlp_w2', 'a_w_in', 'a_ln_g', 'a_ln_b', 'a_w_s', 'a_b_s', 'a_w_out', 'b_mu', 'b_w_in', 'b_w0', 'b_w1', 'b_w2', 'b_a0', 'b_a1', 'b_a2', 'b_g1', 'b_g2', 'b_k_k', 'b_k_a', 'b_r_k', 'b_ln_g', 'b_ln_b', 'b_w_out', 'final_g', 'loss_target', 'm_ada_w', 'm_ada_b', 'm_mlp_w1', 'm_mlp_w2', 'm_a_w_in', 'm_a_ln_g', 'm_a_ln_b', 'm_a_w_s', 'm_a_b_s', 'm_a_w_out', 'm_b_mu', 'm_b_w_in', 'm_b_w0', 'm_b_w1', 'm_b_w2', 'm_b_a0', 'm_b_a1', 'm_b_a2', 'm_b_g1', 'm_b_g2', 'm_b_k_k', 'm_b_k_a', 'm_b_r_k', 'm_b_ln_g', 'm_b_ln_b', 'm_b_w_out', 'm_final_g', 'v_ada_w', 'v_ada_b', 'v_mlp_w1', 'v_mlp_w2', 'v_a_w_in', 'v_a_ln_g', 'v_a_ln_b', 'v_a_w_s', 'v_a_b_s', 'v_a_w_out', 'v_b_mu', 'v_b_w_in', 'v_b_w0', 'v_b_w1', 'v_b_w2', 'v_b_a0', 'v_b_a1', 'v_b_a2', 'v_b_g1', 'v_b_g2', 'v_b_k_k', 'v_b_k_a', 'v_b_r_k', 'v_b_ln_g', 'v_b_ln_b', 'v_b_w_out', 'v_final_g']
TWIN_OUTPUTS = ['loss', 'grad_x', 'grad_ada_w', 'grad_ada_b', 'grad_mlp_w1', 'grad_mlp_w2', 'grad_a_w_in', 'grad_a_ln_g', 'grad_a_ln_b', 'grad_a_w_s', 'grad_a_b_s', 'grad_a_w_out', 'grad_b_mu', 'grad_b_w_in', 'grad_b_w0', 'grad_b_w1', 'grad_b_w2', 'grad_b_a0', 'grad_b_a1', 'grad_b_a2', 'grad_b_g1', 'grad_b_g2', 'grad_b_k_k', 'grad_b_k_a', 'grad_b_r_k', 'grad_b_ln_g', 'grad_b_ln_b', 'grad_b_w_out', 'grad_final_g', 'delta_ada_w', 'delta_ada_b', 'delta_mlp_w1', 'delta_mlp_w2', 'delta_a_w_in', 'delta_a_ln_g', 'delta_a_ln_b', 'delta_a_w_s', 'delta_a_b_s', 'delta_a_w_out', 'delta_b_mu', 'delta_b_w_in', 'delta_b_w0', 'delta_b_w1', 'delta_b_w2', 'delta_b_a0', 'delta_b_a1', 'delta_b_a2', 'delta_b_g1', 'delta_b_g2', 'delta_b_k_k', 'delta_b_k_a', 'delta_b_r_k', 'delta_b_ln_g', 'delta_b_ln_b', 'delta_b_w_out', 'delta_final_g', 'new_m_ada_w', 'new_m_ada_b', 'new_m_mlp_w1', 'new_m_mlp_w2', 'new_m_a_w_in', 'new_m_a_ln_g', 'new_m_a_ln_b', 'new_m_a_w_s', 'new_m_a_b_s', 'new_m_a_w_out', 'new_m_b_mu', 'new_m_b_w_in', 'new_m_b_w0', 'new_m_b_w1', 'new_m_b_w2', 'new_m_b_a0', 'new_m_b_a1', 'new_m_b_a2', 'new_m_b_g1', 'new_m_b_g2', 'new_m_b_k_k', 'new_m_b_k_a', 'new_m_b_r_k', 'new_m_b_ln_g', 'new_m_b_ln_b', 'new_m_b_w_out', 'new_m_final_g', 'new_v_ada_w', 'new_v_ada_b', 'new_v_mlp_w1', 'new_v_mlp_w2', 'new_v_a_w_in', 'new_v_a_ln_g', 'new_v_a_ln_b', 'new_v_a_w_s', 'new_v_a_b_s', 'new_v_a_w_out', 'new_v_b_mu', 'new_v_b_w_in', 'new_v_b_w0', 'new_v_b_w1', 'new_v_b_w2', 'new_v_b_a0', 'new_v_b_a1', 'new_v_b_a2', 'new_v_b_g1', 'new_v_b_g2', 'new_v_b_k_k', 'new_v_b_k_a', 'new_v_b_r_k', 'new_v_b_ln_g', 'new_v_b_ln_b', 'new_v_b_w_out', 'new_v_final_g']
TWIN_LEAF_KINDS = {'loss': 'loss', 'grad_x': 'grad_x', 'grad_ada_w': 'grad_w', 'grad_ada_b': 'grad_w', 'grad_mlp_w1': 'grad_w', 'grad_mlp_w2': 'grad_w', 'grad_a_w_in': 'grad_w', 'grad_a_ln_g': 'grad_w', 'grad_a_ln_b': 'grad_w', 'grad_a_w_s': 'grad_w', 'grad_a_b_s': 'grad_w', 'grad_a_w_out': 'grad_w', 'grad_b_mu': 'grad_w', 'grad_b_w_in': 'grad_w', 'grad_b_w0': 'grad_w', 'grad_b_w1': 'grad_w', 'grad_b_w2': 'grad_w', 'grad_b_a0': 'grad_w', 'grad_b_a1': 'grad_w', 'grad_b_a2': 'grad_w', 'grad_b_g1': 'grad_w', 'grad_b_g2': 'grad_w', 'grad_b_k_k': 'grad_w', 'grad_b_k_a': 'grad_w', 'grad_b_r_k': 'grad_w', 'grad_b_ln_g': 'grad_w', 'grad_b_ln_b': 'grad_w', 'grad_b_w_out': 'grad_w', 'grad_final_g': 'grad_w', 'delta_ada_w': 'delta_w', 'delta_ada_b': 'delta_w', 'delta_mlp_w1': 'delta_w', 'delta_mlp_w2': 'delta_w', 'delta_a_w_in': 'delta_w', 'delta_a_ln_g': 'delta_w', 'delta_a_ln_b': 'delta_w', 'delta_a_w_s': 'delta_w', 'delta_a_b_s': 'delta_w', 'delta_a_w_out': 'delta_w', 'delta_b_mu': 'delta_w', 'delta_b_w_in': 'delta_w', 'delta_b_w0': 'delta_w', 'delta_b_w1': 'delta_w', 'delta_b_w2': 'delta_w', 'delta_b_a0': 'delta_w', 'delta_b_a1': 'delta_w', 'delta_b_a2': 'delta_w', 'delta_b_g1': 'delta_w', 'delta_b_g2': 'delta_w', 'delta_b_k_k': 'delta_w', 'delta_b_k_a': 'delta_w', 'delta_b_r_k': 'delta_w', 'delta_b_ln_g': 'delta_w', 'delta_b_ln_b': 'delta_w', 'delta_b_w_out': 'delta_w', 'delta_final_g': 'delta_w', 'new_m_ada_w': 'new_m', 'new_m_ada_b': 'new_m', 'new_m_mlp_w1': 'new_m', 'new_m_mlp_w2': 'new_m', 'new_m_a_w_in': 'new_m', 'new_m_a_ln_g': 'new_m', 'new_m_a_ln_b': 'new_m', 'new_m_a_w_s': 'new_m', 'new_m_a_b_s': 'new_m', 'new_m_a_w_out': 'new_m', 'new_m_b_mu': 'new_m', 'new_m_b_w_in': 'new_m', 'new_m_b_w0': 'new_m', 'new_m_b_w1': 'new_m', 'new_m_b_w2': 'new_m', 'new_m_b_a0': 'new_m', 'new_m_b_a1': 'new_m', 'new_m_b_a2': 'new_m', 'new_m_b_g1': 'new_m', 'new_m_b_g2': 'new_m', 'new_m_b_k_k': 'new_m', 'new_m_b_k_a': 'new_m', 'new_m_b_r_k': 'new_m', 'new_m_b_ln_g': 'new_m', 'new_m_b_ln_b': 'new_m', 'new_m_b_w_out': 'new_m', 'new_m_final_g': 'new_m', 'new_v_ada_w': 'new_v', 'new_v_ada_b': 'new_v', 'new_v_mlp_w1': 'new_v', 'new_v_mlp_w2': 'new_v', 'new_v_a_w_in': 'new_v', 'new_v_a_ln_g': 'new_v', 'new_v_a_ln_b': 'new_v', 'new_v_a_w_s': 'new_v', 'new_v_a_b_s': 'new_v', 'new_v_a_w_out': 'new_v', 'new_v_b_mu': 'new_v', 'new_v_b_w_in': 'new_v', 'new_v_b_w0': 'new_v', 'new_v_b_w1': 'new_v', 'new_v_b_w2': 'new_v', 'new_v_b_a0': 'new_v', 'new_v_b_a1': 'new_v', 'new_v_b_a2': 'new_v', 'new_v_b_g1': 'new_v', 'new_v_b_g2': 'new_v', 'new_v_b_k_k': 'new_v', 'new_v_b_k_a': 'new_v', 'new_v_b_r_k': 'new_v', 'new_v_b_ln_g': 'new_v', 'new_v_b_ln_b': 'new_v', 'new_v_b_w_out': 'new_v', 'new_v_final_g': 'new_v'}


def _forward(args):
    return _fwd_reference(*[args[k] for k in FWD_PARAMS])


def _output_shape():
    out = _jax.eval_shape(lambda: _forward(_fwd_setup_inputs(0)))
    return out.shape, out.dtype

N_MICROBATCH = 1
ADAM_LR = 0.001
ADAM_B1 = 0.9
ADAM_B2 = 0.999
ADAM_EPS = 1e-08
ADAM_WD = 0.01
ADAM_STEP = 10
PER_EXAMPLE_BATCH_AXIS = {'x': 0, 'c': 0, 'loss_target': 0}
SHARED_INPUTS = []
_WEIGHT_DTYPES = {'ada_w': _jnp.float32, 'ada_b': _jnp.float32, 'mlp_w1': _jnp.float32, 'mlp_w2': _jnp.float32, 'a_w_in': _jnp.float32, 'a_ln_g': _jnp.float32, 'a_ln_b': _jnp.float32, 'a_w_s': _jnp.float32, 'a_b_s': _jnp.float32, 'a_w_out': _jnp.float32, 'b_mu': _jnp.float32, 'b_w_in': _jnp.float32, 'b_w0': _jnp.float32, 'b_w1': _jnp.float32, 'b_w2': _jnp.float32, 'b_a0': _jnp.float32, 'b_a1': _jnp.float32, 'b_a2': _jnp.float32, 'b_g1': _jnp.float32, 'b_g2': _jnp.float32, 'b_k_k': _jnp.float32, 'b_k_a': _jnp.float32, 'b_r_k': _jnp.float32, 'b_ln_g': _jnp.float32, 'b_ln_b': _jnp.float32, 'b_w_out': _jnp.float32, 'final_g': _jnp.float32}
MOMENT_SCALE = {'ada_w': 1.030354e-01, 'ada_b': 1.752695e-01, 'mlp_w1': 3.344517e-02, 'mlp_w2': 5.857515e-02, 'a_w_in': 3.398604e-02, 'a_ln_g': 2.294233e-02, 'a_ln_b': 2.309107e-02, 'a_w_s': 2.292798e-02, 'a_b_s': 3.145364e-02, 'a_w_out': 3.894810e-02, 'b_mu': 2.986565e-02, 'b_w_in': 2.653943e-02, 'b_w0': 9.271156e-03, 'b_w1': 6.796733e-04, 'b_w2': 1.521113e-03, 'b_a0': 9.976400e-03, 'b_a1': 1.921443e-02, 'b_a2': 9.685203e-03, 'b_g1': 2.158100e-02, 'b_g2': 2.570909e-02, 'b_k_k': 7.598379e-02, 'b_k_a': 4.052490e-02, 'b_r_k': 6.583739e-02, 'b_ln_g': 2.611031e-02, 'b_ln_b': 2.806022e-02, 'b_w_out': 2.466473e-02, 'final_g': 6.402083e+01}


def _to_microbatches(a, axis):
    t = _jnp.moveaxis(a, axis, 0)
    t = t.reshape((N_MICROBATCH, t.shape[0] // N_MICROBATCH) + t.shape[1:])
    return _jnp.moveaxis(t, 1, axis + 1)


def setup_inputs(seed: int = 0) -> dict:
    inp = _fwd_setup_inputs(seed)
    key = _jax.random.fold_in(_jax.random.key(seed), 7919)
    shape, _ = _output_shape()
    out = dict(inp)
    out["loss_target"] = _jax.random.normal(_jax.random.fold_in(key, 0), shape, _jnp.float32)
    for i, name in enumerate(TWIN_WEIGHTS):
        w = inp[name].astype(_jnp.float32)
        if MOMENT_SCALE is None:
            s = _jnp.sqrt(_jnp.mean(_jnp.square(w)) + 1e-30)
        else:
            s = MOMENT_SCALE[name]
        km, kv = _jax.random.split(_jax.random.fold_in(key, i + 1))
        out[name] = w
        out["m_" + name] = s * _jax.random.normal(km, w.shape, _jnp.float32)
        out["v_" + name] = (s * s) * _jax.random.uniform(kv, w.shape, _jnp.float32, 0.5, 1.5)
    if N_MICROBATCH > 1:
        for name, axis in PER_EXAMPLE_BATCH_AXIS.items():
            out[name] = _to_microbatches(out[name], axis)
    return {'x': out['x'], 'c': out['c'], 'ada_w': out['ada_w'], 'ada_b': out['ada_b'], 'mlp_w1': out['mlp_w1'], 'mlp_w2': out['mlp_w2'], 'a_w_in': out['a_w_in'], 'a_ln_g': out['a_ln_g'], 'a_ln_b': out['a_ln_b'], 'a_w_s': out['a_w_s'], 'a_b_s': out['a_b_s'], 'a_w_out': out['a_w_out'], 'b_mu': out['b_mu'], 'b_w_in': out['b_w_in'], 'b_w0': out['b_w0'], 'b_w1': out['b_w1'], 'b_w2': out['b_w2'], 'b_a0': out['b_a0'], 'b_a1': out['b_a1'], 'b_a2': out['b_a2'], 'b_g1': out['b_g1'], 'b_g2': out['b_g2'], 'b_k_k': out['b_k_k'], 'b_k_a': out['b_k_a'], 'b_r_k': out['b_r_k'], 'b_ln_g': out['b_ln_g'], 'b_ln_b': out['b_ln_b'], 'b_w_out': out['b_w_out'], 'final_g': out['final_g'], 'loss_target': out['loss_target'], 'm_ada_w': out['m_ada_w'], 'm_ada_b': out['m_ada_b'], 'm_mlp_w1': out['m_mlp_w1'], 'm_mlp_w2': out['m_mlp_w2'], 'm_a_w_in': out['m_a_w_in'], 'm_a_ln_g': out['m_a_ln_g'], 'm_a_ln_b': out['m_a_ln_b'], 'm_a_w_s': out['m_a_w_s'], 'm_a_b_s': out['m_a_b_s'], 'm_a_w_out': out['m_a_w_out'], 'm_b_mu': out['m_b_mu'], 'm_b_w_in': out['m_b_w_in'], 'm_b_w0': out['m_b_w0'], 'm_b_w1': out['m_b_w1'], 'm_b_w2': out['m_b_w2'], 'm_b_a0': out['m_b_a0'], 'm_b_a1': out['m_b_a1'], 'm_b_a2': out['m_b_a2'], 'm_b_g1': out['m_b_g1'], 'm_b_g2': out['m_b_g2'], 'm_b_k_k': out['m_b_k_k'], 'm_b_k_a': out['m_b_k_a'], 'm_b_r_k': out['m_b_r_k'], 'm_b_ln_g': out['m_b_ln_g'], 'm_b_ln_b': out['m_b_ln_b'], 'm_b_w_out': out['m_b_w_out'], 'm_final_g': out['m_final_g'], 'v_ada_w': out['v_ada_w'], 'v_ada_b': out['v_ada_b'], 'v_mlp_w1': out['v_mlp_w1'], 'v_mlp_w2': out['v_mlp_w2'], 'v_a_w_in': out['v_a_w_in'], 'v_a_ln_g': out['v_a_ln_g'], 'v_a_ln_b': out['v_a_ln_b'], 'v_a_w_s': out['v_a_w_s'], 'v_a_b_s': out['v_a_b_s'], 'v_a_w_out': out['v_a_w_out'], 'v_b_mu': out['v_b_mu'], 'v_b_w_in': out['v_b_w_in'], 'v_b_w0': out['v_b_w0'], 'v_b_w1': out['v_b_w1'], 'v_b_w2': out['v_b_w2'], 'v_b_a0': out['v_b_a0'], 'v_b_a1': out['v_b_a1'], 'v_b_a2': out['v_b_a2'], 'v_b_g1': out['v_b_g1'], 'v_b_g2': out['v_b_g2'], 'v_b_k_k': out['v_b_k_k'], 'v_b_k_a': out['v_b_k_a'], 'v_b_r_k': out['v_b_r_k'], 'v_b_ln_g': out['v_b_ln_g'], 'v_b_ln_b': out['v_b_ln_b'], 'v_b_w_out': out['v_b_w_out'], 'v_final_g': out['v_final_g']}


def _loss(weights, diff, rest, loss_target):
    with _jax.named_scope("forward"):
        args = {**rest, TWIN_DIFF_INPUT: diff, **{k: w.astype(_WEIGHT_DTYPES[k]) for k, w in weights.items()}}
        y = _forward(args)
    with _jax.named_scope("loss_head"):
        err = _jnp.square(y.astype(_jnp.float32) - loss_target)
        return 0.5 * _jnp.sum(_jnp.mean(err, axis=-1)) if err.ndim else 0.5 * err


def _adamw(w, g, m, v):
    m = ADAM_B1 * m + (1.0 - ADAM_B1) * g
    v = ADAM_B2 * v + (1.0 - ADAM_B2) * _jnp.square(g)
    m_hat = m / (1.0 - ADAM_B1 ** ADAM_STEP)
    v_hat = v / (1.0 - ADAM_B2 ** ADAM_STEP)
    delta = -ADAM_LR * (m_hat / (_jnp.sqrt(v_hat) + ADAM_EPS) + ADAM_WD * w)
    return delta, m, v


def reference(x, c, ada_w, ada_b, mlp_w1, mlp_w2, a_w_in, a_ln_g, a_ln_b, a_w_s, a_b_s, a_w_out, b_mu, b_w_in, b_w0, b_w1, b_w2, b_a0, b_a1, b_a2, b_g1, b_g2, b_k_k, b_k_a, b_r_k, b_ln_g, b_ln_b, b_w_out, final_g, loss_target, m_ada_w, m_ada_b, m_mlp_w1, m_mlp_w2, m_a_w_in, m_a_ln_g, m_a_ln_b, m_a_w_s, m_a_b_s, m_a_w_out, m_b_mu, m_b_w_in, m_b_w0, m_b_w1, m_b_w2, m_b_a0, m_b_a1, m_b_a2, m_b_g1, m_b_g2, m_b_k_k, m_b_k_a, m_b_r_k, m_b_ln_g, m_b_ln_b, m_b_w_out, m_final_g, v_ada_w, v_ada_b, v_mlp_w1, v_mlp_w2, v_a_w_in, v_a_ln_g, v_a_ln_b, v_a_w_s, v_a_b_s, v_a_w_out, v_b_mu, v_b_w_in, v_b_w0, v_b_w1, v_b_w2, v_b_a0, v_b_a1, v_b_a2, v_b_g1, v_b_g2, v_b_k_k, v_b_k_a, v_b_r_k, v_b_ln_g, v_b_ln_b, v_b_w_out, v_final_g):
    given = dict(x=x, c=c, ada_w=ada_w, ada_b=ada_b, mlp_w1=mlp_w1, mlp_w2=mlp_w2, a_w_in=a_w_in, a_ln_g=a_ln_g, a_ln_b=a_ln_b, a_w_s=a_w_s, a_b_s=a_b_s, a_w_out=a_w_out, b_mu=b_mu, b_w_in=b_w_in, b_w0=b_w0, b_w1=b_w1, b_w2=b_w2, b_a0=b_a0, b_a1=b_a1, b_a2=b_a2, b_g1=b_g1, b_g2=b_g2, b_k_k=b_k_k, b_k_a=b_k_a, b_r_k=b_r_k, b_ln_g=b_ln_g, b_ln_b=b_ln_b, b_w_out=b_w_out, final_g=final_g, loss_target=loss_target, m_ada_w=m_ada_w, m_ada_b=m_ada_b, m_mlp_w1=m_mlp_w1, m_mlp_w2=m_mlp_w2, m_a_w_in=m_a_w_in, m_a_ln_g=m_a_ln_g, m_a_ln_b=m_a_ln_b, m_a_w_s=m_a_w_s, m_a_b_s=m_a_b_s, m_a_w_out=m_a_w_out, m_b_mu=m_b_mu, m_b_w_in=m_b_w_in, m_b_w0=m_b_w0, m_b_w1=m_b_w1, m_b_w2=m_b_w2, m_b_a0=m_b_a0, m_b_a1=m_b_a1, m_b_a2=m_b_a2, m_b_g1=m_b_g1, m_b_g2=m_b_g2, m_b_k_k=m_b_k_k, m_b_k_a=m_b_k_a, m_b_r_k=m_b_r_k, m_b_ln_g=m_b_ln_g, m_b_ln_b=m_b_ln_b, m_b_w_out=m_b_w_out, m_final_g=m_final_g, v_ada_w=v_ada_w, v_ada_b=v_ada_b, v_mlp_w1=v_mlp_w1, v_mlp_w2=v_mlp_w2, v_a_w_in=v_a_w_in, v_a_ln_g=v_a_ln_g, v_a_ln_b=v_a_ln_b, v_a_w_s=v_a_w_s, v_a_b_s=v_a_b_s, v_a_w_out=v_a_w_out, v_b_mu=v_b_mu, v_b_w_in=v_b_w_in, v_b_w0=v_b_w0, v_b_w1=v_b_w1, v_b_w2=v_b_w2, v_b_a0=v_b_a0, v_b_a1=v_b_a1, v_b_a2=v_b_a2, v_b_g1=v_b_g1, v_b_g2=v_b_g2, v_b_k_k=v_b_k_k, v_b_k_a=v_b_k_a, v_b_r_k=v_b_r_k, v_b_ln_g=v_b_ln_g, v_b_ln_b=v_b_ln_b, v_b_w_out=v_b_w_out, v_final_g=v_final_g)
    weights = {n: given[n] for n in TWIN_WEIGHTS}
    shared = {n: given[n] for n in SHARED_INPUTS}
    per_example = {n: given[n] for n in ['x', 'c']}
    grad_fn = _jax.value_and_grad(_loss, argnums=(0, 1))

    def one_microbatch(ex, loss_target):
        ex = dict(ex)
        diff = ex.pop(TWIN_DIFF_INPUT)
        return grad_fn(weights, diff, {**shared, **ex}, loss_target)

    if N_MICROBATCH == 1:
        loss, (grad_w, grad_x) = one_microbatch(per_example, given["loss_target"])
    else:
        def body(carry, xs):
            loss_sum, grad_sum = carry
            l_k, (gw_k, gx_k) = one_microbatch(xs[0], xs[1])
            with _jax.named_scope("update"):
                return (loss_sum + l_k, _jax.tree.map(_jnp.add, grad_sum, gw_k)), gx_k

        init = (_jnp.zeros((), _jnp.float32), _jax.tree.map(_jnp.zeros_like, weights))
        (loss, grad_w), grad_x = _jax.lax.scan(body, init, (per_example, given["loss_target"]))
    with _jax.named_scope("update"):
        delta_w, new_m, new_v = {}, {}, {}
        for n in TWIN_WEIGHTS:
            delta_w[n], new_m[n], new_v[n] = _adamw(weights[n], grad_w[n], given["m_" + n], given["v_" + n])
    return (loss, grad_x, *[grad_w[n] for n in TWIN_WEIGHTS], *[delta_w[n] for n in TWIN_WEIGHTS],
            *[new_m[n] for n in TWIN_WEIGHTS], *[new_v[n] for n in TWIN_WEIGHTS])
```

```python
import functools

import jax
import jax.numpy as jnp
from jax import lax
from jax.experimental import pallas as pl
from jax.experimental.pallas import tpu as pltpu

F32 = jnp.float32
BF16 = jnp.bfloat16
MESH = pl.DeviceIdType.MESH
AXES = ("x", "y", "c")

D = 1024
N_MOD = 6
HEAD = 64
CHUNK = 128
GROUPS = 8
LANES = 128
ROW_W = 1024
N_CHIPS = 4
N_DEV = 8

RMS_EPS = 1e-6
LN_EPS = 1e-5
GN_EPS = HEAD * 1e-5
L2_EPS = 1e-12

ADAM_LR = 0.001
ADAM_B1 = 0.9
ADAM_B2 = 0.999
ADAM_EPS = 1e-08
ADAM_WD = 0.01
ADAM_STEP = 10

VMEM_LIMIT_V7X = 56 * 1024 * 1024
HIGHEST = lax.Precision.HIGHEST


def _params(sem=None):
    return pltpu.CompilerParams(dimension_semantics=sem, vmem_limit_bytes=VMEM_LIMIT_V7X)


def _tile(dim, target):
    if dim <= target:
        return dim
    for cand in range(target, 0, -LANES):
        if dim % cand == 0:
            return cand
    raise ValueError((dim, target))


def _matmul(a, b, *, mode, name, out_dtype=F32, out_shards=1, tm=1024, tn=1024, tk=1024):
    if mode == "nn":
        (m, k), (k2, n) = a.shape, b.shape
    elif mode == "nt":
        (m, k), (n, k2) = a.shape, b.shape
    else:
        (k, m), (k2, n) = a.shape, b.shape
    assert k == k2, (name, a.shape, b.shape)
    n_sh = n // out_shards
    tm, tn, tk = _tile(m, tm), _tile(n_sh, tn), _tile(k, tk)
    nk = k // tk
    nb = n_sh // tn
    use_scratch = nk > 1 and out_dtype != F32

    if mode == "tn":
        a_spec = pl.BlockSpec((tk, tm), lambda i, j, kk: (kk, i))
    else:
        a_spec = pl.BlockSpec((tm, tk), lambda i, j, kk: (i, kk))
    if mode == "nt":
        b_spec = pl.BlockSpec((tn, tk), lambda i, j, kk: (j, kk))
    else:
        b_spec = pl.BlockSpec((tk, tn), lambda i, j, kk: (kk, j))
    if out_shards == 1:
        out_shape = jax.ShapeDtypeStruct((m, n), out_dtype)
        o_spec = pl.BlockSpec((tm, tn), lambda i, j, kk: (i, j))
    else:
        out_shape = jax.ShapeDtypeStruct((out_shards, m, n_sh), out_dtype)
        o_spec = pl.BlockSpec((None, tm, tn), lambda i, j, kk: (j // nb, i, j % nb))

    def body(a_ref, b_ref, o_ref, *scratch):
        kk = pl.program_id(2)
        av = a_ref[...].astype(BF16)
        bv = b_ref[...].astype(BF16)
        if mode == "nn":
            dims = (((1,), (0,)), ((), ()))
        elif mode == "nt":
            dims = (((1,), (1,)), ((), ()))
        else:
            dims = (((0,), (0,)), ((), ()))
        part = lax.dot_general(av, bv, dims, preferred_element_type=F32)
        if nk == 1:
            o_ref[...] = part.astype(o_ref.dtype)
            return
        acc_ref = scratch[0] if use_scratch else o_ref

        @pl.when(kk == 0)
        def _():
            acc_ref[...] = part

        @pl.when(kk != 0)
        def _():
            acc_ref[...] += part

        if use_scratch:
            @pl.when(kk == nk - 1)
            def _():
                o_ref[...] = acc_ref[...].astype(o_ref.dtype)

    return pl.pallas_call(
        body, name=name, out_shape=out_shape,
        grid=(m // tm, n // tn, nk),
        in_specs=[a_spec, b_spec], out_specs=o_spec,
        scratch_shapes=[pltpu.VMEM((tm, tn), F32)] if use_scratch else [],
        compiler_params=_params(("parallel", "parallel", "arbitrary")),
    )(a, b)


def _matmul_ep(a, b, extras, epilogue, out_dtypes, *, mode, name, tm=1024, tn=1024, tk=1024):
    if mode == "nn":
        (m, k), (k2, n) = a.shape, b.shape
    else:
        (m, k), (n, k2) = a.shape, b.shape
    assert k == k2 and mode in ("nn", "nt"), (name, a.shape, b.shape)
    tm, tn, tk = _tile(m, tm), _tile(n, tn), _tile(k, tk)
    nk = k // tk
    n_ex, n_out = len(extras), len(out_dtypes)

    def body(a_ref, b_ref, *rest):
        extra_refs, out_refs = rest[:n_ex], rest[n_ex:n_ex + n_out]
        kk = pl.program_id(2)
        dims = (((1,), (0,)), ((), ())) if mode == "nn" else (((1,), (1,)), ((), ()))
        part = lax.dot_general(a_ref[...].astype(BF16), b_ref[...].astype(BF16), dims,
                               preferred_element_type=F32)

        def finish(acc):
            for ref, val in zip(out_refs, epilogue(acc, *[r[...] for r in extra_refs])):
                ref[...] = val.astype(ref.dtype)

        if nk == 1:
            finish(part)
            return
        acc_ref = rest[-1]

        @pl.when(kk == 0)
        def _():
            acc_ref[...] = part

        @pl.when(kk != 0)
        def _():
            acc_ref[...] += part

        @pl.when(kk == nk - 1)
        def _():
            finish(acc_ref[...])

    a_spec = pl.BlockSpec((tm, tk), lambda i, j, kk: (i, kk))
    b_spec = (pl.BlockSpec((tk, tn), lambda i, j, kk: (kk, j)) if mode == "nn"
              else pl.BlockSpec((tn, tk), lambda i, j, kk: (j, kk)))
    o_spec = pl.BlockSpec((tm, tn), lambda i, j, kk: (i, j))
    res = pl.pallas_call(
        body, name=name, out_shape=[jax.ShapeDtypeStruct((m, n), dt) for dt in out_dtypes],
        grid=(m // tm, n // tn, nk),
        in_specs=[a_spec, b_spec] + [o_spec] * n_ex, out_specs=[o_spec] * n_out,
        scratch_shapes=[pltpu.VMEM((tm, tn), F32)] if nk > 1 else [],
        compiler_params=_params(("parallel", "parallel", "arbitrary")),
    )(a, b, *extras)
    return list(res)


class _Ctx:
    def __init__(self, first, last):
        self.first = first
        self.last = last


def _rowwise(fn, name, *, seq, tm, rows=(), prev8=(), next8=(), bvecs=(), params=(),
             out_rows=(), out_bacc=(), out_pacc=()):
    n = rows[0].shape[0]
    tm = min(tm, seq)
    assert n % seq == 0 and seq % tm == 0 and tm % 8 == 0
    tpb = seq // tm
    nt = n // tm
    nbat = n // seq
    r8 = tm // 8
    counts = [len(rows), len(prev8), len(next8), len(bvecs), len(params)]
    n_in = sum(counts)

    def body(*refs):
        i = pl.program_id(0)
        first = (i % tpb) == 0
        last = (i % tpb) == (tpb - 1)
        vals = [r[...] for r in refs[:n_in]]
        groups, pos = [], 0
        for cnt in counts:
            groups.append(vals[pos:pos + cnt])
            pos += cnt
        ro, bo, po = fn(_Ctx(first, last), *groups)
        outs = refs[n_in:]
        assert len(ro) == len(out_rows) and len(bo) == len(out_bacc) and len(po) == len(out_pacc)
        for ref, val in zip(outs[:len(ro)], ro):
            ref[...] = val.astype(ref.dtype)
        for ref, val in zip(outs[len(ro):len(ro) + len(bo)], bo):
            @pl.when(first)
            def _(ref=ref, val=val):
                ref[...] = val

            @pl.when(jnp.logical_not(first))
            def _(ref=ref, val=val):
                ref[...] += val
        for ref, val in zip(outs[len(ro) + len(bo):], po):
            @pl.when(i == 0)
            def _(ref=ref, val=val):
                ref[...] = val

            @pl.when(i != 0)
            def _(ref=ref, val=val):
                ref[...] += val

    in_specs = []
    for arr in rows:
        in_specs.append(pl.BlockSpec((tm, arr.shape[1]), lambda i: (i, 0)))
    for arr in prev8:
        in_specs.append(pl.BlockSpec((8, arr.shape[1]), lambda i: (jnp.maximum(i * r8 - 1, 0), 0)))
    for arr in next8:
        in_specs.append(pl.BlockSpec((8, arr.shape[1]), lambda i: (jnp.minimum((i + 1) * r8, n // 8 - 1), 0)))
    for arr in bvecs:
        in_specs.append(pl.BlockSpec((None,) + arr.shape[1:], lambda i: (i // tpb, 0, 0)))
    for arr in params:
        in_specs.append(pl.BlockSpec(arr.shape, lambda i: (0, 0)))
    out_shape, out_specs = [], []
    for d, dt in out_rows:
        out_shape.append(jax.ShapeDtypeStruct((n, d), dt))
        out_specs.append(pl.BlockSpec((tm, d), lambda i: (i, 0)))
    for r, d in out_bacc:
        out_shape.append(jax.ShapeDtypeStruct((nbat, r, d), F32))
        out_specs.append(pl.BlockSpec((None, r, d), lambda i: (i // tpb, 0, 0)))
    for r, d in out_pacc:
        out_shape.append(jax.ShapeDtypeStruct((r, d), F32))
        out_specs.append(pl.BlockSpec((r, d), lambda i: (0, 0)))
    res = pl.pallas_call(
        body, name=name, out_shape=out_shape, grid=(nt,),
        in_specs=in_specs, out_specs=out_specs,
        compiler_params=_params(("arbitrary",)),
    )(*rows, *prev8, *next8, *bvecs, *params)
    return list(res)


def _rw_fwd(f, name, *, seq, tm, rows, bvecs=(), params=(), outs):
    def fn(ctx, rv, pv, nv, bv, pa):
        res = f(*[v.astype(F32) for v in rv], *bv, *pa)
        return list(res), [], []
    return _rowwise(fn, name, seq=seq, tm=tm, rows=rows, bvecs=bvecs, params=params, out_rows=outs)


def _rw_bwd(f, name, *, seq, tm, rows, bvecs=(), params=(), cts, need_rows, extra=None, dtypes=None):
    nr, nb, npar = len(rows), len(bvecs), len(params)
    all_rows = list(rows) + list(cts) + ([extra] if extra is not None else [])

    def fn(ctx, rv, pv, nv, bv, pa):
        prim = [v.astype(F32) for v in rv[:nr]]
        ct = tuple(v.astype(F32) for v in rv[nr:nr + len(cts)])
        _, vjp = jax.vjp(f, *prim, *bv, *pa)
        g = vjp(ct)
        d_rows = [g[j] for j in need_rows]
        if extra is not None:
            d_rows[0] = d_rows[0] + rv[-1].astype(F32)
        return d_rows, list(g[nr:nr + nb]), list(g[nr + nb:])

    return _rowwise(
        fn, name, seq=seq, tm=tm, rows=all_rows, bvecs=bvecs, params=params,
        out_rows=[(rows[j].shape[1], F32 if dtypes is None else dtypes[i]) for i, j in enumerate(need_rows)],
        out_bacc=[b.shape[1:] for b in bvecs], out_pacc=[p.shape for p in params])


def _small(fn, name, arrays, out_shapes):
    def body(*refs):
        res = fn(*[r[...] for r in refs[:len(arrays)]])
        for ref, val in zip(refs[len(arrays):], res):
            ref[...] = val.astype(ref.dtype)

    vm = pl.BlockSpec(memory_space=pltpu.VMEM)
    res = pl.pallas_call(
        body, name=name,
        out_shape=[jax.ShapeDtypeStruct(s, F32) for s in out_shapes],
        in_specs=[vm] * len(arrays), out_specs=[vm] * len(out_shapes),
        compiler_params=_params(),
    )(*arrays)
    return list(res)


def _elementwise(fn, name, arrays, n_out):
    shape = arrays[0].shape
    size = arrays[0].size
    if size % ROW_W == 0 and (size // ROW_W) % 8 == 0:
        view = (size // ROW_W, ROW_W)
    else:
        view = (1, size) if len(shape) < 2 else (size // shape[-1], shape[-1])
    rows = view[0]
    tr = rows
    for cand in (256, 128, 64, 32, 16, 8):
        if rows > cand and rows % cand == 0:
            tr = cand
            break

    def body(*refs):
        res = fn(*[r[...] for r in refs[:len(arrays)]])
        for ref, val in zip(refs[len(arrays):], res):
            ref[...] = val

    spec = pl.BlockSpec((tr, view[1]), lambda i: (i, 0))
    res = pl.pallas_call(
        body, name=name,
        out_shape=[jax.ShapeDtypeStruct(view, F32)] * n_out,
        grid=(rows // tr,), in_specs=[spec] * len(arrays), out_specs=[spec] * n_out,
        compiler_params=_params(("parallel",)),
    )(*[a.reshape(view) for a in arrays])
    return [r.reshape(shape) for r in res]


def _rms(x):
    return x * lax.rsqrt(jnp.mean(x * x, axis=-1, keepdims=True) + RMS_EPS)


def _rmsmod(x, sh, sc):
    return _rms(x) * (1.0 + sc) + sh


def _f_norm1(x, mod):
    return (_rmsmod(x, mod[0:1], mod[1:2]),)


def _f_sgu_pre(uvp, ln_g, ln_b):
    uv = 0.5 * uvp * (1.0 + lax.erf(uvp * (2.0 ** -0.5)))
    u = uv[:, :D]
    v = uv[:, D:]
    mu = jnp.mean(v, axis=-1, keepdims=True)
    vc = v - mu
    var = jnp.mean(vc * vc, axis=-1, keepdims=True)
    return u, vc * lax.rsqrt(var + LN_EPS) * ln_g + ln_b


def _f_res_norm2(x, mix, mod):
    x1 = x + mod[2:3] * mix
    return x1, _rmsmod(x1, mod[3:4], mod[4:5])


def _f_relu2(p):
    r = jnp.maximum(p, 0.0)
    return (r * r,)


def _f_res2(x1, ff, mod):
    return (x1 + mod[5:6] * ff,)


def _f_lora_act(t1, t3):
    return jnp.tanh(t1), 1.0 / (1.0 + jnp.exp(-t3))


def _f_loss(x, tgt, fg):
    err = _rms(x) * fg - tgt
    return 0.5 * jnp.sum(jnp.mean(err * err, axis=-1))


def _shift_mix(ctx, x, xprev8, mod, mu):
    h = _rmsmod(x, mod[0:1], mod[1:2])
    hprev = _rmsmod(xprev8, mod[0:1], mod[1:2])[7:8]
    hprev = jnp.where(ctx.first, jnp.zeros_like(hprev), hprev)
    rowid = lax.broadcasted_iota(jnp.int32, h.shape, 0)
    hp = jnp.where(rowid == 0, hprev, pltpu.roll(h, 1, 0))
    xx = hp - h
    return h, xx, [h + xx * mu[j:j + 1] for j in range(6)]


def _split_bf16(t, parts):
    out, rest = [], t.astype(F32)
    for _ in range(parts):
        piece = rest.astype(BF16)
        out.append(piece)
        rest = rest - piece.astype(F32)
    return out


def _make_mm(na, nb):
    def raw(a, b, pa, pb):
        if pa == 0:
            return jnp.dot(a, b, precision=HIGHEST, preferred_element_type=F32)
        acc = None
        bs = _split_bf16(b, pb)
        for i, ai in enumerate(_split_bf16(a, pa)):
            for j, bj in enumerate(bs):
                if i + j < max(pa, pb):
                    term = jnp.dot(ai, bj, preferred_element_type=F32)
                    acc = term if acc is None else acc + term
        return acc

    @jax.custom_vjp
    def mm(a, b):
        return raw(a, b, na, nb)

    def fwd(a, b):
        return raw(a, b, na, nb), (a, b)

    def bwd(res, ct):
        a, b = res
        nc = max(na, nb)
        return raw(ct, b.T, nc, nb), raw(a.T, ct, na, nc)

    mm.defvjp(fwd, bwd)
    return mm


class _WkvMms:
    def __init__(self, head_sum, cum, score, square, apply, out, state):
        self.head_sum, self.cum, self.score = head_sum, cum, score
        self.square, self.apply, self.out, self.state = square, apply, out, state


def _wkv_mms(cfg):
    table = {"x": (0, 0), "1": (1, 1), "2": (2, 2), "3": (3, 3), "a": (2, 1), "b": (1, 2)}
    hs, cu, sc_, sq, ap, ou, st = [table[ch] for ch in cfg]
    return _WkvMms(_make_mm(hs[0], 1) if hs[0] else _make_mm(0, 0),
                   _make_mm(1, cu[1]) if cu[0] else _make_mm(0, 0),
                   _make_mm(*sc_), _make_mm(*sq), _make_mm(*ap), _make_mm(*ou), _make_mm(*st))


WKV_PRECISION = "2221b11"


SGU_CHUNKS_PER_STEP = 4


def _sgu_tile(mm, u, vn, ws, bias):
    row = lax.broadcasted_iota(jnp.int32, (CHUNK, CHUNK), 0)
    col = lax.broadcasted_iota(jnp.int32, (CHUNK, CHUNK), 1)
    wm = [jnp.where(col <= row, w, 0.0) for w in ws]
    out_rows = []
    for ch in range(u.shape[0] // CHUNK):
        rs = slice(ch * CHUNK, (ch + 1) * CHUNK)
        out_rows.append(jnp.concatenate(
            [mm(wm[g], vn[rs, g * LANES:(g + 1) * LANES]) + bias[g] for g in range(GROUPS)], axis=1))
    return u * jnp.concatenate(out_rows, axis=0)


def _sgu_fwd(u, vn, ws, bias):
    n = u.shape[0]
    rows = CHUNK * SGU_CHUNKS_PER_STEP
    mm = _make_mm(1, 1)

    def body(u_ref, v_ref, w_ref, b_ref, z_ref):
        ws_l = [w_ref[g] for g in range(GROUPS)]
        bias_l = [b_ref[g] for g in range(GROUPS)]
        z_ref[...] = _sgu_tile(mm, u_ref[...], v_ref[...], ws_l, bias_l).astype(z_ref.dtype)

    tok = pl.BlockSpec((rows, D), lambda i: (i, 0))
    grp = pl.BlockSpec((GROUPS, CHUNK, LANES), lambda i: (0, 0, 0))
    return pl.pallas_call(
        body, name="sgu_fwd", out_shape=jax.ShapeDtypeStruct((n, D), BF16),
        grid=(n // rows,), in_specs=[tok, tok, grp, grp], out_specs=tok,
        compiler_params=_params(("parallel",)),
    )(u, vn, ws, bias)


def _sgu_bwd(u, vn, ws, bias, dz):
    n = u.shape[0]
    rows = CHUNK * SGU_CHUNKS_PER_STEP
    mm = _make_mm(1, 1)

    def body(u_ref, v_ref, w_ref, b_ref, dz_ref, du_ref, dv_ref, dw_ref, db_ref):
        i = pl.program_id(0)
        ws_l = [w_ref[g] for g in range(GROUPS)]
        bias_l = [b_ref[g] for g in range(GROUPS)]
        _, vjp = jax.vjp(functools.partial(_sgu_tile, mm), u_ref[...], v_ref[...], ws_l, bias_l)
        du, dv, dw, db = vjp(dz_ref[...].astype(F32))
        du_ref[...] = du
        dv_ref[...] = dv
        for g in range(GROUPS):
            @pl.when(i == 0)
            def _(g=g):
                dw_ref[g] = dw[g]
                db_ref[g] = db[g]

            @pl.when(i != 0)
            def _(g=g):
                dw_ref[g] += dw[g]
                db_ref[g] += db[g]

    tok = pl.BlockSpec((rows, D), lambda i: (i, 0))
    grp = pl.BlockSpec((GROUPS, CHUNK, LANES), lambda i: (0, 0, 0))
    return pl.pallas_call(
        body, name="sgu_bwd",
        out_shape=[jax.ShapeDtypeStruct((n, D), F32), jax.ShapeDtypeStruct((n, D), F32),
                   jax.ShapeDtypeStruct((GROUPS, CHUNK, LANES), F32),
                   jax.ShapeDtypeStruct((GROUPS, CHUNK, LANES), F32)],
        grid=(n // rows,), in_specs=[tok, tok, grp, grp, tok], out_specs=[tok, tok, grp, grp],
        compiler_params=_params(("arbitrary",)),
    )(u, vn, ws, bias, dz)


def _chains(t):
    return [t[i] for i in range(t.shape[0])] if t.ndim == 3 else [t]


def _bmm(mm, a, b):
    if a.ndim == 2 and b.ndim == 2:
        return mm(a, b)
    ca, cb = _chains(a), _chains(b)
    n = max(len(ca), len(cb))
    return jnp.stack([mm(ca[i % len(ca)], cb[i % len(cb)]) for i in range(n)])


def _bt(a):
    return a.T if a.ndim == 2 else jnp.stack([t.T for t in _chains(a)])


def _wkv_chunk(mms, s0, r, k, v, wl, al, g, w0, a0, k_k, k_a, r_k, ln_g, ln_b):
    ln = CHUNK
    row = lax.broadcasted_iota(jnp.int32, (ln, ln), 0)
    col = lax.broadcasted_iota(jnp.int32, (ln, ln), 1)
    incl = (col <= row).astype(F32)
    strict = (col < row).astype(F32)
    same_head = ((row // HEAD) == (col // HEAD)).astype(F32)
    lane = lax.broadcasted_iota(jnp.int32, (1, LANES), 1)
    m_a = (lane < HEAD).astype(F32)
    m_b = 1.0 - m_a
    rowid = lax.broadcasted_iota(jnp.int32, (ln, LANES), 0)
    cat = jnp.concatenate

    def hsum(t):
        return _bmm(mms.head_sum, t, same_head)

    def pick_row(t, j):
        return jnp.sum(jnp.where(rowid == j, t, 0.0), axis=-2, keepdims=True)

    z = w0 + wl
    softplus_neg = jnp.maximum(-z, 0.0) + jnp.log(1.0 + jnp.exp(-jnp.abs(z)))
    lw = -jnp.exp(-softplus_neg - 0.5)
    a = 1.0 / (1.0 + jnp.exp(-(a0 + al)))
    kx = k * k_k
    kkn = kx / jnp.maximum(jnp.sqrt(hsum(kx * kx)), L2_EPS)
    kp = k * (1.0 + (a - 1.0) * k_a)
    aa = -kkn
    bb = kkn * a

    c = _bmm(mms.cum, incl, lw)
    c_mid = pick_row(c, ln // 2 - 1)
    ce = c - c_mid
    e_pos = jnp.exp(ce)
    e_neg = jnp.exp(-ce)
    at = aa * jnp.exp(ce - lw)
    bt = bb * e_neg
    kt = kp * e_neg
    rt = r * e_pos
    s0p = s0 * jnp.exp(c_mid)

    bk = cat([bt, kt], axis=-2)
    sc = _bmm(mms.score, cat([at * m_a, at * m_b, rt * m_a, rt * m_b], axis=-2), _bt(bk))
    ab_a, ak_a = sc[..., 0:ln, 0:ln] * strict, sc[..., 0:ln, ln:] * strict
    ab_b, ak_b = sc[..., ln:2 * ln, 0:ln] * strict, sc[..., ln:2 * ln, ln:] * strict
    incl2 = cat([incl, incl], axis=1)
    p_a = sc[..., 2 * ln:3 * ln, :] * incl2
    p_b = sc[..., 3 * ln:, :] * incl2

    base = _bmm(mms.score, cat([at, rt], axis=-2), _bt(s0p))
    rhs = base[..., :ln, :] + m_a * _bmm(mms.out, ak_a, v) + m_b * _bmm(mms.out, ak_b, v)

    pa, pb = ab_a, ab_b
    xa = rhs + _bmm(mms.apply, pa, rhs)
    xb = rhs + _bmm(mms.apply, pb, rhs)
    for _ in range(6):
        pa = _bmm(mms.square, pa, pa)
        pb = _bmm(mms.square, pb, pb)
        xa = xa + _bmm(mms.apply, pa, xa)
        xb = xb + _bmm(mms.apply, pb, xb)
    u = m_a * xa + m_b * xb
    uv = cat([u, v], axis=-2)
    y = base[..., ln:, :] + m_a * _bmm(mms.out, p_a, uv) + m_b * _bmm(mms.out, p_b, uv)
    s_new = (s0p + _bmm(mms.state, _bt(uv), bk)) * same_head * jnp.exp(pick_row(ce, ln - 1))

    mean = hsum(y) * (1.0 / HEAD)
    yc = y - mean
    var = hsum(yc * yc) * (1.0 / HEAD)
    yn = yc * lax.rsqrt(var + GN_EPS) * ln_g + ln_b
    bonus = hsum(r * kp * r_k) * v
    return (yn + bonus) * g, s_new


N_WKV_ROWS = 6
N_WKV_PAR = 7


WKV_PAIRS_PER_STEP = 2


def _to_chains(val, nbat, pp):
    if val.ndim == 2:
        return jnp.stack([val[:, q * LANES:(q + 1) * LANES] for _ in range(nbat) for q in range(pp)])
    return jnp.stack([val[b, :, q * LANES:(q + 1) * LANES] for b in range(nbat) for q in range(pp)])


def _wkv_fwd(seq, rows, pars):
    n = rows[0].shape[0]
    nbat, nch, npair, pp = n // seq, seq // CHUNK, D // LANES, WKV_PAIRS_PER_STEP
    chunk_fn = functools.partial(_wkv_chunk, _wkv_mms(WKV_PRECISION))

    def body(*refs):
        row_vals = [_to_chains(r[...], nbat, pp) for r in refs[:N_WKV_ROWS]]
        par_vals = [_to_chains(r[...], nbat, pp) for r in refs[N_WKV_ROWS:N_WKV_ROWS + N_WKV_PAR]]
        yo_ref, ck_ref, s_ref = refs[N_WKV_ROWS + N_WKV_PAR:]
        ch = pl.program_id(1)

        @pl.when(ch == 0)
        def _():
            s_ref[...] = jnp.zeros_like(s_ref)

        s0 = s_ref[...]
        yo, s_new = chunk_fn(s0, *row_vals, *par_vals)
        s_ref[...] = s_new
        for b in range(nbat):
            for q in range(pp):
                ck_ref[b, q] = s0[b * pp + q]
                yo_ref[b, :, q * LANES:(q + 1) * LANES] = yo[b * pp + q].astype(yo_ref.dtype)

    tok = pl.BlockSpec((nbat, CHUNK, pp * LANES), lambda p, ch: (0, ch, p))
    par = pl.BlockSpec((1, pp * LANES), lambda p, ch: (0, p))
    ck = pl.BlockSpec((nbat, pp, None, LANES, LANES), lambda p, ch: (0, p, ch, 0, 0))
    yo, ckpt = pl.pallas_call(
        body, name="wkv_fwd",
        out_shape=[jax.ShapeDtypeStruct((nbat, seq, D), BF16),
                   jax.ShapeDtypeStruct((nbat, npair, nch, LANES, LANES), F32)],
        grid=(npair // pp, nch),
        in_specs=[tok] * N_WKV_ROWS + [par] * N_WKV_PAR, out_specs=[tok, ck],
        scratch_shapes=[pltpu.VMEM((nbat * pp, LANES, LANES), F32)],
        compiler_params=_params(("parallel", "arbitrary")),
    )(*[t.reshape(nbat, seq, D) for t in rows], *pars)
    return yo.reshape(n, D), ckpt


def _wkv_bwd(seq, rows, pars, ckpt, dyo):
    n = rows[0].shape[0]
    nbat, nch, npair, pp = n // seq, seq // CHUNK, D // LANES, WKV_PAIRS_PER_STEP
    chunk_fn = functools.partial(_wkv_chunk, _wkv_mms(WKV_PRECISION))
    n_in = N_WKV_ROWS + N_WKV_PAR

    def body(*refs):
        row_vals = [_to_chains(r[...], nbat, pp) for r in refs[:N_WKV_ROWS]]
        par_vals = [_to_chains(r[...], nbat, pp) for r in refs[N_WKV_ROWS:n_in]]
        ck_ref, dyo_ref = refs[n_in:n_in + 2]
        d_rows = refs[n_in + 2:n_in + 2 + N_WKV_ROWS]
        d_pars = refs[n_in + 2 + N_WKV_ROWS:n_in + 2 + N_WKV_ROWS + N_WKV_PAR]
        ds_ref = refs[-1]
        ch = pl.program_id(1)

        @pl.when(ch == 0)
        def _():
            ds_ref[...] = jnp.zeros_like(ds_ref)

        s0 = jnp.stack([ck_ref[b, q] for b in range(nbat) for q in range(pp)])
        dyo_v = _to_chains(dyo_ref[...].astype(F32), nbat, pp)
        _, vjp = jax.vjp(chunk_fn, s0, *row_vals, *par_vals)
        grads = vjp((dyo_v, ds_ref[...]))
        ds_ref[...] = grads[0]
        for ref, val in zip(d_rows, grads[1:1 + N_WKV_ROWS]):
            for b in range(nbat):
                for q in range(pp):
                    ref[b, :, q * LANES:(q + 1) * LANES] = val[b * pp + q].astype(ref.dtype)
        for ref, val in zip(d_pars, grads[1 + N_WKV_ROWS:]):
            per_pair = [functools.reduce(lambda s, t: s + t, [val[b * pp + q] for b in range(nbat)])
                        for q in range(pp)]
            tot = jnp.concatenate(per_pair, axis=1)

            @pl.when(ch == 0)
            def _(ref=ref, tot=tot):
                ref[...] = tot

            @pl.when(ch != 0)
            def _(ref=ref, tot=tot):
                ref[...] += tot

    tok = pl.BlockSpec((nbat, CHUNK, pp * LANES), lambda p, ch: (0, nch - 1 - ch, p))
    par = pl.BlockSpec((1, pp * LANES), lambda p, ch: (0, p))
    ck = pl.BlockSpec((nbat, pp, None, LANES, LANES), lambda p, ch: (0, p, nch - 1 - ch, 0, 0))
    res = pl.pallas_call(
        body, name="wkv_bwd",
        out_shape=[jax.ShapeDtypeStruct((nbat, seq, D), BF16)] * N_WKV_ROWS
        + [jax.ShapeDtypeStruct((1, D), F32)] * N_WKV_PAR,
        grid=(npair // pp, nch),
        in_specs=[tok] * N_WKV_ROWS + [par] * N_WKV_PAR + [ck, tok],
        out_specs=[tok] * N_WKV_ROWS + [par] * N_WKV_PAR,
        scratch_shapes=[pltpu.VMEM((nbat * pp, LANES, LANES), F32)],
        compiler_params=_params(("parallel", "arbitrary")),
    )(*[t.reshape(nbat, seq, D) for t in rows], *pars, ckpt, dyo.reshape(nbat, seq, D))
    return [t.reshape(n, D) for t in res[:N_WKV_ROWS]] + list(res[N_WKV_ROWS:])


def _place():
    return lax.axis_index("x"), lax.axis_index("y"), lax.axis_index("c")


def _all_gather8(blk):
    m_per, n = blk.shape
    assert m_per % 8 == 0

    def body(x_ref, out_ref, send_sems, recv_sems, local_sem):
        x, y, c = _place()
        me, sibling = (x, y, c), (x, y, 1 - c)
        chips = [(1 - x, y), (x, 1 - y), (1 - x, 1 - y)]

        def rows(px, py, pc):
            return out_ref.at[pl.ds((4 * px + 2 * py + pc) * m_per, m_per), :]

        def copy(k, block, to, src=None):
            return pltpu.make_async_remote_copy(
                src_ref=rows(*block) if src is None else src, dst_ref=rows(*block),
                send_sem=send_sems.at[k], recv_sem=recv_sems.at[k],
                device_id=to, device_id_type=MESH)

        mine = pltpu.make_async_copy(x_ref, rows(*me), local_sem)
        mine.start()
        first = [copy(0, me, sibling, src=x_ref)]
        first += [copy(1 + j, me, (*chip, c), src=x_ref) for j, chip in enumerate(chips)]
        for cp in first:
            cp.start()
        passed = [copy(4 + j, (*chip, c), sibling) for j, chip in enumerate(chips)]
        for j, chip in enumerate(chips):
            copy(1 + j, (*chip, c), me).wait_recv()
            passed[j].start()
        copy(0, sibling, me).wait_recv()
        for j, chip in enumerate(chips):
            copy(4 + j, (*chip, 1 - c), me).wait_recv()
        for cp in first + passed:
            cp.wait_send()
        mine.wait()

    vm = pl.BlockSpec(memory_space=pltpu.VMEM)
    return pl.pallas_call(
        body, name="all_gather8_%dx%d" % (m_per, n),
        out_shape=jax.ShapeDtypeStruct((N_DEV * m_per, n), blk.dtype),
        in_specs=[vm], out_specs=vm,
        scratch_shapes=[pltpu.SemaphoreType.DMA((7,)), pltpu.SemaphoreType.DMA((7,)),
                        pltpu.SemaphoreType.DMA],
        compiler_params=_params(),
    )(blk)


def _own_slot(src, name):
    r, w = src.shape[-2:]
    tr = _tile(r, 1008)
    xi, yi, _ = _place()
    chip = jnp.reshape(2 * xi + yi, (1,)).astype(jnp.int32)

    def body(chip_ref, x_ref, o_ref):
        o_ref[...] = x_ref[...]

    if src.ndim == 2:
        in_spec = pl.BlockSpec((tr, w), lambda i, chip_ref: (i, 0))
    else:
        in_spec = pl.BlockSpec((None, tr, w), lambda i, chip_ref: (chip_ref[0], i, 0))
    return pl.pallas_call(
        body, name=name,
        out_shape=jax.ShapeDtypeStruct((N_CHIPS, r, w), src.dtype),
        grid_spec=pltpu.PrefetchScalarGridSpec(
            num_scalar_prefetch=1, grid=(r // tr,), in_specs=[in_spec],
            out_specs=pl.BlockSpec((None, tr, w), lambda i, chip_ref: (chip_ref[0], i, 0))),
        compiler_params=_params(("parallel",)),
    )(chip, src)


def _chip_all_gather(shard):
    r, w = shard.shape
    half = r // 2
    assert r % 2 == 0 and half % 16 == 0

    def body(x_ref, buf_ref, out_ref, send_sems, recv_sems):
        del buf_ref
        x, y, c = _place()
        sibling = (x, y, 1 - c)
        me_p = 2 * x + y
        chips = [(1 - x, y), (x, 1 - y), (1 - x, 1 - y)]

        def piece(p, h):
            return out_ref.at[p, pl.ds(h * half, half), :]

        def copy(k, p, h, to, src=None):
            return pltpu.make_async_remote_copy(
                src_ref=piece(p, h) if src is None else src, dst_ref=piece(p, h),
                send_sem=send_sems.at[k], recv_sem=recv_sems.at[k],
                device_id=to, device_id_type=MESH)

        my_half = x_ref.at[pl.ds(c * half, half), :]
        first = [copy(j, me_p, c, (*chip, c), src=my_half) for j, chip in enumerate(chips)]
        for cp in first:
            cp.start()
        passed = [copy(3 + j, 2 * chip[0] + chip[1], c, sibling) for j, chip in enumerate(chips)]
        for j, chip in enumerate(chips):
            copy(j, 2 * chip[0] + chip[1], c, sibling).wait_recv()
            passed[j].start()
        for j, chip in enumerate(chips):
            copy(3 + j, 2 * chip[0] + chip[1], 1 - c, sibling).wait_recv()
        for cp in first + passed:
            cp.wait_send()

    hbm = pl.BlockSpec(memory_space=pl.ANY)
    return pl.pallas_call(
        body, name="chip_all_gather",
        out_shape=jax.ShapeDtypeStruct((N_CHIPS, r, w), shard.dtype),
        in_specs=[hbm, hbm], out_specs=hbm, input_output_aliases={1: 0},
        scratch_shapes=[pltpu.SemaphoreType.DMA((6,)), pltpu.SemaphoreType.DMA((6,))],
        compiler_params=_params(),
    )(shard, _own_slot(shard, "gather_own_slot"))


def _sibling_swap_halves(g):
    _, r, w = g.shape
    half = r // 2

    def body(g_ref, t_ref, send_sem, recv_sem):
        x, y, c = _place()
        sibling = (x, y, 1 - c)
        cp = pltpu.make_async_remote_copy(
            src_ref=g_ref.at[:, pl.ds((1 - c) * half, half), :], dst_ref=t_ref,
            send_sem=send_sem, recv_sem=recv_sem, device_id=sibling, device_id_type=MESH)
        cp.start()
        cp.wait()

    hbm = pl.BlockSpec(memory_space=pl.ANY)
    return pl.pallas_call(
        body, name="rs_sibling_halves",
        out_shape=jax.ShapeDtypeStruct((N_CHIPS, half, w), g.dtype),
        in_specs=[hbm], out_specs=hbm,
        scratch_shapes=[pltpu.SemaphoreType.DMA, pltpu.SemaphoreType.DMA],
        compiler_params=_params(),
    )(g)


def _add_own_half(g, t):
    _, r, w = g.shape
    half = r // 2
    tr = 1008 if half % 1008 == 0 else 16
    assert half % tr == 0
    cidx = jnp.reshape(lax.axis_index("c"), (1,)).astype(jnp.int32)

    def body(c_ref, g_ref, t_ref, o_ref):
        o_ref[...] = (g_ref[...] + t_ref[...]).astype(o_ref.dtype)

    return pl.pallas_call(
        body, name="rs_add_halves",
        out_shape=jax.ShapeDtypeStruct((N_CHIPS, half, w), BF16),
        grid_spec=pltpu.PrefetchScalarGridSpec(
            num_scalar_prefetch=1, grid=(N_CHIPS, half // tr),
            in_specs=[pl.BlockSpec((None, None, tr, w), lambda p, i, c_ref: (p, c_ref[0], i, 0)),
                      pl.BlockSpec((None, tr, w), lambda p, i, c_ref: (p, i, 0))],
            out_specs=pl.BlockSpec((None, tr, w), lambda p, i, c_ref: (p, i, 0))),
        compiler_params=_params(("parallel", "parallel")),
    )(cidx, g.reshape(N_CHIPS, 2, half, w), t)


def _chip_exchange(h):
    _, hh, w = h.shape

    def body(h_ref, buf_ref, t_ref, send_sems, recv_sems):
        del buf_ref
        x, y, c = _place()
        me_p = 2 * x + y
        chips = [(1 - x, y), (x, 1 - y), (1 - x, 1 - y)]
        cps = []
        for j, chip in enumerate(chips):
            q = 2 * chip[0] + chip[1]
            cps.append(pltpu.make_async_remote_copy(
                src_ref=h_ref.at[q], dst_ref=t_ref.at[me_p],
                send_sem=send_sems.at[j], recv_sem=recv_sems.at[j],
                device_id=(*chip, c), device_id_type=MESH))
        for cp in cps:
            cp.start()
        for j, chip in enumerate(chips):
            q = 2 * chip[0] + chip[1]
            pltpu.make_async_remote_copy(
                src_ref=h_ref.at[q], dst_ref=t_ref.at[q],
                send_sem=send_sems.at[j], recv_sem=recv_sems.at[j],
                device_id=(*chip, c), device_id_type=MESH).wait_recv()
        for cp in cps:
            cp.wait_send()

    hbm = pl.BlockSpec(memory_space=pl.ANY)
    return pl.pallas_call(
        body, name="rs_chip_exchange",
        out_shape=jax.ShapeDtypeStruct(h.shape, h.dtype),
        in_specs=[hbm, hbm], out_specs=hbm, input_output_aliases={1: 0},
        scratch_shapes=[pltpu.SemaphoreType.DMA((3,)), pltpu.SemaphoreType.DMA((3,))],
        compiler_params=_params(),
    )(h, _own_slot(h, "rs_own_slot"))


def _sum_slots(t):
    _, hh, w = t.shape
    tr = 1008 if hh % 1008 == 0 else 16
    assert hh % tr == 0
    nblk = hh // tr
    cidx = jnp.reshape(lax.axis_index("c"), (1,)).astype(jnp.int32)

    def body(c_ref, t_ref, o_ref):
        s0, s1, s2, s3 = [t_ref[j].astype(F32) for j in range(N_CHIPS)]
        o_ref[...] = ((s0 + s1) + s2) + s3

    return pl.pallas_call(
        body, name="rs_sum_slots", out_shape=jax.ShapeDtypeStruct((2 * hh, w), F32),
        grid_spec=pltpu.PrefetchScalarGridSpec(
            num_scalar_prefetch=1, grid=(nblk,),
            in_specs=[pl.BlockSpec((N_CHIPS, tr, w), lambda i, c_ref: (0, i, 0))],
            out_specs=pl.BlockSpec((tr, w), lambda i, c_ref: (c_ref[0] * nblk + i, 0))),
        compiler_params=_params(("parallel",)),
    )(cidx, t)


def _sibling_join_halves(s):
    h2, w = s.shape
    hh = h2 // 2

    def body(s_ref, o_ref, send_sem, recv_sem):
        del s_ref
        x, y, c = _place()
        sibling = (x, y, 1 - c)
        cp = pltpu.make_async_remote_copy(
            src_ref=o_ref.at[pl.ds(c * hh, hh), :], dst_ref=o_ref.at[pl.ds(c * hh, hh), :],
            send_sem=send_sem, recv_sem=recv_sem, device_id=sibling, device_id_type=MESH)
        cp.start()
        pltpu.make_async_remote_copy(
            src_ref=o_ref.at[pl.ds((1 - c) * hh, hh), :], dst_ref=o_ref.at[pl.ds((1 - c) * hh, hh), :],
            send_sem=send_sem, recv_sem=recv_sem, device_id=sibling, device_id_type=MESH).wait_recv()
        cp.wait_send()

    hbm = pl.BlockSpec(memory_space=pl.ANY)
    return pl.pallas_call(
        body, name="rs_sibling_join",
        out_shape=jax.ShapeDtypeStruct(s.shape, s.dtype),
        in_specs=[hbm], out_specs=hbm, input_output_aliases={0: 0},
        scratch_shapes=[pltpu.SemaphoreType.DMA, pltpu.SemaphoreType.DMA],
        compiler_params=_params(),
    )(s)


def _reduce_scatter(g):
    h = _add_own_half(g, _sibling_swap_halves(g))
    return _sibling_join_halves(_sum_slots(_chip_exchange(h)))


def _unshard_cols(piece):
    p, k, n = piece.shape
    return jnp.transpose(piece, (1, 0, 2)).reshape(k, p * n)


def _shard_cols(full):
    k, n4 = full.shape
    return jnp.transpose(full.reshape(k, N_CHIPS, n4 // N_CHIPS), (1, 0, 2))


def _rows_of(piece):
    return piece.reshape(N_CHIPS, -1, ROW_W)


def _pad_rows(a, mult):
    pad = (-a.shape[-2]) % mult
    if pad == 0:
        return a
    widths = [(0, 0)] * (a.ndim - 2) + [(0, pad), (0, 0)]
    return jnp.pad(a, widths)


def _adamw(w, g, m, v):
    m2 = ADAM_B1 * m + (1.0 - ADAM_B1) * g
    v2 = ADAM_B2 * v + (1.0 - ADAM_B2) * (g * g)
    m_hat = m2 / (1.0 - ADAM_B1 ** ADAM_STEP)
    v_hat = v2 / (1.0 - ADAM_B2 ** ADAM_STEP)
    delta = -ADAM_LR * (m_hat / (jnp.sqrt(v_hat) + ADAM_EPS) + ADAM_WD * w)
    return delta, m2, v2


def _mlp_fwd(seq, tm, x_in, mix, mod, w1, w2, tag):
    x1, h2 = _rw_fwd(_f_res_norm2, "res_norm2_" + tag, seq=seq, tm=tm, rows=[x_in, mix], bvecs=[mod],
                     outs=[(D, F32), (D, BF16)])
    p, f = _matmul_ep(h2, w1, [], lambda acc: (acc, _f_relu2(acc)[0]), [F32, BF16],
                      mode="nn", name="mlp_up_" + tag)
    ff = _matmul(f, w2, mode="nn", name="mlp_down_" + tag)
    (x2,) = _rw_fwd(_f_res2, "res2_" + tag, seq=seq, tm=tm, rows=[x1, ff], bvecs=[mod], outs=[(D, F32)])
    return x2, (x1, h2, p, f, ff)


def _mlp_bwd(seq, tm, saved, x_in, mix, mod, w1, w2, dx2, tag):
    x1, h2, p, f, ff = saved
    dff, dmod_a = _rw_bwd(_f_res2, "res2_bwd_" + tag, seq=seq, tm=tm, rows=[x1, ff], bvecs=[mod],
                          cts=[dx2], need_rows=[1], dtypes=[BF16])
    (dp,) = _matmul_ep(dff, w2, [p], lambda acc, pt: (2.0 * jnp.maximum(pt, 0.0) * acc,), [BF16],
                       mode="nt", name="mlp_down_dx_" + tag)
    dw2 = _matmul(f, dff, mode="tn", name="mlp_down_dw_" + tag)
    dh2 = _matmul(dp, w1, mode="nt", name="mlp_up_dx_" + tag)
    dw1 = _matmul(h2, dp, mode="tn", name="mlp_up_dw_" + tag, out_shards=N_CHIPS)
    dx_in, dmix, dmod_b = _rw_bwd(_f_res_norm2, "res_norm2_bwd_" + tag, seq=seq, tm=tm, rows=[x_in, mix],
                                  bvecs=[mod], cts=[dx2, dh2], need_rows=[0, 1], dtypes=[F32, BF16])
    return dx_in, dmix, dmod_a + dmod_b, dw1, dw2


def kernel(x, c, ada_w, ada_b, mlp_w1, mlp_w2, a_w_in, a_ln_g, a_ln_b, a_w_s, a_b_s, a_w_out, b_mu, b_w_in, b_w0, b_w1, b_w2, b_a0, b_a1, b_a2, b_g1, b_g2, b_k_k, b_k_a, b_r_k, b_ln_g, b_ln_b, b_w_out, final_g, loss_target, m_ada_w, m_ada_b, m_mlp_w1, m_mlp_w2, m_a_w_in, m_a_ln_g, m_a_ln_b, m_a_w_s, m_a_b_s, m_a_w_out, m_b_mu, m_b_w_in, m_b_w0, m_b_w1, m_b_w2, m_b_a0, m_b_a1, m_b_a2, m_b_g1, m_b_g2, m_b_k_k, m_b_k_a, m_b_r_k, m_b_ln_g, m_b_ln_b, m_b_w_out, m_final_g, v_ada_w, v_ada_b, v_mlp_w1, v_mlp_w2, v_a_w_in, v_a_ln_g, v_a_ln_b, v_a_w_s, v_a_b_s, v_a_w_out, v_b_mu, v_b_w_in, v_b_w0, v_b_w1, v_b_w2, v_b_a0, v_b_a1, v_b_a2, v_b_g1, v_b_g2, v_b_k_k, v_b_k_a, v_b_r_k, v_b_ln_g, v_b_ln_b, v_b_w_out, v_final_g):
    weights = dict(ada_w=ada_w, ada_b=ada_b, mlp_w1=mlp_w1, mlp_w2=mlp_w2, a_w_in=a_w_in, a_ln_g=a_ln_g,
                   a_ln_b=a_ln_b, a_w_s=a_w_s, a_b_s=a_b_s, a_w_out=a_w_out, b_mu=b_mu, b_w_in=b_w_in,
                   b_w0=b_w0, b_w1=b_w1, b_w2=b_w2, b_a0=b_a0, b_a1=b_a1, b_a2=b_a2, b_g1=b_g1, b_g2=b_g2,
                   b_k_k=b_k_k, b_k_a=b_k_a, b_r_k=b_r_k, b_ln_g=b_ln_g, b_ln_b=b_ln_b, b_w_out=b_w_out,
                   final_g=final_g)
    moms = dict(ada_w=(m_ada_w, v_ada_w), ada_b=(m_ada_b, v_ada_b), mlp_w1=(m_mlp_w1, v_mlp_w1),
                mlp_w2=(m_mlp_w2, v_mlp_w2), a_w_in=(m_a_w_in, v_a_w_in), a_ln_g=(m_a_ln_g, v_a_ln_g),
                a_ln_b=(m_a_ln_b, v_a_ln_b), a_w_s=(m_a_w_s, v_a_w_s), a_b_s=(m_a_b_s, v_a_b_s),
                a_w_out=(m_a_w_out, v_a_w_out), b_mu=(m_b_mu, v_b_mu), b_w_in=(m_b_w_in, v_b_w_in),
                b_w0=(m_b_w0, v_b_w0), b_w1=(m_b_w1, v_b_w1), b_w2=(m_b_w2, v_b_w2), b_a0=(m_b_a0, v_b_a0),
                b_a1=(m_b_a1, v_b_a1), b_a2=(m_b_a2, v_b_a2), b_g1=(m_b_g1, v_b_g1), b_g2=(m_b_g2, v_b_g2),
                b_k_k=(m_b_k_k, v_b_k_k), b_k_a=(m_b_k_a, v_b_k_a), b_r_k=(m_b_r_k, v_b_r_k),
                b_ln_g=(m_b_ln_g, v_b_ln_g), b_ln_b=(m_b_ln_b, v_b_ln_b), b_w_out=(m_b_w_out, v_b_w_out),
                final_g=(m_final_g, v_final_g))
    order = list(weights)

    nbat, seq, _ = x.shape
    n = nbat * seq
    tm = 256
    xi, yi, ci = _place()
    chip = 2 * xi + yi
    dev = 2 * chip + ci
    x0 = x.reshape(n, D)
    tgt = loss_target.reshape(n, D)
    lora_w, lora_g = b_w1.shape[-1], b_g1.shape[-1]
    lora_wp, lora_gp = LANES, 2 * LANES

    (cond,) = _small(lambda cc: (cc / (1.0 + jnp.exp(-cc)),), "silu_c", [c], [c.shape])
    vec_names = ["b_w0", "b_a0", "b_k_k", "b_k_a", "b_ln_g", "b_ln_b"]
    vec_shard = jnp.concatenate([b_mu[0]] + [weights[k] for k in vec_names], axis=0)
    n_vec = vec_shard.shape[0]
    vec_rows = vec_shard.reshape(-1, ROW_W)
    blk = _pad_rows(jnp.concatenate([cond, vec_rows], axis=0), 8)
    assert blk.shape[0] == 8
    gathered = _all_gather8(blk).reshape(N_DEV, 8, D)
    cond_all = gathered[:, :nbat].reshape(N_DEV * nbat, D)
    vec_all = gathered[0::2, nbat:nbat + vec_rows.shape[0]].reshape(N_CHIPS, n_vec, D // N_CHIPS)
    vec_full = jnp.transpose(vec_all, (1, 0, 2)).reshape(n_vec, D)
    mu_full = vec_full[0:6]
    w0_f, a0_f, kk_f, ka_f, lng_f, lnb_f = [vec_full[6 + j:7 + j] for j in range(6)]
    rk_f = b_r_k.reshape(1, D)

    n_ada = ada_w.shape[-1]
    parts = jnp.concatenate(
        [_matmul(cond_all, ada_w[i], mode="nn", name="ada_fwd_%d" % i) for i in range(2)], axis=1)
    parts_all = _all_gather8(parts).reshape(N_DEV, N_DEV * nbat, 2, n_ada)[0::2]
    mine = lax.dynamic_slice_in_dim(parts_all, dev * nbat, nbat, axis=1)
    mods = []
    for i in range(2):
        full = jnp.transpose(mine[:, :, i], (1, 0, 2)).reshape(nbat, N_MOD * D) + ada_b[i]
        mods.append(full.reshape(nbat, N_MOD, D))

    big = [("mlp_w1_0", mlp_w1[0]), ("mlp_w1_1", mlp_w1[1]), ("mlp_w2_0", mlp_w2[0]), ("mlp_w2_1", mlp_w2[1]),
           ("a_w_in", a_w_in[0]), ("a_w_out", a_w_out[0]), ("b_w_in", b_w_in[0]), ("b_w_out", b_w_out[0]),
           ("b_w1", b_w1[0]), ("b_w2", b_w2[0]), ("b_a1", b_a1[0]), ("b_a2", b_a2[0]),
           ("b_g1", b_g1[0]), ("b_g2", b_g2[0])]
    offs, pos = {}, 0
    for name, arr in big:
        rows_k = arr.size // ROW_W
        offs[name] = (pos, rows_k, arr.shape)
        pos += rows_k
    n_big_rows = pos
    wflat = _pad_rows(jnp.concatenate([arr.astype(BF16).reshape(-1, ROW_W) for _, arr in big], axis=0), 32)
    wg = _chip_all_gather(wflat)

    def gathered_piece(name):
        start, rows_k, shape = offs[name]
        return wg[:, start:start + rows_k].reshape((N_CHIPS,) + shape)

    def col_w(name):
        return _unshard_cols(gathered_piece(name))

    def row_w(name):
        piece = gathered_piece(name)
        return piece.reshape(N_CHIPS * piece.shape[1], piece.shape[2])

    w1_l = [col_w("mlp_w1_0"), col_w("mlp_w1_1")]
    w2_l = [row_w("mlp_w2_0"), row_w("mlp_w2_1")]
    a_win, a_wout = col_w("a_w_in"), row_w("a_w_out")
    b_win, b_wout = col_w("b_w_in"), row_w("b_w_out")
    w_r, w_k, w_v = b_win[:, :D], b_win[:, D:2 * D], b_win[:, 2 * D:]
    w1p = jnp.pad(row_w("b_w1"), ((0, 0), (0, lora_wp - lora_w)))
    a1p = jnp.pad(row_w("b_a1"), ((0, 0), (0, lora_wp - lora_w)))
    g1p = jnp.pad(row_w("b_g1"), ((0, 0), (0, lora_gp - lora_g)))
    w2p = jnp.pad(col_w("b_w2"), ((0, lora_wp - lora_w), (0, 0)))
    a2p = jnp.pad(col_w("b_a2"), ((0, lora_wp - lora_w), (0, 0)))
    g2p = jnp.pad(col_w("b_g2"), ((0, lora_gp - lora_g), (0, 0)))

    mod0, mod1 = mods
    (h_a,) = _rw_fwd(_f_norm1, "norm1_a", seq=seq, tm=tm, rows=[x0], bvecs=[mod0], outs=[(D, BF16)])
    uvp = _matmul(h_a, a_win, mode="nn", name="sgu_in")
    u, vn = _rw_fwd(_f_sgu_pre, "sgu_pre", seq=seq, tm=tm, rows=[uvp], params=[a_ln_g, a_ln_b],
                    outs=[(D, F32), (D, F32)])
    ws = a_w_s[0]
    bias = jnp.broadcast_to(a_b_s[0][:, :, None], (GROUPS, CHUNK, LANES))
    z = _sgu_fwd(u, vn, ws, bias)
    mix0 = _matmul(z, a_wout, mode="nn", name="sgu_out")
    x2, saved0 = _mlp_fwd(seq, tm, x0, mix0, mod0, w1_l[0], w2_l[0], "0")

    def shift_fwd(ctx, rv, pv, nv, bv, pa):
        _, _, mixes = _shift_mix(ctx, rv[0], pv[0], bv[0], pa[0])
        return mixes, [], []

    xr, xw, xk, xv, xa, xg = _rowwise(shift_fwd, "shift_mix", seq=seq, tm=tm, rows=[x2], prev8=[x2],
                                      bvecs=[mod1], params=[mu_full], out_rows=[(D, BF16)] * 6)
    r = _matmul(xr, w_r, mode="nn", name="rwkv_r")
    k = _matmul(xk, w_k, mode="nn", name="rwkv_k")
    v = _matmul(xv, w_v, mode="nn", name="rwkv_v")
    t1 = _matmul(xw, w1p, mode="nn", name="lora_w1")
    t2 = _matmul(xa, a1p, mode="nn", name="lora_a1", out_dtype=BF16)
    t3 = _matmul(xg, g1p, mode="nn", name="lora_g1")
    (th,) = _rw_fwd(lambda t: (jnp.tanh(t),), "lora_tanh", seq=seq, tm=tm, rows=[t1], outs=[(lora_wp, BF16)])
    (sg,) = _rw_fwd(lambda t: (1.0 / (1.0 + jnp.exp(-t)),), "lora_sigmoid", seq=seq, tm=tm, rows=[t3],
                    outs=[(lora_gp, BF16)])
    wl = _matmul(th, w2p, mode="nn", name="lora_w2")
    al = _matmul(t2, a2p, mode="nn", name="lora_a2")
    g = _matmul(sg, g2p, mode="nn", name="lora_g2")
    wkv_rows = [r, k, v, wl, al, g]
    wkv_pars = [w0_f, a0_f, kk_f, ka_f, rk_f, lng_f, lnb_f]
    yo, ckpt = _wkv_fwd(seq, wkv_rows, wkv_pars)
    mix1 = _matmul(yo, b_wout, mode="nn", name="rwkv_out")
    x4, saved1 = _mlp_fwd(seq, tm, x2, mix1, mod1, w1_l[1], w2_l[1], "1")

    def loss_fn(ctx, rv, pv, nv, bv, pa):
        val, (dx, dfg) = jax.value_and_grad(_f_loss, argnums=(0, 2))(rv[0], rv[1], pa[0])
        return [dx], [], [dfg, jnp.full((1, LANES), val, F32)]

    dx4, d_final_g, loss_acc = _rowwise(loss_fn, "loss_head", seq=seq, tm=tm, rows=[x4, tgt],
                                        params=[final_g.reshape(1, D)], out_rows=[(D, F32)],
                                        out_pacc=[(1, D), (1, LANES)])
    loss = lax.psum(loss_acc[0, 0], AXES)

    dx2_a, dmix1, dmod1, dw1_1, dw2_1 = _mlp_bwd(seq, tm, saved1, x2, mix1, mod1, w1_l[1], w2_l[1], dx4, "1")
    dyo = _matmul(dmix1, b_wout, mode="nt", name="rwkv_out_dx")
    d_b_wout = _matmul(yo, dmix1, mode="tn", name="rwkv_out_dw")
    wkv_grads = _wkv_bwd(seq, wkv_rows, wkv_pars, ckpt, dyo)
    dr, dk, dv, dwl, dal, dg = wkv_grads[:N_WKV_ROWS]
    d_w0, d_a0, d_kk, d_ka, d_rk, d_lng, d_lnb = wkv_grads[N_WKV_ROWS:]
    dth = _matmul(dwl, w2p, mode="nt", name="lora_w2_dx")
    d_w2p = _matmul(th, dwl, mode="tn", name="lora_w2_dw")
    dt2 = _matmul(dal, a2p, mode="nt", name="lora_a2_dx", out_dtype=BF16)
    d_a2p = _matmul(t2, dal, mode="tn", name="lora_a2_dw")
    dsg = _matmul(dg, g2p, mode="nt", name="lora_g2_dx")
    d_g2p = _matmul(sg, dg, mode="tn", name="lora_g2_dw")
    (dt1,) = _rw_bwd(lambda t: (jnp.tanh(t),), "lora_tanh_bwd", seq=seq, tm=tm, rows=[t1], cts=[dth],
                     need_rows=[0], dtypes=[BF16])
    (dt3,) = _rw_bwd(lambda t: (1.0 / (1.0 + jnp.exp(-t)),), "lora_sigmoid_bwd", seq=seq, tm=tm, rows=[t3],
                     cts=[dsg], need_rows=[0], dtypes=[BF16])
    dxw = _matmul(dt1, w1p, mode="nt", name="lora_w1_dx")
    d_w1p = _matmul(xw, dt1, mode="tn", name="lora_w1_dw")
    dxa = _matmul(dt2, a1p, mode="nt", name="lora_a1_dx")
    d_a1p = _matmul(xa, dt2, mode="tn", name="lora_a1_dw")
    dxg = _matmul(dt3, g1p, mode="nt", name="lora_g1_dx")
    d_g1p = _matmul(xg, dt3, mode="tn", name="lora_g1_dw")
    dxr = _matmul(dr, w_r, mode="nt", name="rwkv_r_dx")
    dxk = _matmul(dk, w_k, mode="nt", name="rwkv_k_dx")
    dxv = _matmul(dv, w_v, mode="nt", name="rwkv_v_dx")
    d_b_win = jnp.concatenate([_matmul(xr, dr, mode="tn", name="rwkv_r_dw"),
                               _matmul(xk, dk, mode="tn", name="rwkv_k_dw"),
                               _matmul(xv, dv, mode="tn", name="rwkv_v_dw")], axis=1)

    def shift_bwd(ctx, rv, pv, nv, bv, pa):
        xt, dres = rv[0], rv[1]
        dmix_in = rv[2:8]
        mod, mu = bv[0], pa[0]
        f_h = lambda xx_, mod_: _rmsmod(xx_, mod_[0:1], mod_[1:2])
        h, vjp = jax.vjp(f_h, xt, mod)
        hprev = f_h(pv[0], mod)[7:8]
        hprev = jnp.where(ctx.first, jnp.zeros_like(hprev), hprev)
        rowid = lax.broadcasted_iota(jnp.int32, h.shape, 0)
        xx = jnp.where(rowid == 0, hprev, pltpu.roll(h, 1, 0)) - h
        tot = dmix_in[0]
        wsum = dmix_in[0] * mu[0:1]
        for j in range(1, 6):
            tot = tot + dmix_in[j]
            wsum = wsum + dmix_in[j] * mu[j:j + 1]
        nxt = nv[0][0:1] * mu[0:1]
        for j in range(1, 6):
            nxt = nxt + nv[j][0:1] * mu[j:j + 1]
        nxt = jnp.where(ctx.last, jnp.zeros_like(nxt), nxt)
        tmr = h.shape[0]
        wshift = jnp.where(rowid == tmr - 1, nxt, pltpu.roll(wsum, tmr - 1, 0))
        dh = tot - wsum + wshift
        dx_, dmod_ = vjp(dh)
        dmu = jnp.concatenate([jnp.sum(dmix_in[j] * xx, axis=0, keepdims=True) for j in range(6)], axis=0)
        return [dx_ + dres], [dmod_], [dmu]

    dmix_list = [dxr, dxw, dxk, dxv, dxa, dxg]
    dx2, dmod1_c, d_mu = _rowwise(shift_bwd, "shift_mix_bwd", seq=seq, tm=tm, rows=[x2, dx2_a] + dmix_list,
                                  prev8=[x2], next8=dmix_list, bvecs=[mod1], params=[mu_full],
                                  out_rows=[(D, F32)], out_bacc=[(N_MOD, D)], out_pacc=[(6, D)])
    dmod1 = dmod1 + dmod1_c

    dx0_a, dmix0, dmod0, dw1_0, dw2_0 = _mlp_bwd(seq, tm, saved0, x0, mix0, mod0, w1_l[0], w2_l[0], dx2, "0")
    dz = _matmul(dmix0, a_wout, mode="nt", name="sgu_out_dx")
    d_a_wout = _matmul(z, dmix0, mode="tn", name="sgu_out_dw")
    du, dvn, d_ws, d_bias = _sgu_bwd(u, vn, ws, bias, dz)
    duvp, d_a_lng, d_a_lnb = _rw_bwd(_f_sgu_pre, "sgu_pre_bwd", seq=seq, tm=tm, rows=[uvp],
                                     params=[a_ln_g, a_ln_b], cts=[du, dvn], need_rows=[0], dtypes=[BF16])
    dh_a = _matmul(duvp, a_win, mode="nt", name="sgu_in_dx")
    d_a_win = _matmul(h_a, duvp, mode="tn", name="sgu_in_dw", out_shards=N_CHIPS)
    grad_x, dmod0_c = _rw_bwd(_f_norm1, "norm1_a_bwd", seq=seq, tm=tm, rows=[x0], bvecs=[mod0], cts=[dh_a],
                              need_rows=[0], extra=dx0_a)
    dmod0 = dmod0 + dmod0_c

    dmod_blk = _pad_rows(jnp.concatenate([dmod0.reshape(nbat, -1), dmod1.reshape(nbat, -1)], axis=1), 8)
    dmod_all = _all_gather8(dmod_blk).reshape(N_DEV, 8, 2, N_MOD * D)[:, :nbat].reshape(N_DEV * nbat, 2, N_MOD * D)
    g_ada_w, g_ada_b = [], []
    for i in range(2):
        cols = lax.dynamic_slice_in_dim(dmod_all[:, i], chip * n_ada, n_ada, axis=1)
        g_ada_w.append(_matmul(cond_all, cols, mode="tn", name="ada_dw_%d" % i))
    (g_ada_b_all,) = _small(lambda t: (jnp.sum(t, axis=0),), "ada_db", [dmod_all], [(2, N_MOD * D)])
    grads = {"ada_w": jnp.stack(g_ada_w), "ada_b": g_ada_b_all}

    rep = _pad_rows(jnp.concatenate([
        d_a_lng, d_a_lnb, jnp.sum(d_bias, axis=-1).reshape(1, D), d_rk, d_final_g,
        jnp.zeros((3, D), F32), d_ws.reshape(-1, D)], axis=0), 8)
    rep_rows = rep.shape[0]
    rep_all = _all_gather8(rep)
    (rep_sum,) = _small(lambda t: (functools.reduce(lambda p, q: p + q,
                                                     [t[j * rep_rows:(j + 1) * rep_rows] for j in range(N_DEV)]),),
                        "replicated_sum", [rep_all], [(rep_rows, D)])
    grads["a_ln_g"] = rep_sum[0:1]
    grads["a_ln_b"] = rep_sum[1:2]
    grads["a_b_s"] = rep_sum[2:3].reshape(a_b_s.shape)
    grads["b_r_k"] = rep_sum[3:4].reshape(b_r_k.shape)
    grads["final_g"] = rep_sum[4].reshape(final_g.shape)
    grads["a_w_s"] = rep_sum[8:8 + GROUPS * CHUNK * LANES // D].reshape(a_w_s.shape)

    vec_grads = jnp.concatenate([d_mu, d_w0, d_a0, d_kk, d_ka, d_lng, d_lnb], axis=0)
    packed = {
        "mlp_w1_0": dw1_0, "mlp_w1_1": dw1_1,
        "mlp_w2_0": dw2_0.reshape(N_CHIPS, -1, D), "mlp_w2_1": dw2_1.reshape(N_CHIPS, -1, D),
        "a_w_in": d_a_win, "a_w_out": d_a_wout.reshape(N_CHIPS, -1, D),
        "b_w_in": _shard_cols(d_b_win), "b_w_out": d_b_wout.reshape(N_CHIPS, -1, D),
        "b_w1": d_w1p[:, :lora_w].reshape(N_CHIPS, -1, lora_w), "b_w2": _shard_cols(d_w2p[:lora_w]),
        "b_a1": d_a1p[:, :lora_w].reshape(N_CHIPS, -1, lora_w), "b_a2": _shard_cols(d_a2p[:lora_w]),
        "b_g1": d_g1p[:, :lora_g].reshape(N_CHIPS, -1, lora_g), "b_g2": _shard_cols(d_g2p[:lora_g]),
    }
    pieces = [_rows_of(packed[name]) for name, _ in big] + [_pad_rows(_rows_of(_shard_cols(vec_grads)), 8)]
    used = sum(p.shape[1] for p in pieces)
    pieces.append(jnp.zeros((N_CHIPS, (-used) % 2016, ROW_W), F32))
    g_pack = jnp.concatenate(pieces, axis=1)
    g_red = _reduce_scatter(g_pack)
    for name, _ in big:
        start, rows_k, shape = offs[name]
        grads[name] = g_red[start:start + rows_k].reshape(shape)
    vec_red = g_red[n_big_rows:n_big_rows + vec_rows.shape[0]].reshape(n_vec, D // N_CHIPS)
    grads["b_mu"] = vec_red[0:6].reshape(b_mu.shape)
    for j, name in enumerate(vec_names):
        grads[name] = vec_red[6 + j:7 + j].reshape(weights[name].shape)
    for base in ("mlp_w1", "mlp_w2"):
        grads[base] = jnp.stack([grads.pop(base + "_0"), grads.pop(base + "_1")])
    for name in ("a_w_in", "a_w_out", "b_w_in", "b_w_out", "b_w1", "b_w2", "b_a1", "b_a2", "b_g1", "b_g2"):
        grads[name] = grads[name].reshape(weights[name].shape)

    deltas, new_m, new_v = {}, {}, {}
    for name in order:
        gr = grads[name].reshape(weights[name].shape)
        grads[name] = gr
        deltas[name], new_m[name], new_v[name] = _elementwise(
            _adamw, "adamw_" + name, [weights[name], gr, moms[name][0], moms[name][1]], 3)

    return (loss, grad_x.reshape(x.shape), *[grads[k] for k in order], *[deltas[k] for k in order],
            *[new_m[k] for k in order], *[new_v[k] for k in order])
```

```python
import functools

import jax
import jax.numpy as jnp
from jax import lax
from jax.experimental import pallas as pl
from jax.experimental.pallas import tpu as pltpu

F32 = jnp.float32
BF16 = jnp.bfloat16
MESH = pl.DeviceIdType.MESH
AXES = ("x", "y", "c")

D = 1024
N_MOD = 6
HEAD = 64
CHUNK = 128
GROUPS = 8
LANES = 128
ROW_W = 1024
N_CHIPS = 4
N_DEV = 8

RMS_EPS = 1e-6
LN_EPS = 1e-5
GN_EPS = HEAD * 1e-5
L2_EPS = 1e-12

ADAM_LR = 0.001
ADAM_B1 = 0.9
ADAM_B2 = 0.999
ADAM_EPS = 1e-08
ADAM_WD = 0.01
ADAM_STEP = 10

VMEM_LIMIT_V7X = 56 * 1024 * 1024
HIGHEST = lax.Precision.HIGHEST


def _params(sem=None):
    return pltpu.CompilerParams(dimension_semantics=sem, vmem_limit_bytes=VMEM_LIMIT_V7X)


def _tile(dim, target):
    if dim <= target:
        return dim
    for cand in range(target, 0, -LANES):
        if dim % cand == 0:
            return cand
    raise ValueError((dim, target))


def _matmul(a, b, *, mode, name, out_dtype=F32, out_shards=1, tm=1024, tn=1024, tk=4096):
    if mode == "nn":
        (m, k), (k2, n) = a.shape, b.shape
    elif mode == "nt":
        (m, k), (n, k2) = a.shape, b.shape
    else:
        (k, m), (k2, n) = a.shape, b.shape
    assert k == k2, (name, a.shape, b.shape)
    n_sh = n // out_shards
    tm, tn, tk = _tile(m, tm), _tile(n_sh, tn), _tile(k, tk)
    nk = k // tk
    nb = n_sh // tn
    use_scratch = nk > 1 and out_dtype != F32

    if mode == "tn":
        a_spec = pl.BlockSpec((tk, tm), lambda i, j, kk: (kk, i))
    else:
        a_spec = pl.BlockSpec((tm, tk), lambda i, j, kk: (i, kk))
    if mode == "nt":
        b_spec = pl.BlockSpec((tn, tk), lambda i, j, kk: (j, kk))
    else:
        b_spec = pl.BlockSpec((tk, tn), lambda i, j, kk: (kk, j))
    if out_shards == 1:
        out_shape = jax.ShapeDtypeStruct((m, n), out_dtype)
        o_spec = pl.BlockSpec((tm, tn), lambda i, j, kk: (i, j))
    else:
        out_shape = jax.ShapeDtypeStruct((out_shards, m, n_sh), out_dtype)
        o_spec = pl.BlockSpec((None, tm, tn), lambda i, j, kk: (j // nb, i, j % nb))

    def body(a_ref, b_ref, o_ref, *scratch):
        kk = pl.program_id(2)
        av = a_ref[...].astype(BF16)
        bv = b_ref[...].astype(BF16)
        if mode == "nn":
            dims = (((1,), (0,)), ((), ()))
        elif mode == "nt":
            dims = (((1,), (1,)), ((), ()))
        else:
            dims = (((0,), (0,)), ((), ()))
        part = lax.dot_general(av, bv, dims, preferred_element_type=F32)
        if nk == 1:
            o_ref[...] = part.astype(o_ref.dtype)
            return
        acc_ref = scratch[0] if use_scratch else o_ref

        @pl.when(kk == 0)
        def _():
            acc_ref[...] = part

        @pl.when(kk != 0)
        def _():
            acc_ref[...] += part

        if use_scratch:
            @pl.when(kk == nk - 1)
            def _():
                o_ref[...] = acc_ref[...].astype(o_ref.dtype)

    return pl.pallas_call(
        body, name=name, out_shape=out_shape,
        grid=(m // tm, n // tn, nk),
        in_specs=[a_spec, b_spec], out_specs=o_spec,
        scratch_shapes=[pltpu.VMEM((tm, tn), F32)] if use_scratch else [],
        compiler_params=_params(("parallel", "parallel", "arbitrary")),
    )(a, b)


def _matmul_ep(a, b, extras, epilogue, out_dtypes, *, mode, name, tm=1024, tn=1024, tk=2048):
    if mode == "nn":
        (m, k), (k2, n) = a.shape, b.shape
    else:
        (m, k), (n, k2) = a.shape, b.shape
    assert k == k2 and mode in ("nn", "nt"), (name, a.shape, b.shape)
    tm, tn, tk = _tile(m, tm), _tile(n, tn), _tile(k, tk)
    nk = k // tk
    n_ex, n_out = len(extras), len(out_dtypes)

    def body(a_ref, b_ref, *rest):
        extra_refs, out_refs = rest[:n_ex], rest[n_ex:n_ex + n_out]
        kk = pl.program_id(2)
        dims = (((1,), (0,)), ((), ())) if mode == "nn" else (((1,), (1,)), ((), ()))
        part = lax.dot_general(a_ref[...].astype(BF16), b_ref[...].astype(BF16), dims,
                               preferred_element_type=F32)

        def finish(acc):
            for ref, val in zip(out_refs, epilogue(acc, *[r[...] for r in extra_refs])):
                ref[...] = val.astype(ref.dtype)

        if nk == 1:
            finish(part)
            return
        acc_ref = rest[-1]

        @pl.when(kk == 0)
        def _():
            acc_ref[...] = part

        @pl.when(kk != 0)
        def _():
            acc_ref[...] += part

        @pl.when(kk == nk - 1)
        def _():
            finish(acc_ref[...])

    a_spec = pl.BlockSpec((tm, tk), lambda i, j, kk: (i, kk))
    b_spec = (pl.BlockSpec((tk, tn), lambda i, j, kk: (kk, j)) if mode == "nn"
              else pl.BlockSpec((tn, tk), lambda i, j, kk: (j, kk)))
    o_spec = pl.BlockSpec((tm, tn), lambda i, j, kk: (i, j))
    res = pl.pallas_call(
        body, name=name, out_shape=[jax.ShapeDtypeStruct((m, n), dt) for dt in out_dtypes],
        grid=(m // tm, n // tn, nk),
        in_specs=[a_spec, b_spec] + [o_spec] * n_ex, out_specs=[o_spec] * n_out,
        scratch_shapes=[pltpu.VMEM((tm, tn), F32)] if nk > 1 else [],
        compiler_params=_params(("parallel", "parallel", "arbitrary")),
    )(a, b, *extras)
    return list(res)


class _Ctx:
    def __init__(self, first, last):
        self.first = first
        self.last = last


def _rowwise(fn, name, *, seq, tm, rows=(), prev8=(), next8=(), bvecs=(), params=(),
             out_rows=(), out_bacc=(), out_pacc=()):
    n = rows[0].shape[0]
    tm = min(tm, seq)
    assert n % seq == 0 and seq % tm == 0 and tm % 8 == 0
    tpb = seq // tm
    nt = n // tm
    nbat = n // seq
    r8 = tm // 8
    counts = [len(rows), len(prev8), len(next8), len(bvecs), len(params)]
    n_in = sum(counts)

    def body(*refs):
        i = pl.program_id(0)
        first = (i % tpb) == 0
        last = (i % tpb) == (tpb - 1)
        vals = [r[...] for r in refs[:n_in]]
        groups, pos = [], 0
        for cnt in counts:
            groups.append(vals[pos:pos + cnt])
            pos += cnt
        ro, bo, po = fn(_Ctx(first, last), *groups)
        outs = refs[n_in:]
        assert len(ro) == len(out_rows) and len(bo) == len(out_bacc) and len(po) == len(out_pacc)
        for ref, val in zip(outs[:len(ro)], ro):
            ref[...] = val.astype(ref.dtype)
        for ref, val in zip(outs[len(ro):len(ro) + len(bo)], bo):
            @pl.when(first)
            def _(ref=ref, val=val):
                ref[...] = val

            @pl.when(jnp.logical_not(first))
            def _(ref=ref, val=val):
                ref[...] += val
        for ref, val in zip(outs[len(ro) + len(bo):], po):
            @pl.when(i == 0)
            def _(ref=ref, val=val):
                ref[...] = val

            @pl.when(i != 0)
            def _(ref=ref, val=val):
                ref[...] += val

    in_specs = []
    for arr in rows:
        in_specs.append(pl.BlockSpec((tm, arr.shape[1]), lambda i: (i, 0)))
    for arr in prev8:
        in_specs.append(pl.BlockSpec((8, arr.shape[1]), lambda i: (jnp.maximum(i * r8 - 1, 0), 0)))
    for arr in next8:
        in_specs.append(pl.BlockSpec((8, arr.shape[1]), lambda i: (jnp.minimum((i + 1) * r8, n // 8 - 1), 0)))
    for arr in bvecs:
        in_specs.append(pl.BlockSpec((None,) + arr.shape[1:], lambda i: (i // tpb, 0, 0)))
    for arr in params:
        in_specs.append(pl.BlockSpec(arr.shape, lambda i: (0, 0)))
    out_shape, out_specs = [], []
    for d, dt in out_rows:
        out_shape.append(jax.ShapeDtypeStruct((n, d), dt))
        out_specs.append(pl.BlockSpec((tm, d), lambda i: (i, 0)))
    for r, d in out_bacc:
        out_shape.append(jax.ShapeDtypeStruct((nbat, r, d), F32))
        out_specs.append(pl.BlockSpec((None, r, d), lambda i: (i // tpb, 0, 0)))
    for r, d in out_pacc:
        out_shape.append(jax.ShapeDtypeStruct((r, d), F32))
        out_specs.append(pl.BlockSpec((r, d), lambda i: (0, 0)))
    res = pl.pallas_call(
        body, name=name, out_shape=out_shape, grid=(nt,),
        in_specs=in_specs, out_specs=out_specs,
        compiler_params=_params(("arbitrary",)),
    )(*rows, *prev8, *next8, *bvecs, *params)
    return list(res)


def _rw_fwd(f, name, *, seq, tm, rows, bvecs=(), params=(), outs):
    def fn(ctx, rv, pv, nv, bv, pa):
        res = f(*[v.astype(F32) for v in rv], *bv, *pa)
        return list(res), [], []
    return _rowwise(fn, name, seq=seq, tm=tm, rows=rows, bvecs=bvecs, params=params, out_rows=outs)


def _rw_bwd(f, name, *, seq, tm, rows, bvecs=(), params=(), cts, need_rows, extra=None, dtypes=None):
    nr, nb, npar = len(rows), len(bvecs), len(params)
    all_rows = list(rows) + list(cts) + ([extra] if extra is not None else [])

    def fn(ctx, rv, pv, nv, bv, pa):
        prim = [v.astype(F32) for v in rv[:nr]]
        ct = tuple(v.astype(F32) for v in rv[nr:nr + len(cts)])
        _, vjp = jax.vjp(f, *prim, *bv, *pa)
        g = vjp(ct)
        d_rows = [g[j] for j in need_rows]
        if extra is not None:
            d_rows[0] = d_rows[0] + rv[-1].astype(F32)
        return d_rows, list(g[nr:nr + nb]), list(g[nr + nb:])

    return _rowwise(
        fn, name, seq=seq, tm=tm, rows=all_rows, bvecs=bvecs, params=params,
        out_rows=[(rows[j].shape[1], F32 if dtypes is None else dtypes[i]) for i, j in enumerate(need_rows)],
        out_bacc=[b.shape[1:] for b in bvecs], out_pacc=[p.shape for p in params])


def _small(fn, name, arrays, out_shapes):
    def body(*refs):
        res = fn(*[r[...] for r in refs[:len(arrays)]])
        for ref, val in zip(refs[len(arrays):], res):
            ref[...] = val.astype(ref.dtype)

    vm = pl.BlockSpec(memory_space=pltpu.VMEM)
    res = pl.pallas_call(
        body, name=name,
        out_shape=[jax.ShapeDtypeStruct(s, F32) for s in out_shapes],
        in_specs=[vm] * len(arrays), out_specs=[vm] * len(out_shapes),
        compiler_params=_params(),
    )(*arrays)
    return list(res)


def _elementwise(fn, name, arrays, n_out):
    shape = arrays[0].shape
    size = arrays[0].size
    if size % ROW_W == 0 and (size // ROW_W) % 8 == 0:
        view = (size // ROW_W, ROW_W)
    else:
        view = (1, size) if len(shape) < 2 else (size // shape[-1], shape[-1])
    rows = view[0]
    tr = rows
    for cand in (256, 128, 64, 32, 16, 8):
        if rows > cand and rows % cand == 0:
            tr = cand
            break

    def body(*refs):
        res = fn(*[r[...] for r in refs[:len(arrays)]])
        for ref, val in zip(refs[len(arrays):], res):
            ref[...] = val

    spec = pl.BlockSpec((tr, view[1]), lambda i: (i, 0))
    res = pl.pallas_call(
        body, name=name,
        out_shape=[jax.ShapeDtypeStruct(view, F32)] * n_out,
        grid=(rows // tr,), in_specs=[spec] * len(arrays), out_specs=[spec] * n_out,
        compiler_params=_params(("parallel",)),
    )(*[a.reshape(view) for a in arrays])
    return [r.reshape(shape) for r in res]


def _rms(x):
    return x * lax.rsqrt(jnp.mean(x * x, axis=-1, keepdims=True) + RMS_EPS)


def _rmsmod(x, sh, sc):
    return _rms(x) * (1.0 + sc) + sh


def _f_norm1(x, mod):
    return (_rmsmod(x, mod[0:1], mod[1:2]),)


def _f_sgu_pre(uvp, ln_g, ln_b):
    uv = 0.5 * uvp * (1.0 + lax.erf(uvp * (2.0 ** -0.5)))
    u = uv[:, :D]
    v = uv[:, D:]
    mu = jnp.mean(v, axis=-1, keepdims=True)
    vc = v - mu
    var = jnp.mean(vc * vc, axis=-1, keepdims=True)
    return u, vc * lax.rsqrt(var + LN_EPS) * ln_g + ln_b


def _f_res_norm2(x, mix, mod):
    x1 = x + mod[2:3] * mix
    return x1, _rmsmod(x1, mod[3:4], mod[4:5])


def _f_relu2(p):
    r = jnp.maximum(p, 0.0)
    return (r * r,)


def _f_res2(x1, ff, mod):
    return (x1 + mod[5:6] * ff,)


def _f_lora_act(t1, t3):
    return jnp.tanh(t1), 1.0 / (1.0 + jnp.exp(-t3))


def _f_loss(x, tgt, fg):
    err = _rms(x) * fg - tgt
    return 0.5 * jnp.sum(jnp.mean(err * err, axis=-1))


def _shift_mix(ctx, x, xprev8, mod, mu):
    h = _rmsmod(x, mod[0:1], mod[1:2])
    hprev = _rmsmod(xprev8, mod[0:1], mod[1:2])[7:8]
    hprev = jnp.where(ctx.first, jnp.zeros_like(hprev), hprev)
    rowid = lax.broadcasted_iota(jnp.int32, h.shape, 0)
    hp = jnp.where(rowid == 0, hprev, pltpu.roll(h, 1, 0))
    xx = hp - h
    return h, xx, [h + xx * mu[j:j + 1] for j in range(6)]


def _split_bf16(t, parts):
    out, rest = [], t.astype(F32)
    for _ in range(parts):
        piece = rest.astype(BF16)
        out.append(piece)
        rest = rest - piece.astype(F32)
    return out


def _make_mm(na, nb):
    def raw(a, b, pa, pb):
        if pa == 0:
            return jnp.dot(a, b, precision=HIGHEST, preferred_element_type=F32)
        acc = None
        bs = _split_bf16(b, pb)
        for i, ai in enumerate(_split_bf16(a, pa)):
            for j, bj in enumerate(bs):
                if i + j < max(pa, pb):
                    term = jnp.dot(ai, bj, preferred_element_type=F32)
                    acc = term if acc is None else acc + term
        return acc

    @jax.custom_vjp
    def mm(a, b):
        return raw(a, b, na, nb)

    def fwd(a, b):
        return raw(a, b, na, nb), (a, b)

    def bwd(res, ct):
        a, b = res
        if na == 0:
            return raw(ct, b.T, 0, 0), raw(a.T, ct, 0, 0)
        return raw(ct, b.T, 1, 1), raw(a.T, ct, 1, 1)

    mm.defvjp(fwd, bwd)
    return mm


class _WkvMms:
    def __init__(self, head_sum, cum, score, square, apply, out, state):
        self.head_sum, self.cum, self.score = head_sum, cum, score
        self.square, self.apply, self.out, self.state = square, apply, out, state


def _wkv_mms(cfg):
    table = {"x": (0, 0), "1": (1, 1), "2": (2, 2), "3": (3, 3), "a": (2, 1), "b": (1, 2)}
    hs, cu, sc_, sq, ap, ou, st = [table[ch] for ch in cfg]
    return _WkvMms(_make_mm(hs[0], 1) if hs[0] else _make_mm(0, 0),
                   _make_mm(1, cu[1]) if cu[0] else _make_mm(0, 0),
                   _make_mm(*sc_), _make_mm(*sq), _make_mm(*ap), _make_mm(*ou), _make_mm(*st))


WKV_PRECISION = "2221b11"


SGU_CHUNKS_PER_STEP = 4


def _sgu_tile(mm, u, vn, ws, bias):
    row = lax.broadcasted_iota(jnp.int32, (CHUNK, CHUNK), 0)
    col = lax.broadcasted_iota(jnp.int32, (CHUNK, CHUNK), 1)
    wm = [jnp.where(col <= row, w, 0.0) for w in ws]
    out_rows = []
    for ch in range(u.shape[0] // CHUNK):
        rs = slice(ch * CHUNK, (ch + 1) * CHUNK)
        out_rows.append(jnp.concatenate(
            [mm(wm[g], vn[rs, g * LANES:(g + 1) * LANES]) + bias[g] for g in range(GROUPS)], axis=1))
    return u * jnp.concatenate(out_rows, axis=0)


def _sgu_fwd(u, vn, ws, bias):
    n = u.shape[0]
    rows = CHUNK * SGU_CHUNKS_PER_STEP
    mm = _make_mm(1, 1)

    def body(u_ref, v_ref, w_ref, b_ref, z_ref):
        ws_l = [w_ref[g] for g in range(GROUPS)]
        bias_l = [b_ref[g] for g in range(GROUPS)]
        z_ref[...] = _sgu_tile(mm, u_ref[...], v_ref[...], ws_l, bias_l).astype(z_ref.dtype)

    tok = pl.BlockSpec((rows, D), lambda i: (i, 0))
    grp = pl.BlockSpec((GROUPS, CHUNK, LANES), lambda i: (0, 0, 0))
    return pl.pallas_call(
        body, name="sgu_fwd", out_shape=jax.ShapeDtypeStruct((n, D), BF16),
        grid=(n // rows,), in_specs=[tok, tok, grp, grp], out_specs=tok,
        compiler_params=_params(("parallel",)),
    )(u, vn, ws, bias)


def _sgu_bwd(u, vn, ws, bias, dz):
    n = u.shape[0]
    rows = CHUNK * SGU_CHUNKS_PER_STEP
    mm = _make_mm(1, 1)

    def body(u_ref, v_ref, w_ref, b_ref, dz_ref, du_ref, dv_ref, dw_ref, db_ref):
        i = pl.program_id(0)
        ws_l = [w_ref[g] for g in range(GROUPS)]
        bias_l = [b_ref[g] for g in range(GROUPS)]
        _, vjp = jax.vjp(functools.partial(_sgu_tile, mm), u_ref[...], v_ref[...], ws_l, bias_l)
        du, dv, dw, db = vjp(dz_ref[...].astype(F32))
        du_ref[...] = du
        dv_ref[...] = dv
        for g in range(GROUPS):
            @pl.when(i == 0)
            def _(g=g):
                dw_ref[g] = dw[g]
                db_ref[g] = db[g]

            @pl.when(i != 0)
            def _(g=g):
                dw_ref[g] += dw[g]
                db_ref[g] += db[g]

    tok = pl.BlockSpec((rows, D), lambda i: (i, 0))
    grp = pl.BlockSpec((GROUPS, CHUNK, LANES), lambda i: (0, 0, 0))
    return pl.pallas_call(
        body, name="sgu_bwd",
        out_shape=[jax.ShapeDtypeStruct((n, D), F32), jax.ShapeDtypeStruct((n, D), F32),
                   jax.ShapeDtypeStruct((GROUPS, CHUNK, LANES), F32),
                   jax.ShapeDtypeStruct((GROUPS, CHUNK, LANES), F32)],
        grid=(n // rows,), in_specs=[tok, tok, grp, grp, tok], out_specs=[tok, tok, grp, grp],
        compiler_params=_params(("arbitrary",)),
    )(u, vn, ws, bias, dz)


def _chains(t):
    return [t[i] for i in range(t.shape[0])] if t.ndim == 3 else [t]


def _bmm(mm, a, b):
    if a.ndim == 2 and b.ndim == 2:
        return mm(a, b)
    ca, cb = _chains(a), _chains(b)
    n = max(len(ca), len(cb))
    return jnp.stack([mm(ca[i % len(ca)], cb[i % len(cb)]) for i in range(n)])


def _bt(a):
    return a.T if a.ndim == 2 else jnp.stack([t.T for t in _chains(a)])


def _wkv_chunk(mms, s0, r, k, v, wl, al, g, w0, a0, k_k, k_a, r_k, ln_g, ln_b):
    ln = CHUNK
    row = lax.broadcasted_iota(jnp.int32, (ln, ln), 0)
    col = lax.broadcasted_iota(jnp.int32, (ln, ln), 1)
    incl = (col <= row).astype(F32)
    strict = (col < row).astype(F32)
    same_head = ((row // HEAD) == (col // HEAD)).astype(F32)
    lane = lax.broadcasted_iota(jnp.int32, (1, LANES), 1)
    m_a = (lane < HEAD).astype(F32)
    m_b = 1.0 - m_a
    rowid = lax.broadcasted_iota(jnp.int32, (ln, LANES), 0)
    cat = jnp.concatenate

    def hsum(t):
        return _bmm(mms.head_sum, t, same_head)

    def pick_row(t, j):
        return jnp.sum(jnp.where(rowid == j, t, 0.0), axis=-2, keepdims=True)

    z = w0 + wl
    softplus_neg = jnp.maximum(-z, 0.0) + jnp.log(1.0 + jnp.exp(-jnp.abs(z)))
    lw = -jnp.exp(-softplus_neg - 0.5)
    a = 1.0 / (1.0 + jnp.exp(-(a0 + al)))
    kx = k * k_k
    kkn = kx / jnp.maximum(jnp.sqrt(hsum(kx * kx)), L2_EPS)
    kp = k * (1.0 + (a - 1.0) * k_a)
    aa = -kkn
    bb = kkn * a

    c = _bmm(mms.cum, incl, lw)
    c_mid = pick_row(c, ln // 2 - 1)
    ce = c - c_mid
    e_pos = jnp.exp(ce)
    e_neg = jnp.exp(-ce)
    at = aa * jnp.exp(ce - lw)
    bt = bb * e_neg
    kt = kp * e_neg
    rt = r * e_pos
    s0p = s0 * jnp.exp(c_mid)

    bk = cat([bt, kt], axis=-2)
    sc = _bmm(mms.score, cat([at * m_a, at * m_b, rt * m_a, rt * m_b], axis=-2), _bt(bk))
    ab_a, ak_a = sc[..., 0:ln, 0:ln] * strict, sc[..., 0:ln, ln:] * strict
    ab_b, ak_b = sc[..., ln:2 * ln, 0:ln] * strict, sc[..., ln:2 * ln, ln:] * strict
    incl2 = cat([incl, incl], axis=1)
    p_a = sc[..., 2 * ln:3 * ln, :] * incl2
    p_b = sc[..., 3 * ln:, :] * incl2

    base = _bmm(mms.score, cat([at, rt], axis=-2), _bt(s0p))
    rhs = base[..., :ln, :] + m_a * _bmm(mms.out, ak_a, v) + m_b * _bmm(mms.out, ak_b, v)

    pa, pb = ab_a, ab_b
    xa = rhs + _bmm(mms.apply, pa, rhs)
    xb = rhs + _bmm(mms.apply, pb, rhs)
    for _ in range(6):
        pa = _bmm(mms.square, pa, pa)
        pb = _bmm(mms.square, pb, pb)
        xa = xa + _bmm(mms.apply, pa, xa)
        xb = xb + _bmm(mms.apply, pb, xb)
    u = m_a * xa + m_b * xb
    uv = cat([u, v], axis=-2)
    y = base[..., ln:, :] + m_a * _bmm(mms.out, p_a, uv) + m_b * _bmm(mms.out, p_b, uv)
    s_new = (s0p + _bmm(mms.state, _bt(uv), bk)) * same_head * jnp.exp(pick_row(ce, ln - 1))

    mean = hsum(y) * (1.0 / HEAD)
    yc = y - mean
    var = hsum(yc * yc) * (1.0 / HEAD)
    yn = yc * lax.rsqrt(var + GN_EPS) * ln_g + ln_b
    bonus = hsum(r * kp * r_k) * v
    return (yn + bonus) * g, s_new


N_WKV_ROWS = 6
N_WKV_PAR = 7


WKV_PAIRS_PER_STEP = 2


def _to_chains(val, nbat, pp):
    if val.ndim == 2:
        return jnp.stack([val[:, q * LANES:(q + 1) * LANES] for _ in range(nbat) for q in range(pp)])
    return jnp.stack([val[b, :, q * LANES:(q + 1) * LANES] for b in range(nbat) for q in range(pp)])


def _wkv_fwd(seq, rows, pars):
    n = rows[0].shape[0]
    nbat, nch, npair, pp = n // seq, seq // CHUNK, D // LANES, WKV_PAIRS_PER_STEP
    chunk_fn = functools.partial(_wkv_chunk, _wkv_mms(WKV_PRECISION))

    def body(*refs):
        row_vals = [_to_chains(r[...], nbat, pp) for r in refs[:N_WKV_ROWS]]
        par_vals = [_to_chains(r[...], nbat, pp) for r in refs[N_WKV_ROWS:N_WKV_ROWS + N_WKV_PAR]]
        yo_ref, ck_ref, s_ref = refs[N_WKV_ROWS + N_WKV_PAR:]
        ch = pl.program_id(1)

        @pl.when(ch == 0)
        def _():
            s_ref[...] = jnp.zeros_like(s_ref)

        s0 = s_ref[...]
        yo, s_new = chunk_fn(s0, *row_vals, *par_vals)
        s_ref[...] = s_new
        for b in range(nbat):
            for q in range(pp):
                ck_ref[b, q] = s0[b * pp + q]
                yo_ref[b, :, q * LANES:(q + 1) * LANES] = yo[b * pp + q].astype(yo_ref.dtype)

    tok = pl.BlockSpec((nbat, CHUNK, pp * LANES), lambda p, ch: (0, ch, p))
    par = pl.BlockSpec((1, pp * LANES), lambda p, ch: (0, p))
    ck = pl.BlockSpec((nbat, pp, None, LANES, LANES), lambda p, ch: (0, p, ch, 0, 0))
    yo, ckpt = pl.pallas_call(
        body, name="wkv_fwd",
        out_shape=[jax.ShapeDtypeStruct((nbat, seq, D), BF16),
                   jax.ShapeDtypeStruct((nbat, npair, nch, LANES, LANES), F32)],
        grid=(npair // pp, nch),
        in_specs=[tok] * N_WKV_ROWS + [par] * N_WKV_PAR, out_specs=[tok, ck],
        scratch_shapes=[pltpu.VMEM((nbat * pp, LANES, LANES), F32)],
        compiler_params=_params(("parallel", "arbitrary")),
    )(*[t.reshape(nbat, seq, D) for t in rows], *pars)
    return yo.reshape(n, D), ckpt


def _wkv_bwd(seq, rows, pars, ckpt, dyo):
    n = rows[0].shape[0]
    nbat, nch, npair, pp = n // seq, seq // CHUNK, D // LANES, WKV_PAIRS_PER_STEP
    chunk_fn = functools.partial(_wkv_chunk, _wkv_mms(WKV_PRECISION))
    n_in = N_WKV_ROWS + N_WKV_PAR

    def body(*refs):
        row_vals = [_to_chains(r[...], nbat, pp) for r in refs[:N_WKV_ROWS]]
        par_vals = [_to_chains(r[...], nbat, pp) for r in refs[N_WKV_ROWS:n_in]]
        ck_ref, dyo_ref = refs[n_in:n_in + 2]
        d_rows = refs[n_in + 2:n_in + 2 + N_WKV_ROWS]
        d_pars = refs[n_in + 2 + N_WKV_ROWS:n_in + 2 + N_WKV_ROWS + N_WKV_PAR]
        ds_ref = refs[-1]
        ch = pl.program_id(1)

        @pl.when(ch == 0)
        def _():
            ds_ref[...] = jnp.zeros_like(ds_ref)

        s0 = jnp.stack([ck_ref[b, q] for b in range(nbat) for q in range(pp)])
        dyo_v = _to_chains(dyo_ref[...].astype(F32), nbat, pp)
        _, vjp = jax.vjp(chunk_fn, s0, *row_vals, *par_vals)
        grads = vjp((dyo_v, ds_ref[...]))
        ds_ref[...] = grads[0]
        for ref, val in zip(d_rows, grads[1:1 + N_WKV_ROWS]):
            for b in range(nbat):
                for q in range(pp):
                    ref[b, :, q * LANES:(q + 1) * LANES] = val[b * pp + q].astype(ref.dtype)
        for ref, val in zip(d_pars, grads[1 + N_WKV_ROWS:]):
            per_pair = [functools.reduce(lambda s, t: s + t, [val[b * pp + q] for b in range(nbat)])
                        for q in range(pp)]
            tot = jnp.concatenate(per_pair, axis=1)

            @pl.when(ch == 0)
            def _(ref=ref, tot=tot):
                ref[...] = tot

            @pl.when(ch != 0)
            def _(ref=ref, tot=tot):
                ref[...] += tot

    tok = pl.BlockSpec((nbat, CHUNK, pp * LANES), lambda p, ch: (0, nch - 1 - ch, p))
    par = pl.BlockSpec((1, pp * LANES), lambda p, ch: (0, p))
    ck = pl.BlockSpec((nbat, pp, None, LANES, LANES), lambda p, ch: (0, p, nch - 1 - ch, 0, 0))
    res = pl.pallas_call(
        body, name="wkv_bwd",
        out_shape=[jax.ShapeDtypeStruct((nbat, seq, D), BF16)] * N_WKV_ROWS
        + [jax.ShapeDtypeStruct((1, D), F32)] * N_WKV_PAR,
        grid=(npair // pp, nch),
        in_specs=[tok] * N_WKV_ROWS + [par] * N_WKV_PAR + [ck, tok],
        out_specs=[tok] * N_WKV_ROWS + [par] * N_WKV_PAR,
        scratch_shapes=[pltpu.VMEM((nbat * pp, LANES, LANES), F32)],
        compiler_params=_params(("parallel", "arbitrary")),
    )(*[t.reshape(nbat, seq, D) for t in rows], *pars, ckpt, dyo.reshape(nbat, seq, D))
    return [t.reshape(n, D) for t in res[:N_WKV_ROWS]] + list(res[N_WKV_ROWS:])


def _place():
    return lax.axis_index("x"), lax.axis_index("y"), lax.axis_index("c")


def _all_gather8(blk):
    m_per, n = blk.shape
    assert m_per % 8 == 0

    def body(x_ref, out_ref, send_sems, recv_sems, local_sem):
        x, y, c = _place()
        me, sibling = (x, y, c), (x, y, 1 - c)
        chips = [(1 - x, y), (x, 1 - y), (1 - x, 1 - y)]

        def rows(px, py, pc):
            return out_ref.at[pl.ds((4 * px + 2 * py + pc) * m_per, m_per), :]

        def copy(k, block, to, src=None):
            return pltpu.make_async_remote_copy(
                src_ref=rows(*block) if src is None else src, dst_ref=rows(*block),
                send_sem=send_sems.at[k], recv_sem=recv_sems.at[k],
                device_id=to, device_id_type=MESH)

        mine = pltpu.make_async_copy(x_ref, rows(*me), local_sem)
        mine.start()
        first = [copy(0, me, sibling, src=x_ref)]
        first += [copy(1 + j, me, (*chip, c), src=x_ref) for j, chip in enumerate(chips)]
        for cp in first:
            cp.start()
        passed = [copy(4 + j, (*chip, c), sibling) for j, chip in enumerate(chips)]
        for j, chip in enumerate(chips):
            copy(1 + j, (*chip, c), me).wait_recv()
            passed[j].start()
        copy(0, sibling, me).wait_recv()
        for j, chip in enumerate(chips):
            copy(4 + j, (*chip, 1 - c), me).wait_recv()
        for cp in first + passed:
            cp.wait_send()
        mine.wait()

    vm = pl.BlockSpec(memory_space=pltpu.VMEM)
    return pl.pallas_call(
        body, name="all_gather8_%dx%d" % (m_per, n),
        out_shape=jax.ShapeDtypeStruct((N_DEV * m_per, n), blk.dtype),
        in_specs=[vm], out_specs=vm,
        scratch_shapes=[pltpu.SemaphoreType.DMA((7,)), pltpu.SemaphoreType.DMA((7,)),
                        pltpu.SemaphoreType.DMA],
        compiler_params=_params(),
    )(blk)


def _own_slot(src, name):
    r, w = src.shape[-2:]
    tr = _tile(r, 1008)
    xi, yi, _ = _place()
    chip = jnp.reshape(2 * xi + yi, (1,)).astype(jnp.int32)

    def body(chip_ref, x_ref, o_ref):
        o_ref[...] = x_ref[...]

    if src.ndim == 2:
        in_spec = pl.BlockSpec((tr, w), lambda i, chip_ref: (i, 0))
    else:
        in_spec = pl.BlockSpec((None, tr, w), lambda i, chip_ref: (chip_ref[0], i, 0))
    return pl.pallas_call(
        body, name=name,
        out_shape=jax.ShapeDtypeStruct((N_CHIPS, r, w), src.dtype),
        grid_spec=pltpu.PrefetchScalarGridSpec(
            num_scalar_prefetch=1, grid=(r // tr,), in_specs=[in_spec],
            out_specs=pl.BlockSpec((None, tr, w), lambda i, chip_ref: (chip_ref[0], i, 0))),
        compiler_params=_params(("parallel",)),
    )(chip, src)


def _chip_all_gather(shard):
    r, w = shard.shape
    half = r // 2
    assert r % 2 == 0 and half % 16 == 0

    def body(x_ref, buf_ref, out_ref, send_sems, recv_sems):
        del buf_ref
        x, y, c = _place()
        sibling = (x, y, 1 - c)
        me_p = 2 * x + y
        chips = [(1 - x, y), (x, 1 - y), (1 - x, 1 - y)]

        def piece(p, h):
            return out_ref.at[p, pl.ds(h * half, half), :]

        def copy(k, p, h, to, src=None):
            return pltpu.make_async_remote_copy(
                src_ref=piece(p, h) if src is None else src, dst_ref=piece(p, h),
                send_sem=send_sems.at[k], recv_sem=recv_sems.at[k],
                device_id=to, device_id_type=MESH)

        my_half = x_ref.at[pl.ds(c * half, half), :]
        first = [copy(j, me_p, c, (*chip, c), src=my_half) for j, chip in enumerate(chips)]
        for cp in first:
            cp.start()
        passed = [copy(3 + j, 2 * chip[0] + chip[1], c, sibling) for j, chip in enumerate(chips)]
        for j, chip in enumerate(chips):
            copy(j, 2 * chip[0] + chip[1], c, sibling).wait_recv()
            passed[j].start()
        for j, chip in enumerate(chips):
            copy(3 + j, 2 * chip[0] + chip[1], 1 - c, sibling).wait_recv()
        for cp in first + passed:
            cp.wait_send()

    hbm = pl.BlockSpec(memory_space=pl.ANY)
    return pl.pallas_call(
        body, name="chip_all_gather",
        out_shape=jax.ShapeDtypeStruct((N_CHIPS, r, w), shard.dtype),
        in_specs=[hbm, hbm], out_specs=hbm, input_output_aliases={1: 0},
        scratch_shapes=[pltpu.SemaphoreType.DMA((6,)), pltpu.SemaphoreType.DMA((6,))],
        compiler_params=_params(),
    )(shard, _own_slot(shard, "gather_own_slot"))


def _sibling_swap_halves(g):
    _, r, w = g.shape
    half = r // 2

    def body(g_ref, t_ref, send_sem, recv_sem):
        x, y, c = _place()
        sibling = (x, y, 1 - c)
        cp = pltpu.make_async_remote_copy(
            src_ref=g_ref.at[:, pl.ds((1 - c) * half, half), :], dst_ref=t_ref,
            send_sem=send_sem, recv_sem=recv_sem, device_id=sibling, device_id_type=MESH)
        cp.start()
        cp.wait()

    hbm = pl.BlockSpec(memory_space=pl.ANY)
    return pl.pallas_call(
        body, name="rs_sibling_halves",
        out_shape=jax.ShapeDtypeStruct((N_CHIPS, half, w), g.dtype),
        in_specs=[hbm], out_specs=hbm,
        scratch_shapes=[pltpu.SemaphoreType.DMA, pltpu.SemaphoreType.DMA],
        compiler_params=_params(),
    )(g)


def _add_own_half(g, t):
    _, r, w = g.shape
    half = r // 2
    tr = 1008 if half % 1008 == 0 else 16
    assert half % tr == 0
    cidx = jnp.reshape(lax.axis_index("c"), (1,)).astype(jnp.int32)

    def body(c_ref, g_ref, t_ref, o_ref):
        o_ref[...] = (g_ref[...] + t_ref[...]).astype(o_ref.dtype)

    return pl.pallas_call(
        body, name="rs_add_halves",
        out_shape=jax.ShapeDtypeStruct((N_CHIPS, half, w), BF16),
        grid_spec=pltpu.PrefetchScalarGridSpec(
            num_scalar_prefetch=1, grid=(N_CHIPS, half // tr),
            in_specs=[pl.BlockSpec((None, None, tr, w), lambda p, i, c_ref: (p, c_ref[0], i, 0)),
                      pl.BlockSpec((None, tr, w), lambda p, i, c_ref: (p, i, 0))],
            out_specs=pl.BlockSpec((None, tr, w), lambda p, i, c_ref: (p, i, 0))),
        compiler_params=_params(("parallel", "parallel")),
    )(cidx, g.reshape(N_CHIPS, 2, half, w), t)


def _chip_exchange(h):
    _, hh, w = h.shape

    def body(h_ref, buf_ref, t_ref, send_sems, recv_sems):
        del buf_ref
        x, y, c = _place()
        me_p = 2 * x + y
        chips = [(1 - x, y), (x, 1 - y), (1 - x, 1 - y)]
        cps = []
        for j, chip in enumerate(chips):
            q = 2 * chip[0] + chip[1]
            cps.append(pltpu.make_async_remote_copy(
                src_ref=h_ref.at[q], dst_ref=t_ref.at[me_p],
                send_sem=send_sems.at[j], recv_sem=recv_sems.at[j],
                device_id=(*chip, c), device_id_type=MESH))
        for cp in cps:
            cp.start()
        for j, chip in enumerate(chips):
            q = 2 * chip[0] + chip[1]
            pltpu.make_async_remote_copy(
                src_ref=h_ref.at[q], dst_ref=t_ref.at[q],
                send_sem=send_sems.at[j], recv_sem=recv_sems.at[j],
                device_id=(*chip, c), device_id_type=MESH).wait_recv()
        for cp in cps:
            cp.wait_send()

    hbm = pl.BlockSpec(memory_space=pl.ANY)
    return pl.pallas_call(
        body, name="rs_chip_exchange",
        out_shape=jax.ShapeDtypeStruct(h.shape, h.dtype),
        in_specs=[hbm, hbm], out_specs=hbm, input_output_aliases={1: 0},
        scratch_shapes=[pltpu.SemaphoreType.DMA((3,)), pltpu.SemaphoreType.DMA((3,))],
        compiler_params=_params(),
    )(h, _own_slot(h, "rs_own_slot"))


def _sum_slots(t):
    _, hh, w = t.shape
    tr = 1008 if hh % 1008 == 0 else 16
    assert hh % tr == 0
    nblk = hh // tr
    cidx = jnp.reshape(lax.axis_index("c"), (1,)).astype(jnp.int32)

    def body(c_ref, t_ref, o_ref):
        s0, s1, s2, s3 = [t_ref[j].astype(F32) for j in range(N_CHIPS)]
        o_ref[...] = ((s0 + s1) + s2) + s3

    return pl.pallas_call(
        body, name="rs_sum_slots", out_shape=jax.ShapeDtypeStruct((2 * hh, w), F32),
        grid_spec=pltpu.PrefetchScalarGridSpec(
            num_scalar_prefetch=1, grid=(nblk,),
            in_specs=[pl.BlockSpec((N_CHIPS, tr, w), lambda i, c_ref: (0, i, 0))],
            out_specs=pl.BlockSpec((tr, w), lambda i, c_ref: (c_ref[0] * nblk + i, 0))),
        compiler_params=_params(("parallel",)),
    )(cidx, t)


def _sibling_join_halves(s):
    h2, w = s.shape
    hh = h2 // 2

    def body(s_ref, o_ref, send_sem, recv_sem):
        del s_ref
        x, y, c = _place()
        sibling = (x, y, 1 - c)
        cp = pltpu.make_async_remote_copy(
            src_ref=o_ref.at[pl.ds(c * hh, hh), :], dst_ref=o_ref.at[pl.ds(c * hh, hh), :],
            send_sem=send_sem, recv_sem=recv_sem, device_id=sibling, device_id_type=MESH)
        cp.start()
        pltpu.make_async_remote_copy(
            src_ref=o_ref.at[pl.ds((1 - c) * hh, hh), :], dst_ref=o_ref.at[pl.ds((1 - c) * hh, hh), :],
            send_sem=send_sem, recv_sem=recv_sem, device_id=sibling, device_id_type=MESH).wait_recv()
        cp.wait_send()

    hbm = pl.BlockSpec(memory_space=pl.ANY)
    return pl.pallas_call(
        body, name="rs_sibling_join",
        out_shape=jax.ShapeDtypeStruct(s.shape, s.dtype),
        in_specs=[hbm], out_specs=hbm, input_output_aliases={0: 0},
        scratch_shapes=[pltpu.SemaphoreType.DMA, pltpu.SemaphoreType.DMA],
        compiler_params=_params(),
    )(s)


def _reduce_scatter(g):
    h = _add_own_half(g, _sibling_swap_halves(g))
    return _sibling_join_halves(_sum_slots(_chip_exchange(h)))


def _unshard_cols(piece):
    p, k, n = piece.shape
    return jnp.transpose(piece, (1, 0, 2)).reshape(k, p * n)


def _shard_cols(full):
    k, n4 = full.shape
    return jnp.transpose(full.reshape(k, N_CHIPS, n4 // N_CHIPS), (1, 0, 2))


def _rows_of(piece):
    return piece.reshape(N_CHIPS, -1, ROW_W)


def _pad_rows(a, mult):
    pad = (-a.shape[-2]) % mult
    if pad == 0:
        return a
    widths = [(0, 0)] * (a.ndim - 2) + [(0, pad), (0, 0)]
    return jnp.pad(a, widths)


def _adamw(w, g, m, v):
    m2 = ADAM_B1 * m + (1.0 - ADAM_B1) * g
    v2 = ADAM_B2 * v + (1.0 - ADAM_B2) * (g * g)
    m_hat = m2 / (1.0 - ADAM_B1 ** ADAM_STEP)
    v_hat = v2 / (1.0 - ADAM_B2 ** ADAM_STEP)
    delta = -ADAM_LR * (m_hat / (jnp.sqrt(v_hat) + ADAM_EPS) + ADAM_WD * w)
    return delta, m2, v2


def _mlp_fwd(seq, tm, x_in, mix, mod, w1, w2, tag):
    x1, h2 = _rw_fwd(_f_res_norm2, "res_norm2_" + tag, seq=seq, tm=tm, rows=[x_in, mix], bvecs=[mod],
                     outs=[(D, F32), (D, BF16)])
    p, f = _matmul_ep(h2, w1, [], lambda acc: (acc, _f_relu2(acc)[0]), [F32, BF16],
                      mode="nn", name="mlp_up_" + tag)
    ff = _matmul(f, w2, mode="nn", name="mlp_down_" + tag)
    (x2,) = _rw_fwd(_f_res2, "res2_" + tag, seq=seq, tm=tm, rows=[x1, ff], bvecs=[mod], outs=[(D, F32)])
    return x2, (x1, h2, p, f, ff)


def _mlp_bwd(seq, tm, saved, x_in, mix, mod, w1, w2, dx2, tag):
    x1, h2, p, f, ff = saved
    dff, dmod_a = _rw_bwd(_f_res2, "res2_bwd_" + tag, seq=seq, tm=tm, rows=[x1, ff], bvecs=[mod],
                          cts=[dx2], need_rows=[1], dtypes=[BF16])
    (dp,) = _matmul_ep(dff, w2, [p], lambda acc, pt: (2.0 * jnp.maximum(pt, 0.0) * acc,), [BF16],
                       mode="nt", name="mlp_down_dx_" + tag)
    dw2 = _matmul(f, dff, mode="tn", name="mlp_down_dw_" + tag)
    dh2 = _matmul(dp, w1, mode="nt", name="mlp_up_dx_" + tag)
    dw1 = _matmul(h2, dp, mode="tn", name="mlp_up_dw_" + tag, out_shards=N_CHIPS)
    dx_in, dmix, dmod_b = _rw_bwd(_f_res_norm2, "res_norm2_bwd_" + tag, seq=seq, tm=tm, rows=[x_in, mix],
                                  bvecs=[mod], cts=[dx2, dh2], need_rows=[0, 1], dtypes=[F32, BF16])
    return dx_in, dmix, dmod_a + dmod_b, dw1, dw2


def kernel(x, c, ada_w, ada_b, mlp_w1, mlp_w2, a_w_in, a_ln_g, a_ln_b, a_w_s, a_b_s, a_w_out, b_mu, b_w_in, b_w0, b_w1, b_w2, b_a0, b_a1, b_a2, b_g1, b_g2, b_k_k, b_k_a, b_r_k, b_ln_g, b_ln_b, b_w_out, final_g, loss_target, m_ada_w, m_ada_b, m_mlp_w1, m_mlp_w2, m_a_w_in, m_a_ln_g, m_a_ln_b, m_a_w_s, m_a_b_s, m_a_w_out, m_b_mu, m_b_w_in, m_b_w0, m_b_w1, m_b_w2, m_b_a0, m_b_a1, m_b_a2, m_b_g1, m_b_g2, m_b_k_k, m_b_k_a, m_b_r_k, m_b_ln_g, m_b_ln_b, m_b_w_out, m_final_g, v_ada_w, v_ada_b, v_mlp_w1, v_mlp_w2, v_a_w_in, v_a_ln_g, v_a_ln_b, v_a_w_s, v_a_b_s, v_a_w_out, v_b_mu, v_b_w_in, v_b_w0, v_b_w1, v_b_w2, v_b_a0, v_b_a1, v_b_a2, v_b_g1, v_b_g2, v_b_k_k, v_b_k_a, v_b_r_k, v_b_ln_g, v_b_ln_b, v_b_w_out, v_final_g):
    weights = dict(ada_w=ada_w, ada_b=ada_b, mlp_w1=mlp_w1, mlp_w2=mlp_w2, a_w_in=a_w_in, a_ln_g=a_ln_g,
                   a_ln_b=a_ln_b, a_w_s=a_w_s, a_b_s=a_b_s, a_w_out=a_w_out, b_mu=b_mu, b_w_in=b_w_in,
                   b_w0=b_w0, b_w1=b_w1, b_w2=b_w2, b_a0=b_a0, b_a1=b_a1, b_a2=b_a2, b_g1=b_g1, b_g2=b_g2,
                   b_k_k=b_k_k, b_k_a=b_k_a, b_r_k=b_r_k, b_ln_g=b_ln_g, b_ln_b=b_ln_b, b_w_out=b_w_out,
                   final_g=final_g)
    moms = dict(ada_w=(m_ada_w, v_ada_w), ada_b=(m_ada_b, v_ada_b), mlp_w1=(m_mlp_w1, v_mlp_w1),
                mlp_w2=(m_mlp_w2, v_mlp_w2), a_w_in=(m_a_w_in, v_a_w_in), a_ln_g=(m_a_ln_g, v_a_ln_g),
                a_ln_b=(m_a_ln_b, v_a_ln_b), a_w_s=(m_a_w_s, v_a_w_s), a_b_s=(m_a_b_s, v_a_b_s),
                a_w_out=(m_a_w_out, v_a_w_out), b_mu=(m_b_mu, v_b_mu), b_w_in=(m_b_w_in, v_b_w_in),
                b_w0=(m_b_w0, v_b_w0), b_w1=(m_b_w1, v_b_w1), b_w2=(m_b_w2, v_b_w2), b_a0=(m_b_a0, v_b_a0),
                b_a1=(m_b_a1, v_b_a1), b_a2=(m_b_a2, v_b_a2), b_g1=(m_b_g1, v_b_g1), b_g2=(m_b_g2, v_b_g2),
                b_k_k=(m_b_k_k, v_b_k_k), b_k_a=(m_b_k_a, v_b_k_a), b_r_k=(m_b_r_k, v_b_r_k),
                b_ln_g=(m_b_ln_g, v_b_ln_g), b_ln_b=(m_b_ln_b, v_b_ln_b), b_w_out=(m_b_w_out, v_b_w_out),
                final_g=(m_final_g, v_final_g))
    order = list(weights)

    nbat, seq, _ = x.shape
    n = nbat * seq
    tm = 256
    xi, yi, ci = _place()
    chip = 2 * xi + yi
    dev = 2 * chip + ci
    x0 = x.reshape(n, D)
    tgt = loss_target.reshape(n, D)
    lora_w, lora_g = b_w1.shape[-1], b_g1.shape[-1]
    lora_wp, lora_gp = LANES, 2 * LANES

    (cond,) = _small(lambda cc: (cc / (1.0 + jnp.exp(-cc)),), "silu_c", [c], [c.shape])
    vec_names = ["b_w0", "b_a0", "b_k_k", "b_k_a", "b_ln_g", "b_ln_b"]
    vec_shard = jnp.concatenate([b_mu[0]] + [weights[k] for k in vec_names], axis=0)
    n_vec = vec_shard.shape[0]
    vec_rows = vec_shard.reshape(-1, ROW_W)
    blk = _pad_rows(jnp.concatenate([cond, vec_rows], axis=0), 8)
    assert blk.shape[0] == 8
    gathered = _all_gather8(blk).reshape(N_DEV, 8, D)
    cond_all = gathered[:, :nbat].reshape(N_DEV * nbat, D)
    vec_all = gathered[0::2, nbat:nbat + vec_rows.shape[0]].reshape(N_CHIPS, n_vec, D // N_CHIPS)
    vec_full = jnp.transpose(vec_all, (1, 0, 2)).reshape(n_vec, D)
    mu_full = vec_full[0:6]
    w0_f, a0_f, kk_f, ka_f, lng_f, lnb_f = [vec_full[6 + j:7 + j] for j in range(6)]
    rk_f = b_r_k.reshape(1, D)

    n_ada = ada_w.shape[-1]
    parts = jnp.concatenate(
        [_matmul(cond_all, ada_w[i], mode="nn", name="ada_fwd_%d" % i) for i in range(2)], axis=1)
    parts_all = _all_gather8(parts).reshape(N_DEV, N_DEV * nbat, 2, n_ada)[0::2]
    mine = lax.dynamic_slice_in_dim(parts_all, dev * nbat, nbat, axis=1)
    mods = []
    for i in range(2):
        full = jnp.transpose(mine[:, :, i], (1, 0, 2)).reshape(nbat, N_MOD * D) + ada_b[i]
        mods.append(full.reshape(nbat, N_MOD, D))

    big = [("mlp_w1_0", mlp_w1[0]), ("mlp_w1_1", mlp_w1[1]), ("mlp_w2_0", mlp_w2[0]), ("mlp_w2_1", mlp_w2[1]),
           ("a_w_in", a_w_in[0]), ("a_w_out", a_w_out[0]), ("b_w_in", b_w_in[0]), ("b_w_out", b_w_out[0]),
           ("b_w1", b_w1[0]), ("b_w2", b_w2[0]), ("b_a1", b_a1[0]), ("b_a2", b_a2[0]),
           ("b_g1", b_g1[0]), ("b_g2", b_g2[0])]
    offs, pos = {}, 0
    for name, arr in big:
        rows_k = arr.size // ROW_W
        offs[name] = (pos, rows_k, arr.shape)
        pos += rows_k
    n_big_rows = pos
    wflat = _pad_rows(jnp.concatenate([arr.astype(BF16).reshape(-1, ROW_W) for _, arr in big], axis=0), 32)
    wg = _chip_all_gather(wflat)

    def gathered_piece(name):
        start, rows_k, shape = offs[name]
        return wg[:, start:start + rows_k].reshape((N_CHIPS,) + shape)

    def col_w(name):
        return _unshard_cols(gathered_piece(name))

    def row_w(name):
        piece = gathered_piece(name)
        return piece.reshape(N_CHIPS * piece.shape[1], piece.shape[2])

    w1_l = [col_w("mlp_w1_0"), col_w("mlp_w1_1")]
    w2_l = [row_w("mlp_w2_0"), row_w("mlp_w2_1")]
    a_win, a_wout = col_w("a_w_in"), row_w("a_w_out")
    b_win, b_wout = col_w("b_w_in"), row_w("b_w_out")
    w_r, w_k, w_v = b_win[:, :D], b_win[:, D:2 * D], b_win[:, 2 * D:]
    w1p = jnp.pad(row_w("b_w1"), ((0, 0), (0, lora_wp - lora_w)))
    a1p = jnp.pad(row_w("b_a1"), ((0, 0), (0, lora_wp - lora_w)))
    g1p = jnp.pad(row_w("b_g1"), ((0, 0), (0, lora_gp - lora_g)))
    w2p = jnp.pad(col_w("b_w2"), ((0, lora_wp - lora_w), (0, 0)))
    a2p = jnp.pad(col_w("b_a2"), ((0, lora_wp - lora_w), (0, 0)))
    g2p = jnp.pad(col_w("b_g2"), ((0, lora_gp - lora_g), (0, 0)))

    mod0, mod1 = mods
    (h_a,) = _rw_fwd(_f_norm1, "norm1_a", seq=seq, tm=tm, rows=[x0], bvecs=[mod0], outs=[(D, BF16)])
    uvp = _matmul(h_a, a_win, mode="nn", name="sgu_in")
    u, vn = _rw_fwd(_f_sgu_pre, "sgu_pre", seq=seq, tm=tm, rows=[uvp], params=[a_ln_g, a_ln_b],
                    outs=[(D, F32), (D, F32)])
    ws = a_w_s[0]
    bias = jnp.broadcast_to(a_b_s[0][:, :, None], (GROUPS, CHUNK, LANES))
    z = _sgu_fwd(u, vn, ws, bias)
    mix0 = _matmul(z, a_wout, mode="nn", name="sgu_out")
    x2, saved0 = _mlp_fwd(seq, tm, x0, mix0, mod0, w1_l[0], w2_l[0], "0")

    def shift_fwd(ctx, rv, pv, nv, bv, pa):
        _, _, mixes = _shift_mix(ctx, rv[0], pv[0], bv[0], pa[0])
        return mixes, [], []

    xr, xw, xk, xv, xa, xg = _rowwise(shift_fwd, "shift_mix", seq=seq, tm=tm, rows=[x2], prev8=[x2],
                                      bvecs=[mod1], params=[mu_full], out_rows=[(D, BF16)] * 6)
    r = _matmul(xr, w_r, mode="nn", name="rwkv_r")
    k = _matmul(xk, w_k, mode="nn", name="rwkv_k")
    v = _matmul(xv, w_v, mode="nn", name="rwkv_v")
    t1 = _matmul(xw, w1p, mode="nn", name="lora_w1")
    t2 = _matmul(xa, a1p, mode="nn", name="lora_a1", out_dtype=BF16)
    t3 = _matmul(xg, g1p, mode="nn", name="lora_g1")
    (th,) = _rw_fwd(lambda t: (jnp.tanh(t),), "lora_tanh", seq=seq, tm=tm, rows=[t1], outs=[(lora_wp, BF16)])
    (sg,) = _rw_fwd(lambda t: (1.0 / (1.0 + jnp.exp(-t)),), "lora_sigmoid", seq=seq, tm=tm, rows=[t3],
                    outs=[(lora_gp, BF16)])
    wl = _matmul(th, w2p, mode="nn", name="lora_w2")
    al = _matmul(t2, a2p, mode="nn", name="lora_a2")
    g = _matmul(sg, g2p, mode="nn", name="lora_g2")
    wkv_rows = [r, k, v, wl, al, g]
    wkv_pars = [w0_f, a0_f, kk_f, ka_f, rk_f, lng_f, lnb_f]
    yo, ckpt = _wkv_fwd(seq, wkv_rows, wkv_pars)
    mix1 = _matmul(yo, b_wout, mode="nn", name="rwkv_out")
    x4, saved1 = _mlp_fwd(seq, tm, x2, mix1, mod1, w1_l[1], w2_l[1], "1")

    def loss_fn(ctx, rv, pv, nv, bv, pa):
        val, (dx, dfg) = jax.value_and_grad(_f_loss, argnums=(0, 2))(rv[0], rv[1], pa[0])
        return [dx], [], [dfg, jnp.full((1, LANES), val, F32)]

    dx4, d_final_g, loss_acc = _rowwise(loss_fn, "loss_head", seq=seq, tm=tm, rows=[x4, tgt],
                                        params=[final_g.reshape(1, D)], out_rows=[(D, F32)],
                                        out_pacc=[(1, D), (1, LANES)])
    loss = lax.psum(loss_acc[0, 0], AXES)

    dx2_a, dmix1, dmod1, dw1_1, dw2_1 = _mlp_bwd(seq, tm, saved1, x2, mix1, mod1, w1_l[1], w2_l[1], dx4, "1")
    dyo = _matmul(dmix1, b_wout, mode="nt", name="rwkv_out_dx")
    d_b_wout = _matmul(yo, dmix1, mode="tn", name="rwkv_out_dw")
    wkv_grads = _wkv_bwd(seq, wkv_rows, wkv_pars, ckpt, dyo)
    dr, dk, dv, dwl, dal, dg = wkv_grads[:N_WKV_ROWS]
    d_w0, d_a0, d_kk, d_ka, d_rk, d_lng, d_lnb = wkv_grads[N_WKV_ROWS:]
    dth = _matmul(dwl, w2p, mode="nt", name="lora_w2_dx")
    d_w2p = _matmul(th, dwl, mode="tn", name="lora_w2_dw")
    dt2 = _matmul(dal, a2p, mode="nt", name="lora_a2_dx", out_dtype=BF16)
    d_a2p = _matmul(t2, dal, mode="tn", name="lora_a2_dw")
    dsg = _matmul(dg, g2p, mode="nt", name="lora_g2_dx")
    d_g2p = _matmul(sg, dg, mode="tn", name="lora_g2_dw")
    (dt1,) = _rw_bwd(lambda t: (jnp.tanh(t),), "lora_tanh_bwd", seq=seq, tm=tm, rows=[t1], cts=[dth],
                     need_rows=[0], dtypes=[BF16])
    (dt3,) = _rw_bwd(lambda t: (1.0 / (1.0 + jnp.exp(-t)),), "lora_sigmoid_bwd", seq=seq, tm=tm, rows=[t3],
                     cts=[dsg], need_rows=[0], dtypes=[BF16])
    dxw = _matmul(dt1, w1p, mode="nt", name="lora_w1_dx")
    d_w1p = _matmul(xw, dt1, mode="tn", name="lora_w1_dw")
    dxa = _matmul(dt2, a1p, mode="nt", name="lora_a1_dx")
    d_a1p = _matmul(xa, dt2, mode="tn", name="lora_a1_dw")
    dxg = _matmul(dt3, g1p, mode="nt", name="lora_g1_dx")
    d_g1p = _matmul(xg, dt3, mode="tn", name="lora_g1_dw")
    dxr = _matmul(dr, w_r, mode="nt", name="rwkv_r_dx")
    dxk = _matmul(dk, w_k, mode="nt", name="rwkv_k_dx")
    dxv = _matmul(dv, w_v, mode="nt", name="rwkv_v_dx")
    d_b_win = jnp.concatenate([_matmul(xr, dr, mode="tn", name="rwkv_r_dw"),
                               _matmul(xk, dk, mode="tn", name="rwkv_k_dw"),
                               _matmul(xv, dv, mode="tn", name="rwkv_v_dw")], axis=1)

    def shift_bwd(ctx, rv, pv, nv, bv, pa):
        xt, dres = rv[0], rv[1]
        dmix_in = rv[2:8]
        mod, mu = bv[0], pa[0]
        f_h = lambda xx_, mod_: _rmsmod(xx_, mod_[0:1], mod_[1:2])
        h, vjp = jax.vjp(f_h, xt, mod)
        hprev = f_h(pv[0], mod)[7:8]
        hprev = jnp.where(ctx.first, jnp.zeros_like(hprev), hprev)
        rowid = lax.broadcasted_iota(jnp.int32, h.shape, 0)
        xx = jnp.where(rowid == 0, hprev, pltpu.roll(h, 1, 0)) - h
        tot = dmix_in[0]
        wsum = dmix_in[0] * mu[0:1]
        for j in range(1, 6):
            tot = tot + dmix_in[j]
            wsum = wsum + dmix_in[j] * mu[j:j + 1]
        nxt = nv[0][0:1] * mu[0:1]
        for j in range(1, 6):
            nxt = nxt + nv[j][0:1] * mu[j:j + 1]
        nxt = jnp.where(ctx.last, jnp.zeros_like(nxt), nxt)
        tmr = h.shape[0]
        wshift = jnp.where(rowid == tmr - 1, nxt, pltpu.roll(wsum, tmr - 1, 0))
        dh = tot - wsum + wshift
        dx_, dmod_ = vjp(dh)
        dmu = jnp.concatenate([jnp.sum(dmix_in[j] * xx, axis=0, keepdims=True) for j in range(6)], axis=0)
        return [dx_ + dres], [dmod_], [dmu]

    dmix_list = [dxr, dxw, dxk, dxv, dxa, dxg]
    dx2, dmod1_c, d_mu = _rowwise(shift_bwd, "shift_mix_bwd", seq=seq, tm=tm, rows=[x2, dx2_a] + dmix_list,
                                  prev8=[x2], next8=dmix_list, bvecs=[mod1], params=[mu_full],
                                  out_rows=[(D, F32)], out_bacc=[(N_MOD, D)], out_pacc=[(6, D)])
    dmod1 = dmod1 + dmod1_c

    dx0_a, dmix0, dmod0, dw1_0, dw2_0 = _mlp_bwd(seq, tm, saved0, x0, mix0, mod0, w1_l[0], w2_l[0], dx2, "0")
    dz = _matmul(dmix0, a_wout, mode="nt", name="sgu_out_dx")
    d_a_wout = _matmul(z, dmix0, mode="tn", name="sgu_out_dw")
    du, dvn, d_ws, d_bias = _sgu_bwd(u, vn, ws, bias, dz)
    duvp, d_a_lng, d_a_lnb = _rw_bwd(_f_sgu_pre, "sgu_pre_bwd", seq=seq, tm=tm, rows=[uvp],
                                     params=[a_ln_g, a_ln_b], cts=[du, dvn], need_rows=[0], dtypes=[BF16])
    dh_a = _matmul(duvp, a_win, mode="nt", name="sgu_in_dx")
    d_a_win = _matmul(h_a, duvp, mode="tn", name="sgu_in_dw", out_shards=N_CHIPS)
    grad_x, dmod0_c = _rw_bwd(_f_norm1, "norm1_a_bwd", seq=seq, tm=tm, rows=[x0], bvecs=[mod0], cts=[dh_a],
                              need_rows=[0], extra=dx0_a)
    dmod0 = dmod0 + dmod0_c

    dmod_blk = _pad_rows(jnp.concatenate([dmod0.reshape(nbat, -1), dmod1.reshape(nbat, -1)], axis=1), 8)
    dmod_all = _all_gather8(dmod_blk).reshape(N_DEV, 8, 2, N_MOD * D)[:, :nbat].reshape(N_DEV * nbat, 2, N_MOD * D)
    g_ada_w, g_ada_b = [], []
    for i in range(2):
        cols = lax.dynamic_slice_in_dim(dmod_all[:, i], chip * n_ada, n_ada, axis=1)
        g_ada_w.append(_matmul(cond_all, cols, mode="tn", name="ada_dw_%d" % i))
    (g_ada_b_all,) = _small(lambda t: (jnp.sum(t, axis=0),), "ada_db", [dmod_all], [(2, N_MOD * D)])
    grads = {"ada_w": jnp.stack(g_ada_w), "ada_b": g_ada_b_all}

    rep = _pad_rows(jnp.concatenate([
        d_a_lng, d_a_lnb, jnp.sum(d_bias, axis=-1).reshape(1, D), d_rk, d_final_g,
        jnp.zeros((3, D), F32), d_ws.reshape(-1, D)], axis=0), 8)
    rep_rows = rep.shape[0]
    rep_all = _all_gather8(rep)
    (rep_sum,) = _small(lambda t: (functools.reduce(lambda p, q: p + q,
                                                     [t[j * rep_rows:(j + 1) * rep_rows] for j in range(N_DEV)]),),
                        "replicated_sum", [rep_all], [(rep_rows, D)])
    grads["a_ln_g"] = rep_sum[0:1]
    grads["a_ln_b"] = rep_sum[1:2]
    grads["a_b_s"] = rep_sum[2:3].reshape(a_b_s.shape)
    grads["b_r_k"] = rep_sum[3:4].reshape(b_r_k.shape)
    grads["final_g"] = rep_sum[4].reshape(final_g.shape)
    grads["a_w_s"] = rep_sum[8:8 + GROUPS * CHUNK * LANES // D].reshape(a_w_s.shape)

    vec_grads = jnp.concatenate([d_mu, d_w0, d_a0, d_kk, d_ka, d_lng, d_lnb], axis=0)
    packed = {
        "mlp_w1_0": dw1_0, "mlp_w1_1": dw1_1,
        "mlp_w2_0": dw2_0.reshape(N_CHIPS, -1, D), "mlp_w2_1": dw2_1.reshape(N_CHIPS, -1, D),
        "a_w_in": d_a_win, "a_w_out": d_a_wout.reshape(N_CHIPS, -1, D),
        "b_w_in": _shard_cols(d_b_win), "b_w_out": d_b_wout.reshape(N_CHIPS, -1, D),
        "b_w1": d_w1p[:, :lora_w].reshape(N_CHIPS, -1, lora_w), "b_w2": _shard_cols(d_w2p[:lora_w]),
        "b_a1": d_a1p[:, :lora_w].reshape(N_CHIPS, -1, lora_w), "b_a2": _shard_cols(d_a2p[:lora_w]),
        "b_g1": d_g1p[:, :lora_g].reshape(N_CHIPS, -1, lora_g), "b_g2": _shard_cols(d_g2p[:lora_g]),
    }
    pieces = [_rows_of(packed[name]) for name, _ in big] + [_pad_rows(_rows_of(_shard_cols(vec_grads)), 8)]
    used = sum(p.shape[1] for p in pieces)
    pieces.append(jnp.zeros((N_CHIPS, (-used) % 2016, ROW_W), F32))
    g_pack = jnp.concatenate(pieces, axis=1)
    g_red = _reduce_scatter(g_pack)
    for name, _ in big:
        start, rows_k, shape = offs[name]
        grads[name] = g_red[start:start + rows_k].reshape(shape)
    vec_red = g_red[n_big_rows:n_big_rows + vec_rows.shape[0]].reshape(n_vec, D // N_CHIPS)
    grads["b_mu"] = vec_red[0:6].reshape(b_mu.shape)
    for j, name in enumerate(vec_names):
        grads[name] = vec_red[6 + j:7 + j].reshape(weights[name].shape)
    for base in ("mlp_w1", "mlp_w2"):
        grads[base] = jnp.stack([grads.pop(base + "_0"), grads.pop(base + "_1")])
    for name in ("a_w_in", "a_w_out", "b_w_in", "b_w_out", "b_w1", "b_w2", "b_a1", "b_a2", "b_g1", "b_g2"):
        grads[name] = grads[name].reshape(weights[name].shape)

    deltas, new_m, new_v = {}, {}, {}
    for name in order:
        gr = grads[name].reshape(weights[name].shape)
        grads[name] = gr
        deltas[name], new_m[name], new_v[name] = _elementwise(
            _adamw, "adamw_" + name, [weights[name], gr, moms[name][0], moms[name][1]], 3)

    return (loss, grad_x.reshape(x.shape), *[grads[k] for k in order], *[deltas[k] for k in order],
            *[new_m[k] for k in order], *[new_v[k] for k in order])
```

```python
import functools

import jax
import jax.numpy as jnp
from jax import lax
from jax.experimental import pallas as pl
from jax.experimental.pallas import tpu as pltpu

F32 = jnp.float32
BF16 = jnp.bfloat16
MESH = pl.DeviceIdType.MESH
AXES = ("x", "y", "c")

D = 1024
N_MOD = 6
HEAD = 64
CHUNK = 128
GROUPS = 8
LANES = 128
ROW_W = 1024
N_CHIPS = 4
N_DEV = 8

RMS_EPS = 1e-6
LN_EPS = 1e-5
GN_EPS = HEAD * 1e-5
L2_EPS = 1e-12

ADAM_LR = 0.001
ADAM_B1 = 0.9
ADAM_B2 = 0.999
ADAM_EPS = 1e-08
ADAM_WD = 0.01
ADAM_STEP = 10

VMEM_LIMIT_V7X = 56 * 1024 * 1024
HIGHEST = lax.Precision.HIGHEST


def _params(sem=None):
    return pltpu.CompilerParams(dimension_semantics=sem, vmem_limit_bytes=VMEM_LIMIT_V7X)


def _tile(dim, target):
    if dim <= target:
        return dim
    for cand in range(target, 0, -LANES):
        if dim % cand == 0:
            return cand
    raise ValueError((dim, target))


def _matmul(a, b, *, mode, name, out_dtype=F32, out_shards=1, tm=1024, tn=1024, tk=4096):
    if mode == "nn":
        (m, k), (k2, n) = a.shape, b.shape
    elif mode == "nt":
        (m, k), (n, k2) = a.shape, b.shape
    else:
        (k, m), (k2, n) = a.shape, b.shape
    assert k == k2, (name, a.shape, b.shape)
    n_sh = n // out_shards
    tm, tn, tk = _tile(m, tm), _tile(n_sh, tn), _tile(k, tk)
    nk = k // tk
    nb = n_sh // tn
    use_scratch = nk > 1 and out_dtype != F32

    if mode == "tn":
        a_spec = pl.BlockSpec((tk, tm), lambda i, j, kk: (kk, i))
    else:
        a_spec = pl.BlockSpec((tm, tk), lambda i, j, kk: (i, kk))
    if mode == "nt":
        b_spec = pl.BlockSpec((tn, tk), lambda i, j, kk: (j, kk))
    else:
        b_spec = pl.BlockSpec((tk, tn), lambda i, j, kk: (kk, j))
    if out_shards == 1:
        out_shape = jax.ShapeDtypeStruct((m, n), out_dtype)
        o_spec = pl.BlockSpec((tm, tn), lambda i, j, kk: (i, j))
    else:
        out_shape = jax.ShapeDtypeStruct((out_shards, m, n_sh), out_dtype)
        o_spec = pl.BlockSpec((None, tm, tn), lambda i, j, kk: (j // nb, i, j % nb))

    def body(a_ref, b_ref, o_ref, *scratch):
        kk = pl.program_id(2)
        av = a_ref[...].astype(BF16)
        bv = b_ref[...].astype(BF16)
        if mode == "nn":
            dims = (((1,), (0,)), ((), ()))
        elif mode == "nt":
            dims = (((1,), (1,)), ((), ()))
        else:
            dims = (((0,), (0,)), ((), ()))
        part = lax.dot_general(av, bv, dims, preferred_element_type=F32)
        if nk == 1:
            o_ref[...] = part.astype(o_ref.dtype)
            return
        acc_ref = scratch[0] if use_scratch else o_ref

        @pl.when(kk == 0)
        def _():
            acc_ref[...] = part

        @pl.when(kk != 0)
        def _():
            acc_ref[...] += part

        if use_scratch:
            @pl.when(kk == nk - 1)
            def _():
                o_ref[...] = acc_ref[...].astype(o_ref.dtype)

    return pl.pallas_call(
        body, name=name, out_shape=out_shape,
        grid=(m // tm, n // tn, nk),
        in_specs=[a_spec, b_spec], out_specs=o_spec,
        scratch_shapes=[pltpu.VMEM((tm, tn), F32)] if use_scratch else [],
        compiler_params=_params(("parallel", "parallel", "arbitrary")),
    )(a, b)


def _matmul_ep(a, b, extras, epilogue, out_dtypes, *, mode, name, tm=1024, tn=1024, tk=2048):
    if mode == "nn":
        (m, k), (k2, n) = a.shape, b.shape
    else:
        (m, k), (n, k2) = a.shape, b.shape
    assert k == k2 and mode in ("nn", "nt"), (name, a.shape, b.shape)
    tm, tn, tk = _tile(m, tm), _tile(n, tn), _tile(k, tk)
    nk = k // tk
    n_ex, n_out = len(extras), len(out_dtypes)

    def body(a_ref, b_ref, *rest):
        extra_refs, out_refs = rest[:n_ex], rest[n_ex:n_ex + n_out]
        kk = pl.program_id(2)
        dims = (((1,), (0,)), ((), ())) if mode == "nn" else (((1,), (1,)), ((), ()))
        part = lax.dot_general(a_ref[...].astype(BF16), b_ref[...].astype(BF16), dims,
                               preferred_element_type=F32)

        def finish(acc):
            for ref, val in zip(out_refs, epilogue(acc, *[r[...] for r in extra_refs])):
                ref[...] = val.astype(ref.dtype)

        if nk == 1:
            finish(part)
            return
        acc_ref = rest[-1]

        @pl.when(kk == 0)
        def _():
            acc_ref[...] = part

        @pl.when(kk != 0)
        def _():
            acc_ref[...] += part

        @pl.when(kk == nk - 1)
        def _():
            finish(acc_ref[...])

    a_spec = pl.BlockSpec((tm, tk), lambda i, j, kk: (i, kk))
    b_spec = (pl.BlockSpec((tk, tn), lambda i, j, kk: (kk, j)) if mode == "nn"
              else pl.BlockSpec((tn, tk), lambda i, j, kk: (j, kk)))
    o_spec = pl.BlockSpec((tm, tn), lambda i, j, kk: (i, j))
    res = pl.pallas_call(
        body, name=name, out_shape=[jax.ShapeDtypeStruct((m, n), dt) for dt in out_dtypes],
        grid=(m // tm, n // tn, nk),
        in_specs=[a_spec, b_spec] + [o_spec] * n_ex, out_specs=[o_spec] * n_out,
        scratch_shapes=[pltpu.VMEM((tm, tn), F32)] if nk > 1 else [],
        compiler_params=_params(("parallel", "parallel", "arbitrary")),
    )(a, b, *extras)
    return list(res)


class _Ctx:
    def __init__(self, first, last):
        self.first = first
        self.last = last


def _rowwise(fn, name, *, seq, tm, rows=(), prev8=(), next8=(), bvecs=(), params=(),
             out_rows=(), out_bacc=(), out_pacc=()):
    n = rows[0].shape[0]
    tm = min(tm, seq)
    assert n % seq == 0 and seq % tm == 0 and tm % 8 == 0
    tpb = seq // tm
    nt = n // tm
    nbat = n // seq
    r8 = tm // 8
    counts = [len(rows), len(prev8), len(next8), len(bvecs), len(params)]
    n_in = sum(counts)

    def body(*refs):
        i = pl.program_id(0)
        first = (i % tpb) == 0
        last = (i % tpb) == (tpb - 1)
        vals = [r[...] for r in refs[:n_in]]
        groups, pos = [], 0
        for cnt in counts:
            groups.append(vals[pos:pos + cnt])
            pos += cnt
        ro, bo, po = fn(_Ctx(first, last), *groups)
        outs = refs[n_in:]
        assert len(ro) == len(out_rows) and len(bo) == len(out_bacc) and len(po) == len(out_pacc)
        for ref, val in zip(outs[:len(ro)], ro):
            ref[...] = val.astype(ref.dtype)
        for ref, val in zip(outs[len(ro):len(ro) + len(bo)], bo):
            @pl.when(first)
            def _(ref=ref, val=val):
                ref[...] = val

            @pl.when(jnp.logical_not(first))
            def _(ref=ref, val=val):
                ref[...] += val
        for ref, val in zip(outs[len(ro) + len(bo):], po):
            @pl.when(i == 0)
            def _(ref=ref, val=val):
                ref[...] = val

            @pl.when(i != 0)
            def _(ref=ref, val=val):
                ref[...] += val

    in_specs = []
    for arr in rows:
        in_specs.append(pl.BlockSpec((tm, arr.shape[1]), lambda i: (i, 0)))
    for arr in prev8:
        in_specs.append(pl.BlockSpec((8, arr.shape[1]), lambda i: (jnp.maximum(i * r8 - 1, 0), 0)))
    for arr in next8:
        in_specs.append(pl.BlockSpec((8, arr.shape[1]), lambda i: (jnp.minimum((i + 1) * r8, n // 8 - 1), 0)))
    for arr in bvecs:
        in_specs.append(pl.BlockSpec((None,) + arr.shape[1:], lambda i: (i // tpb, 0, 0)))
    for arr in params:
        in_specs.append(pl.BlockSpec(arr.shape, lambda i: (0, 0)))
    out_shape, out_specs = [], []
    for d, dt in out_rows:
        out_shape.append(jax.ShapeDtypeStruct((n, d), dt))
        out_specs.append(pl.BlockSpec((tm, d), lambda i: (i, 0)))
    for r, d in out_bacc:
        out_shape.append(jax.ShapeDtypeStruct((nbat, r, d), F32))
        out_specs.append(pl.BlockSpec((None, r, d), lambda i: (i // tpb, 0, 0)))
    for r, d in out_pacc:
        out_shape.append(jax.ShapeDtypeStruct((r, d), F32))
        out_specs.append(pl.BlockSpec((r, d), lambda i: (0, 0)))
    res = pl.pallas_call(
        body, name=name, out_shape=out_shape, grid=(nt,),
        in_specs=in_specs, out_specs=out_specs,
        compiler_params=_params(("arbitrary",)),
    )(*rows, *prev8, *next8, *bvecs, *params)
    return list(res)


def _rw_fwd(f, name, *, seq, tm, rows, bvecs=(), params=(), outs):
    def fn(ctx, rv, pv, nv, bv, pa):
        res = f(*[v.astype(F32) for v in rv], *bv, *pa)
        return list(res), [], []
    return _rowwise(fn, name, seq=seq, tm=tm, rows=rows, bvecs=bvecs, params=params, out_rows=outs)


def _rw_bwd(f, name, *, seq, tm, rows, bvecs=(), params=(), cts, need_rows, extra=None, dtypes=None):
    nr, nb, npar = len(rows), len(bvecs), len(params)
    all_rows = list(rows) + list(cts) + ([extra] if extra is not None else [])

    def fn(ctx, rv, pv, nv, bv, pa):
        prim = [v.astype(F32) for v in rv[:nr]]
        ct = tuple(v.astype(F32) for v in rv[nr:nr + len(cts)])
        _, vjp = jax.vjp(f, *prim, *bv, *pa)
        g = vjp(ct)
        d_rows = [g[j] for j in need_rows]
        if extra is not None:
            d_rows[0] = d_rows[0] + rv[-1].astype(F32)
        return d_rows, list(g[nr:nr + nb]), list(g[nr + nb:])

    return _rowwise(
        fn, name, seq=seq, tm=tm, rows=all_rows, bvecs=bvecs, params=params,
        out_rows=[(rows[j].shape[1], F32 if dtypes is None else dtypes[i]) for i, j in enumerate(need_rows)],
        out_bacc=[b.shape[1:] for b in bvecs], out_pacc=[p.shape for p in params])


def _small(fn, name, arrays, out_shapes):
    def body(*refs):
        res = fn(*[r[...] for r in refs[:len(arrays)]])
        for ref, val in zip(refs[len(arrays):], res):
            ref[...] = val.astype(ref.dtype)

    vm = pl.BlockSpec(memory_space=pltpu.VMEM)
    res = pl.pallas_call(
        body, name=name,
        out_shape=[jax.ShapeDtypeStruct(s, F32) for s in out_shapes],
        in_specs=[vm] * len(arrays), out_specs=[vm] * len(out_shapes),
        compiler_params=_params(),
    )(*arrays)
    return list(res)


def _elementwise(fn, name, arrays, n_out):
    shape = arrays[0].shape
    size = arrays[0].size
    if len(shape) >= 2 and shape[-1] % LANES == 0 and (size // shape[-1]) % 8 == 0:
        view = (size // shape[-1], shape[-1])
    elif size % ROW_W == 0 and (size // ROW_W) % 8 == 0:
        view = (size // ROW_W, ROW_W)
    else:
        view = (1, size) if len(shape) < 2 else (size // shape[-1], shape[-1])
    rows = view[0]
    tr = rows
    for cand in (256, 128, 64, 32, 16, 8):
        if rows > cand and rows % cand == 0:
            tr = cand
            break

    def body(*refs):
        res = fn(*[r[...] for r in refs[:len(arrays)]])
        for ref, val in zip(refs[len(arrays):], res):
            ref[...] = val

    spec = pl.BlockSpec((tr, view[1]), lambda i: (i, 0))
    res = pl.pallas_call(
        body, name=name,
        out_shape=[jax.ShapeDtypeStruct(view, F32)] * n_out,
        grid=(rows // tr,), in_specs=[spec] * len(arrays), out_specs=[spec] * n_out,
        compiler_params=_params(("parallel",)),
    )(*[a.reshape(view) for a in arrays])
    return [r.reshape(shape) for r in res]


def _rms(x):
    return x * lax.rsqrt(jnp.mean(x * x, axis=-1, keepdims=True) + RMS_EPS)


def _rmsmod(x, sh, sc):
    return _rms(x) * (1.0 + sc) + sh


def _f_norm1(x, mod):
    return (_rmsmod(x, mod[0:1], mod[1:2]),)


def _f_sgu_pre(uvp, ln_g, ln_b):
    uv = 0.5 * uvp * (1.0 + lax.erf(uvp * (2.0 ** -0.5)))
    u = uv[:, :D]
    v = uv[:, D:]
    mu = jnp.mean(v, axis=-1, keepdims=True)
    vc = v - mu
    var = jnp.mean(vc * vc, axis=-1, keepdims=True)
    return u, vc * lax.rsqrt(var + LN_EPS) * ln_g + ln_b


def _f_res_norm2(x, mix, mod):
    x1 = x + mod[2:3] * mix
    return x1, _rmsmod(x1, mod[3:4], mod[4:5])


def _f_relu2(p):
    r = jnp.maximum(p, 0.0)
    return (r * r,)


def _f_res2(x1, ff, mod):
    return (x1 + mod[5:6] * ff,)


def _f_lora_act(t1, t3):
    return jnp.tanh(t1), 1.0 / (1.0 + jnp.exp(-t3))


def _f_loss(x, tgt, fg):
    err = _rms(x) * fg - tgt
    return 0.5 * jnp.sum(jnp.mean(err * err, axis=-1))


def _shift_mix(ctx, x, xprev8, mod, mu):
    h = _rmsmod(x, mod[0:1], mod[1:2])
    hprev = _rmsmod(xprev8, mod[0:1], mod[1:2])[7:8]
    hprev = jnp.where(ctx.first, jnp.zeros_like(hprev), hprev)
    rowid = lax.broadcasted_iota(jnp.int32, h.shape, 0)
    hp = jnp.where(rowid == 0, hprev, pltpu.roll(h, 1, 0))
    xx = hp - h
    return h, xx, [h + xx * mu[j:j + 1] for j in range(6)]


def _split_bf16(t, parts):
    out, rest = [], t.astype(F32)
    for _ in range(parts):
        piece = rest.astype(BF16)
        out.append(piece)
        rest = rest - piece.astype(F32)
    return out


def _make_mm(na, nb, ct_pieces=1):
    def raw(a, b, pa, pb):
        if pa == 0:
            return jnp.dot(a, b, precision=HIGHEST, preferred_element_type=F32)
        acc = None
        bs = _split_bf16(b, pb)
        for i, ai in enumerate(_split_bf16(a, pa)):
            for j, bj in enumerate(bs):
                if i + j < max(pa, pb):
                    term = jnp.dot(ai, bj, preferred_element_type=F32)
                    acc = term if acc is None else acc + term
        return acc

    @jax.custom_vjp
    def mm(a, b):
        return raw(a, b, na, nb)

    def fwd(a, b):
        return raw(a, b, na, nb), (a, b)

    def bwd(res, ct):
        a, b = res
        if na == 0:
            return raw(ct, b.T, 0, 0), raw(a.T, ct, 0, 0)
        return raw(ct, b.T, ct_pieces, 1), raw(a.T, ct, 1, ct_pieces)

    mm.defvjp(fwd, bwd)
    return mm


class _WkvMms:
    def __init__(self, head_sum, cum, score, square, apply, out, state):
        self.head_sum, self.cum, self.score = head_sum, cum, score
        self.square, self.apply, self.out, self.state = square, apply, out, state


def _wkv_mms(cfg):
    table = {"x": (0, 0), "1": (1, 1), "2": (2, 2), "3": (3, 3), "a": (2, 1), "b": (1, 2)}
    hs, cu, sc_, sq, ap, ou, st = [table[ch] for ch in cfg]
    return _WkvMms(_make_mm(hs[0], 1) if hs[0] else _make_mm(0, 0),
                   _make_mm(1, cu[1], ct_pieces=2) if cu[0] else _make_mm(0, 0),
                   _make_mm(*sc_), _make_mm(*sq), _make_mm(*ap), _make_mm(*ou), _make_mm(*st))


WKV_PRECISION = "2221b11"


SGU_CHUNKS_PER_STEP = 4


def _sgu_tile(mm, u, vn, ws, bias):
    row = lax.broadcasted_iota(jnp.int32, (CHUNK, CHUNK), 0)
    col = lax.broadcasted_iota(jnp.int32, (CHUNK, CHUNK), 1)
    wm = [jnp.where(col <= row, w, 0.0) for w in ws]
    out_rows = []
    for ch in range(u.shape[0] // CHUNK):
        rs = slice(ch * CHUNK, (ch + 1) * CHUNK)
        out_rows.append(jnp.concatenate(
            [mm(wm[g], vn[rs, g * LANES:(g + 1) * LANES]) + bias[g] for g in range(GROUPS)], axis=1))
    return u * jnp.concatenate(out_rows, axis=0)


def _sgu_fwd(u, vn, ws, bias):
    n = u.shape[0]
    rows = CHUNK * SGU_CHUNKS_PER_STEP
    mm = _make_mm(1, 1)

    def body(u_ref, v_ref, w_ref, b_ref, z_ref):
        ws_l = [w_ref[g] for g in range(GROUPS)]
        bias_l = [b_ref[g] for g in range(GROUPS)]
        z_ref[...] = _sgu_tile(mm, u_ref[...], v_ref[...], ws_l, bias_l).astype(z_ref.dtype)

    tok = pl.BlockSpec((rows, D), lambda i: (i, 0))
    grp = pl.BlockSpec((GROUPS, CHUNK, LANES), lambda i: (0, 0, 0))
    return pl.pallas_call(
        body, name="sgu_fwd", out_shape=jax.ShapeDtypeStruct((n, D), BF16),
        grid=(n // rows,), in_specs=[tok, tok, grp, grp], out_specs=tok,
        compiler_params=_params(("parallel",)),
    )(u, vn, ws, bias)


def _sgu_bwd(u, vn, ws, bias, dz):
    n = u.shape[0]
    rows = CHUNK * SGU_CHUNKS_PER_STEP
    mm = _make_mm(1, 1)

    def body(u_ref, v_ref, w_ref, b_ref, dz_ref, du_ref, dv_ref, dw_ref, db_ref):
        i = pl.program_id(0)
        ws_l = [w_ref[g] for g in range(GROUPS)]
        bias_l = [b_ref[g] for g in range(GROUPS)]
        _, vjp = jax.vjp(functools.partial(_sgu_tile, mm), u_ref[...], v_ref[...], ws_l, bias_l)
        du, dv, dw, db = vjp(dz_ref[...].astype(F32))
        du_ref[...] = du
        dv_ref[...] = dv
        for g in range(GROUPS):
            @pl.when(i == 0)
            def _(g=g):
                dw_ref[g] = dw[g]
                db_ref[g] = db[g]

            @pl.when(i != 0)
            def _(g=g):
                dw_ref[g] += dw[g]
                db_ref[g] += db[g]

    tok = pl.BlockSpec((rows, D), lambda i: (i, 0))
    grp = pl.BlockSpec((GROUPS, CHUNK, LANES), lambda i: (0, 0, 0))
    return pl.pallas_call(
        body, name="sgu_bwd",
        out_shape=[jax.ShapeDtypeStruct((n, D), F32), jax.ShapeDtypeStruct((n, D), F32),
                   jax.ShapeDtypeStruct((GROUPS, CHUNK, LANES), F32),
                   jax.ShapeDtypeStruct((GROUPS, CHUNK, LANES), F32)],
        grid=(n // rows,), in_specs=[tok, tok, grp, grp, tok], out_specs=[tok, tok, grp, grp],
        compiler_params=_params(("arbitrary",)),
    )(u, vn, ws, bias, dz)


def _chains(t):
    return [t[i] for i in range(t.shape[0])] if t.ndim == 3 else [t]


def _bmm(mm, a, b):
    if a.ndim == 2 and b.ndim == 2:
        return mm(a, b)
    ca, cb = _chains(a), _chains(b)
    n = max(len(ca), len(cb))
    return jnp.stack([mm(ca[i % len(ca)], cb[i % len(cb)]) for i in range(n)])


def _bt(a):
    return a.T if a.ndim == 2 else jnp.stack([t.T for t in _chains(a)])


def _wkv_chunk(mms, s0, r, k, v, wl, al, g, w0, a0, k_k, k_a, r_k, ln_g, ln_b):
    ln = CHUNK
    row = lax.broadcasted_iota(jnp.int32, (ln, ln), 0)
    col = lax.broadcasted_iota(jnp.int32, (ln, ln), 1)
    incl = (col <= row).astype(F32)
    strict = (col < row).astype(F32)
    same_head = ((row // HEAD) == (col // HEAD)).astype(F32)
    lane = lax.broadcasted_iota(jnp.int32, (1, LANES), 1)
    m_a = (lane < HEAD).astype(F32)
    m_b = 1.0 - m_a
    rowid = lax.broadcasted_iota(jnp.int32, (ln, LANES), 0)
    cat = jnp.concatenate

    def hsum(t):
        return _bmm(mms.head_sum, t, same_head)

    def pick_row(t, j):
        return jnp.sum(jnp.where(rowid == j, t, 0.0), axis=-2, keepdims=True)

    z = w0 + wl
    softplus_neg = jnp.maximum(-z, 0.0) + jnp.log(1.0 + jnp.exp(-jnp.abs(z)))
    lw = -jnp.exp(-softplus_neg - 0.5)
    a = 1.0 / (1.0 + jnp.exp(-(a0 + al)))
    kx = k * k_k
    kkn = kx / jnp.maximum(jnp.sqrt(hsum(kx * kx)), L2_EPS)
    kp = k * (1.0 + (a - 1.0) * k_a)
    aa = -kkn
    bb = kkn * a

    c = _bmm(mms.cum, incl, lw)
    c_mid = pick_row(c, ln // 2 - 1)
    ce = c - c_mid
    e_pos = jnp.exp(ce)
    e_neg = jnp.exp(-ce)
    at = aa * jnp.exp(ce - lw)
    bt = bb * e_neg
    kt = kp * e_neg
    rt = r * e_pos
    s0p = s0 * jnp.exp(c_mid)

    bk = cat([bt, kt], axis=-2)
    sc = _bmm(mms.score, cat([at * m_a, at * m_b, rt * m_a, rt * m_b], axis=-2), _bt(bk))
    ab_a, ak_a = sc[..., 0:ln, 0:ln] * strict, sc[..., 0:ln, ln:] * strict
    ab_b, ak_b = sc[..., ln:2 * ln, 0:ln] * strict, sc[..., ln:2 * ln, ln:] * strict
    incl2 = cat([incl, incl], axis=1)
    p_a = sc[..., 2 * ln:3 * ln, :] * incl2
    p_b = sc[..., 3 * ln:, :] * incl2

    base = _bmm(mms.score, cat([at, rt], axis=-2), _bt(s0p))
    rhs = base[..., :ln, :] + m_a * _bmm(mms.out, ak_a, v) + m_b * _bmm(mms.out, ak_b, v)

    pa, pb = ab_a, ab_b
    xa = rhs + _bmm(mms.apply, pa, rhs)
    xb = rhs + _bmm(mms.apply, pb, rhs)
    for _ in range(6):
        pa = _bmm(mms.square, pa, pa)
        pb = _bmm(mms.square, pb, pb)
        xa = xa + _bmm(mms.apply, pa, xa)
        xb = xb + _bmm(mms.apply, pb, xb)
    u = m_a * xa + m_b * xb
    uv = cat([u, v], axis=-2)
    y = base[..., ln:, :] + m_a * _bmm(mms.out, p_a, uv) + m_b * _bmm(mms.out, p_b, uv)
    s_new = (s0p + _bmm(mms.state, _bt(uv), bk)) * same_head * jnp.exp(pick_row(ce, ln - 1))

    mean = hsum(y) * (1.0 / HEAD)
    yc = y - mean
    var = hsum(yc * yc) * (1.0 / HEAD)
    yn = yc * lax.rsqrt(var + GN_EPS) * ln_g + ln_b
    bonus = hsum(r * kp * r_k) * v
    return (yn + bonus) * g, s_new


N_WKV_ROWS = 6
N_WKV_PAR = 7


WKV_PAIRS_PER_STEP = 2


def _to_chains(val, nbat, pp):
    if val.ndim == 2:
        return jnp.stack([val[:, q * LANES:(q + 1) * LANES] for _ in range(nbat) for q in range(pp)])
    return jnp.stack([val[b, :, q * LANES:(q + 1) * LANES] for b in range(nbat) for q in range(pp)])


def _wkv_fwd(seq, rows, pars):
    n = rows[0].shape[0]
    nbat, nch, npair, pp = n // seq, seq // CHUNK, D // LANES, WKV_PAIRS_PER_STEP
    chunk_fn = functools.partial(_wkv_chunk, _wkv_mms(WKV_PRECISION))

    def body(*refs):
        row_vals = [_to_chains(r[...], nbat, pp) for r in refs[:N_WKV_ROWS]]
        par_vals = [_to_chains(r[...], nbat, pp) for r in refs[N_WKV_ROWS:N_WKV_ROWS + N_WKV_PAR]]
        yo_ref, ck_ref, s_ref = refs[N_WKV_ROWS + N_WKV_PAR:]
        ch = pl.program_id(1)

        @pl.when(ch == 0)
        def _():
            s_ref[...] = jnp.zeros_like(s_ref)

        s0 = s_ref[...]
        yo, s_new = chunk_fn(s0, *row_vals, *par_vals)
        s_ref[...] = s_new
        for b in range(nbat):
            for q in range(pp):
                ck_ref[b, q] = s0[b * pp + q]
                yo_ref[b, :, q * LANES:(q + 1) * LANES] = yo[b * pp + q].astype(yo_ref.dtype)

    tok = pl.BlockSpec((nbat, CHUNK, pp * LANES), lambda p, ch: (0, ch, p))
    par = pl.BlockSpec((1, pp * LANES), lambda p, ch: (0, p))
    ck = pl.BlockSpec((nbat, pp, None, LANES, LANES), lambda p, ch: (0, p, ch, 0, 0))
    yo, ckpt = pl.pallas_call(
        body, name="wkv_fwd",
        out_shape=[jax.ShapeDtypeStruct((nbat, seq, D), BF16),
                   jax.ShapeDtypeStruct((nbat, npair, nch, LANES, LANES), F32)],
        grid=(npair // pp, nch),
        in_specs=[tok] * N_WKV_ROWS + [par] * N_WKV_PAR, out_specs=[tok, ck],
        scratch_shapes=[pltpu.VMEM((nbat * pp, LANES, LANES), F32)],
        compiler_params=_params(("parallel", "arbitrary")),
    )(*[t.reshape(nbat, seq, D) for t in rows], *pars)
    return yo.reshape(n, D), ckpt


def _wkv_bwd(seq, rows, pars, ckpt, dyo):
    n = rows[0].shape[0]
    nbat, nch, npair, pp = n // seq, seq // CHUNK, D // LANES, WKV_PAIRS_PER_STEP
    chunk_fn = functools.partial(_wkv_chunk, _wkv_mms(WKV_PRECISION))
    n_in = N_WKV_ROWS + N_WKV_PAR

    def body(*refs):
        row_vals = [_to_chains(r[...], nbat, pp) for r in refs[:N_WKV_ROWS]]
        par_vals = [_to_chains(r[...], nbat, pp) for r in refs[N_WKV_ROWS:n_in]]
        ck_ref, dyo_ref = refs[n_in:n_in + 2]
        d_rows = refs[n_in + 2:n_in + 2 + N_WKV_ROWS]
        d_pars = refs[n_in + 2 + N_WKV_ROWS:n_in + 2 + N_WKV_ROWS + N_WKV_PAR]
        ds_ref = refs[-1]
        ch = pl.program_id(1)

        @pl.when(ch == 0)
        def _():
            ds_ref[...] = jnp.zeros_like(ds_ref)

        s0 = jnp.stack([ck_ref[b, q] for b in range(nbat) for q in range(pp)])
        dyo_v = _to_chains(dyo_ref[...].astype(F32), nbat, pp)
        _, vjp = jax.vjp(chunk_fn, s0, *row_vals, *par_vals)
        grads = vjp((dyo_v, ds_ref[...]))
        ds_ref[...] = grads[0]
        for ref, val in zip(d_rows, grads[1:1 + N_WKV_ROWS]):
            for b in range(nbat):
                for q in range(pp):
                    ref[b, :, q * LANES:(q + 1) * LANES] = val[b * pp + q].astype(ref.dtype)
        for ref, val in zip(d_pars, grads[1 + N_WKV_ROWS:]):
            per_pair = [functools.reduce(lambda s, t: s + t, [val[b * pp + q] for b in range(nbat)])
                        for q in range(pp)]
            tot = jnp.concatenate(per_pair, axis=1)

            @pl.when(ch == 0)
            def _(ref=ref, tot=tot):
                ref[...] = tot

            @pl.when(ch != 0)
            def _(ref=ref, tot=tot):
                ref[...] += tot

    tok = pl.BlockSpec((nbat, CHUNK, pp * LANES), lambda p, ch: (0, nch - 1 - ch, p))
    par = pl.BlockSpec((1, pp * LANES), lambda p, ch: (0, p))
    ck = pl.BlockSpec((nbat, pp, None, LANES, LANES), lambda p, ch: (0, p, nch - 1 - ch, 0, 0))
    res = pl.pallas_call(
        body, name="wkv_bwd",
        out_shape=[jax.ShapeDtypeStruct((nbat, seq, D), BF16)] * N_WKV_ROWS
        + [jax.ShapeDtypeStruct((1, D), F32)] * N_WKV_PAR,
        grid=(npair // pp, nch),
        in_specs=[tok] * N_WKV_ROWS + [par] * N_WKV_PAR + [ck, tok],
        out_specs=[tok] * N_WKV_ROWS + [par] * N_WKV_PAR,
        scratch_shapes=[pltpu.VMEM((nbat * pp, LANES, LANES), F32)],
        compiler_params=_params(("parallel", "arbitrary")),
    )(*[t.reshape(nbat, seq, D) for t in rows], *pars, ckpt, dyo.reshape(nbat, seq, D))
    return [t.reshape(n, D) for t in res[:N_WKV_ROWS]] + list(res[N_WKV_ROWS:])


def _place():
    return lax.axis_index("x"), lax.axis_index("y"), lax.axis_index("c")


def _all_gather8(blk):
    m_per, n = blk.shape
    assert m_per % 8 == 0

    def body(x_ref, out_ref, send_sems, recv_sems, local_sem):
        x, y, c = _place()
        me, sibling = (x, y, c), (x, y, 1 - c)
        chips = [(1 - x, y), (x, 1 - y), (1 - x, 1 - y)]

        def rows(px, py, pc):
            return out_ref.at[pl.ds((4 * px + 2 * py + pc) * m_per, m_per), :]

        def copy(k, block, to, src=None):
            return pltpu.make_async_remote_copy(
                src_ref=rows(*block) if src is None else src, dst_ref=rows(*block),
                send_sem=send_sems.at[k], recv_sem=recv_sems.at[k],
                device_id=to, device_id_type=MESH)

        mine = pltpu.make_async_copy(x_ref, rows(*me), local_sem)
        mine.start()
        first = [copy(0, me, sibling, src=x_ref)]
        first += [copy(1 + j, me, (*chip, c), src=x_ref) for j, chip in enumerate(chips)]
        for cp in first:
            cp.start()
        passed = [copy(4 + j, (*chip, c), sibling) for j, chip in enumerate(chips)]
        for j, chip in enumerate(chips):
            copy(1 + j, (*chip, c), me).wait_recv()
            passed[j].start()
        copy(0, sibling, me).wait_recv()
        for j, chip in enumerate(chips):
            copy(4 + j, (*chip, 1 - c), me).wait_recv()
        for cp in first + passed:
            cp.wait_send()
        mine.wait()

    vm = pl.BlockSpec(memory_space=pltpu.VMEM)
    return pl.pallas_call(
        body, name="all_gather8_%dx%d" % (m_per, n),
        out_shape=jax.ShapeDtypeStruct((N_DEV * m_per, n), blk.dtype),
        in_specs=[vm], out_specs=vm,
        scratch_shapes=[pltpu.SemaphoreType.DMA((7,)), pltpu.SemaphoreType.DMA((7,)),
                        pltpu.SemaphoreType.DMA],
        compiler_params=_params(),
    )(blk)


def _own_slot(src, name):
    r, w = src.shape[-2:]
    tr = _tile(r, 1008)
    xi, yi, _ = _place()
    chip = jnp.reshape(2 * xi + yi, (1,)).astype(jnp.int32)

    def body(chip_ref, x_ref, o_ref):
        o_ref[...] = x_ref[...]

    if src.ndim == 2:
        in_spec = pl.BlockSpec((tr, w), lambda i, chip_ref: (i, 0))
    else:
        in_spec = pl.BlockSpec((None, tr, w), lambda i, chip_ref: (chip_ref[0], i, 0))
    return pl.pallas_call(
        body, name=name,
        out_shape=jax.ShapeDtypeStruct((N_CHIPS, r, w), src.dtype),
        grid_spec=pltpu.PrefetchScalarGridSpec(
            num_scalar_prefetch=1, grid=(r // tr,), in_specs=[in_spec],
            out_specs=pl.BlockSpec((None, tr, w), lambda i, chip_ref: (chip_ref[0], i, 0))),
        compiler_params=_params(("parallel",)),
    )(chip, src)


def _chip_all_gather(shard):
    r, w = shard.shape
    half = r // 2
    assert r % 2 == 0 and half % 16 == 0

    def body(x_ref, buf_ref, out_ref, send_sems, recv_sems):
        del buf_ref
        x, y, c = _place()
        sibling = (x, y, 1 - c)
        me_p = 2 * x + y
        chips = [(1 - x, y), (x, 1 - y), (1 - x, 1 - y)]

        def piece(p, h):
            return out_ref.at[p, pl.ds(h * half, half), :]

        def copy(k, p, h, to, src=None):
            return pltpu.make_async_remote_copy(
                src_ref=piece(p, h) if src is None else src, dst_ref=piece(p, h),
                send_sem=send_sems.at[k], recv_sem=recv_sems.at[k],
                device_id=to, device_id_type=MESH)

        my_half = x_ref.at[pl.ds(c * half, half), :]
        first = [copy(j, me_p, c, (*chip, c), src=my_half) for j, chip in enumerate(chips)]
        for cp in first:
            cp.start()
        passed = [copy(3 + j, 2 * chip[0] + chip[1], c, sibling) for j, chip in enumerate(chips)]
        for j, chip in enumerate(chips):
            copy(j, 2 * chip[0] + chip[1], c, sibling).wait_recv()
            passed[j].start()
        for j, chip in enumerate(chips):
            copy(3 + j, 2 * chip[0] + chip[1], 1 - c, sibling).wait_recv()
        for cp in first + passed:
            cp.wait_send()

    hbm = pl.BlockSpec(memory_space=pl.ANY)
    return pl.pallas_call(
        body, name="chip_all_gather",
        out_shape=jax.ShapeDtypeStruct((N_CHIPS, r, w), shard.dtype),
        in_specs=[hbm, hbm], out_specs=hbm, input_output_aliases={1: 0},
        scratch_shapes=[pltpu.SemaphoreType.DMA((6,)), pltpu.SemaphoreType.DMA((6,))],
        compiler_params=_params(),
    )(shard, _own_slot(shard, "gather_own_slot"))


def _sibling_swap_halves(g):
    _, r, w = g.shape
    half = r // 2

    def body(g_ref, t_ref, send_sem, recv_sem):
        x, y, c = _place()
        sibling = (x, y, 1 - c)
        cp = pltpu.make_async_remote_copy(
            src_ref=g_ref.at[:, pl.ds((1 - c) * half, half), :], dst_ref=t_ref,
            send_sem=send_sem, recv_sem=recv_sem, device_id=sibling, device_id_type=MESH)
        cp.start()
        cp.wait()

    hbm = pl.BlockSpec(memory_space=pl.ANY)
    return pl.pallas_call(
        body, name="rs_sibling_halves",
        out_shape=jax.ShapeDtypeStruct((N_CHIPS, half, w), g.dtype),
        in_specs=[hbm], out_specs=hbm,
        scratch_shapes=[pltpu.SemaphoreType.DMA, pltpu.SemaphoreType.DMA],
        compiler_params=_params(),
    )(g)


def _add_own_half(g, t):
    _, r, w = g.shape
    half = r // 2
    tr = 1008 if half % 1008 == 0 else 16
    assert half % tr == 0
    cidx = jnp.reshape(lax.axis_index("c"), (1,)).astype(jnp.int32)

    def body(c_ref, g_ref, t_ref, o_ref):
        o_ref[...] = (g_ref[...] + t_ref[...]).astype(o_ref.dtype)

    return pl.pallas_call(
        body, name="rs_add_halves",
        out_shape=jax.ShapeDtypeStruct((N_CHIPS, half, w), BF16),
        grid_spec=pltpu.PrefetchScalarGridSpec(
            num_scalar_prefetch=1, grid=(N_CHIPS, half // tr),
            in_specs=[pl.BlockSpec((None, None, tr, w), lambda p, i, c_ref: (p, c_ref[0], i, 0)),
                      pl.BlockSpec((None, tr, w), lambda p, i, c_ref: (p, i, 0))],
            out_specs=pl.BlockSpec((None, tr, w), lambda p, i, c_ref: (p, i, 0))),
        compiler_params=_params(("parallel", "parallel")),
    )(cidx, g.reshape(N_CHIPS, 2, half, w), t)


def _chip_exchange(h):
    _, hh, w = h.shape

    def body(h_ref, buf_ref, t_ref, send_sems, recv_sems):
        del buf_ref
        x, y, c = _place()
        me_p = 2 * x + y
        chips = [(1 - x, y), (x, 1 - y), (1 - x, 1 - y)]
        cps = []
        for j, chip in enumerate(chips):
            q = 2 * chip[0] + chip[1]
            cps.append(pltpu.make_async_remote_copy(
                src_ref=h_ref.at[q], dst_ref=t_ref.at[me_p],
                send_sem=send_sems.at[j], recv_sem=recv_sems.at[j],
                device_id=(*chip, c), device_id_type=MESH))
        for cp in cps:
            cp.start()
        for j, chip in enumerate(chips):
            q = 2 * chip[0] + chip[1]
            pltpu.make_async_remote_copy(
                src_ref=h_ref.at[q], dst_ref=t_ref.at[q],
                send_sem=send_sems.at[j], recv_sem=recv_sems.at[j],
                device_id=(*chip, c), device_id_type=MESH).wait_recv()
        for cp in cps:
            cp.wait_send()

    hbm = pl.BlockSpec(memory_space=pl.ANY)
    return pl.pallas_call(
        body, name="rs_chip_exchange",
        out_shape=jax.ShapeDtypeStruct(h.shape, h.dtype),
        in_specs=[hbm, hbm], out_specs=hbm, input_output_aliases={1: 0},
        scratch_shapes=[pltpu.SemaphoreType.DMA((3,)), pltpu.SemaphoreType.DMA((3,))],
        compiler_params=_params(),
    )(h, _own_slot(h, "rs_own_slot"))


def _sum_slots(t):
    _, hh, w = t.shape
    tr = 1008 if hh % 1008 == 0 else 16
    assert hh % tr == 0
    nblk = hh // tr
    cidx = jnp.reshape(lax.axis_index("c"), (1,)).astype(jnp.int32)

    def body(c_ref, t_ref, o_ref):
        s0, s1, s2, s3 = [t_ref[j].astype(F32) for j in range(N_CHIPS)]
        o_ref[...] = ((s0 + s1) + s2) + s3

    return pl.pallas_call(
        body, name="rs_sum_slots", out_shape=jax.ShapeDtypeStruct((2 * hh, w), F32),
        grid_spec=pltpu.PrefetchScalarGridSpec(
            num_scalar_prefetch=1, grid=(nblk,),
            in_specs=[pl.BlockSpec((N_CHIPS, tr, w), lambda i, c_ref: (0, i, 0))],
            out_specs=pl.BlockSpec((tr, w), lambda i, c_ref: (c_ref[0] * nblk + i, 0))),
        compiler_params=_params(("parallel",)),
    )(cidx, t)


def _sibling_join_halves(s):
    h2, w = s.shape
    hh = h2 // 2

    def body(s_ref, o_ref, send_sem, recv_sem):
        del s_ref
        x, y, c = _place()
        sibling = (x, y, 1 - c)
        cp = pltpu.make_async_remote_copy(
            src_ref=o_ref.at[pl.ds(c * hh, hh), :], dst_ref=o_ref.at[pl.ds(c * hh, hh), :],
            send_sem=send_sem, recv_sem=recv_sem, device_id=sibling, device_id_type=MESH)
        cp.start()
        pltpu.make_async_remote_copy(
            src_ref=o_ref.at[pl.ds((1 - c) * hh, hh), :], dst_ref=o_ref.at[pl.ds((1 - c) * hh, hh), :],
            send_sem=send_sem, recv_sem=recv_sem, device_id=sibling, device_id_type=MESH).wait_recv()
        cp.wait_send()

    hbm = pl.BlockSpec(memory_space=pl.ANY)
    return pl.pallas_call(
        body, name="rs_sibling_join",
        out_shape=jax.ShapeDtypeStruct(s.shape, s.dtype),
        in_specs=[hbm], out_specs=hbm, input_output_aliases={0: 0},
        scratch_shapes=[pltpu.SemaphoreType.DMA, pltpu.SemaphoreType.DMA],
        compiler_params=_params(),
    )(s)


def _reduce_scatter(g):
    h = _add_own_half(g, _sibling_swap_halves(g))
    return _sibling_join_halves(_sum_slots(_chip_exchange(h)))


def _unshard_cols(piece):
    p, k, n = piece.shape
    return jnp.transpose(piece, (1, 0, 2)).reshape(k, p * n)


def _shard_cols(full):
    k, n4 = full.shape
    return jnp.transpose(full.reshape(k, N_CHIPS, n4 // N_CHIPS), (1, 0, 2))


def _rows_of(piece):
    return piece.reshape(N_CHIPS, -1, ROW_W)


def _pad_rows(a, mult):
    pad = (-a.shape[-2]) % mult
    if pad == 0:
        return a
    widths = [(0, 0)] * (a.ndim - 2) + [(0, pad), (0, 0)]
    return jnp.pad(a, widths)


def _adamw(w, g, m, v):
    m2 = ADAM_B1 * m + (1.0 - ADAM_B1) * g
    v2 = ADAM_B2 * v + (1.0 - ADAM_B2) * (g * g)
    m_hat = m2 / (1.0 - ADAM_B1 ** ADAM_STEP)
    v_hat = v2 / (1.0 - ADAM_B2 ** ADAM_STEP)
    delta = -ADAM_LR * (m_hat / (jnp.sqrt(v_hat) + ADAM_EPS) + ADAM_WD * w)
    return delta, m2, v2


def _mlp_fwd(seq, tm, x_in, mix, mod, w1, w2, tag, residual=True):
    x1, h2 = _rw_fwd(_f_res_norm2, "res_norm2_" + tag, seq=seq, tm=tm, rows=[x_in, mix], bvecs=[mod],
                     outs=[(D, F32), (D, BF16)])
    p, f = _matmul_ep(h2, w1, [], lambda acc: (acc, _f_relu2(acc)[0]), [F32, BF16],
                      mode="nn", name="mlp_up_" + tag)
    ff = _matmul(f, w2, mode="nn", name="mlp_down_" + tag)
    x2 = None
    if residual:
        (x2,) = _rw_fwd(_f_res2, "res2_" + tag, seq=seq, tm=tm, rows=[x1, ff], bvecs=[mod], outs=[(D, F32)])
    return x2, (x1, h2, p, f, ff)


def _mlp_bwd(seq, tm, saved, x_in, mix, mod, w1, w2, dx2, tag, res2_grads=None):
    x1, h2, p, f, ff = saved
    if res2_grads is None:
        dff, dmod_a = _rw_bwd(_f_res2, "res2_bwd_" + tag, seq=seq, tm=tm, rows=[x1, ff], bvecs=[mod],
                              cts=[dx2], need_rows=[1], dtypes=[BF16])
    else:
        dff, dmod_a = res2_grads
    (dp,) = _matmul_ep(dff, w2, [p], lambda acc, pt: (2.0 * jnp.maximum(pt, 0.0) * acc,), [BF16],
                       mode="nt", name="mlp_down_dx_" + tag)
    dw2 = _matmul(f, dff, mode="tn", name="mlp_down_dw_" + tag)
    dh2 = _matmul(dp, w1, mode="nt", name="mlp_up_dx_" + tag)
    dw1 = _matmul(h2, dp, mode="tn", name="mlp_up_dw_" + tag, out_shards=N_CHIPS)
    dx_in, dmix, dmod_b = _rw_bwd(_f_res_norm2, "res_norm2_bwd_" + tag, seq=seq, tm=tm, rows=[x_in, mix],
                                  bvecs=[mod], cts=[dx2, dh2], need_rows=[0, 1], dtypes=[F32, BF16])
    return dx_in, dmix, dmod_a + dmod_b, dw1, dw2


def kernel(x, c, ada_w, ada_b, mlp_w1, mlp_w2, a_w_in, a_ln_g, a_ln_b, a_w_s, a_b_s, a_w_out, b_mu, b_w_in, b_w0, b_w1, b_w2, b_a0, b_a1, b_a2, b_g1, b_g2, b_k_k, b_k_a, b_r_k, b_ln_g, b_ln_b, b_w_out, final_g, loss_target, m_ada_w, m_ada_b, m_mlp_w1, m_mlp_w2, m_a_w_in, m_a_ln_g, m_a_ln_b, m_a_w_s, m_a_b_s, m_a_w_out, m_b_mu, m_b_w_in, m_b_w0, m_b_w1, m_b_w2, m_b_a0, m_b_a1, m_b_a2, m_b_g1, m_b_g2, m_b_k_k, m_b_k_a, m_b_r_k, m_b_ln_g, m_b_ln_b, m_b_w_out, m_final_g, v_ada_w, v_ada_b, v_mlp_w1, v_mlp_w2, v_a_w_in, v_a_ln_g, v_a_ln_b, v_a_w_s, v_a_b_s, v_a_w_out, v_b_mu, v_b_w_in, v_b_w0, v_b_w1, v_b_w2, v_b_a0, v_b_a1, v_b_a2, v_b_g1, v_b_g2, v_b_k_k, v_b_k_a, v_b_r_k, v_b_ln_g, v_b_ln_b, v_b_w_out, v_final_g):
    weights = dict(ada_w=ada_w, ada_b=ada_b, mlp_w1=mlp_w1, mlp_w2=mlp_w2, a_w_in=a_w_in, a_ln_g=a_ln_g,
                   a_ln_b=a_ln_b, a_w_s=a_w_s, a_b_s=a_b_s, a_w_out=a_w_out, b_mu=b_mu, b_w_in=b_w_in,
                   b_w0=b_w0, b_w1=b_w1, b_w2=b_w2, b_a0=b_a0, b_a1=b_a1, b_a2=b_a2, b_g1=b_g1, b_g2=b_g2,
                   b_k_k=b_k_k, b_k_a=b_k_a, b_r_k=b_r_k, b_ln_g=b_ln_g, b_ln_b=b_ln_b, b_w_out=b_w_out,
                   final_g=final_g)
    moms = dict(ada_w=(m_ada_w, v_ada_w), ada_b=(m_ada_b, v_ada_b), mlp_w1=(m_mlp_w1, v_mlp_w1),
                mlp_w2=(m_mlp_w2, v_mlp_w2), a_w_in=(m_a_w_in, v_a_w_in), a_ln_g=(m_a_ln_g, v_a_ln_g),
                a_ln_b=(m_a_ln_b, v_a_ln_b), a_w_s=(m_a_w_s, v_a_w_s), a_b_s=(m_a_b_s, v_a_b_s),
                a_w_out=(m_a_w_out, v_a_w_out), b_mu=(m_b_mu, v_b_mu), b_w_in=(m_b_w_in, v_b_w_in),
                b_w0=(m_b_w0, v_b_w0), b_w1=(m_b_w1, v_b_w1), b_w2=(m_b_w2, v_b_w2), b_a0=(m_b_a0, v_b_a0),
                b_a1=(m_b_a1, v_b_a1), b_a2=(m_b_a2, v_b_a2), b_g1=(m_b_g1, v_b_g1), b_g2=(m_b_g2, v_b_g2),
                b_k_k=(m_b_k_k, v_b_k_k), b_k_a=(m_b_k_a, v_b_k_a), b_r_k=(m_b_r_k, v_b_r_k),
                b_ln_g=(m_b_ln_g, v_b_ln_g), b_ln_b=(m_b_ln_b, v_b_ln_b), b_w_out=(m_b_w_out, v_b_w_out),
                final_g=(m_final_g, v_final_g))
    order = list(weights)

    nbat, seq, _ = x.shape
    n = nbat * seq
    tm = 256
    xi, yi, ci = _place()
    chip = 2 * xi + yi
    dev = 2 * chip + ci
    x0 = x.reshape(n, D)
    tgt = loss_target.reshape(n, D)
    lora_w, lora_g = b_w1.shape[-1], b_g1.shape[-1]
    lora_wp, lora_gp = LANES, 2 * LANES

    (cond,) = _small(lambda cc: (cc / (1.0 + jnp.exp(-cc)),), "silu_c", [c], [c.shape])
    vec_names = ["b_w0", "b_a0", "b_k_k", "b_k_a", "b_ln_g", "b_ln_b"]
    vec_shard = jnp.concatenate([b_mu[0]] + [weights[k] for k in vec_names], axis=0)
    n_vec = vec_shard.shape[0]
    vec_rows = vec_shard.reshape(-1, ROW_W)
    blk = _pad_rows(jnp.concatenate([cond, vec_rows], axis=0), 8)
    assert blk.shape[0] == 8
    gathered = _all_gather8(blk).reshape(N_DEV, 8, D)
    cond_all = gathered[:, :nbat].reshape(N_DEV * nbat, D)
    vec_all = gathered[0::2, nbat:nbat + vec_rows.shape[0]].reshape(N_CHIPS, n_vec, D // N_CHIPS)
    vec_full = jnp.transpose(vec_all, (1, 0, 2)).reshape(n_vec, D)
    mu_full = vec_full[0:6]
    w0_f, a0_f, kk_f, ka_f, lng_f, lnb_f = [vec_full[6 + j:7 + j] for j in range(6)]
    rk_f = b_r_k.reshape(1, D)

    n_ada = ada_w.shape[-1]
    parts = jnp.concatenate(
        [_matmul(cond_all, ada_w[i], mode="nn", name="ada_fwd_%d" % i) for i in range(2)], axis=1)
    parts_all = _all_gather8(parts).reshape(N_DEV, N_DEV * nbat, 2, n_ada)[0::2]
    mine = lax.dynamic_slice_in_dim(parts_all, dev * nbat, nbat, axis=1)
    mods = []
    for i in range(2):
        full = jnp.transpose(mine[:, :, i], (1, 0, 2)).reshape(nbat, N_MOD * D) + ada_b[i]
        mods.append(full.reshape(nbat, N_MOD, D))

    big = [("mlp_w1_0", mlp_w1[0]), ("mlp_w1_1", mlp_w1[1]), ("mlp_w2_0", mlp_w2[0]), ("mlp_w2_1", mlp_w2[1]),
           ("a_w_in", a_w_in[0]), ("a_w_out", a_w_out[0]), ("b_w_in", b_w_in[0]), ("b_w_out", b_w_out[0]),
           ("b_w1", b_w1[0]), ("b_w2", b_w2[0]), ("b_a1", b_a1[0]), ("b_a2", b_a2[0]),
           ("b_g1", b_g1[0]), ("b_g2", b_g2[0])]
    offs, pos = {}, 0
    for name, arr in big:
        rows_k = arr.size // ROW_W
        offs[name] = (pos, rows_k, arr.shape)
        pos += rows_k
    n_big_rows = pos
    wflat = _pad_rows(jnp.concatenate([arr.astype(BF16).reshape(-1, ROW_W) for _, arr in big], axis=0), 32)
    wg = _chip_all_gather(wflat)

    def gathered_piece(name):
        start, rows_k, shape = offs[name]
        return wg[:, start:start + rows_k].reshape((N_CHIPS,) + shape)

    def col_w(name):
        return _unshard_cols(gathered_piece(name))

    def row_w(name):
        piece = gathered_piece(name)
        return piece.reshape(N_CHIPS * piece.shape[1], piece.shape[2])

    w1_l = [col_w("mlp_w1_0"), col_w("mlp_w1_1")]
    w2_l = [row_w("mlp_w2_0"), row_w("mlp_w2_1")]
    a_win, a_wout = col_w("a_w_in"), row_w("a_w_out")
    b_win, b_wout = col_w("b_w_in"), row_w("b_w_out")
    w_r, w_k, w_v = b_win[:, :D], b_win[:, D:2 * D], b_win[:, 2 * D:]
    w1p = jnp.pad(row_w("b_w1"), ((0, 0), (0, lora_wp - lora_w)))
    a1p = jnp.pad(row_w("b_a1"), ((0, 0), (0, lora_wp - lora_w)))
    g1p = jnp.pad(row_w("b_g1"), ((0, 0), (0, lora_gp - lora_g)))
    w2p = jnp.pad(col_w("b_w2"), ((0, lora_wp - lora_w), (0, 0)))
    a2p = jnp.pad(col_w("b_a2"), ((0, lora_wp - lora_w), (0, 0)))
    g2p = jnp.pad(col_w("b_g2"), ((0, lora_gp - lora_g), (0, 0)))

    mod0, mod1 = mods
    (h_a,) = _rw_fwd(_f_norm1, "norm1_a", seq=seq, tm=tm, rows=[x0], bvecs=[mod0], outs=[(D, BF16)])
    uvp = _matmul(h_a, a_win, mode="nn", name="sgu_in")
    u, vn = _rw_fwd(_f_sgu_pre, "sgu_pre", seq=seq, tm=tm, rows=[uvp], params=[a_ln_g, a_ln_b],
                    outs=[(D, F32), (D, F32)])
    ws = a_w_s[0]
    bias = jnp.broadcast_to(a_b_s[0][:, :, None], (GROUPS, CHUNK, LANES))
    z = _sgu_fwd(u, vn, ws, bias)
    mix0 = _matmul(z, a_wout, mode="nn", name="sgu_out")
    x2, saved0 = _mlp_fwd(seq, tm, x0, mix0, mod0, w1_l[0], w2_l[0], "0")

    def shift_fwd(ctx, rv, pv, nv, bv, pa):
        _, _, mixes = _shift_mix(ctx, rv[0], pv[0], bv[0], pa[0])
        return mixes, [], []

    xr, xw, xk, xv, xa, xg = _rowwise(shift_fwd, "shift_mix", seq=seq, tm=tm, rows=[x2], prev8=[x2],
                                      bvecs=[mod1], params=[mu_full], out_rows=[(D, BF16)] * 6)
    r = _matmul(xr, w_r, mode="nn", name="rwkv_r")
    k = _matmul(xk, w_k, mode="nn", name="rwkv_k")
    v = _matmul(xv, w_v, mode="nn", name="rwkv_v")
    t1 = _matmul(xw, w1p, mode="nn", name="lora_w1")
    t2 = _matmul(xa, a1p, mode="nn", name="lora_a1", out_dtype=BF16)
    t3 = _matmul(xg, g1p, mode="nn", name="lora_g1")
    (th,) = _rw_fwd(lambda t: (jnp.tanh(t),), "lora_tanh", seq=seq, tm=tm, rows=[t1], outs=[(lora_wp, BF16)])
    (sg,) = _rw_fwd(lambda t: (1.0 / (1.0 + jnp.exp(-t)),), "lora_sigmoid", seq=seq, tm=tm, rows=[t3],
                    outs=[(lora_gp, BF16)])
    wl = _matmul(th, w2p, mode="nn", name="lora_w2")
    al = _matmul(t2, a2p, mode="nn", name="lora_a2")
    g = _matmul(sg, g2p, mode="nn", name="lora_g2")
    wkv_rows = [r, k, v, wl, al, g]
    wkv_pars = [w0_f, a0_f, kk_f, ka_f, rk_f, lng_f, lnb_f]
    yo, ckpt = _wkv_fwd(seq, wkv_rows, wkv_pars)
    mix1 = _matmul(yo, b_wout, mode="nn", name="rwkv_out")
    _, saved1 = _mlp_fwd(seq, tm, x2, mix1, mod1, w1_l[1], w2_l[1], "1", residual=False)

    def loss_fn(ctx, rv, pv, nv, bv, pa):
        def head(x1, ff, mod, fg):
            return _f_loss(_f_res2(x1, ff, mod)[0], rv[2], fg)
        val, (dx, dff, dmod, dfg) = jax.value_and_grad(head, argnums=(0, 1, 2, 3))(rv[0], rv[1], bv[0], pa[0])
        return [dx, dff], [dmod], [dfg, jnp.full((1, LANES), val, F32)]

    dx4, dff1, dmod1_a, d_final_g, loss_acc = _rowwise(
        loss_fn, "loss_head", seq=seq, tm=tm, rows=[saved1[0], saved1[4], tgt], bvecs=[mod1],
        params=[final_g.reshape(1, D)], out_rows=[(D, F32), (D, BF16)], out_bacc=[(N_MOD, D)],
        out_pacc=[(1, D), (1, LANES)])
    loss = lax.psum(loss_acc[0, 0], AXES)

    dx2_a, dmix1, dmod1, dw1_1, dw2_1 = _mlp_bwd(seq, tm, saved1, x2, mix1, mod1, w1_l[1], w2_l[1], dx4, "1",
                                                 res2_grads=(dff1, dmod1_a))
    dyo = _matmul(dmix1, b_wout, mode="nt", name="rwkv_out_dx")
    d_b_wout = _matmul(yo, dmix1, mode="tn", name="rwkv_out_dw")
    wkv_grads = _wkv_bwd(seq, wkv_rows, wkv_pars, ckpt, dyo)
    dr, dk, dv, dwl, dal, dg = wkv_grads[:N_WKV_ROWS]
    d_w0, d_a0, d_kk, d_ka, d_rk, d_lng, d_lnb = wkv_grads[N_WKV_ROWS:]
    dth = _matmul(dwl, w2p, mode="nt", name="lora_w2_dx")
    d_w2p = _matmul(th, dwl, mode="tn", name="lora_w2_dw")
    dt2 = _matmul(dal, a2p, mode="nt", name="lora_a2_dx", out_dtype=BF16)
    d_a2p = _matmul(t2, dal, mode="tn", name="lora_a2_dw")
    dsg = _matmul(dg, g2p, mode="nt", name="lora_g2_dx")
    d_g2p = _matmul(sg, dg, mode="tn", name="lora_g2_dw")
    (dt1,) = _rw_bwd(lambda t: (jnp.tanh(t),), "lora_tanh_bwd", seq=seq, tm=tm, rows=[t1], cts=[dth],
                     need_rows=[0], dtypes=[BF16])
    (dt3,) = _rw_bwd(lambda t: (1.0 / (1.0 + jnp.exp(-t)),), "lora_sigmoid_bwd", seq=seq, tm=tm, rows=[t3],
                     cts=[dsg], need_rows=[0], dtypes=[BF16])
    dxw = _matmul(dt1, w1p, mode="nt", name="lora_w1_dx")
    d_w1p = _matmul(xw, dt1, mode="tn", name="lora_w1_dw")
    dxa = _matmul(dt2, a1p, mode="nt", name="lora_a1_dx")
    d_a1p = _matmul(xa, dt2, mode="tn", name="lora_a1_dw")
    dxg = _matmul(dt3, g1p, mode="nt", name="lora_g1_dx")
    d_g1p = _matmul(xg, dt3, mode="tn", name="lora_g1_dw")
    dxr = _matmul(dr, w_r, mode="nt", name="rwkv_r_dx")
    dxk = _matmul(dk, w_k, mode="nt", name="rwkv_k_dx")
    dxv = _matmul(dv, w_v, mode="nt", name="rwkv_v_dx")
    d_b_win = jnp.concatenate([_matmul(xr, dr, mode="tn", name="rwkv_r_dw"),
                               _matmul(xk, dk, mode="tn", name="rwkv_k_dw"),
                               _matmul(xv, dv, mode="tn", name="rwkv_v_dw")], axis=1)

    def shift_bwd(ctx, rv, pv, nv, bv, pa):
        xt, dres = rv[0], rv[1]
        dmix_in = rv[2:8]
        mod, mu = bv[0], pa[0]
        f_h = lambda xx_, mod_: _rmsmod(xx_, mod_[0:1], mod_[1:2])
        h, vjp = jax.vjp(f_h, xt, mod)
        hprev = f_h(pv[0], mod)[7:8]
        hprev = jnp.where(ctx.first, jnp.zeros_like(hprev), hprev)
        rowid = lax.broadcasted_iota(jnp.int32, h.shape, 0)
        xx = jnp.where(rowid == 0, hprev, pltpu.roll(h, 1, 0)) - h
        tot = dmix_in[0]
        wsum = dmix_in[0] * mu[0:1]
        for j in range(1, 6):
            tot = tot + dmix_in[j]
            wsum = wsum + dmix_in[j] * mu[j:j + 1]
        nxt = nv[0][0:1] * mu[0:1]
        for j in range(1, 6):
            nxt = nxt + nv[j][0:1] * mu[j:j + 1]
        nxt = jnp.where(ctx.last, jnp.zeros_like(nxt), nxt)
        tmr = h.shape[0]
        wshift = jnp.where(rowid == tmr - 1, nxt, pltpu.roll(wsum, tmr - 1, 0))
        dh = tot - wsum + wshift
        dx_, dmod_ = vjp(dh)
        dmu = jnp.concatenate([jnp.sum(dmix_in[j] * xx, axis=0, keepdims=True) for j in range(6)], axis=0)
        dx2_t = dx_ + dres
        ff_below, mod_below = rv[8], bv[1]
        dff_below = mod_below[5:6] * dx2_t
        dgate = jnp.sum(dx2_t * ff_below, axis=0, keepdims=True)
        gate_row = lax.broadcasted_iota(jnp.int32, (N_MOD, D), 0) == N_MOD - 1
        dmod_below = jnp.where(gate_row, jnp.broadcast_to(dgate, (N_MOD, D)), 0.0)
        return [dx2_t, dff_below], [dmod_, dmod_below], [dmu]

    dmix_list = [dxr, dxw, dxk, dxv, dxa, dxg]
    dx2, dff0, dmod1_c, dmod0_a, d_mu = _rowwise(
        shift_bwd, "shift_mix_bwd", seq=seq, tm=tm, rows=[x2, dx2_a] + dmix_list + [saved0[4]],
        prev8=[x2], next8=dmix_list, bvecs=[mod1, mod0], params=[mu_full],
        out_rows=[(D, F32), (D, BF16)], out_bacc=[(N_MOD, D), (N_MOD, D)], out_pacc=[(6, D)])
    dmod1 = dmod1 + dmod1_c

    dx0_a, dmix0, dmod0, dw1_0, dw2_0 = _mlp_bwd(seq, tm, saved0, x0, mix0, mod0, w1_l[0], w2_l[0], dx2, "0",
                                                 res2_grads=(dff0, dmod0_a))
    dz = _matmul(dmix0, a_wout, mode="nt", name="sgu_out_dx")
    d_a_wout = _matmul(z, dmix0, mode="tn", name="sgu_out_dw")
    du, dvn, d_ws, d_bias = _sgu_bwd(u, vn, ws, bias, dz)
    duvp, d_a_lng, d_a_lnb = _rw_bwd(_f_sgu_pre, "sgu_pre_bwd", seq=seq, tm=tm, rows=[uvp],
                                     params=[a_ln_g, a_ln_b], cts=[du, dvn], need_rows=[0], dtypes=[BF16])
    dh_a = _matmul(duvp, a_win, mode="nt", name="sgu_in_dx")
    d_a_win = _matmul(h_a, duvp, mode="tn", name="sgu_in_dw", out_shards=N_CHIPS)
    grad_x, dmod0_c = _rw_bwd(_f_norm1, "norm1_a_bwd", seq=seq, tm=tm, rows=[x0], bvecs=[mod0], cts=[dh_a],
                              need_rows=[0], extra=dx0_a)
    dmod0 = dmod0 + dmod0_c

    dmod_blk = _pad_rows(jnp.concatenate([dmod0.reshape(nbat, -1), dmod1.reshape(nbat, -1)], axis=1), 8)
    dmod_all = _all_gather8(dmod_blk).reshape(N_DEV, 8, 2, N_MOD * D)[:, :nbat].reshape(N_DEV * nbat, 2, N_MOD * D)
    g_ada_w, g_ada_b = [], []
    for i in range(2):
        cols = lax.dynamic_slice_in_dim(dmod_all[:, i], chip * n_ada, n_ada, axis=1)
        g_ada_w.append(_matmul(cond_all, cols, mode="tn", name="ada_dw_%d" % i))
    (g_ada_b_all,) = _small(lambda t: (jnp.sum(t, axis=0),), "ada_db", [dmod_all], [(2, N_MOD * D)])
    grads = {"ada_w": jnp.stack(g_ada_w), "ada_b": g_ada_b_all}

    rep = _pad_rows(jnp.concatenate([
        d_a_lng, d_a_lnb, jnp.sum(d_bias, axis=-1).reshape(1, D), d_rk, d_final_g,
        jnp.zeros((3, D), F32), d_ws.reshape(-1, D)], axis=0), 8)
    rep_rows = rep.shape[0]
    rep_all = _all_gather8(rep)
    (rep_sum,) = _small(lambda t: (functools.reduce(lambda p, q: p + q,
                                                     [t[j * rep_rows:(j + 1) * rep_rows] for j in range(N_DEV)]),),
                        "replicated_sum", [rep_all], [(rep_rows, D)])
    grads["a_ln_g"] = rep_sum[0:1]
    grads["a_ln_b"] = rep_sum[1:2]
    grads["a_b_s"] = rep_sum[2:3].reshape(a_b_s.shape)
    grads["b_r_k"] = rep_sum[3:4].reshape(b_r_k.shape)
    grads["final_g"] = rep_sum[4].reshape(final_g.shape)
    grads["a_w_s"] = rep_sum[8:8 + GROUPS * CHUNK * LANES // D].reshape(a_w_s.shape)

    vec_grads = jnp.concatenate([d_mu, d_w0, d_a0, d_kk, d_ka, d_lng, d_lnb], axis=0)
    packed = {
        "mlp_w1_0": dw1_0, "mlp_w1_1": dw1_1,
        "mlp_w2_0": dw2_0.reshape(N_CHIPS, -1, D), "mlp_w2_1": dw2_1.reshape(N_CHIPS, -1, D),
        "a_w_in": d_a_win, "a_w_out": d_a_wout.reshape(N_CHIPS, -1, D),
        "b_w_in": _shard_cols(d_b_win), "b_w_out": d_b_wout.reshape(N_CHIPS, -1, D),
        "b_w1": d_w1p[:, :lora_w].reshape(N_CHIPS, -1, lora_w), "b_w2": _shard_cols(d_w2p[:lora_w]),
        "b_a1": d_a1p[:, :lora_w].reshape(N_CHIPS, -1, lora_w), "b_a2": _shard_cols(d_a2p[:lora_w]),
        "b_g1": d_g1p[:, :lora_g].reshape(N_CHIPS, -1, lora_g), "b_g2": _shard_cols(d_g2p[:lora_g]),
    }
    pieces = [_rows_of(packed[name]) for name, _ in big] + [_pad_rows(_rows_of(_shard_cols(vec_grads)), 8)]
    used = sum(p.shape[1] for p in pieces)
    pieces.append(jnp.zeros((N_CHIPS, (-used) % 2016, ROW_W), F32))
    g_pack = jnp.concatenate(pieces, axis=1)
    g_red = _reduce_scatter(g_pack)
    for name, _ in big:
        start, rows_k, shape = offs[name]
        grads[name] = g_red[start:start + rows_k].reshape(shape)
    vec_red = g_red[n_big_rows:n_big_rows + vec_rows.shape[0]].reshape(n_vec, D // N_CHIPS)
    grads["b_mu"] = vec_red[0:6].reshape(b_mu.shape)
    for j, name in enumerate(vec_names):
        grads[name] = vec_red[6 + j:7 + j].reshape(weights[name].shape)
    for base in ("mlp_w1", "mlp_w2"):
        grads[base] = jnp.stack([grads.pop(base + "_0"), grads.pop(base + "_1")])
    for name in ("a_w_in", "a_w_out", "b_w_in", "b_w_out", "b_w1", "b_w2", "b_a1", "b_a2", "b_g1", "b_g2"):
        grads[name] = grads[name].reshape(weights[name].shape)

    deltas, new_m, new_v = {}, {}, {}
    for name in order:
        gr = grads[name].reshape(weights[name].shape)
        grads[name] = gr
        deltas[name], new_m[name], new_v[name] = _elementwise(
            _adamw, "adamw_" + name, [weights[name], gr, moms[name][0], moms[name][1]], 3)

    return (loss, grad_x.reshape(x.shape), *[grads[k] for k in order], *[deltas[k] for k in order],
            *[new_m[k] for k in order], *[new_v[k] for k in order])
```

```python
import functools

import jax
import jax.numpy as jnp
from jax import lax
from jax.experimental import pallas as pl
from jax.experimental.pallas import tpu as pltpu

F32 = jnp.float32
BF16 = jnp.bfloat16
MESH = pl.DeviceIdType.MESH
AXES = ("x", "y", "c")

D = 1024
N_MOD = 6
HEAD = 64
CHUNK = 128
GROUPS = 8
LANES = 128
ROW_W = 1024
N_CHIPS = 4
N_DEV = 8

RMS_EPS = 1e-6
LN_EPS = 1e-5
GN_EPS = HEAD * 1e-5
L2_EPS = 1e-12

ADAM_LR = 0.001
ADAM_B1 = 0.9
ADAM_B2 = 0.999
ADAM_EPS = 1e-08
ADAM_WD = 0.01
ADAM_STEP = 10

VMEM_LIMIT_V7X = 56 * 1024 * 1024
HIGHEST = lax.Precision.HIGHEST


def _params(sem=None):
    return pltpu.CompilerParams(dimension_semantics=sem, vmem_limit_bytes=VMEM_LIMIT_V7X)


def _tile(dim, target):
    if dim <= target:
        return dim
    for cand in range(target, 0, -LANES):
        if dim % cand == 0:
            return cand
    raise ValueError((dim, target))


def _matmul(a, b, *, mode, name, out_dtype=F32, out_shards=1, tm=1024, tn=1024, tk=4096):
    if mode == "nn":
        (m, k), (k2, n) = a.shape, b.shape
    elif mode == "nt":
        (m, k), (n, k2) = a.shape, b.shape
    else:
        (k, m), (k2, n) = a.shape, b.shape
    assert k == k2, (name, a.shape, b.shape)
    n_sh = n // out_shards
    tm, tn, tk = _tile(m, tm), _tile(n_sh, tn), _tile(k, tk)
    nk = k // tk
    nb = n_sh // tn
    use_scratch = nk > 1 and out_dtype != F32

    if mode == "tn":
        a_spec = pl.BlockSpec((tk, tm), lambda i, j, kk: (kk, i))
    else:
        a_spec = pl.BlockSpec((tm, tk), lambda i, j, kk: (i, kk))
    if mode == "nt":
        b_spec = pl.BlockSpec((tn, tk), lambda i, j, kk: (j, kk))
    else:
        b_spec = pl.BlockSpec((tk, tn), lambda i, j, kk: (kk, j))
    if out_shards == 1:
        out_shape = jax.ShapeDtypeStruct((m, n), out_dtype)
        o_spec = pl.BlockSpec((tm, tn), lambda i, j, kk: (i, j))
    else:
        out_shape = jax.ShapeDtypeStruct((out_shards, m, n_sh), out_dtype)
        o_spec = pl.BlockSpec((None, tm, tn), lambda i, j, kk: (j // nb, i, j % nb))

    def body(a_ref, b_ref, o_ref, *scratch):
        kk = pl.program_id(2)
        av = a_ref[...].astype(BF16)
        bv = b_ref[...].astype(BF16)
        if mode == "nn":
            dims = (((1,), (0,)), ((), ()))
        elif mode == "nt":
            dims = (((1,), (1,)), ((), ()))
        else:
            dims = (((0,), (0,)), ((), ()))
        part = lax.dot_general(av, bv, dims, preferred_element_type=F32)
        if nk == 1:
            o_ref[...] = part.astype(o_ref.dtype)
            return
        acc_ref = scratch[0] if use_scratch else o_ref

        @pl.when(kk == 0)
        def _():
            acc_ref[...] = part

        @pl.when(kk != 0)
        def _():
            acc_ref[...] += part

        if use_scratch:
            @pl.when(kk == nk - 1)
            def _():
                o_ref[...] = acc_ref[...].astype(o_ref.dtype)

    return pl.pallas_call(
        body, name=name, out_shape=out_shape,
        grid=(m // tm, n // tn, nk),
        in_specs=[a_spec, b_spec], out_specs=o_spec,
        scratch_shapes=[pltpu.VMEM((tm, tn), F32)] if use_scratch else [],
        compiler_params=_params(("parallel", "parallel", "arbitrary")),
    )(a, b)


def _matmul_ep(a, b, extras, epilogue, out_dtypes, *, mode, name, tm=1024, tn=1024, tk=2048):
    if mode == "nn":
        (m, k), (k2, n) = a.shape, b.shape
    else:
        (m, k), (n, k2) = a.shape, b.shape
    assert k == k2 and mode in ("nn", "nt"), (name, a.shape, b.shape)
    tm, tn, tk = _tile(m, tm), _tile(n, tn), _tile(k, tk)
    nk = k // tk
    n_ex, n_out = len(extras), len(out_dtypes)

    def body(a_ref, b_ref, *rest):
        extra_refs, out_refs = rest[:n_ex], rest[n_ex:n_ex + n_out]
        kk = pl.program_id(2)
        dims = (((1,), (0,)), ((), ())) if mode == "nn" else (((1,), (1,)), ((), ()))
        part = lax.dot_general(a_ref[...].astype(BF16), b_ref[...].astype(BF16), dims,
                               preferred_element_type=F32)

        def finish(acc):
            for ref, val in zip(out_refs, epilogue(acc, *[r[...] for r in extra_refs])):
                ref[...] = val.astype(ref.dtype)

        if nk == 1:
            finish(part)
            return
        acc_ref = rest[-1]

        @pl.when(kk == 0)
        def _():
            acc_ref[...] = part

        @pl.when(kk != 0)
        def _():
            acc_ref[...] += part

        @pl.when(kk == nk - 1)
        def _():
            finish(acc_ref[...])

    a_spec = pl.BlockSpec((tm, tk), lambda i, j, kk: (i, kk))
    b_spec = (pl.BlockSpec((tk, tn), lambda i, j, kk: (kk, j)) if mode == "nn"
              else pl.BlockSpec((tn, tk), lambda i, j, kk: (j, kk)))
    o_spec = pl.BlockSpec((tm, tn), lambda i, j, kk: (i, j))
    res = pl.pallas_call(
        body, name=name, out_shape=[jax.ShapeDtypeStruct((m, n), dt) for dt in out_dtypes],
        grid=(m // tm, n // tn, nk),
        in_specs=[a_spec, b_spec] + [o_spec] * n_ex, out_specs=[o_spec] * n_out,
        scratch_shapes=[pltpu.VMEM((tm, tn), F32)] if nk > 1 else [],
        compiler_params=_params(("parallel", "parallel", "arbitrary")),
    )(a, b, *extras)
    return list(res)


class _Ctx:
    def __init__(self, first, last):
        self.first = first
        self.last = last


def _rowwise(fn, name, *, seq, tm, rows=(), prev8=(), next8=(), bvecs=(), params=(),
             out_rows=(), out_bacc=(), out_pacc=()):
    n = rows[0].shape[0]
    tm = min(tm, seq)
    assert n % seq == 0 and seq % tm == 0 and tm % 8 == 0
    tpb = seq // tm
    nt = n // tm
    nbat = n // seq
    r8 = tm // 8
    counts = [len(rows), len(prev8), len(next8), len(bvecs), len(params)]
    n_in = sum(counts)

    def body(*refs):
        i = pl.program_id(0)
        first = (i % tpb) == 0
        last = (i % tpb) == (tpb - 1)
        vals = [r[...] for r in refs[:n_in]]
        groups, pos = [], 0
        for cnt in counts:
            groups.append(vals[pos:pos + cnt])
            pos += cnt
        ro, bo, po = fn(_Ctx(first, last), *groups)
        outs = refs[n_in:]
        assert len(ro) == len(out_rows) and len(bo) == len(out_bacc) and len(po) == len(out_pacc)
        for ref, val in zip(outs[:len(ro)], ro):
            ref[...] = val.astype(ref.dtype)
        for ref, val in zip(outs[len(ro):len(ro) + len(bo)], bo):
            @pl.when(first)
            def _(ref=ref, val=val):
                ref[...] = val

            @pl.when(jnp.logical_not(first))
            def _(ref=ref, val=val):
                ref[...] += val
        for ref, val in zip(outs[len(ro) + len(bo):], po):
            @pl.when(i == 0)
            def _(ref=ref, val=val):
                ref[...] = val

            @pl.when(i != 0)
            def _(ref=ref, val=val):
                ref[...] += val

    in_specs = []
    for arr in rows:
        in_specs.append(pl.BlockSpec((tm, arr.shape[1]), lambda i: (i, 0)))
    for arr in prev8:
        in_specs.append(pl.BlockSpec((8, arr.shape[1]), lambda i: (jnp.maximum(i * r8 - 1, 0), 0)))
    for arr in next8:
        in_specs.append(pl.BlockSpec((8, arr.shape[1]), lambda i: (jnp.minimum((i + 1) * r8, n // 8 - 1), 0)))
    for arr in bvecs:
        in_specs.append(pl.BlockSpec((None,) + arr.shape[1:], lambda i: (i // tpb, 0, 0)))
    for arr in params:
        in_specs.append(pl.BlockSpec(arr.shape, lambda i: (0, 0)))
    out_shape, out_specs = [], []
    for d, dt in out_rows:
        out_shape.append(jax.ShapeDtypeStruct((n, d), dt))
        out_specs.append(pl.BlockSpec((tm, d), lambda i: (i, 0)))
    for r, d in out_bacc:
        out_shape.append(jax.ShapeDtypeStruct((nbat, r, d), F32))
        out_specs.append(pl.BlockSpec((None, r, d), lambda i: (i // tpb, 0, 0)))
    for r, d in out_pacc:
        out_shape.append(jax.ShapeDtypeStruct((r, d), F32))
        out_specs.append(pl.BlockSpec((r, d), lambda i: (0, 0)))
    res = pl.pallas_call(
        body, name=name, out_shape=out_shape, grid=(nt,),
        in_specs=in_specs, out_specs=out_specs,
        compiler_params=_params(("arbitrary",)),
    )(*rows, *prev8, *next8, *bvecs, *params)
    return list(res)


def _rw_fwd(f, name, *, seq, tm, rows, bvecs=(), params=(), outs):
    def fn(ctx, rv, pv, nv, bv, pa):
        res = f(*[v.astype(F32) for v in rv], *bv, *pa)
        return list(res), [], []
    return _rowwise(fn, name, seq=seq, tm=tm, rows=rows, bvecs=bvecs, params=params, out_rows=outs)


def _rw_bwd(f, name, *, seq, tm, rows, bvecs=(), params=(), cts, need_rows, extra=None, dtypes=None):
    nr, nb, npar = len(rows), len(bvecs), len(params)
    all_rows = list(rows) + list(cts) + ([extra] if extra is not None else [])

    def fn(ctx, rv, pv, nv, bv, pa):
        prim = [v.astype(F32) for v in rv[:nr]]
        ct = tuple(v.astype(F32) for v in rv[nr:nr + len(cts)])
        _, vjp = jax.vjp(f, *prim, *bv, *pa)
        g = vjp(ct)
        d_rows = [g[j] for j in need_rows]
        if extra is not None:
            d_rows[0] = d_rows[0] + rv[-1].astype(F32)
        return d_rows, list(g[nr:nr + nb]), list(g[nr + nb:])

    return _rowwise(
        fn, name, seq=seq, tm=tm, rows=all_rows, bvecs=bvecs, params=params,
        out_rows=[(rows[j].shape[1], F32 if dtypes is None else dtypes[i]) for i, j in enumerate(need_rows)],
        out_bacc=[b.shape[1:] for b in bvecs], out_pacc=[p.shape for p in params])


def _small(fn, name, arrays, out_shapes):
    def body(*refs):
        res = fn(*[r[...] for r in refs[:len(arrays)]])
        for ref, val in zip(refs[len(arrays):], res):
            ref[...] = val.astype(ref.dtype)

    vm = pl.BlockSpec(memory_space=pltpu.VMEM)
    res = pl.pallas_call(
        body, name=name,
        out_shape=[jax.ShapeDtypeStruct(s, F32) for s in out_shapes],
        in_specs=[vm] * len(arrays), out_specs=[vm] * len(out_shapes),
        compiler_params=_params(),
    )(*arrays)
    return list(res)


def _elementwise(fn, name, arrays, n_out):
    shape = arrays[0].shape
    size = arrays[0].size
    if len(shape) >= 2 and shape[-1] % LANES == 0 and (size // shape[-1]) % 8 == 0:
        view = (size // shape[-1], shape[-1])
    elif size % ROW_W == 0 and (size // ROW_W) % 8 == 0:
        view = (size // ROW_W, ROW_W)
    else:
        view = (1, size) if len(shape) < 2 else (size // shape[-1], shape[-1])
    rows = view[0]
    tr = rows
    for cand in (256, 128, 64, 32, 16, 8):
        if rows > cand and rows % cand == 0:
            tr = cand
            break

    def body(*refs):
        res = fn(*[r[...] for r in refs[:len(arrays)]])
        for ref, val in zip(refs[len(arrays):], res):
            ref[...] = val

    spec = pl.BlockSpec((tr, view[1]), lambda i: (i, 0))
    res = pl.pallas_call(
        body, name=name,
        out_shape=[jax.ShapeDtypeStruct(view, F32)] * n_out,
        grid=(rows // tr,), in_specs=[spec] * len(arrays), out_specs=[spec] * n_out,
        compiler_params=_params(("parallel",)),
    )(*[a.reshape(view) for a in arrays])
    return [r.reshape(shape) for r in res]


def _rms(x):
    return x * lax.rsqrt(jnp.mean(x * x, axis=-1, keepdims=True) + RMS_EPS)


def _rmsmod(x, sh, sc):
    return _rms(x) * (1.0 + sc) + sh


def _f_norm1(x, mod):
    return (_rmsmod(x, mod[0:1], mod[1:2]),)


def _f_sgu_pre(uvp, ln_g, ln_b):
    uv = 0.5 * uvp * (1.0 + lax.erf(uvp * (2.0 ** -0.5)))
    u = uv[:, :D]
    v = uv[:, D:]
    mu = jnp.mean(v, axis=-1, keepdims=True)
    vc = v - mu
    var = jnp.mean(vc * vc, axis=-1, keepdims=True)
    return u, vc * lax.rsqrt(var + LN_EPS) * ln_g + ln_b


def _f_res_norm2(x, mix, mod):
    x1 = x + mod[2:3] * mix
    return x1, _rmsmod(x1, mod[3:4], mod[4:5])


def _f_relu2(p):
    r = jnp.maximum(p, 0.0)
    return (r * r,)


def _f_res2(x1, ff, mod):
    return (x1 + mod[5:6] * ff,)


def _f_lora_act(t1, t3):
    return jnp.tanh(t1), 1.0 / (1.0 + jnp.exp(-t3))


def _f_loss(x, tgt, fg):
    err = _rms(x) * fg - tgt
    return 0.5 * jnp.sum(jnp.mean(err * err, axis=-1))


def _shift_mix(ctx, x, xprev8, mod, mu):
    h = _rmsmod(x, mod[0:1], mod[1:2])
    hprev = _rmsmod(xprev8, mod[0:1], mod[1:2])[7:8]
    hprev = jnp.where(ctx.first, jnp.zeros_like(hprev), hprev)
    rowid = lax.broadcasted_iota(jnp.int32, h.shape, 0)
    hp = jnp.where(rowid == 0, hprev, pltpu.roll(h, 1, 0))
    xx = hp - h
    return h, xx, [h + xx * mu[j:j + 1] for j in range(6)]


def _split_bf16(t, parts):
    out, rest = [], t.astype(F32)
    for _ in range(parts):
        piece = rest.astype(BF16)
        out.append(piece)
        rest = rest - piece.astype(F32)
    return out


def _make_mm(na, nb, ct_pieces=1, saved_pieces=1):
    def raw(a, b, pa, pb):
        if pa == 0:
            return jnp.dot(a, b, precision=HIGHEST, preferred_element_type=F32)
        acc = None
        bs = _split_bf16(b, pb)
        for i, ai in enumerate(_split_bf16(a, pa)):
            for j, bj in enumerate(bs):
                if i + j < max(pa, pb):
                    term = jnp.dot(ai, bj, preferred_element_type=F32)
                    acc = term if acc is None else acc + term
        return acc

    @jax.custom_vjp
    def mm(a, b):
        return raw(a, b, na, nb)

    def fwd(a, b):
        return raw(a, b, na, nb), (a, b)

    def bwd(res, ct):
        a, b = res
        if na == 0:
            return raw(ct, b.T, 0, 0), raw(a.T, ct, 0, 0)
        return raw(ct, b.T, ct_pieces, saved_pieces), raw(a.T, ct, saved_pieces, ct_pieces)

    mm.defvjp(fwd, bwd)
    return mm


class _WkvMms:
    def __init__(self, head_sum, cum, score, square, apply, out, state):
        self.head_sum, self.cum, self.score = head_sum, cum, score
        self.square, self.apply, self.out, self.state = square, apply, out, state


def _wkv_mms(cfg):
    table = {"x": (0, 0), "1": (1, 1), "2": (2, 2), "3": (3, 3), "a": (2, 1), "b": (1, 2)}
    hs, cu, sc_, sq, ap, ou, st = [table[ch] for ch in cfg]
    return _WkvMms(_make_mm(hs[0], 1) if hs[0] else _make_mm(0, 0),
                   _make_mm(1, cu[1], ct_pieces=2) if cu[0] else _make_mm(0, 0),
                   _make_mm(*sc_, saved_pieces=2), _make_mm(*sq), _make_mm(*ap, saved_pieces=2),
                   _make_mm(*ou), _make_mm(*st))


WKV_PRECISION = "2221b11"


SGU_CHUNKS_PER_STEP = 4


def _sgu_tile(mm, u, vn, ws, bias):
    row = lax.broadcasted_iota(jnp.int32, (CHUNK, CHUNK), 0)
    col = lax.broadcasted_iota(jnp.int32, (CHUNK, CHUNK), 1)
    wm = [jnp.where(col <= row, w, 0.0) for w in ws]
    out_rows = []
    for ch in range(u.shape[0] // CHUNK):
        rs = slice(ch * CHUNK, (ch + 1) * CHUNK)
        out_rows.append(jnp.concatenate(
            [mm(wm[g], vn[rs, g * LANES:(g + 1) * LANES]) + bias[g] for g in range(GROUPS)], axis=1))
    return u * jnp.concatenate(out_rows, axis=0)


def _sgu_mixer_tile(mm, uvp, ln_g, ln_b, ws, bias):
    u, vn = _f_sgu_pre(uvp, ln_g, ln_b)
    return _sgu_tile(mm, u, vn, ws, bias)


def _sgu_fwd(uvp, ln_g, ln_b, ws, bias):
    n = uvp.shape[0]
    rows = CHUNK * SGU_CHUNKS_PER_STEP
    mm = _make_mm(1, 1)

    def body(x_ref, g_ref, b2_ref, w_ref, b_ref, z_ref):
        ws_l = [w_ref[g] for g in range(GROUPS)]
        bias_l = [b_ref[g] for g in range(GROUPS)]
        z_ref[...] = _sgu_mixer_tile(mm, x_ref[...], g_ref[...], b2_ref[...], ws_l, bias_l).astype(z_ref.dtype)

    tok_in = pl.BlockSpec((rows, 2 * D), lambda i: (i, 0))
    tok = pl.BlockSpec((rows, D), lambda i: (i, 0))
    vec = pl.BlockSpec((1, D), lambda i: (0, 0))
    grp = pl.BlockSpec((GROUPS, CHUNK, LANES), lambda i: (0, 0, 0))
    return pl.pallas_call(
        body, name="sgu_fwd", out_shape=jax.ShapeDtypeStruct((n, D), BF16),
        grid=(n // rows,), in_specs=[tok_in, vec, vec, grp, grp], out_specs=tok,
        compiler_params=_params(("parallel",)),
    )(uvp, ln_g, ln_b, ws, bias)


def _sgu_bwd(uvp, ln_g, ln_b, ws, bias, dz):
    n = uvp.shape[0]
    rows = CHUNK * SGU_CHUNKS_PER_STEP
    mm = _make_mm(1, 1)

    def body(x_ref, g_ref, b2_ref, w_ref, b_ref, dz_ref, dx_ref, dg_ref, db2_ref, dw_ref, db_ref):
        i = pl.program_id(0)
        ws_l = [w_ref[g] for g in range(GROUPS)]
        bias_l = [b_ref[g] for g in range(GROUPS)]
        _, vjp = jax.vjp(functools.partial(_sgu_mixer_tile, mm), x_ref[...], g_ref[...], b2_ref[...], ws_l, bias_l)
        dx, dg, db2, dw, db = vjp(dz_ref[...].astype(F32))
        dx_ref[...] = dx.astype(dx_ref.dtype)

        @pl.when(i == 0)
        def _():
            dg_ref[...] = dg
            db2_ref[...] = db2
            for g in range(GROUPS):
                dw_ref[g] = dw[g]
                db_ref[g] = db[g]

        @pl.when(i != 0)
        def _():
            dg_ref[...] += dg
            db2_ref[...] += db2
            for g in range(GROUPS):
                dw_ref[g] += dw[g]
                db_ref[g] += db[g]

    tok_in = pl.BlockSpec((rows, 2 * D), lambda i: (i, 0))
    tok = pl.BlockSpec((rows, D), lambda i: (i, 0))
    vec = pl.BlockSpec((1, D), lambda i: (0, 0))
    grp = pl.BlockSpec((GROUPS, CHUNK, LANES), lambda i: (0, 0, 0))
    return pl.pallas_call(
        body, name="sgu_bwd",
        out_shape=[jax.ShapeDtypeStruct((n, 2 * D), BF16), jax.ShapeDtypeStruct((1, D), F32),
                   jax.ShapeDtypeStruct((1, D), F32),
                   jax.ShapeDtypeStruct((GROUPS, CHUNK, LANES), F32),
                   jax.ShapeDtypeStruct((GROUPS, CHUNK, LANES), F32)],
        grid=(n // rows,), in_specs=[tok_in, vec, vec, grp, grp, tok],
        out_specs=[tok_in, vec, vec, grp, grp],
        compiler_params=_params(("arbitrary",)),
    )(uvp, ln_g, ln_b, ws, bias, dz)


def _chains(t):
    return [t[i] for i in range(t.shape[0])] if t.ndim == 3 else [t]


def _bmm(mm, a, b):
    if a.ndim == 2 and b.ndim == 2:
        return mm(a, b)
    ca, cb = _chains(a), _chains(b)
    n = max(len(ca), len(cb))
    return jnp.stack([mm(ca[i % len(ca)], cb[i % len(cb)]) for i in range(n)])


def _bt(a):
    return a.T if a.ndim == 2 else jnp.stack([t.T for t in _chains(a)])


def _wkv_chunk(mms, s0, r, k, v, wl, al, g, w0, a0, k_k, k_a, r_k, ln_g, ln_b):
    ln = CHUNK
    row = lax.broadcasted_iota(jnp.int32, (ln, ln), 0)
    col = lax.broadcasted_iota(jnp.int32, (ln, ln), 1)
    incl = (col <= row).astype(F32)
    strict = (col < row).astype(F32)
    same_head = ((row // HEAD) == (col // HEAD)).astype(F32)
    lane = lax.broadcasted_iota(jnp.int32, (1, LANES), 1)
    m_a = (lane < HEAD).astype(F32)
    m_b = 1.0 - m_a
    rowid = lax.broadcasted_iota(jnp.int32, (ln, LANES), 0)
    cat = jnp.concatenate

    def hsum(t):
        return _bmm(mms.head_sum, t, same_head)

    def pick_row(t, j):
        return jnp.sum(jnp.where(rowid == j, t, 0.0), axis=-2, keepdims=True)

    z = w0 + wl
    softplus_neg = jnp.maximum(-z, 0.0) + jnp.log(1.0 + jnp.exp(-jnp.abs(z)))
    lw = -jnp.exp(-softplus_neg - 0.5)
    a = 1.0 / (1.0 + jnp.exp(-(a0 + al)))
    kx = k * k_k
    kkn = kx / jnp.maximum(jnp.sqrt(hsum(kx * kx)), L2_EPS)
    kp = k * (1.0 + (a - 1.0) * k_a)
    aa = -kkn
    bb = kkn * a

    c = _bmm(mms.cum, incl, lw)
    c_mid = pick_row(c, ln // 2 - 1)
    ce = c - c_mid
    e_pos = jnp.exp(ce)
    e_neg = jnp.exp(-ce)
    at = aa * jnp.exp(ce - lw)
    bt = bb * e_neg
    kt = kp * e_neg
    rt = r * e_pos
    s0p = s0 * jnp.exp(c_mid)

    bk = cat([bt, kt], axis=-2)
    sc = _bmm(mms.score, cat([at * m_a, at * m_b, rt * m_a, rt * m_b], axis=-2), _bt(bk))
    ab_a, ak_a = sc[..., 0:ln, 0:ln] * strict, sc[..., 0:ln, ln:] * strict
    ab_b, ak_b = sc[..., ln:2 * ln, 0:ln] * strict, sc[..., ln:2 * ln, ln:] * strict
    incl2 = cat([incl, incl], axis=1)
    p_a = sc[..., 2 * ln:3 * ln, :] * incl2
    p_b = sc[..., 3 * ln:, :] * incl2

    base = _bmm(mms.score, cat([at, rt], axis=-2), _bt(s0p))
    rhs = base[..., :ln, :] + m_a * _bmm(mms.out, ak_a, v) + m_b * _bmm(mms.out, ak_b, v)

    pa, pb = ab_a, ab_b
    xa = rhs + _bmm(mms.apply, pa, rhs)
    xb = rhs + _bmm(mms.apply, pb, rhs)
    for _ in range(6):
        pa = _bmm(mms.square, pa, pa)
        pb = _bmm(mms.square, pb, pb)
        xa = xa + _bmm(mms.apply, pa, xa)
        xb = xb + _bmm(mms.apply, pb, xb)
    u = m_a * xa + m_b * xb
    uv = cat([u, v], axis=-2)
    y = base[..., ln:, :] + m_a * _bmm(mms.out, p_a, uv) + m_b * _bmm(mms.out, p_b, uv)
    s_new = (s0p + _bmm(mms.state, _bt(uv), bk)) * same_head * jnp.exp(pick_row(ce, ln - 1))

    mean = hsum(y) * (1.0 / HEAD)
    yc = y - mean
    var = hsum(yc * yc) * (1.0 / HEAD)
    yn = yc * lax.rsqrt(var + GN_EPS) * ln_g + ln_b
    bonus = hsum(r * kp * r_k) * v
    return (yn + bonus) * g, s_new


N_WKV_ROWS = 6
N_WKV_PAR = 7


WKV_PAIRS_PER_STEP = 2


def _to_chains(val, nbat, pp):
    if val.ndim == 2:
        return jnp.stack([val[:, q * LANES:(q + 1) * LANES] for _ in range(nbat) for q in range(pp)])
    return jnp.stack([val[b, :, q * LANES:(q + 1) * LANES] for b in range(nbat) for q in range(pp)])


def _wkv_fwd(seq, rows, pars):
    n = rows[0].shape[0]
    nbat, nch, npair, pp = n // seq, seq // CHUNK, D // LANES, WKV_PAIRS_PER_STEP
    chunk_fn = functools.partial(_wkv_chunk, _wkv_mms(WKV_PRECISION))

    def body(*refs):
        row_vals = [_to_chains(r[...], nbat, pp) for r in refs[:N_WKV_ROWS]]
        par_vals = [_to_chains(r[...], nbat, pp) for r in refs[N_WKV_ROWS:N_WKV_ROWS + N_WKV_PAR]]
        yo_ref, ck_ref, s_ref = refs[N_WKV_ROWS + N_WKV_PAR:]
        ch = pl.program_id(1)

        @pl.when(ch == 0)
        def _():
            s_ref[...] = jnp.zeros_like(s_ref)

        s0 = s_ref[...]
        yo, s_new = chunk_fn(s0, *row_vals, *par_vals)
        s_ref[...] = s_new
        for b in range(nbat):
            for q in range(pp):
                ck_ref[b, q] = s0[b * pp + q]
                yo_ref[b, :, q * LANES:(q + 1) * LANES] = yo[b * pp + q].astype(yo_ref.dtype)

    tok = pl.BlockSpec((nbat, CHUNK, pp * LANES), lambda p, ch: (0, ch, p))
    par = pl.BlockSpec((1, pp * LANES), lambda p, ch: (0, p))
    ck = pl.BlockSpec((nbat, pp, None, LANES, LANES), lambda p, ch: (0, p, ch, 0, 0))
    yo, ckpt = pl.pallas_call(
        body, name="wkv_fwd",
        out_shape=[jax.ShapeDtypeStruct((nbat, seq, D), BF16),
                   jax.ShapeDtypeStruct((nbat, npair, nch, LANES, LANES), F32)],
        grid=(npair // pp, nch),
        in_specs=[tok] * N_WKV_ROWS + [par] * N_WKV_PAR, out_specs=[tok, ck],
        scratch_shapes=[pltpu.VMEM((nbat * pp, LANES, LANES), F32)],
        compiler_params=_params(("parallel", "arbitrary")),
    )(*[t.reshape(nbat, seq, D) for t in rows], *pars)
    return yo.reshape(n, D), ckpt


def _wkv_bwd(seq, rows, pars, ckpt, dyo):
    n = rows[0].shape[0]
    nbat, nch, npair, pp = n // seq, seq // CHUNK, D // LANES, WKV_PAIRS_PER_STEP
    chunk_fn = functools.partial(_wkv_chunk, _wkv_mms(WKV_PRECISION))
    n_in = N_WKV_ROWS + N_WKV_PAR

    def body(*refs):
        row_vals = [_to_chains(r[...], nbat, pp) for r in refs[:N_WKV_ROWS]]
        par_vals = [_to_chains(r[...], nbat, pp) for r in refs[N_WKV_ROWS:n_in]]
        ck_ref, dyo_ref = refs[n_in:n_in + 2]
        d_rows = refs[n_in + 2:n_in + 2 + N_WKV_ROWS]
        d_pars = refs[n_in + 2 + N_WKV_ROWS:n_in + 2 + N_WKV_ROWS + N_WKV_PAR]
        ds_ref = refs[-1]
        ch = pl.program_id(1)

        @pl.when(ch == 0)
        def _():
            ds_ref[...] = jnp.zeros_like(ds_ref)

        s0 = jnp.stack([ck_ref[b, q] for b in range(nbat) for q in range(pp)])
        dyo_v = _to_chains(dyo_ref[...].astype(F32), nbat, pp)
        _, vjp = jax.vjp(chunk_fn, s0, *row_vals, *par_vals)
        grads = vjp((dyo_v, ds_ref[...]))
        ds_ref[...] = grads[0]
        for ref, val in zip(d_rows, grads[1:1 + N_WKV_ROWS]):
            for b in range(nbat):
                for q in range(pp):
                    ref[b, :, q * LANES:(q + 1) * LANES] = val[b * pp + q].astype(ref.dtype)
        for ref, val in zip(d_pars, grads[1 + N_WKV_ROWS:]):
            per_pair = [functools.reduce(lambda s, t: s + t, [val[b * pp + q] for b in range(nbat)])
                        for q in range(pp)]
            tot = jnp.concatenate(per_pair, axis=1)

            @pl.when(ch == 0)
            def _(ref=ref, tot=tot):
                ref[...] = tot

            @pl.when(ch != 0)
            def _(ref=ref, tot=tot):
                ref[...] += tot

    tok = pl.BlockSpec((nbat, CHUNK, pp * LANES), lambda p, ch: (0, nch - 1 - ch, p))
    par = pl.BlockSpec((1, pp * LANES), lambda p, ch: (0, p))
    ck = pl.BlockSpec((nbat, pp, None, LANES, LANES), lambda p, ch: (0, p, nch - 1 - ch, 0, 0))
    res = pl.pallas_call(
        body, name="wkv_bwd",
        out_shape=[jax.ShapeDtypeStruct((nbat, seq, D), BF16)] * N_WKV_ROWS
        + [jax.ShapeDtypeStruct((1, D), F32)] * N_WKV_PAR,
        grid=(npair // pp, nch),
        in_specs=[tok] * N_WKV_ROWS + [par] * N_WKV_PAR + [ck, tok],
        out_specs=[tok] * N_WKV_ROWS + [par] * N_WKV_PAR,
        scratch_shapes=[pltpu.VMEM((nbat * pp, LANES, LANES), F32)],
        compiler_params=_params(("parallel", "arbitrary")),
    )(*[t.reshape(nbat, seq, D) for t in rows], *pars, ckpt, dyo.reshape(nbat, seq, D))
    return [t.reshape(n, D) for t in res[:N_WKV_ROWS]] + list(res[N_WKV_ROWS:])


def _place():
    return lax.axis_index("x"), lax.axis_index("y"), lax.axis_index("c")


def _all_gather8(blk):
    m_per, n = blk.shape
    assert m_per % 8 == 0

    def body(x_ref, out_ref, send_sems, recv_sems, local_sem):
        x, y, c = _place()
        me, sibling = (x, y, c), (x, y, 1 - c)
        chips = [(1 - x, y), (x, 1 - y), (1 - x, 1 - y)]

        def rows(px, py, pc):
            return out_ref.at[pl.ds((4 * px + 2 * py + pc) * m_per, m_per), :]

        def copy(k, block, to, src=None):
            return pltpu.make_async_remote_copy(
                src_ref=rows(*block) if src is None else src, dst_ref=rows(*block),
                send_sem=send_sems.at[k], recv_sem=recv_sems.at[k],
                device_id=to, device_id_type=MESH)

        mine = pltpu.make_async_copy(x_ref, rows(*me), local_sem)
        mine.start()
        first = [copy(0, me, sibling, src=x_ref)]
        first += [copy(1 + j, me, (*chip, c), src=x_ref) for j, chip in enumerate(chips)]
        for cp in first:
            cp.start()
        passed = [copy(4 + j, (*chip, c), sibling) for j, chip in enumerate(chips)]
        for j, chip in enumerate(chips):
            copy(1 + j, (*chip, c), me).wait_recv()
            passed[j].start()
        copy(0, sibling, me).wait_recv()
        for j, chip in enumerate(chips):
            copy(4 + j, (*chip, 1 - c), me).wait_recv()
        for cp in first + passed:
            cp.wait_send()
        mine.wait()

    vm = pl.BlockSpec(memory_space=pltpu.VMEM)
    return pl.pallas_call(
        body, name="all_gather8_%dx%d" % (m_per, n),
        out_shape=jax.ShapeDtypeStruct((N_DEV * m_per, n), blk.dtype),
        in_specs=[vm], out_specs=vm,
        scratch_shapes=[pltpu.SemaphoreType.DMA((7,)), pltpu.SemaphoreType.DMA((7,)),
                        pltpu.SemaphoreType.DMA],
        compiler_params=_params(),
    )(blk)


def _own_slot(src, name):
    r, w = src.shape[-2:]
    tr = _tile(r, 1008)
    xi, yi, _ = _place()
    chip = jnp.reshape(2 * xi + yi, (1,)).astype(jnp.int32)

    def body(chip_ref, x_ref, o_ref):
        o_ref[...] = x_ref[...]

    if src.ndim == 2:
        in_spec = pl.BlockSpec((tr, w), lambda i, chip_ref: (i, 0))
    else:
        in_spec = pl.BlockSpec((None, tr, w), lambda i, chip_ref: (chip_ref[0], i, 0))
    return pl.pallas_call(
        body, name=name,
        out_shape=jax.ShapeDtypeStruct((N_CHIPS, r, w), src.dtype),
        grid_spec=pltpu.PrefetchScalarGridSpec(
            num_scalar_prefetch=1, grid=(r // tr,), in_specs=[in_spec],
            out_specs=pl.BlockSpec((None, tr, w), lambda i, chip_ref: (chip_ref[0], i, 0))),
        compiler_params=_params(("parallel",)),
    )(chip, src)


def _chip_all_gather(shard):
    r, w = shard.shape
    half = r // 2
    assert r % 2 == 0 and half % 16 == 0

    def body(x_ref, buf_ref, out_ref, send_sems, recv_sems):
        del buf_ref
        x, y, c = _place()
        sibling = (x, y, 1 - c)
        me_p = 2 * x + y
        chips = [(1 - x, y), (x, 1 - y), (1 - x, 1 - y)]

        def piece(p, h):
            return out_ref.at[p, pl.ds(h * half, half), :]

        def copy(k, p, h, to, src=None):
            return pltpu.make_async_remote_copy(
                src_ref=piece(p, h) if src is None else src, dst_ref=piece(p, h),
                send_sem=send_sems.at[k], recv_sem=recv_sems.at[k],
                device_id=to, device_id_type=MESH)

        my_half = x_ref.at[pl.ds(c * half, half), :]
        first = [copy(j, me_p, c, (*chip, c), src=my_half) for j, chip in enumerate(chips)]
        for cp in first:
            cp.start()
        passed = [copy(3 + j, 2 * chip[0] + chip[1], c, sibling) for j, chip in enumerate(chips)]
        for j, chip in enumerate(chips):
            copy(j, 2 * chip[0] + chip[1], c, sibling).wait_recv()
            passed[j].start()
        for j, chip in enumerate(chips):
            copy(3 + j, 2 * chip[0] + chip[1], 1 - c, sibling).wait_recv()
        for cp in first + passed:
            cp.wait_send()

    hbm = pl.BlockSpec(memory_space=pl.ANY)
    return pl.pallas_call(
        body, name="chip_all_gather",
        out_shape=jax.ShapeDtypeStruct((N_CHIPS, r, w), shard.dtype),
        in_specs=[hbm, hbm], out_specs=hbm, input_output_aliases={1: 0},
        scratch_shapes=[pltpu.SemaphoreType.DMA((6,)), pltpu.SemaphoreType.DMA((6,))],
        compiler_params=_params(),
    )(shard, _own_slot(shard, "gather_own_slot"))


def _sibling_swap_halves(g):
    _, r, w = g.shape
    half = r // 2

    def body(g_ref, t_ref, send_sem, recv_sem):
        x, y, c = _place()
        sibling = (x, y, 1 - c)
        cp = pltpu.make_async_remote_copy(
            src_ref=g_ref.at[:, pl.ds((1 - c) * half, half), :], dst_ref=t_ref,
            send_sem=send_sem, recv_sem=recv_sem, device_id=sibling, device_id_type=MESH)
        cp.start()
        cp.wait()

    hbm = pl.BlockSpec(memory_space=pl.ANY)
    return pl.pallas_call(
        body, name="rs_sibling_halves",
        out_shape=jax.ShapeDtypeStruct((N_CHIPS, half, w), g.dtype),
        in_specs=[hbm], out_specs=hbm,
        scratch_shapes=[pltpu.SemaphoreType.DMA, pltpu.SemaphoreType.DMA],
        compiler_params=_params(),
    )(g)


def _add_own_half(g, t):
    _, r, w = g.shape
    half = r // 2
    tr = 1008 if half % 1008 == 0 else 16
    assert half % tr == 0
    cidx = jnp.reshape(lax.axis_index("c"), (1,)).astype(jnp.int32)

    def body(c_ref, g_ref, t_ref, o_ref):
        o_ref[...] = (g_ref[...] + t_ref[...]).astype(o_ref.dtype)

    return pl.pallas_call(
        body, name="rs_add_halves",
        out_shape=jax.ShapeDtypeStruct((N_CHIPS, half, w), BF16),
        grid_spec=pltpu.PrefetchScalarGridSpec(
            num_scalar_prefetch=1, grid=(N_CHIPS, half // tr),
            in_specs=[pl.BlockSpec((None, None, tr, w), lambda p, i, c_ref: (p, c_ref[0], i, 0)),
                      pl.BlockSpec((None, tr, w), lambda p, i, c_ref: (p, i, 0))],
            out_specs=pl.BlockSpec((None, tr, w), lambda p, i, c_ref: (p, i, 0))),
        compiler_params=_params(("parallel", "parallel")),
    )(cidx, g.reshape(N_CHIPS, 2, half, w), t)


def _chip_exchange(h):
    _, hh, w = h.shape

    def body(h_ref, buf_ref, t_ref, send_sems, recv_sems):
        del buf_ref
        x, y, c = _place()
        me_p = 2 * x + y
        chips = [(1 - x, y), (x, 1 - y), (1 - x, 1 - y)]
        cps = []
        for j, chip in enumerate(chips):
            q = 2 * chip[0] + chip[1]
            cps.append(pltpu.make_async_remote_copy(
                src_ref=h_ref.at[q], dst_ref=t_ref.at[me_p],
                send_sem=send_sems.at[j], recv_sem=recv_sems.at[j],
                device_id=(*chip, c), device_id_type=MESH))
        for cp in cps:
            cp.start()
        for j, chip in enumerate(chips):
            q = 2 * chip[0] + chip[1]
            pltpu.make_async_remote_copy(
                src_ref=h_ref.at[q], dst_ref=t_ref.at[q],
                send_sem=send_sems.at[j], recv_sem=recv_sems.at[j],
                device_id=(*chip, c), device_id_type=MESH).wait_recv()
        for cp in cps:
            cp.wait_send()

    hbm = pl.BlockSpec(memory_space=pl.ANY)
    return pl.pallas_call(
        body, name="rs_chip_exchange",
        out_shape=jax.ShapeDtypeStruct(h.shape, h.dtype),
        in_specs=[hbm, hbm], out_specs=hbm, input_output_aliases={1: 0},
        scratch_shapes=[pltpu.SemaphoreType.DMA((3,)), pltpu.SemaphoreType.DMA((3,))],
        compiler_params=_params(),
    )(h, _own_slot(h, "rs_own_slot"))


def _sum_slots(t):
    _, hh, w = t.shape
    tr = 1008 if hh % 1008 == 0 else 16
    assert hh % tr == 0
    nblk = hh // tr
    cidx = jnp.reshape(lax.axis_index("c"), (1,)).astype(jnp.int32)

    def body(c_ref, t_ref, o_ref):
        s0, s1, s2, s3 = [t_ref[j].astype(F32) for j in range(N_CHIPS)]
        o_ref[...] = ((s0 + s1) + s2) + s3

    return pl.pallas_call(
        body, name="rs_sum_slots", out_shape=jax.ShapeDtypeStruct((2 * hh, w), F32),
        grid_spec=pltpu.PrefetchScalarGridSpec(
            num_scalar_prefetch=1, grid=(nblk,),
            in_specs=[pl.BlockSpec((N_CHIPS, tr, w), lambda i, c_ref: (0, i, 0))],
            out_specs=pl.BlockSpec((tr, w), lambda i, c_ref: (c_ref[0] * nblk + i, 0))),
        compiler_params=_params(("parallel",)),
    )(cidx, t)


def _sibling_join_halves(s):
    h2, w = s.shape
    hh = h2 // 2

    def body(s_ref, o_ref, send_sem, recv_sem):
        del s_ref
        x, y, c = _place()
        sibling = (x, y, 1 - c)
        cp = pltpu.make_async_remote_copy(
            src_ref=o_ref.at[pl.ds(c * hh, hh), :], dst_ref=o_ref.at[pl.ds(c * hh, hh), :],
            send_sem=send_sem, recv_sem=recv_sem, device_id=sibling, device_id_type=MESH)
        cp.start()
        pltpu.make_async_remote_copy(
            src_ref=o_ref.at[pl.ds((1 - c) * hh, hh), :], dst_ref=o_ref.at[pl.ds((1 - c) * hh, hh), :],
            send_sem=send_sem, recv_sem=recv_sem, device_id=sibling, device_id_type=MESH).wait_recv()
        cp.wait_send()

    hbm = pl.BlockSpec(memory_space=pl.ANY)
    return pl.pallas_call(
        body, name="rs_sibling_join",
        out_shape=jax.ShapeDtypeStruct(s.shape, s.dtype),
        in_specs=[hbm], out_specs=hbm, input_output_aliases={0: 0},
        scratch_shapes=[pltpu.SemaphoreType.DMA, pltpu.SemaphoreType.DMA],
        compiler_params=_params(),
    )(s)


def _reduce_scatter(g):
    h = _add_own_half(g, _sibling_swap_halves(g))
    return _sibling_join_halves(_sum_slots(_chip_exchange(h)))


def _unshard_cols(piece):
    p, k, n = piece.shape
    return jnp.transpose(piece, (1, 0, 2)).reshape(k, p * n)


def _shard_cols(full):
    k, n4 = full.shape
    return jnp.transpose(full.reshape(k, N_CHIPS, n4 // N_CHIPS), (1, 0, 2))


def _rows_of(piece):
    return piece.reshape(N_CHIPS, -1, ROW_W)


def _pad_rows(a, mult):
    pad = (-a.shape[-2]) % mult
    if pad == 0:
        return a
    widths = [(0, 0)] * (a.ndim - 2) + [(0, pad), (0, 0)]
    return jnp.pad(a, widths)


def _adamw(w, g, m, v):
    m2 = ADAM_B1 * m + (1.0 - ADAM_B1) * g
    v2 = ADAM_B2 * v + (1.0 - ADAM_B2) * (g * g)
    m_hat = m2 / (1.0 - ADAM_B1 ** ADAM_STEP)
    v_hat = v2 / (1.0 - ADAM_B2 ** ADAM_STEP)
    delta = -ADAM_LR * (m_hat / (jnp.sqrt(v_hat) + ADAM_EPS) + ADAM_WD * w)
    return delta, m2, v2


def _mlp_fwd(seq, tm, x_in, mix, mod, w1, w2, tag, residual=True):
    x1, h2 = _rw_fwd(_f_res_norm2, "res_norm2_" + tag, seq=seq, tm=tm, rows=[x_in, mix], bvecs=[mod],
                     outs=[(D, F32), (D, BF16)])
    p, f = _matmul_ep(h2, w1, [], lambda acc: (acc, _f_relu2(acc)[0]), [F32, BF16],
                      mode="nn", name="mlp_up_" + tag)
    ff = _matmul(f, w2, mode="nn", name="mlp_down_" + tag)
    x2 = None
    if residual:
        (x2,) = _rw_fwd(_f_res2, "res2_" + tag, seq=seq, tm=tm, rows=[x1, ff], bvecs=[mod], outs=[(D, F32)])
    return x2, (x1, h2, p, f, ff)


def _mlp_bwd(seq, tm, saved, x_in, mix, mod, w1, w2, dx2, tag, res2_grads=None):
    x1, h2, p, f, ff = saved
    if res2_grads is None:
        dff, dmod_a = _rw_bwd(_f_res2, "res2_bwd_" + tag, seq=seq, tm=tm, rows=[x1, ff], bvecs=[mod],
                              cts=[dx2], need_rows=[1], dtypes=[BF16])
    else:
        dff, dmod_a = res2_grads
    (dp,) = _matmul_ep(dff, w2, [p], lambda acc, pt: (2.0 * jnp.maximum(pt, 0.0) * acc,), [BF16],
                       mode="nt", name="mlp_down_dx_" + tag)
    dw2 = _matmul(f, dff, mode="tn", name="mlp_down_dw_" + tag)
    dh2 = _matmul(dp, w1, mode="nt", name="mlp_up_dx_" + tag)
    dw1 = _matmul(h2, dp, mode="tn", name="mlp_up_dw_" + tag, out_shards=N_CHIPS)
    dx_in, dmix, dmod_b = _rw_bwd(_f_res_norm2, "res_norm2_bwd_" + tag, seq=seq, tm=tm, rows=[x_in, mix],
                                  bvecs=[mod], cts=[dx2, dh2], need_rows=[0, 1], dtypes=[F32, BF16])
    return dx_in, dmix, dmod_a + dmod_b, dw1, dw2


def kernel(x, c, ada_w, ada_b, mlp_w1, mlp_w2, a_w_in, a_ln_g, a_ln_b, a_w_s, a_b_s, a_w_out, b_mu, b_w_in, b_w0, b_w1, b_w2, b_a0, b_a1, b_a2, b_g1, b_g2, b_k_k, b_k_a, b_r_k, b_ln_g, b_ln_b, b_w_out, final_g, loss_target, m_ada_w, m_ada_b, m_mlp_w1, m_mlp_w2, m_a_w_in, m_a_ln_g, m_a_ln_b, m_a_w_s, m_a_b_s, m_a_w_out, m_b_mu, m_b_w_in, m_b_w0, m_b_w1, m_b_w2, m_b_a0, m_b_a1, m_b_a2, m_b_g1, m_b_g2, m_b_k_k, m_b_k_a, m_b_r_k, m_b_ln_g, m_b_ln_b, m_b_w_out, m_final_g, v_ada_w, v_ada_b, v_mlp_w1, v_mlp_w2, v_a_w_in, v_a_ln_g, v_a_ln_b, v_a_w_s, v_a_b_s, v_a_w_out, v_b_mu, v_b_w_in, v_b_w0, v_b_w1, v_b_w2, v_b_a0, v_b_a1, v_b_a2, v_b_g1, v_b_g2, v_b_k_k, v_b_k_a, v_b_r_k, v_b_ln_g, v_b_ln_b, v_b_w_out, v_final_g):
    weights = dict(ada_w=ada_w, ada_b=ada_b, mlp_w1=mlp_w1, mlp_w2=mlp_w2, a_w_in=a_w_in, a_ln_g=a_ln_g,
                   a_ln_b=a_ln_b, a_w_s=a_w_s, a_b_s=a_b_s, a_w_out=a_w_out, b_mu=b_mu, b_w_in=b_w_in,
                   b_w0=b_w0, b_w1=b_w1, b_w2=b_w2, b_a0=b_a0, b_a1=b_a1, b_a2=b_a2, b_g1=b_g1, b_g2=b_g2,
                   b_k_k=b_k_k, b_k_a=b_k_a, b_r_k=b_r_k, b_ln_g=b_ln_g, b_ln_b=b_ln_b, b_w_out=b_w_out,
                   final_g=final_g)
    moms = dict(ada_w=(m_ada_w, v_ada_w), ada_b=(m_ada_b, v_ada_b), mlp_w1=(m_mlp_w1, v_mlp_w1),
                mlp_w2=(m_mlp_w2, v_mlp_w2), a_w_in=(m_a_w_in, v_a_w_in), a_ln_g=(m_a_ln_g, v_a_ln_g),
                a_ln_b=(m_a_ln_b, v_a_ln_b), a_w_s=(m_a_w_s, v_a_w_s), a_b_s=(m_a_b_s, v_a_b_s),
                a_w_out=(m_a_w_out, v_a_w_out), b_mu=(m_b_mu, v_b_mu), b_w_in=(m_b_w_in, v_b_w_in),
                b_w0=(m_b_w0, v_b_w0), b_w1=(m_b_w1, v_b_w1), b_w2=(m_b_w2, v_b_w2), b_a0=(m_b_a0, v_b_a0),
                b_a1=(m_b_a1, v_b_a1), b_a2=(m_b_a2, v_b_a2), b_g1=(m_b_g1, v_b_g1), b_g2=(m_b_g2, v_b_g2),
                b_k_k=(m_b_k_k, v_b_k_k), b_k_a=(m_b_k_a, v_b_k_a), b_r_k=(m_b_r_k, v_b_r_k),
                b_ln_g=(m_b_ln_g, v_b_ln_g), b_ln_b=(m_b_ln_b, v_b_ln_b), b_w_out=(m_b_w_out, v_b_w_out),
                final_g=(m_final_g, v_final_g))
    order = list(weights)

    nbat, seq, _ = x.shape
    n = nbat * seq
    tm = 256
    xi, yi, ci = _place()
    chip = 2 * xi + yi
    dev = 2 * chip + ci
    x0 = x.reshape(n, D)
    tgt = loss_target.reshape(n, D)
    lora_w, lora_g = b_w1.shape[-1], b_g1.shape[-1]
    lora_wp, lora_gp = LANES, 2 * LANES

    (cond,) = _small(lambda cc: (cc / (1.0 + jnp.exp(-cc)),), "silu_c", [c], [c.shape])
    vec_names = ["b_w0", "b_a0", "b_k_k", "b_k_a", "b_ln_g", "b_ln_b"]
    vec_shard = jnp.concatenate([b_mu[0]] + [weights[k] for k in vec_names], axis=0)
    n_vec = vec_shard.shape[0]
    vec_rows = vec_shard.reshape(-1, ROW_W)
    blk = _pad_rows(jnp.concatenate([cond, vec_rows], axis=0), 8)
    assert blk.shape[0] == 8
    gathered = _all_gather8(blk).reshape(N_DEV, 8, D)
    cond_all = gathered[:, :nbat].reshape(N_DEV * nbat, D)
    vec_all = gathered[0::2, nbat:nbat + vec_rows.shape[0]].reshape(N_CHIPS, n_vec, D // N_CHIPS)
    vec_full = jnp.transpose(vec_all, (1, 0, 2)).reshape(n_vec, D)
    mu_full = vec_full[0:6]
    w0_f, a0_f, kk_f, ka_f, lng_f, lnb_f = [vec_full[6 + j:7 + j] for j in range(6)]
    rk_f = b_r_k.reshape(1, D)

    n_ada = ada_w.shape[-1]
    parts = jnp.concatenate(
        [_matmul(cond_all, ada_w[i], mode="nn", name="ada_fwd_%d" % i) for i in range(2)], axis=1)
    parts_all = _all_gather8(parts).reshape(N_DEV, N_DEV * nbat, 2, n_ada)[0::2]
    mine = lax.dynamic_slice_in_dim(parts_all, dev * nbat, nbat, axis=1)
    mods = []
    for i in range(2):
        full = jnp.transpose(mine[:, :, i], (1, 0, 2)).reshape(nbat, N_MOD * D) + ada_b[i]
        mods.append(full.reshape(nbat, N_MOD, D))

    big = [("mlp_w1_0", mlp_w1[0]), ("mlp_w1_1", mlp_w1[1]), ("mlp_w2_0", mlp_w2[0]), ("mlp_w2_1", mlp_w2[1]),
           ("a_w_in", a_w_in[0]), ("a_w_out", a_w_out[0]), ("b_w_in", b_w_in[0]), ("b_w_out", b_w_out[0]),
           ("b_w1", b_w1[0]), ("b_w2", b_w2[0]), ("b_a1", b_a1[0]), ("b_a2", b_a2[0]),
           ("b_g1", b_g1[0]), ("b_g2", b_g2[0])]
    offs, pos = {}, 0
    for name, arr in big:
        rows_k = arr.size // ROW_W
        offs[name] = (pos, rows_k, arr.shape)
        pos += rows_k
    n_big_rows = pos
    wflat = _pad_rows(jnp.concatenate([arr.astype(BF16).reshape(-1, ROW_W) for _, arr in big], axis=0), 32)
    wg = _chip_all_gather(wflat)

    def gathered_piece(name):
        start, rows_k, shape = offs[name]
        return wg[:, start:start + rows_k].reshape((N_CHIPS,) + shape)

    def col_w(name):
        return _unshard_cols(gathered_piece(name))

    def row_w(name):
        piece = gathered_piece(name)
        return piece.reshape(N_CHIPS * piece.shape[1], piece.shape[2])

    w1_l = [col_w("mlp_w1_0"), col_w("mlp_w1_1")]
    w2_l = [row_w("mlp_w2_0"), row_w("mlp_w2_1")]
    a_win, a_wout = col_w("a_w_in"), row_w("a_w_out")
    b_win, b_wout = col_w("b_w_in"), row_w("b_w_out")
    w_r, w_k, w_v = b_win[:, :D], b_win[:, D:2 * D], b_win[:, 2 * D:]
    w1p = jnp.pad(row_w("b_w1"), ((0, 0), (0, lora_wp - lora_w)))
    a1p = jnp.pad(row_w("b_a1"), ((0, 0), (0, lora_wp - lora_w)))
    g1p = jnp.pad(row_w("b_g1"), ((0, 0), (0, lora_gp - lora_g)))
    w2p = jnp.pad(col_w("b_w2"), ((0, lora_wp - lora_w), (0, 0)))
    a2p = jnp.pad(col_w("b_a2"), ((0, lora_wp - lora_w), (0, 0)))
    g2p = jnp.pad(col_w("b_g2"), ((0, lora_gp - lora_g), (0, 0)))

    mod0, mod1 = mods
    (h_a,) = _rw_fwd(_f_norm1, "norm1_a", seq=seq, tm=tm, rows=[x0], bvecs=[mod0], outs=[(D, BF16)])
    uvp = _matmul(h_a, a_win, mode="nn", name="sgu_in")
    ws = a_w_s[0]
    bias = jnp.broadcast_to(a_b_s[0][:, :, None], (GROUPS, CHUNK, LANES))
    z = _sgu_fwd(uvp, a_ln_g, a_ln_b, ws, bias)
    mix0 = _matmul(z, a_wout, mode="nn", name="sgu_out")
    x2, saved0 = _mlp_fwd(seq, tm, x0, mix0, mod0, w1_l[0], w2_l[0], "0")

    def shift_fwd(ctx, rv, pv, nv, bv, pa):
        _, _, mixes = _shift_mix(ctx, rv[0], pv[0], bv[0], pa[0])
        return mixes, [], []

    xr, xw, xk, xv, xa, xg = _rowwise(shift_fwd, "shift_mix", seq=seq, tm=tm, rows=[x2], prev8=[x2],
                                      bvecs=[mod1], params=[mu_full], out_rows=[(D, BF16)] * 6)
    r = _matmul(xr, w_r, mode="nn", name="rwkv_r")
    k = _matmul(xk, w_k, mode="nn", name="rwkv_k")
    v = _matmul(xv, w_v, mode="nn", name="rwkv_v")
    def act_tanh(t):
        return jnp.tanh(t)

    def act_sigmoid(t):
        return 1.0 / (1.0 + jnp.exp(-t))

    t1, th = _matmul_ep(xw, w1p, [], lambda acc: (acc, act_tanh(acc)), [F32, BF16], mode="nn", name="lora_w1")
    t2 = _matmul(xa, a1p, mode="nn", name="lora_a1", out_dtype=BF16)
    t3, sg = _matmul_ep(xg, g1p, [], lambda acc: (acc, act_sigmoid(acc)), [F32, BF16], mode="nn", name="lora_g1")
    wl = _matmul(th, w2p, mode="nn", name="lora_w2")
    al = _matmul(t2, a2p, mode="nn", name="lora_a2")
    g = _matmul(sg, g2p, mode="nn", name="lora_g2")
    wkv_rows = [r, k, v, wl, al, g]
    wkv_pars = [w0_f, a0_f, kk_f, ka_f, rk_f, lng_f, lnb_f]
    yo, ckpt = _wkv_fwd(seq, wkv_rows, wkv_pars)
    mix1 = _matmul(yo, b_wout, mode="nn", name="rwkv_out")
    _, saved1 = _mlp_fwd(seq, tm, x2, mix1, mod1, w1_l[1], w2_l[1], "1", residual=False)

    def loss_fn(ctx, rv, pv, nv, bv, pa):
        def head(x1, ff, mod, fg):
            return _f_loss(_f_res2(x1, ff, mod)[0], rv[2], fg)
        val, (dx, dff, dmod, dfg) = jax.value_and_grad(head, argnums=(0, 1, 2, 3))(rv[0], rv[1], bv[0], pa[0])
        return [dx, dff], [dmod], [dfg, jnp.full((1, LANES), val, F32)]

    dx4, dff1, dmod1_a, d_final_g, loss_acc = _rowwise(
        loss_fn, "loss_head", seq=seq, tm=tm, rows=[saved1[0], saved1[4], tgt], bvecs=[mod1],
        params=[final_g.reshape(1, D)], out_rows=[(D, F32), (D, BF16)], out_bacc=[(N_MOD, D)],
        out_pacc=[(1, D), (1, LANES)])
    loss = lax.psum(loss_acc[0, 0], AXES)

    dx2_a, dmix1, dmod1, dw1_1, dw2_1 = _mlp_bwd(seq, tm, saved1, x2, mix1, mod1, w1_l[1], w2_l[1], dx4, "1",
                                                 res2_grads=(dff1, dmod1_a))
    dyo = _matmul(dmix1, b_wout, mode="nt", name="rwkv_out_dx")
    d_b_wout = _matmul(yo, dmix1, mode="tn", name="rwkv_out_dw")
    wkv_grads = _wkv_bwd(seq, wkv_rows, wkv_pars, ckpt, dyo)
    dr, dk, dv, dwl, dal, dg = wkv_grads[:N_WKV_ROWS]
    d_w0, d_a0, d_kk, d_ka, d_rk, d_lng, d_lnb = wkv_grads[N_WKV_ROWS:]
    def through(act):
        return lambda acc, t: (jax.vjp(act, t)[1](acc)[0],)

    (dt1,) = _matmul_ep(dwl, w2p, [t1], through(act_tanh), [BF16], mode="nt", name="lora_w2_dx")
    d_w2p = _matmul(th, dwl, mode="tn", name="lora_w2_dw")
    dt2 = _matmul(dal, a2p, mode="nt", name="lora_a2_dx", out_dtype=BF16)
    d_a2p = _matmul(t2, dal, mode="tn", name="lora_a2_dw")
    (dt3,) = _matmul_ep(dg, g2p, [t3], through(act_sigmoid), [BF16], mode="nt", name="lora_g2_dx")
    d_g2p = _matmul(sg, dg, mode="tn", name="lora_g2_dw")
    dxw = _matmul(dt1, w1p, mode="nt", name="lora_w1_dx")
    d_w1p = _matmul(xw, dt1, mode="tn", name="lora_w1_dw")
    dxa = _matmul(dt2, a1p, mode="nt", name="lora_a1_dx")
    d_a1p = _matmul(xa, dt2, mode="tn", name="lora_a1_dw")
    dxg = _matmul(dt3, g1p, mode="nt", name="lora_g1_dx")
    d_g1p = _matmul(xg, dt3, mode="tn", name="lora_g1_dw")
    dxr = _matmul(dr, w_r, mode="nt", name="rwkv_r_dx")
    dxk = _matmul(dk, w_k, mode="nt", name="rwkv_k_dx")
    dxv = _matmul(dv, w_v, mode="nt", name="rwkv_v_dx")
    d_b_win = jnp.concatenate([_matmul(xr, dr, mode="tn", name="rwkv_r_dw"),
                               _matmul(xk, dk, mode="tn", name="rwkv_k_dw"),
                               _matmul(xv, dv, mode="tn", name="rwkv_v_dw")], axis=1)

    def shift_bwd(ctx, rv, pv, nv, bv, pa):
        xt, dres = rv[0], rv[1]
        dmix_in = rv[2:8]
        mod, mu = bv[0], pa[0]
        f_h = lambda xx_, mod_: _rmsmod(xx_, mod_[0:1], mod_[1:2])
        h, vjp = jax.vjp(f_h, xt, mod)
        hprev = f_h(pv[0], mod)[7:8]
        hprev = jnp.where(ctx.first, jnp.zeros_like(hprev), hprev)
        rowid = lax.broadcasted_iota(jnp.int32, h.shape, 0)
        xx = jnp.where(rowid == 0, hprev, pltpu.roll(h, 1, 0)) - h
        tot = dmix_in[0]
        wsum = dmix_in[0] * mu[0:1]
        for j in range(1, 6):
            tot = tot + dmix_in[j]
            wsum = wsum + dmix_in[j] * mu[j:j + 1]
        nxt = nv[0][0:1] * mu[0:1]
        for j in range(1, 6):
            nxt = nxt + nv[j][0:1] * mu[j:j + 1]
        nxt = jnp.where(ctx.last, jnp.zeros_like(nxt), nxt)
        tmr = h.shape[0]
        wshift = jnp.where(rowid == tmr - 1, nxt, pltpu.roll(wsum, tmr - 1, 0))
        dh = tot - wsum + wshift
        dx_, dmod_ = vjp(dh)
        dmu = jnp.concatenate([jnp.sum(dmix_in[j] * xx, axis=0, keepdims=True) for j in range(6)], axis=0)
        dx2_t = dx_ + dres
        ff_below, mod_below = rv[8], bv[1]
        dff_below = mod_below[5:6] * dx2_t
        dgate = jnp.sum(dx2_t * ff_below, axis=0, keepdims=True)
        gate_row = lax.broadcasted_iota(jnp.int32, (N_MOD, D), 0) == N_MOD - 1
        dmod_below = jnp.where(gate_row, jnp.broadcast_to(dgate, (N_MOD, D)), 0.0)
        return [dx2_t, dff_below], [dmod_, dmod_below], [dmu]

    dmix_list = [dxr, dxw, dxk, dxv, dxa, dxg]
    dx2, dff0, dmod1_c, dmod0_a, d_mu = _rowwise(
        shift_bwd, "shift_mix_bwd", seq=seq, tm=tm, rows=[x2, dx2_a] + dmix_list + [saved0[4]],
        prev8=[x2], next8=dmix_list, bvecs=[mod1, mod0], params=[mu_full],
        out_rows=[(D, F32), (D, BF16)], out_bacc=[(N_MOD, D), (N_MOD, D)], out_pacc=[(6, D)])
    dmod1 = dmod1 + dmod1_c

    dx0_a, dmix0, dmod0, dw1_0, dw2_0 = _mlp_bwd(seq, tm, saved0, x0, mix0, mod0, w1_l[0], w2_l[0], dx2, "0",
                                                 res2_grads=(dff0, dmod0_a))
    dz = _matmul(dmix0, a_wout, mode="nt", name="sgu_out_dx")
    d_a_wout = _matmul(z, dmix0, mode="tn", name="sgu_out_dw")
    duvp, d_a_lng, d_a_lnb, d_ws, d_bias = _sgu_bwd(uvp, a_ln_g, a_ln_b, ws, bias, dz)
    dh_a = _matmul(duvp, a_win, mode="nt", name="sgu_in_dx")
    d_a_win = _matmul(h_a, duvp, mode="tn", name="sgu_in_dw", out_shards=N_CHIPS)
    grad_x, dmod0_c = _rw_bwd(_f_norm1, "norm1_a_bwd", seq=seq, tm=tm, rows=[x0], bvecs=[mod0], cts=[dh_a],
                              need_rows=[0], extra=dx0_a)
    dmod0 = dmod0 + dmod0_c

    dmod_blk = _pad_rows(jnp.concatenate([dmod0.reshape(nbat, -1), dmod1.reshape(nbat, -1)], axis=1), 8)
    dmod_all = _all_gather8(dmod_blk).reshape(N_DEV, 8, 2, N_MOD * D)[:, :nbat].reshape(N_DEV * nbat, 2, N_MOD * D)
    g_ada_w, g_ada_b = [], []
    for i in range(2):
        cols = lax.dynamic_slice_in_dim(dmod_all[:, i], chip * n_ada, n_ada, axis=1)
        g_ada_w.append(_matmul(cond_all, cols, mode="tn", name="ada_dw_%d" % i))
    (g_ada_b_all,) = _small(lambda t: (jnp.sum(t, axis=0),), "ada_db", [dmod_all], [(2, N_MOD * D)])
    grads = {"ada_w": jnp.stack(g_ada_w), "ada_b": g_ada_b_all}

    rep = _pad_rows(jnp.concatenate([
        d_a_lng, d_a_lnb, jnp.sum(d_bias, axis=-1).reshape(1, D), d_rk, d_final_g,
        jnp.zeros((3, D), F32), d_ws.reshape(-1, D)], axis=0), 8)
    rep_rows = rep.shape[0]
    rep_all = _all_gather8(rep)
    (rep_sum,) = _small(lambda t: (functools.reduce(lambda p, q: p + q,
                                                     [t[j * rep_rows:(j + 1) * rep_rows] for j in range(N_DEV)]),),
                        "replicated_sum", [rep_all], [(rep_rows, D)])
    grads["a_ln_g"] = rep_sum[0:1]
    grads["a_ln_b"] = rep_sum[1:2]
    grads["a_b_s"] = rep_sum[2:3].reshape(a_b_s.shape)
    grads["b_r_k"] = rep_sum[3:4].reshape(b_r_k.shape)
    grads["final_g"] = rep_sum[4].reshape(final_g.shape)
    grads["a_w_s"] = rep_sum[8:8 + GROUPS * CHUNK * LANES // D].reshape(a_w_s.shape)

    vec_grads = jnp.concatenate([d_mu, d_w0, d_a0, d_kk, d_ka, d_lng, d_lnb], axis=0)
    packed = {
        "mlp_w1_0": dw1_0, "mlp_w1_1": dw1_1,
        "mlp_w2_0": dw2_0.reshape(N_CHIPS, -1, D), "mlp_w2_1": dw2_1.reshape(N_CHIPS, -1, D),
        "a_w_in": d_a_win, "a_w_out": d_a_wout.reshape(N_CHIPS, -1, D),
        "b_w_in": _shard_cols(d_b_win), "b_w_out": d_b_wout.reshape(N_CHIPS, -1, D),
        "b_w1": d_w1p[:, :lora_w].reshape(N_CHIPS, -1, lora_w), "b_w2": _shard_cols(d_w2p[:lora_w]),
        "b_a1": d_a1p[:, :lora_w].reshape(N_CHIPS, -1, lora_w), "b_a2": _shard_cols(d_a2p[:lora_w]),
        "b_g1": d_g1p[:, :lora_g].reshape(N_CHIPS, -1, lora_g), "b_g2": _shard_cols(d_g2p[:lora_g]),
    }
    pieces = [_rows_of(packed[name]) for name, _ in big] + [_pad_rows(_rows_of(_shard_cols(vec_grads)), 8)]
    used = sum(p.shape[1] for p in pieces)
    pieces.append(jnp.zeros((N_CHIPS, (-used) % 2016, ROW_W), F32))
    g_pack = jnp.concatenate(pieces, axis=1)
    g_red = _reduce_scatter(g_pack)
    for name, _ in big:
        start, rows_k, shape = offs[name]
        grads[name] = g_red[start:start + rows_k].reshape(shape)
    vec_red = g_red[n_big_rows:n_big_rows + vec_rows.shape[0]].reshape(n_vec, D // N_CHIPS)
    grads["b_mu"] = vec_red[0:6].reshape(b_mu.shape)
    for j, name in enumerate(vec_names):
        grads[name] = vec_red[6 + j:7 + j].reshape(weights[name].shape)
    for base in ("mlp_w1", "mlp_w2"):
        grads[base] = jnp.stack([grads.pop(base + "_0"), grads.pop(base + "_1")])
    for name in ("a_w_in", "a_w_out", "b_w_in", "b_w_out", "b_w1", "b_w2", "b_a1", "b_a2", "b_g1", "b_g2"):
        grads[name] = grads[name].reshape(weights[name].shape)

    deltas, new_m, new_v = {}, {}, {}
    for name in order:
        gr = grads[name].reshape(weights[name].shape)
        grads[name] = gr
        deltas[name], new_m[name], new_v[name] = _elementwise(
            _adamw, "adamw_" + name, [weights[name], gr, moms[name][0], moms[name][1]], 3)

    return (loss, grad_x.reshape(x.shape), *[grads[k] for k in order], *[deltas[k] for k in order],
            *[new_m[k] for k in order], *[new_v[k] for k in order])
```

```python
import functools

import jax
import jax.numpy as jnp
from jax import lax
from jax.experimental import pallas as pl
from jax.experimental.pallas import tpu as pltpu

F32 = jnp.float32
BF16 = jnp.bfloat16
MESH = pl.DeviceIdType.MESH
AXES = ("x", "y", "c")

D = 1024
N_MOD = 6
HEAD = 64
CHUNK = 128
GROUPS = 8
LANES = 128
ROW_W = 1024
N_CHIPS = 4
N_DEV = 8

RMS_EPS = 1e-6
LN_EPS = 1e-5
GN_EPS = HEAD * 1e-5
L2_EPS = 1e-12

ADAM_LR = 0.001
ADAM_B1 = 0.9
ADAM_B2 = 0.999
ADAM_EPS = 1e-08
ADAM_WD = 0.01
ADAM_STEP = 10

VMEM_LIMIT_V7X = 56 * 1024 * 1024
HIGHEST = lax.Precision.HIGHEST


def _params(sem=None):
    return pltpu.CompilerParams(dimension_semantics=sem, vmem_limit_bytes=VMEM_LIMIT_V7X)


def _tile(dim, target):
    if dim <= target:
        return dim
    for cand in range(target, 0, -LANES):
        if dim % cand == 0:
            return cand
    raise ValueError((dim, target))


def _matmul(a, b, *, mode, name, out_dtype=F32, out_shards=1, tm=1024, tn=1024, tk=4096):
    if mode == "nn":
        (m, k), (k2, n) = a.shape, b.shape
    elif mode == "nt":
        (m, k), (n, k2) = a.shape, b.shape
    else:
        (k, m), (k2, n) = a.shape, b.shape
    assert k == k2, (name, a.shape, b.shape)
    n_sh = n // out_shards
    tm, tn, tk = _tile(m, tm), _tile(n_sh, tn), _tile(k, tk)
    nk = k // tk
    nb = n_sh // tn
    use_scratch = nk > 1 and out_dtype != F32

    if mode == "tn":
        a_spec = pl.BlockSpec((tk, tm), lambda i, j, kk: (kk, i))
    else:
        a_spec = pl.BlockSpec((tm, tk), lambda i, j, kk: (i, kk))
    if mode == "nt":
        b_spec = pl.BlockSpec((tn, tk), lambda i, j, kk: (j, kk))
    else:
        b_spec = pl.BlockSpec((tk, tn), lambda i, j, kk: (kk, j))
    if out_shards == 1:
        out_shape = jax.ShapeDtypeStruct((m, n), out_dtype)
        o_spec = pl.BlockSpec((tm, tn), lambda i, j, kk: (i, j))
    else:
        out_shape = jax.ShapeDtypeStruct((out_shards, m, n_sh), out_dtype)
        o_spec = pl.BlockSpec((None, tm, tn), lambda i, j, kk: (j // nb, i, j % nb))

    def body(a_ref, b_ref, o_ref, *scratch):
        kk = pl.program_id(2)
        av = a_ref[...].astype(BF16)
        bv = b_ref[...].astype(BF16)
        if mode == "nn":
            dims = (((1,), (0,)), ((), ()))
        elif mode == "nt":
            dims = (((1,), (1,)), ((), ()))
        else:
            dims = (((0,), (0,)), ((), ()))
        part = lax.dot_general(av, bv, dims, preferred_element_type=F32)
        if nk == 1:
            o_ref[...] = part.astype(o_ref.dtype)
            return
        acc_ref = scratch[0] if use_scratch else o_ref

        @pl.when(kk == 0)
        def _():
            acc_ref[...] = part

        @pl.when(kk != 0)
        def _():
            acc_ref[...] += part

        if use_scratch:
            @pl.when(kk == nk - 1)
            def _():
                o_ref[...] = acc_ref[...].astype(o_ref.dtype)

    return pl.pallas_call(
        body, name=name, out_shape=out_shape,
        grid=(m // tm, n // tn, nk),
        in_specs=[a_spec, b_spec], out_specs=o_spec,
        scratch_shapes=[pltpu.VMEM((tm, tn), F32)] if use_scratch else [],
        compiler_params=_params(("parallel", "parallel", "arbitrary")),
    )(a, b)


def _matmul_ep(a, b, extras, epilogue, out_dtypes, *, mode, name, tm=1024, tn=1024, tk=2048):
    if mode == "nn":
        (m, k), (k2, n) = a.shape, b.shape
    else:
        (m, k), (n, k2) = a.shape, b.shape
    assert k == k2 and mode in ("nn", "nt"), (name, a.shape, b.shape)
    tm, tn, tk = _tile(m, tm), _tile(n, tn), _tile(k, tk)
    nk = k // tk
    n_ex, n_out = len(extras), len(out_dtypes)

    def body(a_ref, b_ref, *rest):
        extra_refs, out_refs = rest[:n_ex], rest[n_ex:n_ex + n_out]
        kk = pl.program_id(2)
        dims = (((1,), (0,)), ((), ())) if mode == "nn" else (((1,), (1,)), ((), ()))
        part = lax.dot_general(a_ref[...].astype(BF16), b_ref[...].astype(BF16), dims,
                               preferred_element_type=F32)

        def finish(acc):
            for ref, val in zip(out_refs, epilogue(acc, *[r[...] for r in extra_refs])):
                ref[...] = val.astype(ref.dtype)

        if nk == 1:
            finish(part)
            return
        acc_ref = rest[-1]

        @pl.when(kk == 0)
        def _():
            acc_ref[...] = part

        @pl.when(kk != 0)
        def _():
            acc_ref[...] += part

        @pl.when(kk == nk - 1)
        def _():
            finish(acc_ref[...])

    a_spec = pl.BlockSpec((tm, tk), lambda i, j, kk: (i, kk))
    b_spec = (pl.BlockSpec((tk, tn), lambda i, j, kk: (kk, j)) if mode == "nn"
              else pl.BlockSpec((tn, tk), lambda i, j, kk: (j, kk)))
    o_spec = pl.BlockSpec((tm, tn), lambda i, j, kk: (i, j))
    res = pl.pallas_call(
        body, name=name, out_shape=[jax.ShapeDtypeStruct((m, n), dt) for dt in out_dtypes],
        grid=(m // tm, n // tn, nk),
        in_specs=[a_spec, b_spec] + [o_spec] * n_ex, out_specs=[o_spec] * n_out,
        scratch_shapes=[pltpu.VMEM((tm, tn), F32)] if nk > 1 else [],
        compiler_params=_params(("parallel", "parallel", "arbitrary")),
    )(a, b, *extras)
    return list(res)


class _Ctx:
    def __init__(self, first, last):
        self.first = first
        self.last = last


def _rowwise(fn, name, *, seq, tm, rows=(), prev8=(), next8=(), bvecs=(), params=(),
             out_rows=(), out_bacc=(), out_pacc=()):
    n = rows[0].shape[0]
    tm = min(tm, seq)
    assert n % seq == 0 and seq % tm == 0 and tm % 8 == 0
    tpb = seq // tm
    nt = n // tm
    nbat = n // seq
    r8 = tm // 8
    counts = [len(rows), len(prev8), len(next8), len(bvecs), len(params)]
    n_in = sum(counts)

    def body(*refs):
        i = pl.program_id(0)
        first = (i % tpb) == 0
        last = (i % tpb) == (tpb - 1)
        vals = [r[...] for r in refs[:n_in]]
        groups, pos = [], 0
        for cnt in counts:
            groups.append(vals[pos:pos + cnt])
            pos += cnt
        ro, bo, po = fn(_Ctx(first, last), *groups)
        outs = refs[n_in:]
        assert len(ro) == len(out_rows) and len(bo) == len(out_bacc) and len(po) == len(out_pacc)
        for ref, val in zip(outs[:len(ro)], ro):
            ref[...] = val.astype(ref.dtype)
        for ref, val in zip(outs[len(ro):len(ro) + len(bo)], bo):
            @pl.when(first)
            def _(ref=ref, val=val):
                ref[...] = val

            @pl.when(jnp.logical_not(first))
            def _(ref=ref, val=val):
                ref[...] += val
        for ref, val in zip(outs[len(ro) + len(bo):], po):
            @pl.when(i == 0)
            def _(ref=ref, val=val):
                ref[...] = val

            @pl.when(i != 0)
            def _(ref=ref, val=val):
                ref[...] += val

    in_specs = []
    for arr in rows:
        in_specs.append(pl.BlockSpec((tm, arr.shape[1]), lambda i: (i, 0)))
    for arr in prev8:
        in_specs.append(pl.BlockSpec((8, arr.shape[1]), lambda i: (jnp.maximum(i * r8 - 1, 0), 0)))
    for arr in next8:
        in_specs.append(pl.BlockSpec((8, arr.shape[1]), lambda i: (jnp.minimum((i + 1) * r8, n // 8 - 1), 0)))
    for arr in bvecs:
        in_specs.append(pl.BlockSpec((None,) + arr.shape[1:], lambda i: (i // tpb, 0, 0)))
    for arr in params:
        in_specs.append(pl.BlockSpec(arr.shape, lambda i: (0, 0)))
    out_shape, out_specs = [], []
    for d, dt in out_rows:
        out_shape.append(jax.ShapeDtypeStruct((n, d), dt))
        out_specs.append(pl.BlockSpec((tm, d), lambda i: (i, 0)))
    for r, d in out_bacc:
        out_shape.append(jax.ShapeDtypeStruct((nbat, r, d), F32))
        out_specs.append(pl.BlockSpec((None, r, d), lambda i: (i // tpb, 0, 0)))
    for r, d in out_pacc:
        out_shape.append(jax.ShapeDtypeStruct((r, d), F32))
        out_specs.append(pl.BlockSpec((r, d), lambda i: (0, 0)))
    res = pl.pallas_call(
        body, name=name, out_shape=out_shape, grid=(nt,),
        in_specs=in_specs, out_specs=out_specs,
        compiler_params=_params(("arbitrary",)),
    )(*rows, *prev8, *next8, *bvecs, *params)
    return list(res)


def _rw_fwd(f, name, *, seq, tm, rows, bvecs=(), params=(), outs):
    def fn(ctx, rv, pv, nv, bv, pa):
        res = f(*[v.astype(F32) for v in rv], *bv, *pa)
        return list(res), [], []
    return _rowwise(fn, name, seq=seq, tm=tm, rows=rows, bvecs=bvecs, params=params, out_rows=outs)


def _rw_bwd(f, name, *, seq, tm, rows, bvecs=(), params=(), cts, need_rows, extra=None, dtypes=None):
    nr, nb, npar = len(rows), len(bvecs), len(params)
    all_rows = list(rows) + list(cts) + ([extra] if extra is not None else [])

    def fn(ctx, rv, pv, nv, bv, pa):
        prim = [v.astype(F32) for v in rv[:nr]]
        ct = tuple(v.astype(F32) for v in rv[nr:nr + len(cts)])
        _, vjp = jax.vjp(f, *prim, *bv, *pa)
        g = vjp(ct)
        d_rows = [g[j] for j in need_rows]
        if extra is not None:
            d_rows[0] = d_rows[0] + rv[-1].astype(F32)
        return d_rows, list(g[nr:nr + nb]), list(g[nr + nb:])

    return _rowwise(
        fn, name, seq=seq, tm=tm, rows=all_rows, bvecs=bvecs, params=params,
        out_rows=[(rows[j].shape[1], F32 if dtypes is None else dtypes[i]) for i, j in enumerate(need_rows)],
        out_bacc=[b.shape[1:] for b in bvecs], out_pacc=[p.shape for p in params])


def _small(fn, name, arrays, out_shapes):
    def body(*refs):
        res = fn(*[r[...] for r in refs[:len(arrays)]])
        for ref, val in zip(refs[len(arrays):], res):
            ref[...] = val.astype(ref.dtype)

    vm = pl.BlockSpec(memory_space=pltpu.VMEM)
    res = pl.pallas_call(
        body, name=name,
        out_shape=[jax.ShapeDtypeStruct(s, F32) for s in out_shapes],
        in_specs=[vm] * len(arrays), out_specs=[vm] * len(out_shapes),
        compiler_params=_params(),
    )(*arrays)
    return list(res)


def _elementwise(fn, name, arrays, n_out):
    shape = arrays[0].shape
    size = arrays[0].size
    if len(shape) >= 2 and shape[-1] % LANES == 0 and (size // shape[-1]) % 8 == 0:
        view = (size // shape[-1], shape[-1])
    elif size % ROW_W == 0 and (size // ROW_W) % 8 == 0:
        view = (size // ROW_W, ROW_W)
    else:
        view = (1, size) if len(shape) < 2 else (size // shape[-1], shape[-1])
    rows = view[0]
    tr = rows
    for cand in (256, 128, 64, 32, 16, 8):
        if rows > cand and rows % cand == 0:
            tr = cand
            break

    def body(*refs):
        res = fn(*[r[...] for r in refs[:len(arrays)]])
        for ref, val in zip(refs[len(arrays):], res):
            ref[...] = val

    spec = pl.BlockSpec((tr, view[1]), lambda i: (i, 0))
    res = pl.pallas_call(
        body, name=name,
        out_shape=[jax.ShapeDtypeStruct(view, F32)] * n_out,
        grid=(rows // tr,), in_specs=[spec] * len(arrays), out_specs=[spec] * n_out,
        compiler_params=_params(("parallel",)),
    )(*[a.reshape(view) for a in arrays])
    return [r.reshape(shape) for r in res]


def _rms(x):
    return x * lax.rsqrt(jnp.mean(x * x, axis=-1, keepdims=True) + RMS_EPS)


def _rmsmod(x, sh, sc):
    return _rms(x) * (1.0 + sc) + sh


def _f_norm1(x, mod):
    return (_rmsmod(x, mod[0:1], mod[1:2]),)


def _f_sgu_pre(uvp, ln_g, ln_b):
    uv = 0.5 * uvp * (1.0 + lax.erf(uvp * (2.0 ** -0.5)))
    u = uv[:, :D]
    v = uv[:, D:]
    mu = jnp.mean(v, axis=-1, keepdims=True)
    vc = v - mu
    var = jnp.mean(vc * vc, axis=-1, keepdims=True)
    return u, vc * lax.rsqrt(var + LN_EPS) * ln_g + ln_b


def _f_res_norm2(x, mix, mod):
    x1 = x + mod[2:3] * mix
    return x1, _rmsmod(x1, mod[3:4], mod[4:5])


def _f_res2(x1, ff, mod):
    return (x1 + mod[5:6] * ff,)


def _f_loss(x, tgt, fg):
    err = _rms(x) * fg - tgt
    return 0.5 * jnp.sum(jnp.mean(err * err, axis=-1))


def _shift_mix(ctx, x, xprev8, mod, mu):
    h = _rmsmod(x, mod[0:1], mod[1:2])
    hprev = _rmsmod(xprev8, mod[0:1], mod[1:2])[7:8]
    hprev = jnp.where(ctx.first, jnp.zeros_like(hprev), hprev)
    rowid = lax.broadcasted_iota(jnp.int32, h.shape, 0)
    hp = jnp.where(rowid == 0, hprev, pltpu.roll(h, 1, 0))
    xx = hp - h
    return h, xx, [h + xx * mu[j:j + 1] for j in range(6)]


def _split_bf16(t, parts):
    out, rest = [], t.astype(F32)
    for _ in range(parts):
        piece = rest.astype(BF16)
        out.append(piece)
        rest = rest - piece.astype(F32)
    return out


def _make_mm(na, nb, ct_pieces=1, saved_pieces=1):
    def raw(a, b, pa, pb):
        if pa == 0:
            return jnp.dot(a, b, precision=HIGHEST, preferred_element_type=F32)
        acc = None
        bs = _split_bf16(b, pb)
        for i, ai in enumerate(_split_bf16(a, pa)):
            for j, bj in enumerate(bs):
                if i + j < max(pa, pb):
                    term = jnp.dot(ai, bj, preferred_element_type=F32)
                    acc = term if acc is None else acc + term
        return acc

    @jax.custom_vjp
    def mm(a, b):
        return raw(a, b, na, nb)

    def fwd(a, b):
        return raw(a, b, na, nb), (a, b)

    def bwd(res, ct):
        a, b = res
        if na == 0:
            return raw(ct, b.T, 0, 0), raw(a.T, ct, 0, 0)
        return raw(ct, b.T, ct_pieces, saved_pieces), raw(a.T, ct, saved_pieces, ct_pieces)

    mm.defvjp(fwd, bwd)
    return mm


class _WkvMms:
    def __init__(self, head_sum, cum, score, square, apply, out, state):
        self.head_sum, self.cum, self.score = head_sum, cum, score
        self.square, self.apply, self.out, self.state = square, apply, out, state


def _wkv_mms(cfg):
    table = {"x": (0, 0), "1": (1, 1), "2": (2, 2), "3": (3, 3), "a": (2, 1), "b": (1, 2)}
    hs, cu, sc_, sq, ap, ou, st = [table[ch] for ch in cfg]
    return _WkvMms(_make_mm(hs[0], 1) if hs[0] else _make_mm(0, 0),
                   _make_mm(1, cu[1], ct_pieces=2) if cu[0] else _make_mm(0, 0),
                   _make_mm(*sc_, saved_pieces=2), _make_mm(*sq), _make_mm(*ap, saved_pieces=2),
                   _make_mm(*ou), _make_mm(*st))


WKV_PRECISION = "2221b11"


SGU_CHUNKS_PER_STEP = 4


def _sgu_tile(mm, u, vn, ws, bias):
    row = lax.broadcasted_iota(jnp.int32, (CHUNK, CHUNK), 0)
    col = lax.broadcasted_iota(jnp.int32, (CHUNK, CHUNK), 1)
    wm = [jnp.where(col <= row, w, 0.0) for w in ws]
    out_rows = []
    for ch in range(u.shape[0] // CHUNK):
        rs = slice(ch * CHUNK, (ch + 1) * CHUNK)
        out_rows.append(jnp.concatenate(
            [mm(wm[g], vn[rs, g * LANES:(g + 1) * LANES]) + bias[g] for g in range(GROUPS)], axis=1))
    return u * jnp.concatenate(out_rows, axis=0)


def _sgu_mixer_tile(mm, uvp, ln_g, ln_b, ws, bias):
    u, vn = _f_sgu_pre(uvp, ln_g, ln_b)
    return _sgu_tile(mm, u, vn, ws, bias)


def _sgu_fwd(uvp, ln_g, ln_b, ws, bias):
    n = uvp.shape[0]
    rows = CHUNK * SGU_CHUNKS_PER_STEP
    mm = _make_mm(1, 1)

    def body(x_ref, g_ref, b2_ref, w_ref, b_ref, z_ref):
        ws_l = [w_ref[g] for g in range(GROUPS)]
        bias_l = [b_ref[g] for g in range(GROUPS)]
        z_ref[...] = _sgu_mixer_tile(mm, x_ref[...], g_ref[...], b2_ref[...], ws_l, bias_l).astype(z_ref.dtype)

    tok_in = pl.BlockSpec((rows, 2 * D), lambda i: (i, 0))
    tok = pl.BlockSpec((rows, D), lambda i: (i, 0))
    vec = pl.BlockSpec((1, D), lambda i: (0, 0))
    grp = pl.BlockSpec((GROUPS, CHUNK, LANES), lambda i: (0, 0, 0))
    return pl.pallas_call(
        body, name="sgu_fwd", out_shape=jax.ShapeDtypeStruct((n, D), BF16),
        grid=(n // rows,), in_specs=[tok_in, vec, vec, grp, grp], out_specs=tok,
        compiler_params=_params(("parallel",)),
    )(uvp, ln_g, ln_b, ws, bias)


def _sgu_bwd(uvp, ln_g, ln_b, ws, bias, dz):
    n = uvp.shape[0]
    rows = CHUNK * SGU_CHUNKS_PER_STEP
    mm = _make_mm(1, 1)

    def body(x_ref, g_ref, b2_ref, w_ref, b_ref, dz_ref, dx_ref, dg_ref, db2_ref, dw_ref, db_ref):
        i = pl.program_id(0)
        ws_l = [w_ref[g] for g in range(GROUPS)]
        bias_l = [b_ref[g] for g in range(GROUPS)]
        _, vjp = jax.vjp(functools.partial(_sgu_mixer_tile, mm), x_ref[...], g_ref[...], b2_ref[...], ws_l, bias_l)
        dx, dg, db2, dw, db = vjp(dz_ref[...].astype(F32))
        dx_ref[...] = dx.astype(dx_ref.dtype)

        @pl.when(i == 0)
        def _():
            dg_ref[...] = dg
            db2_ref[...] = db2
            for g in range(GROUPS):
                dw_ref[g] = dw[g]
                db_ref[g] = db[g]

        @pl.when(i != 0)
        def _():
            dg_ref[...] += dg
            db2_ref[...] += db2
            for g in range(GROUPS):
                dw_ref[g] += dw[g]
                db_ref[g] += db[g]

    tok_in = pl.BlockSpec((rows, 2 * D), lambda i: (i, 0))
    tok = pl.BlockSpec((rows, D), lambda i: (i, 0))
    vec = pl.BlockSpec((1, D), lambda i: (0, 0))
    grp = pl.BlockSpec((GROUPS, CHUNK, LANES), lambda i: (0, 0, 0))
    return pl.pallas_call(
        body, name="sgu_bwd",
        out_shape=[jax.ShapeDtypeStruct((n, 2 * D), BF16), jax.ShapeDtypeStruct((1, D), F32),
                   jax.ShapeDtypeStruct((1, D), F32),
                   jax.ShapeDtypeStruct((GROUPS, CHUNK, LANES), F32),
                   jax.ShapeDtypeStruct((GROUPS, CHUNK, LANES), F32)],
        grid=(n // rows,), in_specs=[tok_in, vec, vec, grp, grp, tok],
        out_specs=[tok_in, vec, vec, grp, grp],
        compiler_params=_params(("arbitrary",)),
    )(uvp, ln_g, ln_b, ws, bias, dz)


def _chains(t):
    return [t[i] for i in range(t.shape[0])] if t.ndim == 3 else [t]


def _bmm(mm, a, b):
    if a.ndim == 2 and b.ndim == 2:
        return mm(a, b)
    ca, cb = _chains(a), _chains(b)
    n = max(len(ca), len(cb))
    return jnp.stack([mm(ca[i % len(ca)], cb[i % len(cb)]) for i in range(n)])


def _bt(a):
    return a.T if a.ndim == 2 else jnp.stack([t.T for t in _chains(a)])


def _wkv_chunk(mms, s0, r, k, v, wl, al, g, w0, a0, k_k, k_a, r_k, ln_g, ln_b):
    ln = CHUNK
    row = lax.broadcasted_iota(jnp.int32, (ln, ln), 0)
    col = lax.broadcasted_iota(jnp.int32, (ln, ln), 1)
    incl = (col <= row).astype(F32)
    strict = (col < row).astype(F32)
    same_head = ((row // HEAD) == (col // HEAD)).astype(F32)
    lane = lax.broadcasted_iota(jnp.int32, (1, LANES), 1)
    m_a = (lane < HEAD).astype(F32)
    m_b = 1.0 - m_a
    rowid = lax.broadcasted_iota(jnp.int32, (ln, LANES), 0)
    cat = jnp.concatenate

    def hsum(t):
        return _bmm(mms.head_sum, t, same_head)

    def pick_row(t, j):
        return jnp.sum(jnp.where(rowid == j, t, 0.0), axis=-2, keepdims=True)

    z = w0 + wl
    softplus_neg = jnp.maximum(-z, 0.0) + jnp.log(1.0 + jnp.exp(-jnp.abs(z)))
    lw = -jnp.exp(-softplus_neg - 0.5)
    a = 1.0 / (1.0 + jnp.exp(-(a0 + al)))
    kx = k * k_k
    kkn = kx / jnp.maximum(jnp.sqrt(hsum(kx * kx)), L2_EPS)
    kp = k * (1.0 + (a - 1.0) * k_a)
    aa = -kkn
    bb = kkn * a

    c = _bmm(mms.cum, incl, lw)
    c_mid = pick_row(c, ln // 2 - 1)
    ce = c - c_mid
    e_pos = jnp.exp(ce)
    e_neg = jnp.exp(-ce)
    at = aa * jnp.exp(ce - lw)
    bt = bb * e_neg
    kt = kp * e_neg
    rt = r * e_pos
    s0p = s0 * jnp.exp(c_mid)

    bk = cat([bt, kt], axis=-2)
    sc = _bmm(mms.score, cat([at * m_a, at * m_b, rt * m_a, rt * m_b], axis=-2), _bt(bk))
    ab_a, ak_a = sc[..., 0:ln, 0:ln] * strict, sc[..., 0:ln, ln:] * strict
    ab_b, ak_b = sc[..., ln:2 * ln, 0:ln] * strict, sc[..., ln:2 * ln, ln:] * strict
    incl2 = cat([incl, incl], axis=1)
    p_a = sc[..., 2 * ln:3 * ln, :] * incl2
    p_b = sc[..., 3 * ln:, :] * incl2

    base = _bmm(mms.score, cat([at, rt], axis=-2), _bt(s0p))
    rhs = base[..., :ln, :] + m_a * _bmm(mms.out, ak_a, v) + m_b * _bmm(mms.out, ak_b, v)

    pa, pb = ab_a, ab_b
    xa = rhs + _bmm(mms.apply, pa, rhs)
    xb = rhs + _bmm(mms.apply, pb, rhs)
    for _ in range(6):
        pa = _bmm(mms.square, pa, pa)
        pb = _bmm(mms.square, pb, pb)
        xa = xa + _bmm(mms.apply, pa, xa)
        xb = xb + _bmm(mms.apply, pb, xb)
    u = m_a * xa + m_b * xb
    uv = cat([u, v], axis=-2)
    y = base[..., ln:, :] + m_a * _bmm(mms.out, p_a, uv) + m_b * _bmm(mms.out, p_b, uv)
    s_new = (s0p + _bmm(mms.state, _bt(uv), bk)) * same_head * jnp.exp(pick_row(ce, ln - 1))

    mean = hsum(y) * (1.0 / HEAD)
    yc = y - mean
    var = hsum(yc * yc) * (1.0 / HEAD)
    yn = yc * lax.rsqrt(var + GN_EPS) * ln_g + ln_b
    bonus = hsum(r * kp * r_k) * v
    return (yn + bonus) * g, s_new


N_WKV_ROWS = 6
N_WKV_PAR = 7


WKV_PAIRS_PER_STEP = 4


def _to_chains(val, nbat, pp):
    if val.ndim == 2:
        return jnp.stack([val[:, q * LANES:(q + 1) * LANES] for _ in range(nbat) for q in range(pp)])
    return jnp.stack([val[b, :, q * LANES:(q + 1) * LANES] for b in range(nbat) for q in range(pp)])


def _wkv_fwd(seq, rows, pars):
    n = rows[0].shape[0]
    nbat, nch, npair, pp = n // seq, seq // CHUNK, D // LANES, WKV_PAIRS_PER_STEP
    chunk_fn = functools.partial(_wkv_chunk, _wkv_mms(WKV_PRECISION))

    def body(*refs):
        row_vals = [_to_chains(r[...], nbat, pp) for r in refs[:N_WKV_ROWS]]
        par_vals = [_to_chains(r[...], nbat, pp) for r in refs[N_WKV_ROWS:N_WKV_ROWS + N_WKV_PAR]]
        yo_ref, ck_ref, s_ref = refs[N_WKV_ROWS + N_WKV_PAR:]
        ch = pl.program_id(1)

        @pl.when(ch == 0)
        def _():
            s_ref[...] = jnp.zeros_like(s_ref)

        s0 = s_ref[...]
        yo, s_new = chunk_fn(s0, *row_vals, *par_vals)
        s_ref[...] = s_new
        for b in range(nbat):
            for q in range(pp):
                ck_ref[b, q] = s0[b * pp + q]
                yo_ref[b, :, q * LANES:(q + 1) * LANES] = yo[b * pp + q].astype(yo_ref.dtype)

    tok = pl.BlockSpec((nbat, CHUNK, pp * LANES), lambda p, ch: (0, ch, p))
    par = pl.BlockSpec((1, pp * LANES), lambda p, ch: (0, p))
    ck = pl.BlockSpec((nbat, pp, None, LANES, LANES), lambda p, ch: (0, p, ch, 0, 0))
    yo, ckpt = pl.pallas_call(
        body, name="wkv_fwd",
        out_shape=[jax.ShapeDtypeStruct((nbat, seq, D), BF16),
                   jax.ShapeDtypeStruct((nbat, npair, nch, LANES, LANES), F32)],
        grid=(npair // pp, nch),
        in_specs=[tok] * N_WKV_ROWS + [par] * N_WKV_PAR, out_specs=[tok, ck],
        scratch_shapes=[pltpu.VMEM((nbat * pp, LANES, LANES), F32)],
        compiler_params=_params(("parallel", "arbitrary")),
    )(*[t.reshape(nbat, seq, D) for t in rows], *pars)
    return yo.reshape(n, D), ckpt


def _wkv_bwd(seq, rows, pars, ckpt, dyo):
    n = rows[0].shape[0]
    nbat, nch, npair, pp = n // seq, seq // CHUNK, D // LANES, WKV_PAIRS_PER_STEP
    chunk_fn = functools.partial(_wkv_chunk, _wkv_mms(WKV_PRECISION))
    n_in = N_WKV_ROWS + N_WKV_PAR

    def body(*refs):
        row_vals = [_to_chains(r[...], nbat, pp) for r in refs[:N_WKV_ROWS]]
        par_vals = [_to_chains(r[...], nbat, pp) for r in refs[N_WKV_ROWS:n_in]]
        ck_ref, dyo_ref = refs[n_in:n_in + 2]
        d_rows = refs[n_in + 2:n_in + 2 + N_WKV_ROWS]
        d_pars = refs[n_in + 2 + N_WKV_ROWS:n_in + 2 + N_WKV_ROWS + N_WKV_PAR]
        ds_ref = refs[-1]
        ch = pl.program_id(1)

        @pl.when(ch == 0)
        def _():
            ds_ref[...] = jnp.zeros_like(ds_ref)

        s0 = jnp.stack([ck_ref[b, q] for b in range(nbat) for q in range(pp)])
        dyo_v = _to_chains(dyo_ref[...].astype(F32), nbat, pp)
        _, vjp = jax.vjp(chunk_fn, s0, *row_vals, *par_vals)
        grads = vjp((dyo_v, ds_ref[...]))
        ds_ref[...] = grads[0]
        for ref, val in zip(d_rows, grads[1:1 + N_WKV_ROWS]):
            for b in range(nbat):
                for q in range(pp):
                    ref[b, :, q * LANES:(q + 1) * LANES] = val[b * pp + q].astype(ref.dtype)
        for ref, val in zip(d_pars, grads[1 + N_WKV_ROWS:]):
            per_pair = [functools.reduce(lambda s, t: s + t, [val[b * pp + q] for b in range(nbat)])
                        for q in range(pp)]
            tot = jnp.concatenate(per_pair, axis=1)

            @pl.when(ch == 0)
            def _(ref=ref, tot=tot):
                ref[...] = tot

            @pl.when(ch != 0)
            def _(ref=ref, tot=tot):
                ref[...] += tot

    tok = pl.BlockSpec((nbat, CHUNK, pp * LANES), lambda p, ch: (0, nch - 1 - ch, p))
    par = pl.BlockSpec((1, pp * LANES), lambda p, ch: (0, p))
    ck = pl.BlockSpec((nbat, pp, None, LANES, LANES), lambda p, ch: (0, p, nch - 1 - ch, 0, 0))
    res = pl.pallas_call(
        body, name="wkv_bwd",
        out_shape=[jax.ShapeDtypeStruct((nbat, seq, D), BF16)] * N_WKV_ROWS
        + [jax.ShapeDtypeStruct((1, D), F32)] * N_WKV_PAR,
        grid=(npair // pp, nch),
        in_specs=[tok] * N_WKV_ROWS + [par] * N_WKV_PAR + [ck, tok],
        out_specs=[tok] * N_WKV_ROWS + [par] * N_WKV_PAR,
        scratch_shapes=[pltpu.VMEM((nbat * pp, LANES, LANES), F32)],
        compiler_params=_params(("parallel", "arbitrary")),
    )(*[t.reshape(nbat, seq, D) for t in rows], *pars, ckpt, dyo.reshape(nbat, seq, D))
    return [t.reshape(n, D) for t in res[:N_WKV_ROWS]] + list(res[N_WKV_ROWS:])


def _place():
    return lax.axis_index("x"), lax.axis_index("y"), lax.axis_index("c")


def _all_gather8(blk):
    m_per, n = blk.shape
    assert m_per % 8 == 0

    def body(x_ref, out_ref, send_sems, recv_sems, local_sem):
        x, y, c = _place()
        me, sibling = (x, y, c), (x, y, 1 - c)
        chips = [(1 - x, y), (x, 1 - y), (1 - x, 1 - y)]

        def rows(px, py, pc):
            return out_ref.at[pl.ds((4 * px + 2 * py + pc) * m_per, m_per), :]

        def copy(k, block, to, src=None):
            return pltpu.make_async_remote_copy(
                src_ref=rows(*block) if src is None else src, dst_ref=rows(*block),
                send_sem=send_sems.at[k], recv_sem=recv_sems.at[k],
                device_id=to, device_id_type=MESH)

        mine = pltpu.make_async_copy(x_ref, rows(*me), local_sem)
        mine.start()
        first = [copy(0, me, sibling, src=x_ref)]
        first += [copy(1 + j, me, (*chip, c), src=x_ref) for j, chip in enumerate(chips)]
        for cp in first:
            cp.start()
        passed = [copy(4 + j, (*chip, c), sibling) for j, chip in enumerate(chips)]
        for j, chip in enumerate(chips):
            copy(1 + j, (*chip, c), me).wait_recv()
            passed[j].start()
        copy(0, sibling, me).wait_recv()
        for j, chip in enumerate(chips):
            copy(4 + j, (*chip, 1 - c), me).wait_recv()
        for cp in first + passed:
            cp.wait_send()
        mine.wait()

    vm = pl.BlockSpec(memory_space=pltpu.VMEM)
    return pl.pallas_call(
        body, name="all_gather8_%dx%d" % (m_per, n),
        out_shape=jax.ShapeDtypeStruct((N_DEV * m_per, n), blk.dtype),
        in_specs=[vm], out_specs=vm,
        scratch_shapes=[pltpu.SemaphoreType.DMA((7,)), pltpu.SemaphoreType.DMA((7,)),
                        pltpu.SemaphoreType.DMA],
        compiler_params=_params(),
    )(blk)


def _own_slot(src, name):
    r, w = src.shape[-2:]
    tr = _tile(r, 1008)
    xi, yi, _ = _place()
    chip = jnp.reshape(2 * xi + yi, (1,)).astype(jnp.int32)

    def body(chip_ref, x_ref, o_ref):
        o_ref[...] = x_ref[...]

    if src.ndim == 2:
        in_spec = pl.BlockSpec((tr, w), lambda i, chip_ref: (i, 0))
    else:
        in_spec = pl.BlockSpec((None, tr, w), lambda i, chip_ref: (chip_ref[0], i, 0))
    return pl.pallas_call(
        body, name=name,
        out_shape=jax.ShapeDtypeStruct((N_CHIPS, r, w), src.dtype),
        grid_spec=pltpu.PrefetchScalarGridSpec(
            num_scalar_prefetch=1, grid=(r // tr,), in_specs=[in_spec],
            out_specs=pl.BlockSpec((None, tr, w), lambda i, chip_ref: (chip_ref[0], i, 0))),
        compiler_params=_params(("parallel",)),
    )(chip, src)


def _chip_all_gather(shard):
    r, w = shard.shape
    half = r // 2
    assert r % 2 == 0 and half % 16 == 0

    def body(x_ref, buf_ref, out_ref, send_sems, recv_sems):
        del buf_ref
        x, y, c = _place()
        sibling = (x, y, 1 - c)
        me_p = 2 * x + y
        chips = [(1 - x, y), (x, 1 - y), (1 - x, 1 - y)]

        def piece(p, h):
            return out_ref.at[p, pl.ds(h * half, half), :]

        def copy(k, p, h, to, src=None):
            return pltpu.make_async_remote_copy(
                src_ref=piece(p, h) if src is None else src, dst_ref=piece(p, h),
                send_sem=send_sems.at[k], recv_sem=recv_sems.at[k],
                device_id=to, device_id_type=MESH)

        my_half = x_ref.at[pl.ds(c * half, half), :]
        first = [copy(j, me_p, c, (*chip, c), src=my_half) for j, chip in enumerate(chips)]
        for cp in first:
            cp.start()
        passed = [copy(3 + j, 2 * chip[0] + chip[1], c, sibling) for j, chip in enumerate(chips)]
        for j, chip in enumerate(chips):
            copy(j, 2 * chip[0] + chip[1], c, sibling).wait_recv()
            passed[j].start()
        for j, chip in enumerate(chips):
            copy(3 + j, 2 * chip[0] + chip[1], 1 - c, sibling).wait_recv()
        for cp in first + passed:
            cp.wait_send()

    hbm = pl.BlockSpec(memory_space=pl.ANY)
    return pl.pallas_call(
        body, name="chip_all_gather",
        out_shape=jax.ShapeDtypeStruct((N_CHIPS, r, w), shard.dtype),
        in_specs=[hbm, hbm], out_specs=hbm, input_output_aliases={1: 0},
        scratch_shapes=[pltpu.SemaphoreType.DMA((6,)), pltpu.SemaphoreType.DMA((6,))],
        compiler_params=_params(),
    )(shard, _own_slot(shard, "gather_own_slot"))


def _sibling_swap_halves(g):
    _, r, w = g.shape
    half = r // 2

    def body(g_ref, t_ref, send_sem, recv_sem):
        x, y, c = _place()
        sibling = (x, y, 1 - c)
        cp = pltpu.make_async_remote_copy(
            src_ref=g_ref.at[:, pl.ds((1 - c) * half, half), :], dst_ref=t_ref,
            send_sem=send_sem, recv_sem=recv_sem, device_id=sibling, device_id_type=MESH)
        cp.start()
        cp.wait()

    hbm = pl.BlockSpec(memory_space=pl.ANY)
    return pl.pallas_call(
        body, name="rs_sibling_halves",
        out_shape=jax.ShapeDtypeStruct((N_CHIPS, half, w), g.dtype),
        in_specs=[hbm], out_specs=hbm,
        scratch_shapes=[pltpu.SemaphoreType.DMA, pltpu.SemaphoreType.DMA],
        compiler_params=_params(),
    )(g)


def _add_own_half(g, t):
    _, r, w = g.shape
    half = r // 2
    tr = 1008 if half % 1008 == 0 else 16
    assert half % tr == 0
    cidx = jnp.reshape(lax.axis_index("c"), (1,)).astype(jnp.int32)

    def body(c_ref, g_ref, t_ref, o_ref):
        o_ref[...] = (g_ref[...] + t_ref[...]).astype(o_ref.dtype)

    return pl.pallas_call(
        body, name="rs_add_halves",
        out_shape=jax.ShapeDtypeStruct((N_CHIPS, half, w), BF16),
        grid_spec=pltpu.PrefetchScalarGridSpec(
            num_scalar_prefetch=1, grid=(N_CHIPS, half // tr),
            in_specs=[pl.BlockSpec((None, None, tr, w), lambda p, i, c_ref: (p, c_ref[0], i, 0)),
                      pl.BlockSpec((None, tr, w), lambda p, i, c_ref: (p, i, 0))],
            out_specs=pl.BlockSpec((None, tr, w), lambda p, i, c_ref: (p, i, 0))),
        compiler_params=_params(("parallel", "parallel")),
    )(cidx, g.reshape(N_CHIPS, 2, half, w), t)


def _chip_exchange(h):
    _, hh, w = h.shape

    def body(h_ref, buf_ref, t_ref, send_sems, recv_sems):
        del buf_ref
        x, y, c = _place()
        me_p = 2 * x + y
        chips = [(1 - x, y), (x, 1 - y), (1 - x, 1 - y)]
        cps = []
        for j, chip in enumerate(chips):
            q = 2 * chip[0] + chip[1]
            cps.append(pltpu.make_async_remote_copy(
                src_ref=h_ref.at[q], dst_ref=t_ref.at[me_p],
                send_sem=send_sems.at[j], recv_sem=recv_sems.at[j],
                device_id=(*chip, c), device_id_type=MESH))
        for cp in cps:
            cp.start()
        for j, chip in enumerate(chips):
            q = 2 * chip[0] + chip[1]
            pltpu.make_async_remote_copy(
                src_ref=h_ref.at[q], dst_ref=t_ref.at[q],
                send_sem=send_sems.at[j], recv_sem=recv_sems.at[j],
                device_id=(*chip, c), device_id_type=MESH).wait_recv()
        for cp in cps:
            cp.wait_send()

    hbm = pl.BlockSpec(memory_space=pl.ANY)
    return pl.pallas_call(
        body, name="rs_chip_exchange",
        out_shape=jax.ShapeDtypeStruct(h.shape, h.dtype),
        in_specs=[hbm, hbm], out_specs=hbm, input_output_aliases={1: 0},
        scratch_shapes=[pltpu.SemaphoreType.DMA((3,)), pltpu.SemaphoreType.DMA((3,))],
        compiler_params=_params(),
    )(h, _own_slot(h, "rs_own_slot"))


def _sum_slots(t):
    _, hh, w = t.shape
    tr = 1008 if hh % 1008 == 0 else 16
    assert hh % tr == 0
    nblk = hh // tr
    cidx = jnp.reshape(lax.axis_index("c"), (1,)).astype(jnp.int32)

    def body(c_ref, t_ref, o_ref):
        s0, s1, s2, s3 = [t_ref[j].astype(F32) for j in range(N_CHIPS)]
        o_ref[...] = ((s0 + s1) + s2) + s3

    return pl.pallas_call(
        body, name="rs_sum_slots", out_shape=jax.ShapeDtypeStruct((2 * hh, w), F32),
        grid_spec=pltpu.PrefetchScalarGridSpec(
            num_scalar_prefetch=1, grid=(nblk,),
            in_specs=[pl.BlockSpec((N_CHIPS, tr, w), lambda i, c_ref: (0, i, 0))],
            out_specs=pl.BlockSpec((tr, w), lambda i, c_ref: (c_ref[0] * nblk + i, 0))),
        compiler_params=_params(("parallel",)),
    )(cidx, t)


def _sibling_join_halves(s):
    h2, w = s.shape
    hh = h2 // 2

    def body(s_ref, o_ref, send_sem, recv_sem):
        del s_ref
        x, y, c = _place()
        sibling = (x, y, 1 - c)
        cp = pltpu.make_async_remote_copy(
            src_ref=o_ref.at[pl.ds(c * hh, hh), :], dst_ref=o_ref.at[pl.ds(c * hh, hh), :],
            send_sem=send_sem, recv_sem=recv_sem, device_id=sibling, device_id_type=MESH)
        cp.start()
        pltpu.make_async_remote_copy(
            src_ref=o_ref.at[pl.ds((1 - c) * hh, hh), :], dst_ref=o_ref.at[pl.ds((1 - c) * hh, hh), :],
            send_sem=send_sem, recv_sem=recv_sem, device_id=sibling, device_id_type=MESH).wait_recv()
        cp.wait_send()

    hbm = pl.BlockSpec(memory_space=pl.ANY)
    return pl.pallas_call(
        body, name="rs_sibling_join",
        out_shape=jax.ShapeDtypeStruct(s.shape, s.dtype),
        in_specs=[hbm], out_specs=hbm, input_output_aliases={0: 0},
        scratch_shapes=[pltpu.SemaphoreType.DMA, pltpu.SemaphoreType.DMA],
        compiler_params=_params(),
    )(s)


def _reduce_scatter(g):
    h = _add_own_half(g, _sibling_swap_halves(g))
    return _sibling_join_halves(_sum_slots(_chip_exchange(h)))


def _unshard_cols(piece):
    p, k, n = piece.shape
    return jnp.transpose(piece, (1, 0, 2)).reshape(k, p * n)


def _shard_cols(full):
    k, n4 = full.shape
    return jnp.transpose(full.reshape(k, N_CHIPS, n4 // N_CHIPS), (1, 0, 2))


def _rows_of(piece):
    return piece.reshape(N_CHIPS, -1, ROW_W)


def _pad_rows(a, mult):
    pad = (-a.shape[-2]) % mult
    if pad == 0:
        return a
    widths = [(0, 0)] * (a.ndim - 2) + [(0, pad), (0, 0)]
    return jnp.pad(a, widths)


def _adamw(w, g, m, v):
    m2 = ADAM_B1 * m + (1.0 - ADAM_B1) * g
    v2 = ADAM_B2 * v + (1.0 - ADAM_B2) * (g * g)
    m_hat = m2 / (1.0 - ADAM_B1 ** ADAM_STEP)
    v_hat = v2 / (1.0 - ADAM_B2 ** ADAM_STEP)
    delta = -ADAM_LR * (m_hat / (jnp.sqrt(v_hat) + ADAM_EPS) + ADAM_WD * w)
    return delta, m2, v2


def _mlp_fwd(seq, tm, x_in, mix, mod, w1, w2, tag, residual=True):
    x1, h2 = _rw_fwd(_f_res_norm2, "res_norm2_" + tag, seq=seq, tm=tm, rows=[x_in, mix], bvecs=[mod],
                     outs=[(D, F32), (D, BF16)])
    def relu_and_square(acc):
        r = jnp.maximum(acc, 0.0)
        return r, r * r

    p, f = _matmul_ep(h2, w1, [], relu_and_square, [BF16, BF16], mode="nn", name="mlp_up_" + tag)
    ff = _matmul(f, w2, mode="nn", name="mlp_down_" + tag)
    x2 = None
    if residual:
        (x2,) = _rw_fwd(_f_res2, "res2_" + tag, seq=seq, tm=tm, rows=[x1, ff], bvecs=[mod], outs=[(D, F32)])
    return x2, (x1, h2, p, f, ff)


def _mlp_bwd(seq, tm, saved, x_in, mix, mod, w1, w2, dx2, tag, res2_grads=None):
    x1, h2, p, f, ff = saved
    if res2_grads is None:
        dff, dmod_a = _rw_bwd(_f_res2, "res2_bwd_" + tag, seq=seq, tm=tm, rows=[x1, ff], bvecs=[mod],
                              cts=[dx2], need_rows=[1], dtypes=[BF16])
    else:
        dff, dmod_a = res2_grads
    (dp,) = _matmul_ep(dff, w2, [p], lambda acc, pt: (2.0 * pt.astype(F32) * acc,), [BF16],
                       mode="nt", name="mlp_down_dx_" + tag)
    dw2 = _matmul(f, dff, mode="tn", name="mlp_down_dw_" + tag)
    dh2 = _matmul(dp, w1, mode="nt", name="mlp_up_dx_" + tag)
    dw1 = _matmul(h2, dp, mode="tn", name="mlp_up_dw_" + tag, out_shards=N_CHIPS)
    dx_in, dmix, dmod_b = _rw_bwd(_f_res_norm2, "res_norm2_bwd_" + tag, seq=seq, tm=tm, rows=[x_in, mix],
                                  bvecs=[mod], cts=[dx2, dh2], need_rows=[0, 1], dtypes=[F32, BF16])
    return dx_in, dmix, dmod_a + dmod_b, dw1, dw2


def kernel(x, c, ada_w, ada_b, mlp_w1, mlp_w2, a_w_in, a_ln_g, a_ln_b, a_w_s, a_b_s, a_w_out, b_mu, b_w_in, b_w0, b_w1, b_w2, b_a0, b_a1, b_a2, b_g1, b_g2, b_k_k, b_k_a, b_r_k, b_ln_g, b_ln_b, b_w_out, final_g, loss_target, m_ada_w, m_ada_b, m_mlp_w1, m_mlp_w2, m_a_w_in, m_a_ln_g, m_a_ln_b, m_a_w_s, m_a_b_s, m_a_w_out, m_b_mu, m_b_w_in, m_b_w0, m_b_w1, m_b_w2, m_b_a0, m_b_a1, m_b_a2, m_b_g1, m_b_g2, m_b_k_k, m_b_k_a, m_b_r_k, m_b_ln_g, m_b_ln_b, m_b_w_out, m_final_g, v_ada_w, v_ada_b, v_mlp_w1, v_mlp_w2, v_a_w_in, v_a_ln_g, v_a_ln_b, v_a_w_s, v_a_b_s, v_a_w_out, v_b_mu, v_b_w_in, v_b_w0, v_b_w1, v_b_w2, v_b_a0, v_b_a1, v_b_a2, v_b_g1, v_b_g2, v_b_k_k, v_b_k_a, v_b_r_k, v_b_ln_g, v_b_ln_b, v_b_w_out, v_final_g):
    weights = dict(ada_w=ada_w, ada_b=ada_b, mlp_w1=mlp_w1, mlp_w2=mlp_w2, a_w_in=a_w_in, a_ln_g=a_ln_g,
                   a_ln_b=a_ln_b, a_w_s=a_w_s, a_b_s=a_b_s, a_w_out=a_w_out, b_mu=b_mu, b_w_in=b_w_in,
                   b_w0=b_w0, b_w1=b_w1, b_w2=b_w2, b_a0=b_a0, b_a1=b_a1, b_a2=b_a2, b_g1=b_g1, b_g2=b_g2,
                   b_k_k=b_k_k, b_k_a=b_k_a, b_r_k=b_r_k, b_ln_g=b_ln_g, b_ln_b=b_ln_b, b_w_out=b_w_out,
                   final_g=final_g)
    moms = dict(ada_w=(m_ada_w, v_ada_w), ada_b=(m_ada_b, v_ada_b), mlp_w1=(m_mlp_w1, v_mlp_w1),
                mlp_w2=(m_mlp_w2, v_mlp_w2), a_w_in=(m_a_w_in, v_a_w_in), a_ln_g=(m_a_ln_g, v_a_ln_g),
                a_ln_b=(m_a_ln_b, v_a_ln_b), a_w_s=(m_a_w_s, v_a_w_s), a_b_s=(m_a_b_s, v_a_b_s),
                a_w_out=(m_a_w_out, v_a_w_out), b_mu=(m_b_mu, v_b_mu), b_w_in=(m_b_w_in, v_b_w_in),
                b_w0=(m_b_w0, v_b_w0), b_w1=(m_b_w1, v_b_w1), b_w2=(m_b_w2, v_b_w2), b_a0=(m_b_a0, v_b_a0),
                b_a1=(m_b_a1, v_b_a1), b_a2=(m_b_a2, v_b_a2), b_g1=(m_b_g1, v_b_g1), b_g2=(m_b_g2, v_b_g2),
                b_k_k=(m_b_k_k, v_b_k_k), b_k_a=(m_b_k_a, v_b_k_a), b_r_k=(m_b_r_k, v_b_r_k),
                b_ln_g=(m_b_ln_g, v_b_ln_g), b_ln_b=(m_b_ln_b, v_b_ln_b), b_w_out=(m_b_w_out, v_b_w_out),
                final_g=(m_final_g, v_final_g))
    order = list(weights)

    nbat, seq, _ = x.shape
    n = nbat * seq
    tm = 256
    xi, yi, ci = _place()
    chip = 2 * xi + yi
    dev = 2 * chip + ci
    x0 = x.reshape(n, D)
    tgt = loss_target.reshape(n, D)
    lora_w, lora_g = b_w1.shape[-1], b_g1.shape[-1]
    lora_wp, lora_gp = LANES, 2 * LANES

    (cond,) = _small(lambda cc: (cc / (1.0 + jnp.exp(-cc)),), "silu_c", [c], [c.shape])
    vec_names = ["b_w0", "b_a0", "b_k_k", "b_k_a", "b_ln_g", "b_ln_b"]
    vec_shard = jnp.concatenate([b_mu[0]] + [weights[k] for k in vec_names], axis=0)
    n_vec = vec_shard.shape[0]
    vec_rows = vec_shard.reshape(-1, ROW_W)
    blk = _pad_rows(jnp.concatenate([cond, vec_rows], axis=0), 8)
    assert blk.shape[0] == 8
    gathered = _all_gather8(blk).reshape(N_DEV, 8, D)
    cond_all = gathered[:, :nbat].reshape(N_DEV * nbat, D)
    vec_all = gathered[0::2, nbat:nbat + vec_rows.shape[0]].reshape(N_CHIPS, n_vec, D // N_CHIPS)
    vec_full = jnp.transpose(vec_all, (1, 0, 2)).reshape(n_vec, D)
    mu_full = vec_full[0:6]
    w0_f, a0_f, kk_f, ka_f, lng_f, lnb_f = [vec_full[6 + j:7 + j] for j in range(6)]
    rk_f = b_r_k.reshape(1, D)

    n_ada = ada_w.shape[-1]
    parts = jnp.concatenate(
        [_matmul(cond_all, ada_w[i], mode="nn", name="ada_fwd_%d" % i) for i in range(2)], axis=1)
    parts_all = _all_gather8(parts).reshape(N_DEV, N_DEV * nbat, 2, n_ada)[0::2]
    mine = lax.dynamic_slice_in_dim(parts_all, dev * nbat, nbat, axis=1)
    mods = []
    for i in range(2):
        full = jnp.transpose(mine[:, :, i], (1, 0, 2)).reshape(nbat, N_MOD * D) + ada_b[i]
        mods.append(full.reshape(nbat, N_MOD, D))

    big = [("mlp_w1_0", mlp_w1[0]), ("mlp_w1_1", mlp_w1[1]), ("mlp_w2_0", mlp_w2[0]), ("mlp_w2_1", mlp_w2[1]),
           ("a_w_in", a_w_in[0]), ("a_w_out", a_w_out[0]), ("b_w_in", b_w_in[0]), ("b_w_out", b_w_out[0]),
           ("b_w1", b_w1[0]), ("b_w2", b_w2[0]), ("b_a1", b_a1[0]), ("b_a2", b_a2[0]),
           ("b_g1", b_g1[0]), ("b_g2", b_g2[0])]
    offs, pos = {}, 0
    for name, arr in big:
        rows_k = arr.size // ROW_W
        offs[name] = (pos, rows_k, arr.shape)
        pos += rows_k
    n_big_rows = pos
    wflat = _pad_rows(jnp.concatenate([arr.astype(BF16).reshape(-1, ROW_W) for _, arr in big], axis=0), 32)
    wg = _chip_all_gather(wflat)

    def gathered_piece(name):
        start, rows_k, shape = offs[name]
        return wg[:, start:start + rows_k].reshape((N_CHIPS,) + shape)

    def col_w(name):
        return _unshard_cols(gathered_piece(name))

    def row_w(name):
        piece = gathered_piece(name)
        return piece.reshape(N_CHIPS * piece.shape[1], piece.shape[2])

    w1_l = [col_w("mlp_w1_0"), col_w("mlp_w1_1")]
    w2_l = [row_w("mlp_w2_0"), row_w("mlp_w2_1")]
    a_win, a_wout = col_w("a_w_in"), row_w("a_w_out")
    b_win, b_wout = col_w("b_w_in"), row_w("b_w_out")
    w_r, w_k, w_v = b_win[:, :D], b_win[:, D:2 * D], b_win[:, 2 * D:]
    w1p = jnp.pad(row_w("b_w1"), ((0, 0), (0, lora_wp - lora_w)))
    a1p = jnp.pad(row_w("b_a1"), ((0, 0), (0, lora_wp - lora_w)))
    g1p = jnp.pad(row_w("b_g1"), ((0, 0), (0, lora_gp - lora_g)))
    w2p = jnp.pad(col_w("b_w2"), ((0, lora_wp - lora_w), (0, 0)))
    a2p = jnp.pad(col_w("b_a2"), ((0, lora_wp - lora_w), (0, 0)))
    g2p = jnp.pad(col_w("b_g2"), ((0, lora_gp - lora_g), (0, 0)))

    mod0, mod1 = mods
    (h_a,) = _rw_fwd(_f_norm1, "norm1_a", seq=seq, tm=tm, rows=[x0], bvecs=[mod0], outs=[(D, BF16)])
    uvp = _matmul(h_a, a_win, mode="nn", name="sgu_in")
    ws = a_w_s[0]
    bias = jnp.broadcast_to(a_b_s[0][:, :, None], (GROUPS, CHUNK, LANES))
    z = _sgu_fwd(uvp, a_ln_g, a_ln_b, ws, bias)
    mix0 = _matmul(z, a_wout, mode="nn", name="sgu_out")
    x2, saved0 = _mlp_fwd(seq, tm, x0, mix0, mod0, w1_l[0], w2_l[0], "0")

    def shift_fwd(ctx, rv, pv, nv, bv, pa):
        _, _, mixes = _shift_mix(ctx, rv[0], pv[0], bv[0], pa[0])
        return mixes, [], []

    xr, xw, xk, xv, xa, xg = _rowwise(shift_fwd, "shift_mix", seq=seq, tm=tm, rows=[x2], prev8=[x2],
                                      bvecs=[mod1], params=[mu_full], out_rows=[(D, BF16)] * 6)
    r = _matmul(xr, w_r, mode="nn", name="rwkv_r")
    k = _matmul(xk, w_k, mode="nn", name="rwkv_k")
    v = _matmul(xv, w_v, mode="nn", name="rwkv_v")
    def act_tanh(t):
        return jnp.tanh(t)

    def act_sigmoid(t):
        return 1.0 / (1.0 + jnp.exp(-t))

    t1, th = _matmul_ep(xw, w1p, [], lambda acc: (acc, act_tanh(acc)), [F32, BF16], mode="nn", name="lora_w1")
    t2 = _matmul(xa, a1p, mode="nn", name="lora_a1", out_dtype=BF16)
    t3, sg = _matmul_ep(xg, g1p, [], lambda acc: (acc, act_sigmoid(acc)), [F32, BF16], mode="nn", name="lora_g1")
    wl = _matmul(th, w2p, mode="nn", name="lora_w2")
    al = _matmul(t2, a2p, mode="nn", name="lora_a2")
    g = _matmul(sg, g2p, mode="nn", name="lora_g2")
    wkv_rows = [r, k, v, wl, al, g]
    wkv_pars = [w0_f, a0_f, kk_f, ka_f, rk_f, lng_f, lnb_f]
    yo, ckpt = _wkv_fwd(seq, wkv_rows, wkv_pars)
    mix1 = _matmul(yo, b_wout, mode="nn", name="rwkv_out")
    _, saved1 = _mlp_fwd(seq, tm, x2, mix1, mod1, w1_l[1], w2_l[1], "1", residual=False)

    def loss_fn(ctx, rv, pv, nv, bv, pa):
        def head(x1, ff, mod, fg):
            return _f_loss(_f_res2(x1, ff, mod)[0], rv[2], fg)
        val, (dx, dff, dmod, dfg) = jax.value_and_grad(head, argnums=(0, 1, 2, 3))(rv[0], rv[1], bv[0], pa[0])
        return [dx, dff], [dmod], [dfg, jnp.full((1, LANES), val, F32)]

    dx4, dff1, dmod1_a, d_final_g, loss_acc = _rowwise(
        loss_fn, "loss_head", seq=seq, tm=tm, rows=[saved1[0], saved1[4], tgt], bvecs=[mod1],
        params=[final_g.reshape(1, D)], out_rows=[(D, F32), (D, BF16)], out_bacc=[(N_MOD, D)],
        out_pacc=[(1, D), (1, LANES)])
    loss = lax.psum(loss_acc[0, 0], AXES)

    dx2_a, dmix1, dmod1, dw1_1, dw2_1 = _mlp_bwd(seq, tm, saved1, x2, mix1, mod1, w1_l[1], w2_l[1], dx4, "1",
                                                 res2_grads=(dff1, dmod1_a))
    dyo = _matmul(dmix1, b_wout, mode="nt", name="rwkv_out_dx")
    d_b_wout = _matmul(yo, dmix1, mode="tn", name="rwkv_out_dw")
    wkv_grads = _wkv_bwd(seq, wkv_rows, wkv_pars, ckpt, dyo)
    dr, dk, dv, dwl, dal, dg = wkv_grads[:N_WKV_ROWS]
    d_w0, d_a0, d_kk, d_ka, d_rk, d_lng, d_lnb = wkv_grads[N_WKV_ROWS:]
    def through(act):
        return lambda acc, t: (jax.vjp(act, t)[1](acc)[0],)

    (dt1,) = _matmul_ep(dwl, w2p, [t1], through(act_tanh), [BF16], mode="nt", name="lora_w2_dx")
    d_w2p = _matmul(th, dwl, mode="tn", name="lora_w2_dw")
    dt2 = _matmul(dal, a2p, mode="nt", name="lora_a2_dx", out_dtype=BF16)
    d_a2p = _matmul(t2, dal, mode="tn", name="lora_a2_dw")
    (dt3,) = _matmul_ep(dg, g2p, [t3], through(act_sigmoid), [BF16], mode="nt", name="lora_g2_dx")
    d_g2p = _matmul(sg, dg, mode="tn", name="lora_g2_dw")
    dxw = _matmul(dt1, w1p, mode="nt", name="lora_w1_dx")
    d_w1p = _matmul(xw, dt1, mode="tn", name="lora_w1_dw")
    dxa = _matmul(dt2, a1p, mode="nt", name="lora_a1_dx")
    d_a1p = _matmul(xa, dt2, mode="tn", name="lora_a1_dw")
    dxg = _matmul(dt3, g1p, mode="nt", name="lora_g1_dx")
    d_g1p = _matmul(xg, dt3, mode="tn", name="lora_g1_dw")
    dxr = _matmul(dr, w_r, mode="nt", name="rwkv_r_dx")
    dxk = _matmul(dk, w_k, mode="nt", name="rwkv_k_dx")
    dxv = _matmul(dv, w_v, mode="nt", name="rwkv_v_dx")
    d_b_win = jnp.concatenate([_matmul(xr, dr, mode="tn", name="rwkv_r_dw"),
                               _matmul(xk, dk, mode="tn", name="rwkv_k_dw"),
                               _matmul(xv, dv, mode="tn", name="rwkv_v_dw")], axis=1)

    def shift_bwd(ctx, rv, pv, nv, bv, pa):
        xt, dres = rv[0], rv[1]
        dmix_in = rv[2:8]
        mod, mu = bv[0], pa[0]
        f_h = lambda xx_, mod_: _rmsmod(xx_, mod_[0:1], mod_[1:2])
        h, vjp = jax.vjp(f_h, xt, mod)
        hprev = f_h(pv[0], mod)[7:8]
        hprev = jnp.where(ctx.first, jnp.zeros_like(hprev), hprev)
        rowid = lax.broadcasted_iota(jnp.int32, h.shape, 0)
        xx = jnp.where(rowid == 0, hprev, pltpu.roll(h, 1, 0)) - h
        tot = dmix_in[0]
        wsum = dmix_in[0] * mu[0:1]
        for j in range(1, 6):
            tot = tot + dmix_in[j]
            wsum = wsum + dmix_in[j] * mu[j:j + 1]
        nxt = nv[0][0:1] * mu[0:1]
        for j in range(1, 6):
            nxt = nxt + nv[j][0:1] * mu[j:j + 1]
        nxt = jnp.where(ctx.last, jnp.zeros_like(nxt), nxt)
        tmr = h.shape[0]
        wshift = jnp.where(rowid == tmr - 1, nxt, pltpu.roll(wsum, tmr - 1, 0))
        dh = tot - wsum + wshift
        dx_, dmod_ = vjp(dh)
        dmu = jnp.concatenate([jnp.sum(dmix_in[j] * xx, axis=0, keepdims=True) for j in range(6)], axis=0)
        dx2_t = dx_ + dres
        ff_below, mod_below = rv[8], bv[1]
        dff_below = mod_below[5:6] * dx2_t
        dgate = jnp.sum(dx2_t * ff_below, axis=0, keepdims=True)
        gate_row = lax.broadcasted_iota(jnp.int32, (N_MOD, D), 0) == N_MOD - 1
        dmod_below = jnp.where(gate_row, jnp.broadcast_to(dgate, (N_MOD, D)), 0.0)
        return [dx2_t, dff_below], [dmod_, dmod_below], [dmu]

    dmix_list = [dxr, dxw, dxk, dxv, dxa, dxg]
    dx2, dff0, dmod1_c, dmod0_a, d_mu = _rowwise(
        shift_bwd, "shift_mix_bwd", seq=seq, tm=tm, rows=[x2, dx2_a] + dmix_list + [saved0[4]],
        prev8=[x2], next8=dmix_list, bvecs=[mod1, mod0], params=[mu_full],
        out_rows=[(D, F32), (D, BF16)], out_bacc=[(N_MOD, D), (N_MOD, D)], out_pacc=[(6, D)])
    dmod1 = dmod1 + dmod1_c

    dx0_a, dmix0, dmod0, dw1_0, dw2_0 = _mlp_bwd(seq, tm, saved0, x0, mix0, mod0, w1_l[0], w2_l[0], dx2, "0",
                                                 res2_grads=(dff0, dmod0_a))
    dz = _matmul(dmix0, a_wout, mode="nt", name="sgu_out_dx")
    d_a_wout = _matmul(z, dmix0, mode="tn", name="sgu_out_dw")
    duvp, d_a_lng, d_a_lnb, d_ws, d_bias = _sgu_bwd(uvp, a_ln_g, a_ln_b, ws, bias, dz)
    dh_a = _matmul(duvp, a_win, mode="nt", name="sgu_in_dx")
    d_a_win = _matmul(h_a, duvp, mode="tn", name="sgu_in_dw", out_shards=N_CHIPS)
    grad_x, dmod0_c = _rw_bwd(_f_norm1, "norm1_a_bwd", seq=seq, tm=tm, rows=[x0], bvecs=[mod0], cts=[dh_a],
                              need_rows=[0], extra=dx0_a)
    dmod0 = dmod0 + dmod0_c

    dmod_blk = _pad_rows(jnp.concatenate([dmod0.reshape(nbat, -1), dmod1.reshape(nbat, -1)], axis=1), 8)
    dmod_all = _all_gather8(dmod_blk).reshape(N_DEV, 8, 2, N_MOD * D)[:, :nbat].reshape(N_DEV * nbat, 2, N_MOD * D)
    g_ada_w, g_ada_b = [], []
    for i in range(2):
        cols = lax.dynamic_slice_in_dim(dmod_all[:, i], chip * n_ada, n_ada, axis=1)
        g_ada_w.append(_matmul(cond_all, cols, mode="tn", name="ada_dw_%d" % i))
    (g_ada_b_all,) = _small(lambda t: (jnp.sum(t, axis=0),), "ada_db", [dmod_all], [(2, N_MOD * D)])
    grads = {"ada_w": jnp.stack(g_ada_w), "ada_b": g_ada_b_all}

    rep = _pad_rows(jnp.concatenate([
        d_a_lng, d_a_lnb, jnp.sum(d_bias, axis=-1).reshape(1, D), d_rk, d_final_g,
        jnp.zeros((3, D), F32), d_ws.reshape(-1, D)], axis=0), 8)
    rep_rows = rep.shape[0]
    rep_all = _all_gather8(rep)
    (rep_sum,) = _small(lambda t: (functools.reduce(lambda p, q: p + q,
                                                     [t[j * rep_rows:(j + 1) * rep_rows] for j in range(N_DEV)]),),
                        "replicated_sum", [rep_all], [(rep_rows, D)])
    grads["a_ln_g"] = rep_sum[0:1]
    grads["a_ln_b"] = rep_sum[1:2]
    grads["a_b_s"] = rep_sum[2:3].reshape(a_b_s.shape)
    grads["b_r_k"] = rep_sum[3:4].reshape(b_r_k.shape)
    grads["final_g"] = rep_sum[4].reshape(final_g.shape)
    grads["a_w_s"] = rep_sum[8:8 + GROUPS * CHUNK * LANES // D].reshape(a_w_s.shape)

    vec_grads = jnp.concatenate([d_mu, d_w0, d_a0, d_kk, d_ka, d_lng, d_lnb], axis=0)
    packed = {
        "mlp_w1_0": dw1_0, "mlp_w1_1": dw1_1,
        "mlp_w2_0": dw2_0.reshape(N_CHIPS, -1, D), "mlp_w2_1": dw2_1.reshape(N_CHIPS, -1, D),
        "a_w_in": d_a_win, "a_w_out": d_a_wout.reshape(N_CHIPS, -1, D),
        "b_w_in": _shard_cols(d_b_win), "b_w_out": d_b_wout.reshape(N_CHIPS, -1, D),
        "b_w1": d_w1p[:, :lora_w].reshape(N_CHIPS, -1, lora_w), "b_w2": _shard_cols(d_w2p[:lora_w]),
        "b_a1": d_a1p[:, :lora_w].reshape(N_CHIPS, -1, lora_w), "b_a2": _shard_cols(d_a2p[:lora_w]),
        "b_g1": d_g1p[:, :lora_g].reshape(N_CHIPS, -1, lora_g), "b_g2": _shard_cols(d_g2p[:lora_g]),
    }
    pieces = [_rows_of(packed[name]) for name, _ in big] + [_pad_rows(_rows_of(_shard_cols(vec_grads)), 8)]
    used = sum(p.shape[1] for p in pieces)
    pieces.append(jnp.zeros((N_CHIPS, (-used) % 2016, ROW_W), F32))
    g_pack = jnp.concatenate(pieces, axis=1)
    g_red = _reduce_scatter(g_pack)
    for name, _ in big:
        start, rows_k, shape = offs[name]
        grads[name] = g_red[start:start + rows_k].reshape(shape)
    vec_red = g_red[n_big_rows:n_big_rows + vec_rows.shape[0]].reshape(n_vec, D // N_CHIPS)
    grads["b_mu"] = vec_red[0:6].reshape(b_mu.shape)
    for j, name in enumerate(vec_names):
        grads[name] = vec_red[6 + j:7 + j].reshape(weights[name].shape)
    for base in ("mlp_w1", "mlp_w2"):
        grads[base] = jnp.stack([grads.pop(base + "_0"), grads.pop(base + "_1")])
    for name in ("a_w_in", "a_w_out", "b_w_in", "b_w_out", "b_w1", "b_w2", "b_a1", "b_a2", "b_g1", "b_g2"):
        grads[name] = grads[name].reshape(weights[name].shape)

    deltas, new_m, new_v = {}, {}, {}
    for name in order:
        gr = grads[name].reshape(weights[name].shape)
        grads[name] = gr
        deltas[name], new_m[name], new_v[name] = _elementwise(
            _adamw, "adamw_" + name, [weights[name], gr, moms[name][0], moms[name][1]], 3)

    return (loss, grad_x.reshape(x.shape), *[grads[k] for k in order], *[deltas[k] for k in order],
            *[new_m[k] for k in order], *[new_v[k] for k in order])
```

```python
import functools

import jax
import jax.numpy as jnp
from jax import lax
from jax.experimental import pallas as pl
from jax.experimental.pallas import tpu as pltpu

F32 = jnp.float32
BF16 = jnp.bfloat16
MESH = pl.DeviceIdType.MESH
AXES = ("x", "y", "c")

D = 1024
N_MOD = 6
HEAD = 64
CHUNK = 128
GROUPS = 8
LANES = 128
ROW_W = 1024
N_CHIPS = 4
N_DEV = 8

RMS_EPS = 1e-6
LN_EPS = 1e-5
GN_EPS = HEAD * 1e-5
L2_EPS = 1e-12

ADAM_LR = 0.001
ADAM_B1 = 0.9
ADAM_B2 = 0.999
ADAM_EPS = 1e-08
ADAM_WD = 0.01
ADAM_STEP = 10

VMEM_LIMIT_V7X = 56 * 1024 * 1024
HIGHEST = lax.Precision.HIGHEST


def _params(sem=None):
    return pltpu.CompilerParams(dimension_semantics=sem, vmem_limit_bytes=VMEM_LIMIT_V7X)


def _tile(dim, target):
    if dim <= target:
        return dim
    for cand in range(target, 0, -LANES):
        if dim % cand == 0:
            return cand
    raise ValueError((dim, target))


def _matmul(a, b, *, mode, name, out_dtype=F32, out_shards=1, tm=1024, tn=1024, tk=4096):
    if mode == "nn":
        (m, k), (k2, n) = a.shape, b.shape
    elif mode == "nt":
        (m, k), (n, k2) = a.shape, b.shape
    else:
        (k, m), (k2, n) = a.shape, b.shape
    assert k == k2, (name, a.shape, b.shape)
    n_sh = n // out_shards
    tm, tn, tk = _tile(m, tm), _tile(n_sh, tn), _tile(k, tk)
    nk = k // tk
    nb = n_sh // tn
    use_scratch = nk > 1 and out_dtype != F32

    if mode == "tn":
        a_spec = pl.BlockSpec((tk, tm), lambda i, j, kk: (kk, i))
    else:
        a_spec = pl.BlockSpec((tm, tk), lambda i, j, kk: (i, kk))
    if mode == "nt":
        b_spec = pl.BlockSpec((tn, tk), lambda i, j, kk: (j, kk))
    else:
        b_spec = pl.BlockSpec((tk, tn), lambda i, j, kk: (kk, j))
    if out_shards == 1:
        out_shape = jax.ShapeDtypeStruct((m, n), out_dtype)
        o_spec = pl.BlockSpec((tm, tn), lambda i, j, kk: (i, j))
    else:
        out_shape = jax.ShapeDtypeStruct((out_shards, m, n_sh), out_dtype)
        o_spec = pl.BlockSpec((None, tm, tn), lambda i, j, kk: (j // nb, i, j % nb))

    def body(a_ref, b_ref, o_ref, *scratch):
        kk = pl.program_id(2)
        av = a_ref[...].astype(BF16)
        bv = b_ref[...].astype(BF16)
        if mode == "nn":
            dims = (((1,), (0,)), ((), ()))
        elif mode == "nt":
            dims = (((1,), (1,)), ((), ()))
        else:
            dims = (((0,), (0,)), ((), ()))
        part = lax.dot_general(av, bv, dims, preferred_element_type=F32)
        if nk == 1:
            o_ref[...] = part.astype(o_ref.dtype)
            return
        acc_ref = scratch[0] if use_scratch else o_ref

        @pl.when(kk == 0)
        def _():
            acc_ref[...] = part

        @pl.when(kk != 0)
        def _():
            acc_ref[...] += part

        if use_scratch:
            @pl.when(kk == nk - 1)
            def _():
                o_ref[...] = acc_ref[...].astype(o_ref.dtype)

    return pl.pallas_call(
        body, name=name, out_shape=out_shape,
        grid=(m // tm, n // tn, nk),
        in_specs=[a_spec, b_spec], out_specs=o_spec,
        scratch_shapes=[pltpu.VMEM((tm, tn), F32)] if use_scratch else [],
        compiler_params=_params(("parallel", "parallel", "arbitrary")),
    )(a, b)


def _matmul_ep(a, b, extras, epilogue, out_dtypes, *, mode, name, tm=1024, tn=1024, tk=2048):
    if mode == "nn":
        (m, k), (k2, n) = a.shape, b.shape
    else:
        (m, k), (n, k2) = a.shape, b.shape
    assert k == k2 and mode in ("nn", "nt"), (name, a.shape, b.shape)
    tm, tn, tk = _tile(m, tm), _tile(n, tn), _tile(k, tk)
    nk = k // tk
    n_ex, n_out = len(extras), len(out_dtypes)

    def body(a_ref, b_ref, *rest):
        extra_refs, out_refs = rest[:n_ex], rest[n_ex:n_ex + n_out]
        kk = pl.program_id(2)
        dims = (((1,), (0,)), ((), ())) if mode == "nn" else (((1,), (1,)), ((), ()))
        part = lax.dot_general(a_ref[...].astype(BF16), b_ref[...].astype(BF16), dims,
                               preferred_element_type=F32)

        def finish(acc):
            for ref, val in zip(out_refs, epilogue(acc, *[r[...] for r in extra_refs])):
                ref[...] = val.astype(ref.dtype)

        if nk == 1:
            finish(part)
            return
        acc_ref = rest[-1]

        @pl.when(kk == 0)
        def _():
            acc_ref[...] = part

        @pl.when(kk != 0)
        def _():
            acc_ref[...] += part

        @pl.when(kk == nk - 1)
        def _():
            finish(acc_ref[...])

    a_spec = pl.BlockSpec((tm, tk), lambda i, j, kk: (i, kk))
    b_spec = (pl.BlockSpec((tk, tn), lambda i, j, kk: (kk, j)) if mode == "nn"
              else pl.BlockSpec((tn, tk), lambda i, j, kk: (j, kk)))
    o_spec = pl.BlockSpec((tm, tn), lambda i, j, kk: (i, j))
    res = pl.pallas_call(
        body, name=name, out_shape=[jax.ShapeDtypeStruct((m, n), dt) for dt in out_dtypes],
        grid=(m // tm, n // tn, nk),
        in_specs=[a_spec, b_spec] + [o_spec] * n_ex, out_specs=[o_spec] * n_out,
        scratch_shapes=[pltpu.VMEM((tm, tn), F32)] if nk > 1 else [],
        compiler_params=_params(("parallel", "parallel", "arbitrary")),
    )(a, b, *extras)
    return list(res)


class _Ctx:
    def __init__(self, first, last):
        self.first = first
        self.last = last


def _rowwise(fn, name, *, seq, tm, rows=(), prev8=(), next8=(), bvecs=(), params=(),
             out_rows=(), out_bacc=(), out_pacc=()):
    n = rows[0].shape[0]
    tm = min(tm, seq)
    assert n % seq == 0 and seq % tm == 0 and tm % 8 == 0
    tpb = seq // tm
    nt = n // tm
    nbat = n // seq
    r8 = tm // 8
    counts = [len(rows), len(prev8), len(next8), len(bvecs), len(params)]
    n_in = sum(counts)

    def body(*refs):
        i = pl.program_id(0)
        first = (i % tpb) == 0
        last = (i % tpb) == (tpb - 1)
        vals = [r[...] for r in refs[:n_in]]
        groups, pos = [], 0
        for cnt in counts:
            groups.append(vals[pos:pos + cnt])
            pos += cnt
        ro, bo, po = fn(_Ctx(first, last), *groups)
        outs = refs[n_in:]
        assert len(ro) == len(out_rows) and len(bo) == len(out_bacc) and len(po) == len(out_pacc)
        for ref, val in zip(outs[:len(ro)], ro):
            ref[...] = val.astype(ref.dtype)
        for ref, val in zip(outs[len(ro):len(ro) + len(bo)], bo):
            @pl.when(first)
            def _(ref=ref, val=val):
                ref[...] = val

            @pl.when(jnp.logical_not(first))
            def _(ref=ref, val=val):
                ref[...] += val
        for ref, val in zip(outs[len(ro) + len(bo):], po):
            @pl.when(i == 0)
            def _(ref=ref, val=val):
                ref[...] = val

            @pl.when(i != 0)
            def _(ref=ref, val=val):
                ref[...] += val

    in_specs = []
    for arr in rows:
        in_specs.append(pl.BlockSpec((tm, arr.shape[1]), lambda i: (i, 0)))
    for arr in prev8:
        in_specs.append(pl.BlockSpec((8, arr.shape[1]), lambda i: (jnp.maximum(i * r8 - 1, 0), 0)))
    for arr in next8:
        in_specs.append(pl.BlockSpec((8, arr.shape[1]), lambda i: (jnp.minimum((i + 1) * r8, n // 8 - 1), 0)))
    for arr in bvecs:
        in_specs.append(pl.BlockSpec((None,) + arr.shape[1:], lambda i: (i // tpb, 0, 0)))
    for arr in params:
        in_specs.append(pl.BlockSpec(arr.shape, lambda i: (0, 0)))
    out_shape, out_specs = [], []
    for d, dt in out_rows:
        out_shape.append(jax.ShapeDtypeStruct((n, d), dt))
        out_specs.append(pl.BlockSpec((tm, d), lambda i: (i, 0)))
    for r, d in out_bacc:
        out_shape.append(jax.ShapeDtypeStruct((nbat, r, d), F32))
        out_specs.append(pl.BlockSpec((None, r, d), lambda i: (i // tpb, 0, 0)))
    for r, d in out_pacc:
        out_shape.append(jax.ShapeDtypeStruct((r, d), F32))
        out_specs.append(pl.BlockSpec((r, d), lambda i: (0, 0)))
    res = pl.pallas_call(
        body, name=name, out_shape=out_shape, grid=(nt,),
        in_specs=in_specs, out_specs=out_specs,
        compiler_params=_params(("arbitrary",)),
    )(*rows, *prev8, *next8, *bvecs, *params)
    return list(res)


def _rw_fwd(f, name, *, seq, tm, rows, bvecs=(), params=(), outs):
    def fn(ctx, rv, pv, nv, bv, pa):
        res = f(*[v.astype(F32) for v in rv], *bv, *pa)
        return list(res), [], []
    return _rowwise(fn, name, seq=seq, tm=tm, rows=rows, bvecs=bvecs, params=params, out_rows=outs)


def _rw_bwd(f, name, *, seq, tm, rows, bvecs=(), params=(), cts, need_rows, extra=None, dtypes=None):
    nr, nb, npar = len(rows), len(bvecs), len(params)
    all_rows = list(rows) + list(cts) + ([extra] if extra is not None else [])

    def fn(ctx, rv, pv, nv, bv, pa):
        prim = [v.astype(F32) for v in rv[:nr]]
        ct = tuple(v.astype(F32) for v in rv[nr:nr + len(cts)])
        _, vjp = jax.vjp(f, *prim, *bv, *pa)
        g = vjp(ct)
        d_rows = [g[j] for j in need_rows]
        if extra is not None:
            d_rows[0] = d_rows[0] + rv[-1].astype(F32)
        return d_rows, list(g[nr:nr + nb]), list(g[nr + nb:])

    return _rowwise(
        fn, name, seq=seq, tm=tm, rows=all_rows, bvecs=bvecs, params=params,
        out_rows=[(rows[j].shape[1], F32 if dtypes is None else dtypes[i]) for i, j in enumerate(need_rows)],
        out_bacc=[b.shape[1:] for b in bvecs], out_pacc=[p.shape for p in params])


def _small(fn, name, arrays, out_shapes):
    def body(*refs):
        res = fn(*[r[...] for r in refs[:len(arrays)]])
        for ref, val in zip(refs[len(arrays):], res):
            ref[...] = val.astype(ref.dtype)

    vm = pl.BlockSpec(memory_space=pltpu.VMEM)
    res = pl.pallas_call(
        body, name=name,
        out_shape=[jax.ShapeDtypeStruct(s, F32) for s in out_shapes],
        in_specs=[vm] * len(arrays), out_specs=[vm] * len(out_shapes),
        compiler_params=_params(),
    )(*arrays)
    return list(res)


def _elementwise(fn, name, arrays, n_out):
    shape = arrays[0].shape
    size = arrays[0].size
    if len(shape) >= 2 and shape[-1] % LANES == 0 and (size // shape[-1]) % 8 == 0:
        view = (size // shape[-1], shape[-1])
    elif size % ROW_W == 0 and (size // ROW_W) % 8 == 0:
        view = (size // ROW_W, ROW_W)
    else:
        view = (1, size) if len(shape) < 2 else (size // shape[-1], shape[-1])
    rows = view[0]
    tr = rows
    for cand in (256, 128, 64, 32, 16, 8):
        if rows > cand and rows % cand == 0:
            tr = cand
            break

    def body(*refs):
        res = fn(*[r[...] for r in refs[:len(arrays)]])
        for ref, val in zip(refs[len(arrays):], res):
            ref[...] = val

    spec = pl.BlockSpec((tr, view[1]), lambda i: (i, 0))
    res = pl.pallas_call(
        body, name=name,
        out_shape=[jax.ShapeDtypeStruct(view, F32)] * n_out,
        grid=(rows // tr,), in_specs=[spec] * len(arrays), out_specs=[spec] * n_out,
        compiler_params=_params(("parallel",)),
    )(*[a.reshape(view) for a in arrays])
    return [r.reshape(shape) for r in res]


def _rms(x):
    return x * lax.rsqrt(jnp.mean(x * x, axis=-1, keepdims=True) + RMS_EPS)


def _rmsmod(x, sh, sc):
    return _rms(x) * (1.0 + sc) + sh


def _f_norm1(x, mod):
    return (_rmsmod(x, mod[0:1], mod[1:2]),)


def _f_sgu_pre(uvp, ln_g, ln_b):
    uv = 0.5 * uvp * (1.0 + lax.erf(uvp * (2.0 ** -0.5)))
    u = uv[:, :D]
    v = uv[:, D:]
    mu = jnp.mean(v, axis=-1, keepdims=True)
    vc = v - mu
    var = jnp.mean(vc * vc, axis=-1, keepdims=True)
    return u, vc * lax.rsqrt(var + LN_EPS) * ln_g + ln_b


def _f_res_norm2(x, mix, mod):
    x1 = x + mod[2:3] * mix
    return x1, _rmsmod(x1, mod[3:4], mod[4:5])


def _f_res2(x1, ff, mod):
    return (x1 + mod[5:6] * ff,)


def _f_loss(x, tgt, fg):
    err = _rms(x) * fg - tgt
    return 0.5 * jnp.sum(jnp.mean(err * err, axis=-1))


def _shift_mix(ctx, x, xprev8, mod, mu):
    h = _rmsmod(x, mod[0:1], mod[1:2])
    hprev = _rmsmod(xprev8, mod[0:1], mod[1:2])[7:8]
    hprev = jnp.where(ctx.first, jnp.zeros_like(hprev), hprev)
    rowid = lax.broadcasted_iota(jnp.int32, h.shape, 0)
    hp = jnp.where(rowid == 0, hprev, pltpu.roll(h, 1, 0))
    xx = hp - h
    return h, xx, [h + xx * mu[j:j + 1] for j in range(6)]


def _split_bf16(t, parts):
    out, rest = [], t.astype(F32)
    for _ in range(parts):
        piece = rest.astype(BF16)
        out.append(piece)
        rest = rest - piece.astype(F32)
    return out


def _make_mm(na, nb, ct_pieces=1, saved_pieces=1):
    def raw(a, b, pa, pb):
        if pa == 0:
            return jnp.dot(a, b, precision=HIGHEST, preferred_element_type=F32)
        acc = None
        bs = _split_bf16(b, pb)
        for i, ai in enumerate(_split_bf16(a, pa)):
            for j, bj in enumerate(bs):
                if i + j < max(pa, pb):
                    term = jnp.dot(ai, bj, preferred_element_type=F32)
                    acc = term if acc is None else acc + term
        return acc

    @jax.custom_vjp
    def mm(a, b):
        return raw(a, b, na, nb)

    def fwd(a, b):
        return raw(a, b, na, nb), (a, b)

    def bwd(res, ct):
        a, b = res
        if na == 0:
            return raw(ct, b.T, 0, 0), raw(a.T, ct, 0, 0)
        return raw(ct, b.T, ct_pieces, saved_pieces), raw(a.T, ct, saved_pieces, ct_pieces)

    mm.defvjp(fwd, bwd)
    return mm


class _WkvMms:
    def __init__(self, head_sum, cum, score, square, apply, out, state):
        self.head_sum, self.cum, self.score = head_sum, cum, score
        self.square, self.apply, self.out, self.state = square, apply, out, state


def _wkv_mms(cfg):
    table = {"x": (0, 0), "1": (1, 1), "2": (2, 2), "3": (3, 3), "a": (2, 1), "b": (1, 2)}
    hs, cu, sc_, sq, ap, ou, st = [table[ch] for ch in cfg]
    return _WkvMms(_make_mm(hs[0], 1) if hs[0] else _make_mm(0, 0),
                   _make_mm(1, cu[1], ct_pieces=2) if cu[0] else _make_mm(0, 0),
                   _make_mm(*sc_, saved_pieces=2), _make_mm(*sq), _make_mm(*ap, saved_pieces=2),
                   _make_mm(*ou), _make_mm(*st))


WKV_PRECISION = "2221b11"


SGU_CHUNKS_PER_STEP = 4


def _sgu_tile(mm, u, vn, ws, bias):
    row = lax.broadcasted_iota(jnp.int32, (CHUNK, CHUNK), 0)
    col = lax.broadcasted_iota(jnp.int32, (CHUNK, CHUNK), 1)
    wm = [jnp.where(col <= row, w, 0.0) for w in ws]
    out_rows = []
    for ch in range(u.shape[0] // CHUNK):
        rs = slice(ch * CHUNK, (ch + 1) * CHUNK)
        out_rows.append(jnp.concatenate(
            [mm(wm[g], vn[rs, g * LANES:(g + 1) * LANES]) + bias[g] for g in range(GROUPS)], axis=1))
    return u * jnp.concatenate(out_rows, axis=0)


def _sgu_mixer_tile(mm, uvp, ln_g, ln_b, ws, bias):
    u, vn = _f_sgu_pre(uvp, ln_g, ln_b)
    return _sgu_tile(mm, u, vn, ws, bias)


def _sgu_fwd(uvp, ln_g, ln_b, ws, bias):
    n = uvp.shape[0]
    rows = CHUNK * SGU_CHUNKS_PER_STEP
    mm = _make_mm(1, 1)

    def body(x_ref, g_ref, b2_ref, w_ref, b_ref, z_ref):
        ws_l = [w_ref[g] for g in range(GROUPS)]
        bias_l = [b_ref[g] for g in range(GROUPS)]
        z_ref[...] = _sgu_mixer_tile(mm, x_ref[...], g_ref[...], b2_ref[...], ws_l, bias_l).astype(z_ref.dtype)

    tok_in = pl.BlockSpec((rows, 2 * D), lambda i: (i, 0))
    tok = pl.BlockSpec((rows, D), lambda i: (i, 0))
    vec = pl.BlockSpec((1, D), lambda i: (0, 0))
    grp = pl.BlockSpec((GROUPS, CHUNK, LANES), lambda i: (0, 0, 0))
    return pl.pallas_call(
        body, name="sgu_fwd", out_shape=jax.ShapeDtypeStruct((n, D), BF16),
        grid=(n // rows,), in_specs=[tok_in, vec, vec, grp, grp], out_specs=tok,
        compiler_params=_params(("parallel",)),
    )(uvp, ln_g, ln_b, ws, bias)


def _sgu_bwd(uvp, ln_g, ln_b, ws, bias, dz):
    n = uvp.shape[0]
    rows = CHUNK * SGU_CHUNKS_PER_STEP
    mm = _make_mm(1, 1)

    def body(x_ref, g_ref, b2_ref, w_ref, b_ref, dz_ref, dx_ref, dg_ref, db2_ref, dw_ref, db_ref):
        i = pl.program_id(0)
        ws_l = [w_ref[g] for g in range(GROUPS)]
        bias_l = [b_ref[g] for g in range(GROUPS)]
        _, vjp = jax.vjp(functools.partial(_sgu_mixer_tile, mm), x_ref[...], g_ref[...], b2_ref[...], ws_l, bias_l)
        dx, dg, db2, dw, db = vjp(dz_ref[...].astype(F32))
        dx_ref[...] = dx.astype(dx_ref.dtype)

        @pl.when(i == 0)
        def _():
            dg_ref[...] = dg
            db2_ref[...] = db2
            for g in range(GROUPS):
                dw_ref[g] = dw[g]
                db_ref[g] = db[g]

        @pl.when(i != 0)
        def _():
            dg_ref[...] += dg
            db2_ref[...] += db2
            for g in range(GROUPS):
                dw_ref[g] += dw[g]
                db_ref[g] += db[g]

    tok_in = pl.BlockSpec((rows, 2 * D), lambda i: (i, 0))
    tok = pl.BlockSpec((rows, D), lambda i: (i, 0))
    vec = pl.BlockSpec((1, D), lambda i: (0, 0))
    grp = pl.BlockSpec((GROUPS, CHUNK, LANES), lambda i: (0, 0, 0))
    return pl.pallas_call(
        body, name="sgu_bwd",
        out_shape=[jax.ShapeDtypeStruct((n, 2 * D), BF16), jax.ShapeDtypeStruct((1, D), F32),
                   jax.ShapeDtypeStruct((1, D), F32),
                   jax.ShapeDtypeStruct((GROUPS, CHUNK, LANES), F32),
                   jax.ShapeDtypeStruct((GROUPS, CHUNK, LANES), F32)],
        grid=(n // rows,), in_specs=[tok_in, vec, vec, grp, grp, tok],
        out_specs=[tok_in, vec, vec, grp, grp],
        compiler_params=_params(("arbitrary",)),
    )(uvp, ln_g, ln_b, ws, bias, dz)


def _chains(t):
    return [t[i] for i in range(t.shape[0])] if t.ndim == 3 else [t]


def _bmm(mm, a, b):
    if a.ndim == 2 and b.ndim == 2:
        return mm(a, b)
    ca, cb = _chains(a), _chains(b)
    n = max(len(ca), len(cb))
    return jnp.stack([mm(ca[i % len(ca)], cb[i % len(cb)]) for i in range(n)])


def _bt(a):
    return a.T if a.ndim == 2 else jnp.stack([t.T for t in _chains(a)])


def _make_solver(mms, given):
    def masks():
        lane = lax.broadcasted_iota(jnp.int32, (1, LANES), 1)
        m_a = (lane < HEAD).astype(F32)
        return m_a, 1.0 - m_a

    def doubling(pa, pb, x):
        xa = x + _bmm(mms.apply, pa, x)
        xb = x + _bmm(mms.apply, pb, x)
        for _ in range(6):
            pa = _bmm(mms.square, pa, pa)
            pb = _bmm(mms.square, pb, pb)
            xa = xa + _bmm(mms.apply, pa, xa)
            xb = xb + _bmm(mms.apply, pb, xb)
        m_a, m_b = masks()
        return m_a * xa + m_b * xb

    @jax.custom_vjp
    def solve(ab_a, ab_b, rhs, hint):
        return hint if given else doubling(ab_a, ab_b, rhs)

    def fwd(ab_a, ab_b, rhs, hint):
        u = hint if given else doubling(ab_a, ab_b, rhs)
        return u, (ab_a, ab_b, u)

    def bwd(res, g):
        ab_a, ab_b, u = res
        m_a, m_b = masks()
        h = doubling(_bt(ab_a), _bt(ab_b), g)
        ut = _bt(u)
        return _bmm(mms.apply, h * m_a, ut), _bmm(mms.apply, h * m_b, ut), h, jnp.zeros_like(u)

    solve.defvjp(fwd, bwd)
    return solve


def _wkv_chunk(mms, s0, r, k, v, wl, al, g, w0, a0, k_k, k_a, r_k, ln_g, ln_b, u_hint=None):
    ln = CHUNK
    row = lax.broadcasted_iota(jnp.int32, (ln, ln), 0)
    col = lax.broadcasted_iota(jnp.int32, (ln, ln), 1)
    incl = (col <= row).astype(F32)
    strict = (col < row).astype(F32)
    same_head = ((row // HEAD) == (col // HEAD)).astype(F32)
    lane = lax.broadcasted_iota(jnp.int32, (1, LANES), 1)
    m_a = (lane < HEAD).astype(F32)
    m_b = 1.0 - m_a
    rowid = lax.broadcasted_iota(jnp.int32, (ln, LANES), 0)
    cat = jnp.concatenate

    def hsum(t):
        return _bmm(mms.head_sum, t, same_head)

    def pick_row(t, j):
        return jnp.sum(jnp.where(rowid == j, t, 0.0), axis=-2, keepdims=True)

    z = w0 + wl
    softplus_neg = jnp.maximum(-z, 0.0) + jnp.log(1.0 + jnp.exp(-jnp.abs(z)))
    lw = -jnp.exp(-softplus_neg - 0.5)
    a = 1.0 / (1.0 + jnp.exp(-(a0 + al)))
    kx = k * k_k
    kkn = kx / jnp.maximum(jnp.sqrt(hsum(kx * kx)), L2_EPS)
    kp = k * (1.0 + (a - 1.0) * k_a)
    aa = -kkn
    bb = kkn * a

    c = _bmm(mms.cum, incl, lw)
    c_mid = pick_row(c, ln // 2 - 1)
    ce = c - c_mid
    e_pos = jnp.exp(ce)
    e_neg = jnp.exp(-ce)
    at = aa * jnp.exp(ce - lw)
    bt = bb * e_neg
    kt = kp * e_neg
    rt = r * e_pos
    s0p = s0 * jnp.exp(c_mid)

    bk = cat([bt, kt], axis=-2)
    sc = _bmm(mms.score, cat([at * m_a, at * m_b, rt * m_a, rt * m_b], axis=-2), _bt(bk))
    ab_a, ak_a = sc[..., 0:ln, 0:ln] * strict, sc[..., 0:ln, ln:] * strict
    ab_b, ak_b = sc[..., ln:2 * ln, 0:ln] * strict, sc[..., ln:2 * ln, ln:] * strict
    incl2 = cat([incl, incl], axis=1)
    p_a = sc[..., 2 * ln:3 * ln, :] * incl2
    p_b = sc[..., 3 * ln:, :] * incl2

    base = _bmm(mms.score, cat([at, rt], axis=-2), _bt(s0p))
    rhs = base[..., :ln, :] + m_a * _bmm(mms.out, ak_a, v) + m_b * _bmm(mms.out, ak_b, v)

    u = _make_solver(mms, u_hint is not None)(ab_a, ab_b, rhs, rhs if u_hint is None else u_hint)
    uv = cat([u, v], axis=-2)
    y = base[..., ln:, :] + m_a * _bmm(mms.out, p_a, uv) + m_b * _bmm(mms.out, p_b, uv)
    s_new = (s0p + _bmm(mms.state, _bt(uv), bk)) * same_head * jnp.exp(pick_row(ce, ln - 1))

    mean = hsum(y) * (1.0 / HEAD)
    yc = y - mean
    var = hsum(yc * yc) * (1.0 / HEAD)
    yn = yc * lax.rsqrt(var + GN_EPS) * ln_g + ln_b
    bonus = hsum(r * kp * r_k) * v
    return ((yn + bonus) * g, s_new), u


N_WKV_ROWS = 6
N_WKV_PAR = 7


WKV_PAIRS_PER_STEP = 4


def _to_chains(val, nbat, pp):
    if val.ndim == 2:
        return jnp.stack([val[:, q * LANES:(q + 1) * LANES] for _ in range(nbat) for q in range(pp)])
    return jnp.stack([val[b, :, q * LANES:(q + 1) * LANES] for b in range(nbat) for q in range(pp)])


def _wkv_fwd(seq, rows, pars):
    n = rows[0].shape[0]
    nbat, nch, npair, pp = n // seq, seq // CHUNK, D // LANES, WKV_PAIRS_PER_STEP
    chunk_fn = functools.partial(_wkv_chunk, _wkv_mms(WKV_PRECISION))

    def body(*refs):
        row_vals = [_to_chains(r[...], nbat, pp) for r in refs[:N_WKV_ROWS]]
        par_vals = [_to_chains(r[...], nbat, pp) for r in refs[N_WKV_ROWS:N_WKV_ROWS + N_WKV_PAR]]
        yo_ref, ck_ref, u_ref, s_ref = refs[N_WKV_ROWS + N_WKV_PAR:]
        ch = pl.program_id(1)

        @pl.when(ch == 0)
        def _():
            s_ref[...] = jnp.zeros_like(s_ref)

        s0 = s_ref[...]
        (yo, s_new), u = chunk_fn(s0, *row_vals, *par_vals)
        s_ref[...] = s_new
        for b in range(nbat):
            for q in range(pp):
                ck_ref[b, q] = s0[b * pp + q]
                yo_ref[b, :, q * LANES:(q + 1) * LANES] = yo[b * pp + q].astype(yo_ref.dtype)
                u_ref[b, :, q * LANES:(q + 1) * LANES] = u[b * pp + q]

    tok = pl.BlockSpec((nbat, CHUNK, pp * LANES), lambda p, ch: (0, ch, p))
    par = pl.BlockSpec((1, pp * LANES), lambda p, ch: (0, p))
    ck = pl.BlockSpec((nbat, pp, None, LANES, LANES), lambda p, ch: (0, p, ch, 0, 0))
    yo, ckpt, u_all = pl.pallas_call(
        body, name="wkv_fwd",
        out_shape=[jax.ShapeDtypeStruct((nbat, seq, D), BF16),
                   jax.ShapeDtypeStruct((nbat, npair, nch, LANES, LANES), F32),
                   jax.ShapeDtypeStruct((nbat, seq, D), F32)],
        grid=(npair // pp, nch),
        in_specs=[tok] * N_WKV_ROWS + [par] * N_WKV_PAR, out_specs=[tok, ck, tok],
        scratch_shapes=[pltpu.VMEM((nbat * pp, LANES, LANES), F32)],
        compiler_params=_params(("parallel", "arbitrary")),
    )(*[t.reshape(nbat, seq, D) for t in rows], *pars)
    return yo.reshape(n, D), ckpt, u_all


def _wkv_bwd(seq, rows, pars, ckpt, u_all, dyo):
    n = rows[0].shape[0]
    nbat, nch, npair, pp = n // seq, seq // CHUNK, D // LANES, WKV_PAIRS_PER_STEP
    chunk_fn = functools.partial(_wkv_chunk, _wkv_mms(WKV_PRECISION))
    n_in = N_WKV_ROWS + N_WKV_PAR

    def body(*refs):
        row_vals = [_to_chains(r[...], nbat, pp) for r in refs[:N_WKV_ROWS]]
        par_vals = [_to_chains(r[...], nbat, pp) for r in refs[N_WKV_ROWS:n_in]]
        ck_ref, u_ref, dyo_ref = refs[n_in:n_in + 3]
        d_rows = refs[n_in + 3:n_in + 3 + N_WKV_ROWS]
        d_pars = refs[n_in + 3 + N_WKV_ROWS:n_in + 3 + N_WKV_ROWS + N_WKV_PAR]
        ds_ref = refs[-1]
        ch = pl.program_id(1)

        @pl.when(ch == 0)
        def _():
            ds_ref[...] = jnp.zeros_like(ds_ref)

        s0 = jnp.stack([ck_ref[b, q] for b in range(nbat) for q in range(pp)])
        dyo_v = _to_chains(dyo_ref[...].astype(F32), nbat, pp)
        u_hint = _to_chains(u_ref[...], nbat, pp)
        _, vjp, _ = jax.vjp(functools.partial(chunk_fn, u_hint=u_hint), s0, *row_vals, *par_vals, has_aux=True)
        grads = vjp((dyo_v, ds_ref[...]))
        ds_ref[...] = grads[0]
        for ref, val in zip(d_rows, grads[1:1 + N_WKV_ROWS]):
            for b in range(nbat):
                for q in range(pp):
                    ref[b, :, q * LANES:(q + 1) * LANES] = val[b * pp + q].astype(ref.dtype)
        for ref, val in zip(d_pars, grads[1 + N_WKV_ROWS:]):
            per_pair = [functools.reduce(lambda s, t: s + t, [val[b * pp + q] for b in range(nbat)])
                        for q in range(pp)]
            tot = jnp.concatenate(per_pair, axis=1)

            @pl.when(ch == 0)
            def _(ref=ref, tot=tot):
                ref[...] = tot

            @pl.when(ch != 0)
            def _(ref=ref, tot=tot):
                ref[...] += tot

    tok = pl.BlockSpec((nbat, CHUNK, pp * LANES), lambda p, ch: (0, nch - 1 - ch, p))
    par = pl.BlockSpec((1, pp * LANES), lambda p, ch: (0, p))
    ck = pl.BlockSpec((nbat, pp, None, LANES, LANES), lambda p, ch: (0, p, nch - 1 - ch, 0, 0))
    res = pl.pallas_call(
        body, name="wkv_bwd",
        out_shape=[jax.ShapeDtypeStruct((nbat, seq, D), BF16)] * N_WKV_ROWS
        + [jax.ShapeDtypeStruct((1, D), F32)] * N_WKV_PAR,
        grid=(npair // pp, nch),
        in_specs=[tok] * N_WKV_ROWS + [par] * N_WKV_PAR + [ck, tok, tok],
        out_specs=[tok] * N_WKV_ROWS + [par] * N_WKV_PAR,
        scratch_shapes=[pltpu.VMEM((nbat * pp, LANES, LANES), F32)],
        compiler_params=_params(("parallel", "arbitrary")),
    )(*[t.reshape(nbat, seq, D) for t in rows], *pars, ckpt, u_all, dyo.reshape(nbat, seq, D))
    return [t.reshape(n, D) for t in res[:N_WKV_ROWS]] + list(res[N_WKV_ROWS:])


def _place():
    return lax.axis_index("x"), lax.axis_index("y"), lax.axis_index("c")


def _all_gather8(blk):
    m_per, n = blk.shape
    assert m_per % 8 == 0

    def body(x_ref, out_ref, send_sems, recv_sems, local_sem):
        x, y, c = _place()
        me, sibling = (x, y, c), (x, y, 1 - c)
        chips = [(1 - x, y), (x, 1 - y), (1 - x, 1 - y)]

        def rows(px, py, pc):
            return out_ref.at[pl.ds((4 * px + 2 * py + pc) * m_per, m_per), :]

        def copy(k, block, to, src=None):
            return pltpu.make_async_remote_copy(
                src_ref=rows(*block) if src is None else src, dst_ref=rows(*block),
                send_sem=send_sems.at[k], recv_sem=recv_sems.at[k],
                device_id=to, device_id_type=MESH)

        mine = pltpu.make_async_copy(x_ref, rows(*me), local_sem)
        mine.start()
        first = [copy(0, me, sibling, src=x_ref)]
        first += [copy(1 + j, me, (*chip, c), src=x_ref) for j, chip in enumerate(chips)]
        for cp in first:
            cp.start()
        passed = [copy(4 + j, (*chip, c), sibling) for j, chip in enumerate(chips)]
        for j, chip in enumerate(chips):
            copy(1 + j, (*chip, c), me).wait_recv()
            passed[j].start()
        copy(0, sibling, me).wait_recv()
        for j, chip in enumerate(chips):
            copy(4 + j, (*chip, 1 - c), me).wait_recv()
        for cp in first + passed:
            cp.wait_send()
        mine.wait()

    vm = pl.BlockSpec(memory_space=pltpu.VMEM)
    return pl.pallas_call(
        body, name="all_gather8_%dx%d" % (m_per, n),
        out_shape=jax.ShapeDtypeStruct((N_DEV * m_per, n), blk.dtype),
        in_specs=[vm], out_specs=vm,
        scratch_shapes=[pltpu.SemaphoreType.DMA((7,)), pltpu.SemaphoreType.DMA((7,)),
                        pltpu.SemaphoreType.DMA],
        compiler_params=_params(),
    )(blk)


def _own_slot(src, name):
    r, w = src.shape[-2:]
    tr = _tile(r, 1008)
    xi, yi, _ = _place()
    chip = jnp.reshape(2 * xi + yi, (1,)).astype(jnp.int32)

    def body(chip_ref, x_ref, o_ref):
        o_ref[...] = x_ref[...]

    if src.ndim == 2:
        in_spec = pl.BlockSpec((tr, w), lambda i, chip_ref: (i, 0))
    else:
        in_spec = pl.BlockSpec((None, tr, w), lambda i, chip_ref: (chip_ref[0], i, 0))
    return pl.pallas_call(
        body, name=name,
        out_shape=jax.ShapeDtypeStruct((N_CHIPS, r, w), src.dtype),
        grid_spec=pltpu.PrefetchScalarGridSpec(
            num_scalar_prefetch=1, grid=(r // tr,), in_specs=[in_spec],
            out_specs=pl.BlockSpec((None, tr, w), lambda i, chip_ref: (chip_ref[0], i, 0))),
        compiler_params=_params(("parallel",)),
    )(chip, src)


def _chip_all_gather(shard):
    r, w = shard.shape
    half = r // 2
    assert r % 2 == 0 and half % 16 == 0

    def body(x_ref, buf_ref, out_ref, send_sems, recv_sems):
        del buf_ref
        x, y, c = _place()
        sibling = (x, y, 1 - c)
        me_p = 2 * x + y
        chips = [(1 - x, y), (x, 1 - y), (1 - x, 1 - y)]

        def piece(p, h):
            return out_ref.at[p, pl.ds(h * half, half), :]

        def copy(k, p, h, to, src=None):
            return pltpu.make_async_remote_copy(
                src_ref=piece(p, h) if src is None else src, dst_ref=piece(p, h),
                send_sem=send_sems.at[k], recv_sem=recv_sems.at[k],
                device_id=to, device_id_type=MESH)

        my_half = x_ref.at[pl.ds(c * half, half), :]
        first = [copy(j, me_p, c, (*chip, c), src=my_half) for j, chip in enumerate(chips)]
        for cp in first:
            cp.start()
        passed = [copy(3 + j, 2 * chip[0] + chip[1], c, sibling) for j, chip in enumerate(chips)]
        for j, chip in enumerate(chips):
            copy(j, 2 * chip[0] + chip[1], c, sibling).wait_recv()
            passed[j].start()
        for j, chip in enumerate(chips):
            copy(3 + j, 2 * chip[0] + chip[1], 1 - c, sibling).wait_recv()
        for cp in first + passed:
            cp.wait_send()

    hbm = pl.BlockSpec(memory_space=pl.ANY)
    return pl.pallas_call(
        body, name="chip_all_gather",
        out_shape=jax.ShapeDtypeStruct((N_CHIPS, r, w), shard.dtype),
        in_specs=[hbm, hbm], out_specs=hbm, input_output_aliases={1: 0},
        scratch_shapes=[pltpu.SemaphoreType.DMA((6,)), pltpu.SemaphoreType.DMA((6,))],
        compiler_params=_params(),
    )(shard, _own_slot(shard, "gather_own_slot"))


def _sibling_swap_halves(g):
    _, r, w = g.shape
    half = r // 2

    def body(g_ref, t_ref, send_sem, recv_sem):
        x, y, c = _place()
        sibling = (x, y, 1 - c)
        cp = pltpu.make_async_remote_copy(
            src_ref=g_ref.at[:, pl.ds((1 - c) * half, half), :], dst_ref=t_ref,
            send_sem=send_sem, recv_sem=recv_sem, device_id=sibling, device_id_type=MESH)
        cp.start()
        cp.wait()

    hbm = pl.BlockSpec(memory_space=pl.ANY)
    return pl.pallas_call(
        body, name="rs_sibling_halves",
        out_shape=jax.ShapeDtypeStruct((N_CHIPS, half, w), g.dtype),
        in_specs=[hbm], out_specs=hbm,
        scratch_shapes=[pltpu.SemaphoreType.DMA, pltpu.SemaphoreType.DMA],
        compiler_params=_params(),
    )(g)


def _add_own_half(g, t):
    _, r, w = g.shape
    half = r // 2
    tr = 1008 if half % 1008 == 0 else 16
    assert half % tr == 0
    cidx = jnp.reshape(lax.axis_index("c"), (1,)).astype(jnp.int32)

    def body(c_ref, g_ref, t_ref, o_ref):
        o_ref[...] = (g_ref[...] + t_ref[...]).astype(o_ref.dtype)

    return pl.pallas_call(
        body, name="rs_add_halves",
        out_shape=jax.ShapeDtypeStruct((N_CHIPS, half, w), BF16),
        grid_spec=pltpu.PrefetchScalarGridSpec(
            num_scalar_prefetch=1, grid=(N_CHIPS, half // tr),
            in_specs=[pl.BlockSpec((None, None, tr, w), lambda p, i, c_ref: (p, c_ref[0], i, 0)),
                      pl.BlockSpec((None, tr, w), lambda p, i, c_ref: (p, i, 0))],
            out_specs=pl.BlockSpec((None, tr, w), lambda p, i, c_ref: (p, i, 0))),
        compiler_params=_params(("parallel", "parallel")),
    )(cidx, g.reshape(N_CHIPS, 2, half, w), t)


def _chip_exchange(h):
    _, hh, w = h.shape

    def body(h_ref, buf_ref, t_ref, send_sems, recv_sems):
        del buf_ref
        x, y, c = _place()
        me_p = 2 * x + y
        chips = [(1 - x, y), (x, 1 - y), (1 - x, 1 - y)]
        cps = []
        for j, chip in enumerate(chips):
            q = 2 * chip[0] + chip[1]
            cps.append(pltpu.make_async_remote_copy(
                src_ref=h_ref.at[q], dst_ref=t_ref.at[me_p],
                send_sem=send_sems.at[j], recv_sem=recv_sems.at[j],
                device_id=(*chip, c), device_id_type=MESH))
        for cp in cps:
            cp.start()
        for j, chip in enumerate(chips):
            q = 2 * chip[0] + chip[1]
            pltpu.make_async_remote_copy(
                src_ref=h_ref.at[q], dst_ref=t_ref.at[q],
                send_sem=send_sems.at[j], recv_sem=recv_sems.at[j],
                device_id=(*chip, c), device_id_type=MESH).wait_recv()
        for cp in cps:
            cp.wait_send()

    hbm = pl.BlockSpec(memory_space=pl.ANY)
    return pl.pallas_call(
        body, name="rs_chip_exchange",
        out_shape=jax.ShapeDtypeStruct(h.shape, h.dtype),
        in_specs=[hbm, hbm], out_specs=hbm, input_output_aliases={1: 0},
        scratch_shapes=[pltpu.SemaphoreType.DMA((3,)), pltpu.SemaphoreType.DMA((3,))],
        compiler_params=_params(),
    )(h, _own_slot(h, "rs_own_slot"))


def _sum_slots(t):
    _, hh, w = t.shape
    tr = 1008 if hh % 1008 == 0 else 16
    assert hh % tr == 0
    nblk = hh // tr
    cidx = jnp.reshape(lax.axis_index("c"), (1,)).astype(jnp.int32)

    def body(c_ref, t_ref, o_ref):
        s0, s1, s2, s3 = [t_ref[j].astype(F32) for j in range(N_CHIPS)]
        o_ref[...] = ((s0 + s1) + s2) + s3

    return pl.pallas_call(
        body, name="rs_sum_slots", out_shape=jax.ShapeDtypeStruct((2 * hh, w), F32),
        grid_spec=pltpu.PrefetchScalarGridSpec(
            num_scalar_prefetch=1, grid=(nblk,),
            in_specs=[pl.BlockSpec((N_CHIPS, tr, w), lambda i, c_ref: (0, i, 0))],
            out_specs=pl.BlockSpec((tr, w), lambda i, c_ref: (c_ref[0] * nblk + i, 0))),
        compiler_params=_params(("parallel",)),
    )(cidx, t)


def _sibling_join_halves(s):
    h2, w = s.shape
    hh = h2 // 2

    def body(s_ref, o_ref, send_sem, recv_sem):
        del s_ref
        x, y, c = _place()
        sibling = (x, y, 1 - c)
        cp = pltpu.make_async_remote_copy(
            src_ref=o_ref.at[pl.ds(c * hh, hh), :], dst_ref=o_ref.at[pl.ds(c * hh, hh), :],
            send_sem=send_sem, recv_sem=recv_sem, device_id=sibling, device_id_type=MESH)
        cp.start()
        pltpu.make_async_remote_copy(
            src_ref=o_ref.at[pl.ds((1 - c) * hh, hh), :], dst_ref=o_ref.at[pl.ds((1 - c) * hh, hh), :],
            send_sem=send_sem, recv_sem=recv_sem, device_id=sibling, device_id_type=MESH).wait_recv()
        cp.wait_send()

    hbm = pl.BlockSpec(memory_space=pl.ANY)
    return pl.pallas_call(
        body, name="rs_sibling_join",
        out_shape=jax.ShapeDtypeStruct(s.shape, s.dtype),
        in_specs=[hbm], out_specs=hbm, input_output_aliases={0: 0},
        scratch_shapes=[pltpu.SemaphoreType.DMA, pltpu.SemaphoreType.DMA],
        compiler_params=_params(),
    )(s)


def _reduce_scatter(g):
    h = _add_own_half(g, _sibling_swap_halves(g))
    return _sibling_join_halves(_sum_slots(_chip_exchange(h)))


def _unshard_cols(piece):
    p, k, n = piece.shape
    return jnp.transpose(piece, (1, 0, 2)).reshape(k, p * n)


def _shard_cols(full):
    k, n4 = full.shape
    return jnp.transpose(full.reshape(k, N_CHIPS, n4 // N_CHIPS), (1, 0, 2))


def _rows_of(piece):
    return piece.reshape(N_CHIPS, -1, ROW_W)


def _pad_rows(a, mult):
    pad = (-a.shape[-2]) % mult
    if pad == 0:
        return a
    widths = [(0, 0)] * (a.ndim - 2) + [(0, pad), (0, 0)]
    return jnp.pad(a, widths)


def _adamw(w, g, m, v):
    m2 = ADAM_B1 * m + (1.0 - ADAM_B1) * g
    v2 = ADAM_B2 * v + (1.0 - ADAM_B2) * (g * g)
    m_hat = m2 / (1.0 - ADAM_B1 ** ADAM_STEP)
    v_hat = v2 / (1.0 - ADAM_B2 ** ADAM_STEP)
    delta = -ADAM_LR * (m_hat / (jnp.sqrt(v_hat) + ADAM_EPS) + ADAM_WD * w)
    return delta, m2, v2


def _mlp_fwd(seq, tm, x_in, mix, mod, w1, w2, tag, residual=True):
    x1, h2 = _rw_fwd(_f_res_norm2, "res_norm2_" + tag, seq=seq, tm=tm, rows=[x_in, mix], bvecs=[mod],
                     outs=[(D, F32), (D, BF16)])
    def relu_and_square(acc):
        r = jnp.maximum(acc, 0.0)
        return r, r * r

    p, f = _matmul_ep(h2, w1, [], relu_and_square, [BF16, BF16], mode="nn", name="mlp_up_" + tag)
    ff = _matmul(f, w2, mode="nn", name="mlp_down_" + tag)
    x2 = None
    if residual:
        (x2,) = _rw_fwd(_f_res2, "res2_" + tag, seq=seq, tm=tm, rows=[x1, ff], bvecs=[mod], outs=[(D, F32)])
    return x2, (x1, h2, p, f, ff)


def _mlp_bwd(seq, tm, saved, x_in, mix, mod, w1, w2, dx2, tag, res2_grads=None):
    x1, h2, p, f, ff = saved
    if res2_grads is None:
        dff, dmod_a = _rw_bwd(_f_res2, "res2_bwd_" + tag, seq=seq, tm=tm, rows=[x1, ff], bvecs=[mod],
                              cts=[dx2], need_rows=[1], dtypes=[BF16])
    else:
        dff, dmod_a = res2_grads
    (dp,) = _matmul_ep(dff, w2, [p], lambda acc, pt: (2.0 * pt.astype(F32) * acc,), [BF16],
                       mode="nt", name="mlp_down_dx_" + tag)
    dw2 = _matmul(f, dff, mode="tn", name="mlp_down_dw_" + tag)
    dh2 = _matmul(dp, w1, mode="nt", name="mlp_up_dx_" + tag)
    dw1 = _matmul(h2, dp, mode="tn", name="mlp_up_dw_" + tag, out_shards=N_CHIPS)
    dx_in, dmix, dmod_b = _rw_bwd(_f_res_norm2, "res_norm2_bwd_" + tag, seq=seq, tm=tm, rows=[x_in, mix],
                                  bvecs=[mod], cts=[dx2, dh2], need_rows=[0, 1], dtypes=[F32, BF16])
    return dx_in, dmix, dmod_a + dmod_b, dw1, dw2


def kernel(x, c, ada_w, ada_b, mlp_w1, mlp_w2, a_w_in, a_ln_g, a_ln_b, a_w_s, a_b_s, a_w_out, b_mu, b_w_in, b_w0, b_w1, b_w2, b_a0, b_a1, b_a2, b_g1, b_g2, b_k_k, b_k_a, b_r_k, b_ln_g, b_ln_b, b_w_out, final_g, loss_target, m_ada_w, m_ada_b, m_mlp_w1, m_mlp_w2, m_a_w_in, m_a_ln_g, m_a_ln_b, m_a_w_s, m_a_b_s, m_a_w_out, m_b_mu, m_b_w_in, m_b_w0, m_b_w1, m_b_w2, m_b_a0, m_b_a1, m_b_a2, m_b_g1, m_b_g2, m_b_k_k, m_b_k_a, m_b_r_k, m_b_ln_g, m_b_ln_b, m_b_w_out, m_final_g, v_ada_w, v_ada_b, v_mlp_w1, v_mlp_w2, v_a_w_in, v_a_ln_g, v_a_ln_b, v_a_w_s, v_a_b_s, v_a_w_out, v_b_mu, v_b_w_in, v_b_w0, v_b_w1, v_b_w2, v_b_a0, v_b_a1, v_b_a2, v_b_g1, v_b_g2, v_b_k_k, v_b_k_a, v_b_r_k, v_b_ln_g, v_b_ln_b, v_b_w_out, v_final_g):
    weights = dict(ada_w=ada_w, ada_b=ada_b, mlp_w1=mlp_w1, mlp_w2=mlp_w2, a_w_in=a_w_in, a_ln_g=a_ln_g,
                   a_ln_b=a_ln_b, a_w_s=a_w_s, a_b_s=a_b_s, a_w_out=a_w_out, b_mu=b_mu, b_w_in=b_w_in,
                   b_w0=b_w0, b_w1=b_w1, b_w2=b_w2, b_a0=b_a0, b_a1=b_a1, b_a2=b_a2, b_g1=b_g1, b_g2=b_g2,
                   b_k_k=b_k_k, b_k_a=b_k_a, b_r_k=b_r_k, b_ln_g=b_ln_g, b_ln_b=b_ln_b, b_w_out=b_w_out,
                   final_g=final_g)
    moms = dict(ada_w=(m_ada_w, v_ada_w), ada_b=(m_ada_b, v_ada_b), mlp_w1=(m_mlp_w1, v_mlp_w1),
                mlp_w2=(m_mlp_w2, v_mlp_w2), a_w_in=(m_a_w_in, v_a_w_in), a_ln_g=(m_a_ln_g, v_a_ln_g),
                a_ln_b=(m_a_ln_b, v_a_ln_b), a_w_s=(m_a_w_s, v_a_w_s), a_b_s=(m_a_b_s, v_a_b_s),
                a_w_out=(m_a_w_out, v_a_w_out), b_mu=(m_b_mu, v_b_mu), b_w_in=(m_b_w_in, v_b_w_in),
                b_w0=(m_b_w0, v_b_w0), b_w1=(m_b_w1, v_b_w1), b_w2=(m_b_w2, v_b_w2), b_a0=(m_b_a0, v_b_a0),
                b_a1=(m_b_a1, v_b_a1), b_a2=(m_b_a2, v_b_a2), b_g1=(m_b_g1, v_b_g1), b_g2=(m_b_g2, v_b_g2),
                b_k_k=(m_b_k_k, v_b_k_k), b_k_a=(m_b_k_a, v_b_k_a), b_r_k=(m_b_r_k, v_b_r_k),
                b_ln_g=(m_b_ln_g, v_b_ln_g), b_ln_b=(m_b_ln_b, v_b_ln_b), b_w_out=(m_b_w_out, v_b_w_out),
                final_g=(m_final_g, v_final_g))
    order = list(weights)

    nbat, seq, _ = x.shape
    n = nbat * seq
    tm = 256
    xi, yi, ci = _place()
    chip = 2 * xi + yi
    dev = 2 * chip + ci
    x0 = x.reshape(n, D)
    tgt = loss_target.reshape(n, D)
    lora_w, lora_g = b_w1.shape[-1], b_g1.shape[-1]
    lora_wp, lora_gp = LANES, 2 * LANES

    (cond,) = _small(lambda cc: (cc / (1.0 + jnp.exp(-cc)),), "silu_c", [c], [c.shape])
    vec_names = ["b_w0", "b_a0", "b_k_k", "b_k_a", "b_ln_g", "b_ln_b"]
    vec_shard = jnp.concatenate([b_mu[0]] + [weights[k] for k in vec_names], axis=0)
    n_vec = vec_shard.shape[0]
    vec_rows = vec_shard.reshape(-1, ROW_W)
    blk = _pad_rows(jnp.concatenate([cond, vec_rows], axis=0), 8)
    assert blk.shape[0] == 8
    gathered = _all_gather8(blk).reshape(N_DEV, 8, D)
    cond_all = gathered[:, :nbat].reshape(N_DEV * nbat, D)
    vec_all = gathered[0::2, nbat:nbat + vec_rows.shape[0]].reshape(N_CHIPS, n_vec, D // N_CHIPS)
    vec_full = jnp.transpose(vec_all, (1, 0, 2)).reshape(n_vec, D)
    mu_full = vec_full[0:6]
    w0_f, a0_f, kk_f, ka_f, lng_f, lnb_f = [vec_full[6 + j:7 + j] for j in range(6)]
    rk_f = b_r_k.reshape(1, D)

    n_ada = ada_w.shape[-1]
    parts = jnp.concatenate(
        [_matmul(cond_all, ada_w[i], mode="nn", name="ada_fwd_%d" % i) for i in range(2)], axis=1)
    parts_all = _all_gather8(parts).reshape(N_DEV, N_DEV * nbat, 2, n_ada)[0::2]
    mine = lax.dynamic_slice_in_dim(parts_all, dev * nbat, nbat, axis=1)
    mods = []
    for i in range(2):
        full = jnp.transpose(mine[:, :, i], (1, 0, 2)).reshape(nbat, N_MOD * D) + ada_b[i]
        mods.append(full.reshape(nbat, N_MOD, D))

    big = [("mlp_w1_0", mlp_w1[0]), ("mlp_w1_1", mlp_w1[1]), ("mlp_w2_0", mlp_w2[0]), ("mlp_w2_1", mlp_w2[1]),
           ("a_w_in", a_w_in[0]), ("a_w_out", a_w_out[0]), ("b_w_in", b_w_in[0]), ("b_w_out", b_w_out[0]),
           ("b_w1", b_w1[0]), ("b_w2", b_w2[0]), ("b_a1", b_a1[0]), ("b_a2", b_a2[0]),
           ("b_g1", b_g1[0]), ("b_g2", b_g2[0])]
    offs, pos = {}, 0
    for name, arr in big:
        rows_k = arr.size // ROW_W
        offs[name] = (pos, rows_k, arr.shape)
        pos += rows_k
    n_big_rows = pos
    wflat = _pad_rows(jnp.concatenate([arr.astype(BF16).reshape(-1, ROW_W) for _, arr in big], axis=0), 32)
    wg = _chip_all_gather(wflat)

    def gathered_piece(name):
        start, rows_k, shape = offs[name]
        return wg[:, start:start + rows_k].reshape((N_CHIPS,) + shape)

    def col_w(name):
        return _unshard_cols(gathered_piece(name))

    def row_w(name):
        piece = gathered_piece(name)
        return piece.reshape(N_CHIPS * piece.shape[1], piece.shape[2])

    w1_l = [col_w("mlp_w1_0"), col_w("mlp_w1_1")]
    w2_l = [row_w("mlp_w2_0"), row_w("mlp_w2_1")]
    a_win, a_wout = col_w("a_w_in"), row_w("a_w_out")
    b_win, b_wout = col_w("b_w_in"), row_w("b_w_out")
    w_r, w_k, w_v = b_win[:, :D], b_win[:, D:2 * D], b_win[:, 2 * D:]
    w1p = jnp.pad(row_w("b_w1"), ((0, 0), (0, lora_wp - lora_w)))
    a1p = jnp.pad(row_w("b_a1"), ((0, 0), (0, lora_wp - lora_w)))
    g1p = jnp.pad(row_w("b_g1"), ((0, 0), (0, lora_gp - lora_g)))
    w2p = jnp.pad(col_w("b_w2"), ((0, lora_wp - lora_w), (0, 0)))
    a2p = jnp.pad(col_w("b_a2"), ((0, lora_wp - lora_w), (0, 0)))
    g2p = jnp.pad(col_w("b_g2"), ((0, lora_gp - lora_g), (0, 0)))

    mod0, mod1 = mods
    (h_a,) = _rw_fwd(_f_norm1, "norm1_a", seq=seq, tm=tm, rows=[x0], bvecs=[mod0], outs=[(D, BF16)])
    uvp = _matmul(h_a, a_win, mode="nn", name="sgu_in")
    ws = a_w_s[0]
    bias = jnp.broadcast_to(a_b_s[0][:, :, None], (GROUPS, CHUNK, LANES))
    z = _sgu_fwd(uvp, a_ln_g, a_ln_b, ws, bias)
    mix0 = _matmul(z, a_wout, mode="nn", name="sgu_out")
    x2, saved0 = _mlp_fwd(seq, tm, x0, mix0, mod0, w1_l[0], w2_l[0], "0")

    def shift_fwd(ctx, rv, pv, nv, bv, pa):
        _, _, mixes = _shift_mix(ctx, rv[0], pv[0], bv[0], pa[0])
        return mixes, [], []

    xr, xw, xk, xv, xa, xg = _rowwise(shift_fwd, "shift_mix", seq=seq, tm=tm, rows=[x2], prev8=[x2],
                                      bvecs=[mod1], params=[mu_full], out_rows=[(D, BF16)] * 6)
    r = _matmul(xr, w_r, mode="nn", name="rwkv_r")
    k = _matmul(xk, w_k, mode="nn", name="rwkv_k")
    v = _matmul(xv, w_v, mode="nn", name="rwkv_v")
    def act_tanh(t):
        return jnp.tanh(t)

    def act_sigmoid(t):
        return 1.0 / (1.0 + jnp.exp(-t))

    t1, th = _matmul_ep(xw, w1p, [], lambda acc: (acc, act_tanh(acc)), [F32, BF16], mode="nn", name="lora_w1")
    t2 = _matmul(xa, a1p, mode="nn", name="lora_a1", out_dtype=BF16)
    t3, sg = _matmul_ep(xg, g1p, [], lambda acc: (acc, act_sigmoid(acc)), [F32, BF16], mode="nn", name="lora_g1")
    wl = _matmul(th, w2p, mode="nn", name="lora_w2")
    al = _matmul(t2, a2p, mode="nn", name="lora_a2")
    g = _matmul(sg, g2p, mode="nn", name="lora_g2")
    wkv_rows = [r, k, v, wl, al, g]
    wkv_pars = [w0_f, a0_f, kk_f, ka_f, rk_f, lng_f, lnb_f]
    yo, ckpt, wkv_u = _wkv_fwd(seq, wkv_rows, wkv_pars)
    mix1 = _matmul(yo, b_wout, mode="nn", name="rwkv_out")
    _, saved1 = _mlp_fwd(seq, tm, x2, mix1, mod1, w1_l[1], w2_l[1], "1", residual=False)

    def loss_fn(ctx, rv, pv, nv, bv, pa):
        def head(x1, ff, mod, fg):
            return _f_loss(_f_res2(x1, ff, mod)[0], rv[2], fg)
        val, (dx, dff, dmod, dfg) = jax.value_and_grad(head, argnums=(0, 1, 2, 3))(rv[0], rv[1], bv[0], pa[0])
        return [dx, dff], [dmod], [dfg, jnp.full((1, LANES), val, F32)]

    dx4, dff1, dmod1_a, d_final_g, loss_acc = _rowwise(
        loss_fn, "loss_head", seq=seq, tm=tm, rows=[saved1[0], saved1[4], tgt], bvecs=[mod1],
        params=[final_g.reshape(1, D)], out_rows=[(D, F32), (D, BF16)], out_bacc=[(N_MOD, D)],
        out_pacc=[(1, D), (1, LANES)])
    loss = lax.psum(loss_acc[0, 0], AXES)

    dx2_a, dmix1, dmod1, dw1_1, dw2_1 = _mlp_bwd(seq, tm, saved1, x2, mix1, mod1, w1_l[1], w2_l[1], dx4, "1",
                                                 res2_grads=(dff1, dmod1_a))
    dyo = _matmul(dmix1, b_wout, mode="nt", name="rwkv_out_dx")
    d_b_wout = _matmul(yo, dmix1, mode="tn", name="rwkv_out_dw")
    wkv_grads = _wkv_bwd(seq, wkv_rows, wkv_pars, ckpt, wkv_u, dyo)
    dr, dk, dv, dwl, dal, dg = wkv_grads[:N_WKV_ROWS]
    d_w0, d_a0, d_kk, d_ka, d_rk, d_lng, d_lnb = wkv_grads[N_WKV_ROWS:]
    def through(act):
        return lambda acc, t: (jax.vjp(act, t)[1](acc)[0],)

    (dt1,) = _matmul_ep(dwl, w2p, [t1], through(act_tanh), [BF16], mode="nt", name="lora_w2_dx")
    d_w2p = _matmul(th, dwl, mode="tn", name="lora_w2_dw")
    dt2 = _matmul(dal, a2p, mode="nt", name="lora_a2_dx", out_dtype=BF16)
    d_a2p = _matmul(t2, dal, mode="tn", name="lora_a2_dw")
    (dt3,) = _matmul_ep(dg, g2p, [t3], through(act_sigmoid), [BF16], mode="nt", name="lora_g2_dx")
    d_g2p = _matmul(sg, dg, mode="tn", name="lora_g2_dw")
    dxw = _matmul(dt1, w1p, mode="nt", name="lora_w1_dx")
    d_w1p = _matmul(xw, dt1, mode="tn", name="lora_w1_dw")
    dxa = _matmul(dt2, a1p, mode="nt", name="lora_a1_dx")
    d_a1p = _matmul(xa, dt2, mode="tn", name="lora_a1_dw")
    dxg = _matmul(dt3, g1p, mode="nt", name="lora_g1_dx")
    d_g1p = _matmul(xg, dt3, mode="tn", name="lora_g1_dw")
    dxr = _matmul(dr, w_r, mode="nt", name="rwkv_r_dx")
    dxk = _matmul(dk, w_k, mode="nt", name="rwkv_k_dx")
    dxv = _matmul(dv, w_v, mode="nt", name="rwkv_v_dx")
    d_b_win = jnp.concatenate([_matmul(xr, dr, mode="tn", name="rwkv_r_dw"),
                               _matmul(xk, dk, mode="tn", name="rwkv_k_dw"),
                               _matmul(xv, dv, mode="tn", name="rwkv_v_dw")], axis=1)

    def shift_bwd(ctx, rv, pv, nv, bv, pa):
        xt, dres = rv[0], rv[1]
        dmix_in = rv[2:8]
        mod, mu = bv[0], pa[0]
        f_h = lambda xx_, mod_: _rmsmod(xx_, mod_[0:1], mod_[1:2])
        h, vjp = jax.vjp(f_h, xt, mod)
        hprev = f_h(pv[0], mod)[7:8]
        hprev = jnp.where(ctx.first, jnp.zeros_like(hprev), hprev)
        rowid = lax.broadcasted_iota(jnp.int32, h.shape, 0)
        xx = jnp.where(rowid == 0, hprev, pltpu.roll(h, 1, 0)) - h
        tot = dmix_in[0]
        wsum = dmix_in[0] * mu[0:1]
        for j in range(1, 6):
            tot = tot + dmix_in[j]
            wsum = wsum + dmix_in[j] * mu[j:j + 1]
        nxt = nv[0][0:1] * mu[0:1]
        for j in range(1, 6):
            nxt = nxt + nv[j][0:1] * mu[j:j + 1]
        nxt = jnp.where(ctx.last, jnp.zeros_like(nxt), nxt)
        tmr = h.shape[0]
        wshift = jnp.where(rowid == tmr - 1, nxt, pltpu.roll(wsum, tmr - 1, 0))
        dh = tot - wsum + wshift
        dx_, dmod_ = vjp(dh)
        dmu = jnp.concatenate([jnp.sum(dmix_in[j] * xx, axis=0, keepdims=True) for j in range(6)], axis=0)
        dx2_t = dx_ + dres
        ff_below, mod_below = rv[8], bv[1]
        dff_below = mod_below[5:6] * dx2_t
        dgate = jnp.sum(dx2_t * ff_below, axis=0, keepdims=True)
        gate_row = lax.broadcasted_iota(jnp.int32, (N_MOD, D), 0) == N_MOD - 1
        dmod_below = jnp.where(gate_row, jnp.broadcast_to(dgate, (N_MOD, D)), 0.0)
        return [dx2_t, dff_below], [dmod_, dmod_below], [dmu]

    dmix_list = [dxr, dxw, dxk, dxv, dxa, dxg]
    dx2, dff0, dmod1_c, dmod0_a, d_mu = _rowwise(
        shift_bwd, "shift_mix_bwd", seq=seq, tm=tm, rows=[x2, dx2_a] + dmix_list + [saved0[4]],
        prev8=[x2], next8=dmix_list, bvecs=[mod1, mod0], params=[mu_full],
        out_rows=[(D, F32), (D, BF16)], out_bacc=[(N_MOD, D), (N_MOD, D)], out_pacc=[(6, D)])
    dmod1 = dmod1 + dmod1_c

    dx0_a, dmix0, dmod0, dw1_0, dw2_0 = _mlp_bwd(seq, tm, saved0, x0, mix0, mod0, w1_l[0], w2_l[0], dx2, "0",
                                                 res2_grads=(dff0, dmod0_a))
    dz = _matmul(dmix0, a_wout, mode="nt", name="sgu_out_dx")
    d_a_wout = _matmul(z, dmix0, mode="tn", name="sgu_out_dw")
    duvp, d_a_lng, d_a_lnb, d_ws, d_bias = _sgu_bwd(uvp, a_ln_g, a_ln_b, ws, bias, dz)
    dh_a = _matmul(duvp, a_win, mode="nt", name="sgu_in_dx")
    d_a_win = _matmul(h_a, duvp, mode="tn", name="sgu_in_dw", out_shards=N_CHIPS)
    grad_x, dmod0_c = _rw_bwd(_f_norm1, "norm1_a_bwd", seq=seq, tm=tm, rows=[x0], bvecs=[mod0], cts=[dh_a],
                              need_rows=[0], extra=dx0_a)
    dmod0 = dmod0 + dmod0_c

    dmod_blk = _pad_rows(jnp.concatenate([dmod0.reshape(nbat, -1), dmod1.reshape(nbat, -1)], axis=1), 8)
    dmod_all = _all_gather8(dmod_blk).reshape(N_DEV, 8, 2, N_MOD * D)[:, :nbat].reshape(N_DEV * nbat, 2, N_MOD * D)
    g_ada_w, g_ada_b = [], []
    for i in range(2):
        cols = lax.dynamic_slice_in_dim(dmod_all[:, i], chip * n_ada, n_ada, axis=1)
        g_ada_w.append(_matmul(cond_all, cols, mode="tn", name="ada_dw_%d" % i))
    (g_ada_b_all,) = _small(lambda t: (jnp.sum(t, axis=0),), "ada_db", [dmod_all], [(2, N_MOD * D)])
    grads = {"ada_w": jnp.stack(g_ada_w), "ada_b": g_ada_b_all}

    rep = _pad_rows(jnp.concatenate([
        d_a_lng, d_a_lnb, jnp.sum(d_bias, axis=-1).reshape(1, D), d_rk, d_final_g,
        jnp.zeros((3, D), F32), d_ws.reshape(-1, D)], axis=0), 8)
    rep_rows = rep.shape[0]
    rep_all = _all_gather8(rep)
    (rep_sum,) = _small(lambda t: (functools.reduce(lambda p, q: p + q,
                                                     [t[j * rep_rows:(j + 1) * rep_rows] for j in range(N_DEV)]),),
                        "replicated_sum", [rep_all], [(rep_rows, D)])
    grads["a_ln_g"] = rep_sum[0:1]
    grads["a_ln_b"] = rep_sum[1:2]
    grads["a_b_s"] = rep_sum[2:3].reshape(a_b_s.shape)
    grads["b_r_k"] = rep_sum[3:4].reshape(b_r_k.shape)
    grads["final_g"] = rep_sum[4].reshape(final_g.shape)
    grads["a_w_s"] = rep_sum[8:8 + GROUPS * CHUNK * LANES // D].reshape(a_w_s.shape)

    vec_grads = jnp.concatenate([d_mu, d_w0, d_a0, d_kk, d_ka, d_lng, d_lnb], axis=0)
    packed = {
        "mlp_w1_0": dw1_0, "mlp_w1_1": dw1_1,
        "mlp_w2_0": dw2_0.reshape(N_CHIPS, -1, D), "mlp_w2_1": dw2_1.reshape(N_CHIPS, -1, D),
        "a_w_in": d_a_win, "a_w_out": d_a_wout.reshape(N_CHIPS, -1, D),
        "b_w_in": _shard_cols(d_b_win), "b_w_out": d_b_wout.reshape(N_CHIPS, -1, D),
        "b_w1": d_w1p[:, :lora_w].reshape(N_CHIPS, -1, lora_w), "b_w2": _shard_cols(d_w2p[:lora_w]),
        "b_a1": d_a1p[:, :lora_w].reshape(N_CHIPS, -1, lora_w), "b_a2": _shard_cols(d_a2p[:lora_w]),
        "b_g1": d_g1p[:, :lora_g].reshape(N_CHIPS, -1, lora_g), "b_g2": _shard_cols(d_g2p[:lora_g]),
    }
    pieces = [_rows_of(packed[name]) for name, _ in big] + [_pad_rows(_rows_of(_shard_cols(vec_grads)), 8)]
    used = sum(p.shape[1] for p in pieces)
    pieces.append(jnp.zeros((N_CHIPS, (-used) % 2016, ROW_W), F32))
    g_pack = jnp.concatenate(pieces, axis=1)
    g_red = _reduce_scatter(g_pack)
    for name, _ in big:
        start, rows_k, shape = offs[name]
        grads[name] = g_red[start:start + rows_k].reshape(shape)
    vec_red = g_red[n_big_rows:n_big_rows + vec_rows.shape[0]].reshape(n_vec, D // N_CHIPS)
    grads["b_mu"] = vec_red[0:6].reshape(b_mu.shape)
    for j, name in enumerate(vec_names):
        grads[name] = vec_red[6 + j:7 + j].reshape(weights[name].shape)
    for base in ("mlp_w1", "mlp_w2"):
        grads[base] = jnp.stack([grads.pop(base + "_0"), grads.pop(base + "_1")])
    for name in ("a_w_in", "a_w_out", "b_w_in", "b_w_out", "b_w1", "b_w2", "b_a1", "b_a2", "b_g1", "b_g2"):
        grads[name] = grads[name].reshape(weights[name].shape)

    deltas, new_m, new_v = {}, {}, {}
    for name in order:
        gr = grads[name].reshape(weights[name].shape)
        grads[name] = gr
        deltas[name], new_m[name], new_v[name] = _elementwise(
            _adamw, "adamw_" + name, [weights[name], gr, moms[name][0], moms[name][1]], 3)

    return (loss, grad_x.reshape(x.shape), *[grads[k] for k in order], *[deltas[k] for k in order],
            *[new_m[k] for k in order], *[new_v[k] for k in order])
```

```python
import functools

import jax
import jax.numpy as jnp
from jax import lax
from jax.experimental import pallas as pl
from jax.experimental.pallas import tpu as pltpu

F32 = jnp.float32
BF16 = jnp.bfloat16
MESH = pl.DeviceIdType.MESH
AXES = ("x", "y", "c")

D = 1024
N_MOD = 6
HEAD = 64
CHUNK = 128
GROUPS = 8
LANES = 128
ROW_W = 1024
N_CHIPS = 4
N_DEV = 8

RMS_EPS = 1e-6
LN_EPS = 1e-5
GN_EPS = HEAD * 1e-5
L2_EPS = 1e-12

ADAM_LR = 0.001
ADAM_B1 = 0.9
ADAM_B2 = 0.999
ADAM_EPS = 1e-08
ADAM_WD = 0.01
ADAM_STEP = 10

VMEM_LIMIT_V7X = 56 * 1024 * 1024
HIGHEST = lax.Precision.HIGHEST


def _params(sem=None):
    return pltpu.CompilerParams(dimension_semantics=sem, vmem_limit_bytes=VMEM_LIMIT_V7X)


def _tile(dim, target):
    if dim <= target:
        return dim
    for cand in range(target, 0, -LANES):
        if dim % cand == 0:
            return cand
    raise ValueError((dim, target))


def _matmul(a, b, *, mode, name, out_dtype=F32, out_shards=1, tm=1024, tn=1024, tk=4096):
    if mode == "nn":
        (m, k), (k2, n) = a.shape, b.shape
    elif mode == "nt":
        (m, k), (n, k2) = a.shape, b.shape
    else:
        (k, m), (k2, n) = a.shape, b.shape
    assert k == k2, (name, a.shape, b.shape)
    n_sh = n // out_shards
    tm, tn, tk = _tile(m, tm), _tile(n_sh, tn), _tile(k, tk)
    nk = k // tk
    nb = n_sh // tn
    use_scratch = nk > 1 and out_dtype != F32

    if mode == "tn":
        a_spec = pl.BlockSpec((tk, tm), lambda i, j, kk: (kk, i))
    else:
        a_spec = pl.BlockSpec((tm, tk), lambda i, j, kk: (i, kk))
    if mode == "nt":
        b_spec = pl.BlockSpec((tn, tk), lambda i, j, kk: (j, kk))
    else:
        b_spec = pl.BlockSpec((tk, tn), lambda i, j, kk: (kk, j))
    if out_shards == 1:
        out_shape = jax.ShapeDtypeStruct((m, n), out_dtype)
        o_spec = pl.BlockSpec((tm, tn), lambda i, j, kk: (i, j))
    else:
        out_shape = jax.ShapeDtypeStruct((out_shards, m, n_sh), out_dtype)
        o_spec = pl.BlockSpec((None, tm, tn), lambda i, j, kk: (j // nb, i, j % nb))

    def body(a_ref, b_ref, o_ref, *scratch):
        kk = pl.program_id(2)
        av = a_ref[...].astype(BF16)
        bv = b_ref[...].astype(BF16)
        if mode == "nn":
            dims = (((1,), (0,)), ((), ()))
        elif mode == "nt":
            dims = (((1,), (1,)), ((), ()))
        else:
            dims = (((0,), (0,)), ((), ()))
        part = lax.dot_general(av, bv, dims, preferred_element_type=F32)
        if nk == 1:
            o_ref[...] = part.astype(o_ref.dtype)
            return
        acc_ref = scratch[0] if use_scratch else o_ref

        @pl.when(kk == 0)
        def _():
            acc_ref[...] = part

        @pl.when(kk != 0)
        def _():
            acc_ref[...] += part

        if use_scratch:
            @pl.when(kk == nk - 1)
            def _():
                o_ref[...] = acc_ref[...].astype(o_ref.dtype)

    return pl.pallas_call(
        body, name=name, out_shape=out_shape,
        grid=(m // tm, n // tn, nk),
        in_specs=[a_spec, b_spec], out_specs=o_spec,
        scratch_shapes=[pltpu.VMEM((tm, tn), F32)] if use_scratch else [],
        compiler_params=_params(("parallel", "parallel", "arbitrary")),
    )(a, b)


def _matmul_ep(a, b, extras, epilogue, out_dtypes, *, mode, name, tm=1024, tn=1024, tk=2048):
    if mode == "nn":
        (m, k), (k2, n) = a.shape, b.shape
    else:
        (m, k), (n, k2) = a.shape, b.shape
    assert k == k2 and mode in ("nn", "nt"), (name, a.shape, b.shape)
    tm, tn, tk = _tile(m, tm), _tile(n, tn), _tile(k, tk)
    nk = k // tk
    n_ex, n_out = len(extras), len(out_dtypes)

    def body(a_ref, b_ref, *rest):
        extra_refs, out_refs = rest[:n_ex], rest[n_ex:n_ex + n_out]
        kk = pl.program_id(2)
        dims = (((1,), (0,)), ((), ())) if mode == "nn" else (((1,), (1,)), ((), ()))
        part = lax.dot_general(a_ref[...].astype(BF16), b_ref[...].astype(BF16), dims,
                               preferred_element_type=F32)

        def finish(acc):
            for ref, val in zip(out_refs, epilogue(acc, *[r[...] for r in extra_refs])):
                ref[...] = val.astype(ref.dtype)

        if nk == 1:
            finish(part)
            return
        acc_ref = rest[-1]

        @pl.when(kk == 0)
        def _():
            acc_ref[...] = part

        @pl.when(kk != 0)
        def _():
            acc_ref[...] += part

        @pl.when(kk == nk - 1)
        def _():
            finish(acc_ref[...])

    a_spec = pl.BlockSpec((tm, tk), lambda i, j, kk: (i, kk))
    b_spec = (pl.BlockSpec((tk, tn), lambda i, j, kk: (kk, j)) if mode == "nn"
              else pl.BlockSpec((tn, tk), lambda i, j, kk: (j, kk)))
    o_spec = pl.BlockSpec((tm, tn), lambda i, j, kk: (i, j))
    res = pl.pallas_call(
        body, name=name, out_shape=[jax.ShapeDtypeStruct((m, n), dt) for dt in out_dtypes],
        grid=(m // tm, n // tn, nk),
        in_specs=[a_spec, b_spec] + [o_spec] * n_ex, out_specs=[o_spec] * n_out,
        scratch_shapes=[pltpu.VMEM((tm, tn), F32)] if nk > 1 else [],
        compiler_params=_params(("parallel", "parallel", "arbitrary")),
    )(a, b, *extras)
    return list(res)


class _Ctx:
    def __init__(self, first, last):
        self.first = first
        self.last = last


def _rowwise(fn, name, *, seq, tm, rows=(), prev8=(), next8=(), bvecs=(), params=(),
             out_rows=(), out_bacc=(), out_pacc=()):
    n = rows[0].shape[0]
    tm = min(tm, seq)
    assert n % seq == 0 and seq % tm == 0 and tm % 8 == 0
    tpb = seq // tm
    nt = n // tm
    nbat = n // seq
    r8 = tm // 8
    counts = [len(rows), len(prev8), len(next8), len(bvecs), len(params)]
    n_in = sum(counts)

    def body(*refs):
        i = pl.program_id(0)
        first = (i % tpb) == 0
        last = (i % tpb) == (tpb - 1)
        vals = [r[...] for r in refs[:n_in]]
        groups, pos = [], 0
        for cnt in counts:
            groups.append(vals[pos:pos + cnt])
            pos += cnt
        ro, bo, po = fn(_Ctx(first, last), *groups)
        outs = refs[n_in:]
        assert len(ro) == len(out_rows) and len(bo) == len(out_bacc) and len(po) == len(out_pacc)
        for ref, val in zip(outs[:len(ro)], ro):
            ref[...] = val.astype(ref.dtype)
        for ref, val in zip(outs[len(ro):len(ro) + len(bo)], bo):
            @pl.when(first)
            def _(ref=ref, val=val):
                ref[...] = val

            @pl.when(jnp.logical_not(first))
            def _(ref=ref, val=val):
                ref[...] += val
        for ref, val in zip(outs[len(ro) + len(bo):], po):
            @pl.when(i == 0)
            def _(ref=ref, val=val):
                ref[...] = val

            @pl.when(i != 0)
            def _(ref=ref, val=val):
                ref[...] += val

    in_specs = []
    for arr in rows:
        in_specs.append(pl.BlockSpec((tm, arr.shape[1]), lambda i: (i, 0)))
    for arr in prev8:
        in_specs.append(pl.BlockSpec((8, arr.shape[1]), lambda i: (jnp.maximum(i * r8 - 1, 0), 0)))
    for arr in next8:
        in_specs.append(pl.BlockSpec((8, arr.shape[1]), lambda i: (jnp.minimum((i + 1) * r8, n // 8 - 1), 0)))
    for arr in bvecs:
        in_specs.append(pl.BlockSpec((None,) + arr.shape[1:], lambda i: (i // tpb, 0, 0)))
    for arr in params:
        in_specs.append(pl.BlockSpec(arr.shape, lambda i: (0, 0)))
    out_shape, out_specs = [], []
    for d, dt in out_rows:
        out_shape.append(jax.ShapeDtypeStruct((n, d), dt))
        out_specs.append(pl.BlockSpec((tm, d), lambda i: (i, 0)))
    for r, d in out_bacc:
        out_shape.append(jax.ShapeDtypeStruct((nbat, r, d), F32))
        out_specs.append(pl.BlockSpec((None, r, d), lambda i: (i // tpb, 0, 0)))
    for r, d in out_pacc:
        out_shape.append(jax.ShapeDtypeStruct((r, d), F32))
        out_specs.append(pl.BlockSpec((r, d), lambda i: (0, 0)))
    res = pl.pallas_call(
        body, name=name, out_shape=out_shape, grid=(nt,),
        in_specs=in_specs, out_specs=out_specs,
        compiler_params=_params(("arbitrary",)),
    )(*rows, *prev8, *next8, *bvecs, *params)
    return list(res)


def _rw_fwd(f, name, *, seq, tm, rows, bvecs=(), params=(), outs):
    def fn(ctx, rv, pv, nv, bv, pa):
        res = f(*[v.astype(F32) for v in rv], *bv, *pa)
        return list(res), [], []
    return _rowwise(fn, name, seq=seq, tm=tm, rows=rows, bvecs=bvecs, params=params, out_rows=outs)


def _rw_bwd(f, name, *, seq, tm, rows, bvecs=(), params=(), cts, need_rows, extra=None, dtypes=None):
    nr, nb, npar = len(rows), len(bvecs), len(params)
    all_rows = list(rows) + list(cts) + ([extra] if extra is not None else [])

    def fn(ctx, rv, pv, nv, bv, pa):
        prim = [v.astype(F32) for v in rv[:nr]]
        ct = tuple(v.astype(F32) for v in rv[nr:nr + len(cts)])
        _, vjp = jax.vjp(f, *prim, *bv, *pa)
        g = vjp(ct)
        d_rows = [g[j] for j in need_rows]
        if extra is not None:
            d_rows[0] = d_rows[0] + rv[-1].astype(F32)
        return d_rows, list(g[nr:nr + nb]), list(g[nr + nb:])

    return _rowwise(
        fn, name, seq=seq, tm=tm, rows=all_rows, bvecs=bvecs, params=params,
        out_rows=[(rows[j].shape[1], F32 if dtypes is None else dtypes[i]) for i, j in enumerate(need_rows)],
        out_bacc=[b.shape[1:] for b in bvecs], out_pacc=[p.shape for p in params])


def _small(fn, name, arrays, out_shapes):
    def body(*refs):
        res = fn(*[r[...] for r in refs[:len(arrays)]])
        for ref, val in zip(refs[len(arrays):], res):
            ref[...] = val.astype(ref.dtype)

    vm = pl.BlockSpec(memory_space=pltpu.VMEM)
    res = pl.pallas_call(
        body, name=name,
        out_shape=[jax.ShapeDtypeStruct(s, F32) for s in out_shapes],
        in_specs=[vm] * len(arrays), out_specs=[vm] * len(out_shapes),
        compiler_params=_params(),
    )(*arrays)
    return list(res)


def _elementwise(fn, name, arrays, n_out):
    shape = arrays[0].shape
    size = arrays[0].size
    if len(shape) >= 2 and shape[-1] % LANES == 0 and (size // shape[-1]) % 8 == 0:
        view = (size // shape[-1], shape[-1])
    elif size % ROW_W == 0 and (size // ROW_W) % 8 == 0:
        view = (size // ROW_W, ROW_W)
    else:
        view = (1, size) if len(shape) < 2 else (size // shape[-1], shape[-1])
    rows = view[0]
    tr = rows
    for cand in (256, 128, 64, 32, 16, 8):
        if rows > cand and rows % cand == 0:
            tr = cand
            break

    def body(*refs):
        res = fn(*[r[...] for r in refs[:len(arrays)]])
        for ref, val in zip(refs[len(arrays):], res):
            ref[...] = val

    spec = pl.BlockSpec((tr, view[1]), lambda i: (i, 0))
    res = pl.pallas_call(
        body, name=name,
        out_shape=[jax.ShapeDtypeStruct(view, F32)] * n_out,
        grid=(rows // tr,), in_specs=[spec] * len(arrays), out_specs=[spec] * n_out,
        compiler_params=_params(("parallel",)),
    )(*[a.reshape(view) for a in arrays])
    return [r.reshape(shape) for r in res]


def _rms(x):
    return x * lax.rsqrt(jnp.mean(x * x, axis=-1, keepdims=True) + RMS_EPS)


def _rmsmod(x, sh, sc):
    return _rms(x) * (1.0 + sc) + sh


def _f_norm1(x, mod):
    return (_rmsmod(x, mod[0:1], mod[1:2]),)


def _f_sgu_pre(uvp, ln_g, ln_b):
    uv = 0.5 * uvp * (1.0 + lax.erf(uvp * (2.0 ** -0.5)))
    u = uv[:, :D]
    v = uv[:, D:]
    mu = jnp.mean(v, axis=-1, keepdims=True)
    vc = v - mu
    var = jnp.mean(vc * vc, axis=-1, keepdims=True)
    return u, vc * lax.rsqrt(var + LN_EPS) * ln_g + ln_b


def _f_res_norm2(x, mix, mod):
    x1 = x + mod[2:3] * mix
    return x1, _rmsmod(x1, mod[3:4], mod[4:5])


def _f_res2(x1, ff, mod):
    return (x1 + mod[5:6] * ff,)


def _f_loss(x, tgt, fg):
    err = _rms(x) * fg - tgt
    return 0.5 * jnp.sum(jnp.mean(err * err, axis=-1))


def _shift_mix(ctx, x, xprev8, mod, mu):
    h = _rmsmod(x, mod[0:1], mod[1:2])
    hprev = _rmsmod(xprev8, mod[0:1], mod[1:2])[7:8]
    hprev = jnp.where(ctx.first, jnp.zeros_like(hprev), hprev)
    rowid = lax.broadcasted_iota(jnp.int32, h.shape, 0)
    hp = jnp.where(rowid == 0, hprev, pltpu.roll(h, 1, 0))
    xx = hp - h
    return h, xx, [h + xx * mu[j:j + 1] for j in range(6)]


def _split_bf16(t, parts):
    out, rest = [], t.astype(F32)
    for _ in range(parts):
        piece = rest.astype(BF16)
        out.append(piece)
        rest = rest - piece.astype(F32)
    return out


def _make_mm(na, nb, ct_pieces=1, saved_pieces=1):
    def raw(a, b, pa, pb):
        if pa == 0:
            return jnp.dot(a, b, precision=HIGHEST, preferred_element_type=F32)
        acc = None
        bs = _split_bf16(b, pb)
        for i, ai in enumerate(_split_bf16(a, pa)):
            for j, bj in enumerate(bs):
                if i + j < max(pa, pb):
                    term = jnp.dot(ai, bj, preferred_element_type=F32)
                    acc = term if acc is None else acc + term
        return acc

    @jax.custom_vjp
    def mm(a, b):
        return raw(a, b, na, nb)

    def fwd(a, b):
        return raw(a, b, na, nb), (a, b)

    def bwd(res, ct):
        a, b = res
        if na == 0:
            return raw(ct, b.T, 0, 0), raw(a.T, ct, 0, 0)
        return raw(ct, b.T, ct_pieces, saved_pieces), raw(a.T, ct, saved_pieces, ct_pieces)

    mm.defvjp(fwd, bwd)
    return mm


class _WkvMms:
    def __init__(self, head_sum, cum, score, square, apply, out, state):
        self.head_sum, self.cum, self.score = head_sum, cum, score
        self.square, self.apply, self.out, self.state = square, apply, out, state


def _wkv_mms(cfg):
    table = {"x": (0, 0), "1": (1, 1), "2": (2, 2), "3": (3, 3), "a": (2, 1), "b": (1, 2)}
    hs, cu, sc_, sq, ap, ou, st = [table[ch] for ch in cfg]
    return _WkvMms(_make_mm(hs[0], 1) if hs[0] else _make_mm(0, 0),
                   _make_mm(1, cu[1], ct_pieces=2) if cu[0] else _make_mm(0, 0),
                   _make_mm(*sc_, saved_pieces=2), _make_mm(*sq), _make_mm(*ap, saved_pieces=2),
                   _make_mm(*ou), _make_mm(*st))


WKV_PRECISION = "1221b11"


SGU_CHUNKS_PER_STEP = 4


def _sgu_tile(mm, u, vn, ws, bias):
    row = lax.broadcasted_iota(jnp.int32, (CHUNK, CHUNK), 0)
    col = lax.broadcasted_iota(jnp.int32, (CHUNK, CHUNK), 1)
    wm = [jnp.where(col <= row, w, 0.0) for w in ws]
    out_rows = []
    for ch in range(u.shape[0] // CHUNK):
        rs = slice(ch * CHUNK, (ch + 1) * CHUNK)
        out_rows.append(jnp.concatenate(
            [mm(wm[g], vn[rs, g * LANES:(g + 1) * LANES]) + bias[g] for g in range(GROUPS)], axis=1))
    return u * jnp.concatenate(out_rows, axis=0)


def _sgu_mixer_tile(mm, uvp, ln_g, ln_b, ws, bias):
    u, vn = _f_sgu_pre(uvp, ln_g, ln_b)
    return _sgu_tile(mm, u, vn, ws, bias)


def _sgu_fwd(uvp, ln_g, ln_b, ws, bias):
    n = uvp.shape[0]
    rows = CHUNK * SGU_CHUNKS_PER_STEP
    mm = _make_mm(1, 1)

    def body(x_ref, g_ref, b2_ref, w_ref, b_ref, z_ref):
        ws_l = [w_ref[g] for g in range(GROUPS)]
        bias_l = [b_ref[g] for g in range(GROUPS)]
        z_ref[...] = _sgu_mixer_tile(mm, x_ref[...], g_ref[...], b2_ref[...], ws_l, bias_l).astype(z_ref.dtype)

    tok_in = pl.BlockSpec((rows, 2 * D), lambda i: (i, 0))
    tok = pl.BlockSpec((rows, D), lambda i: (i, 0))
    vec = pl.BlockSpec((1, D), lambda i: (0, 0))
    grp = pl.BlockSpec((GROUPS, CHUNK, LANES), lambda i: (0, 0, 0))
    return pl.pallas_call(
        body, name="sgu_fwd", out_shape=jax.ShapeDtypeStruct((n, D), BF16),
        grid=(n // rows,), in_specs=[tok_in, vec, vec, grp, grp], out_specs=tok,
        compiler_params=_params(("parallel",)),
    )(uvp, ln_g, ln_b, ws, bias)


def _sgu_bwd(uvp, ln_g, ln_b, ws, bias, dz):
    n = uvp.shape[0]
    rows = CHUNK * SGU_CHUNKS_PER_STEP
    mm = _make_mm(1, 1)

    def body(x_ref, g_ref, b2_ref, w_ref, b_ref, dz_ref, dx_ref, dg_ref, db2_ref, dw_ref, db_ref):
        i = pl.program_id(0)
        ws_l = [w_ref[g] for g in range(GROUPS)]
        bias_l = [b_ref[g] for g in range(GROUPS)]
        _, vjp = jax.vjp(functools.partial(_sgu_mixer_tile, mm), x_ref[...], g_ref[...], b2_ref[...], ws_l, bias_l)
        dx, dg, db2, dw, db = vjp(dz_ref[...].astype(F32))
        dx_ref[...] = dx.astype(dx_ref.dtype)

        @pl.when(i == 0)
        def _():
            dg_ref[...] = dg
            db2_ref[...] = db2
            for g in range(GROUPS):
                dw_ref[g] = dw[g]
                db_ref[g] = db[g]

        @pl.when(i != 0)
        def _():
            dg_ref[...] += dg
            db2_ref[...] += db2
            for g in range(GROUPS):
                dw_ref[g] += dw[g]
                db_ref[g] += db[g]

    tok_in = pl.BlockSpec((rows, 2 * D), lambda i: (i, 0))
    tok = pl.BlockSpec((rows, D), lambda i: (i, 0))
    vec = pl.BlockSpec((1, D), lambda i: (0, 0))
    grp = pl.BlockSpec((GROUPS, CHUNK, LANES), lambda i: (0, 0, 0))
    return pl.pallas_call(
        body, name="sgu_bwd",
        out_shape=[jax.ShapeDtypeStruct((n, 2 * D), BF16), jax.ShapeDtypeStruct((1, D), F32),
                   jax.ShapeDtypeStruct((1, D), F32),
                   jax.ShapeDtypeStruct((GROUPS, CHUNK, LANES), F32),
                   jax.ShapeDtypeStruct((GROUPS, CHUNK, LANES), F32)],
        grid=(n // rows,), in_specs=[tok_in, vec, vec, grp, grp, tok],
        out_specs=[tok_in, vec, vec, grp, grp],
        compiler_params=_params(("arbitrary",)),
    )(uvp, ln_g, ln_b, ws, bias, dz)


def _chains(t):
    return [t[i] for i in range(t.shape[0])] if t.ndim == 3 else [t]


def _bmm(mm, a, b):
    if a.ndim == 2 and b.ndim == 2:
        return mm(a, b)
    ca, cb = _chains(a), _chains(b)
    n = max(len(ca), len(cb))
    return jnp.stack([mm(ca[i % len(ca)], cb[i % len(cb)]) for i in range(n)])


def _bt(a):
    return a.T if a.ndim == 2 else jnp.stack([t.T for t in _chains(a)])


def _make_solver(mms, given):
    def masks():
        lane = lax.broadcasted_iota(jnp.int32, (1, LANES), 1)
        m_a = (lane < HEAD).astype(F32)
        return m_a, 1.0 - m_a

    def doubling(pa, pb, x):
        xa = x + _bmm(mms.apply, pa, x)
        xb = x + _bmm(mms.apply, pb, x)
        for _ in range(6):
            pa = _bmm(mms.square, pa, pa)
            pb = _bmm(mms.square, pb, pb)
            xa = xa + _bmm(mms.apply, pa, xa)
            xb = xb + _bmm(mms.apply, pb, xb)
        m_a, m_b = masks()
        return m_a * xa + m_b * xb

    @jax.custom_vjp
    def solve(ab_a, ab_b, rhs, hint):
        return hint if given else doubling(ab_a, ab_b, rhs)

    def fwd(ab_a, ab_b, rhs, hint):
        u = hint if given else doubling(ab_a, ab_b, rhs)
        return u, (ab_a, ab_b, u)

    def bwd(res, g):
        ab_a, ab_b, u = res
        m_a, m_b = masks()
        h = doubling(_bt(ab_a), _bt(ab_b), g)
        ut = _bt(u)
        return _bmm(mms.apply, h * m_a, ut), _bmm(mms.apply, h * m_b, ut), h, jnp.zeros_like(u)

    solve.defvjp(fwd, bwd)
    return solve


def _wkv_chunk(mms, s0, r, k, v, wl, al, g, w0, a0, k_k, k_a, r_k, ln_g, ln_b, u_hint=None):
    ln = CHUNK
    row = lax.broadcasted_iota(jnp.int32, (ln, ln), 0)
    col = lax.broadcasted_iota(jnp.int32, (ln, ln), 1)
    incl = (col <= row).astype(F32)
    strict = (col < row).astype(F32)
    same_head = ((row // HEAD) == (col // HEAD)).astype(F32)
    lane = lax.broadcasted_iota(jnp.int32, (1, LANES), 1)
    m_a = (lane < HEAD).astype(F32)
    m_b = 1.0 - m_a
    rowid = lax.broadcasted_iota(jnp.int32, (ln, LANES), 0)
    cat = jnp.concatenate

    def hsum(t):
        return _bmm(mms.head_sum, t, same_head)

    def pick_row(t, j):
        return jnp.sum(jnp.where(rowid == j, t, 0.0), axis=-2, keepdims=True)

    z = w0 + wl
    softplus_neg = jnp.maximum(-z, 0.0) + jnp.log(1.0 + jnp.exp(-jnp.abs(z)))
    lw = -jnp.exp(-softplus_neg - 0.5)
    a = 1.0 / (1.0 + jnp.exp(-(a0 + al)))
    kx = k * k_k
    kkn = kx / jnp.maximum(jnp.sqrt(hsum(kx * kx)), L2_EPS)
    kp = k * (1.0 + (a - 1.0) * k_a)
    aa = -kkn
    bb = kkn * a

    c = _bmm(mms.cum, incl, lw)
    c_mid = pick_row(c, ln // 2 - 1)
    ce = c - c_mid
    e_pos = jnp.exp(ce)
    e_neg = jnp.exp(-ce)
    at = aa * jnp.exp(ce - lw)
    bt = bb * e_neg
    kt = kp * e_neg
    rt = r * e_pos
    s0p = s0 * jnp.exp(c_mid)

    bk = cat([bt, kt], axis=-2)
    sc = _bmm(mms.score, cat([at * m_a, at * m_b, rt * m_a, rt * m_b], axis=-2), _bt(bk))
    ab_a, ak_a = sc[..., 0:ln, 0:ln] * strict, sc[..., 0:ln, ln:] * strict
    ab_b, ak_b = sc[..., ln:2 * ln, 0:ln] * strict, sc[..., ln:2 * ln, ln:] * strict
    incl2 = cat([incl, incl], axis=1)
    p_a = sc[..., 2 * ln:3 * ln, :] * incl2
    p_b = sc[..., 3 * ln:, :] * incl2

    base = _bmm(mms.score, cat([at, rt], axis=-2), _bt(s0p))
    rhs = base[..., :ln, :] + m_a * _bmm(mms.out, ak_a, v) + m_b * _bmm(mms.out, ak_b, v)

    u = _make_solver(mms, u_hint is not None)(ab_a, ab_b, rhs, rhs if u_hint is None else u_hint)
    uv = cat([u, v], axis=-2)
    y = base[..., ln:, :] + m_a * _bmm(mms.out, p_a, uv) + m_b * _bmm(mms.out, p_b, uv)
    s_new = (s0p + _bmm(mms.state, _bt(uv), bk)) * same_head * jnp.exp(pick_row(ce, ln - 1))

    mean = hsum(y) * (1.0 / HEAD)
    yc = y - mean
    var = hsum(yc * yc) * (1.0 / HEAD)
    yn = yc * lax.rsqrt(var + GN_EPS) * ln_g + ln_b
    bonus = hsum(r * kp * r_k) * v
    return ((yn + bonus) * g, s_new), u


N_WKV_ROWS = 6
N_WKV_PAR = 7


WKV_PAIRS_PER_STEP = 4


def _to_chains(val, nbat, pp):
    if val.ndim == 2:
        return jnp.stack([val[:, q * LANES:(q + 1) * LANES] for _ in range(nbat) for q in range(pp)])
    return jnp.stack([val[b, :, q * LANES:(q + 1) * LANES] for b in range(nbat) for q in range(pp)])


def _wkv_fwd(seq, rows, pars):
    n = rows[0].shape[0]
    nbat, nch, npair, pp = n // seq, seq // CHUNK, D // LANES, WKV_PAIRS_PER_STEP
    chunk_fn = functools.partial(_wkv_chunk, _wkv_mms(WKV_PRECISION))

    def body(*refs):
        row_vals = [_to_chains(r[...], nbat, pp) for r in refs[:N_WKV_ROWS]]
        par_vals = [_to_chains(r[...], nbat, pp) for r in refs[N_WKV_ROWS:N_WKV_ROWS + N_WKV_PAR]]
        yo_ref, ck_ref, u_ref, s_ref = refs[N_WKV_ROWS + N_WKV_PAR:]
        ch = pl.program_id(1)

        @pl.when(ch == 0)
        def _():
            s_ref[...] = jnp.zeros_like(s_ref)

        s0 = s_ref[...]
        (yo, s_new), u = chunk_fn(s0, *row_vals, *par_vals)
        s_ref[...] = s_new
        for b in range(nbat):
            for q in range(pp):
                ck_ref[b, q] = s0[b * pp + q]
                yo_ref[b, :, q * LANES:(q + 1) * LANES] = yo[b * pp + q].astype(yo_ref.dtype)
                u_ref[b, :, q * LANES:(q + 1) * LANES] = u[b * pp + q]

    tok = pl.BlockSpec((nbat, CHUNK, pp * LANES), lambda p, ch: (0, ch, p))
    par = pl.BlockSpec((1, pp * LANES), lambda p, ch: (0, p))
    ck = pl.BlockSpec((nbat, pp, None, LANES, LANES), lambda p, ch: (0, p, ch, 0, 0))
    yo, ckpt, u_all = pl.pallas_call(
        body, name="wkv_fwd",
        out_shape=[jax.ShapeDtypeStruct((nbat, seq, D), BF16),
                   jax.ShapeDtypeStruct((nbat, npair, nch, LANES, LANES), F32),
                   jax.ShapeDtypeStruct((nbat, seq, D), F32)],
        grid=(npair // pp, nch),
        in_specs=[tok] * N_WKV_ROWS + [par] * N_WKV_PAR, out_specs=[tok, ck, tok],
        scratch_shapes=[pltpu.VMEM((nbat * pp, LANES, LANES), F32)],
        compiler_params=_params(("parallel", "arbitrary")),
    )(*[t.reshape(nbat, seq, D) for t in rows], *pars)
    return yo.reshape(n, D), ckpt, u_all


def _wkv_bwd(seq, rows, pars, ckpt, u_all, dyo):
    n = rows[0].shape[0]
    nbat, nch, npair, pp = n // seq, seq // CHUNK, D // LANES, WKV_PAIRS_PER_STEP
    chunk_fn = functools.partial(_wkv_chunk, _wkv_mms(WKV_PRECISION))
    n_in = N_WKV_ROWS + N_WKV_PAR

    def body(*refs):
        row_vals = [_to_chains(r[...], nbat, pp) for r in refs[:N_WKV_ROWS]]
        par_vals = [_to_chains(r[...], nbat, pp) for r in refs[N_WKV_ROWS:n_in]]
        ck_ref, u_ref, dyo_ref = refs[n_in:n_in + 3]
        d_rows = refs[n_in + 3:n_in + 3 + N_WKV_ROWS]
        d_pars = refs[n_in + 3 + N_WKV_ROWS:n_in + 3 + N_WKV_ROWS + N_WKV_PAR]
        ds_ref = refs[-1]
        ch = pl.program_id(1)

        @pl.when(ch == 0)
        def _():
            ds_ref[...] = jnp.zeros_like(ds_ref)

        s0 = jnp.stack([ck_ref[b, q] for b in range(nbat) for q in range(pp)])
        dyo_v = _to_chains(dyo_ref[...].astype(F32), nbat, pp)
        u_hint = _to_chains(u_ref[...], nbat, pp)
        _, vjp, _ = jax.vjp(functools.partial(chunk_fn, u_hint=u_hint), s0, *row_vals, *par_vals, has_aux=True)
        grads = vjp((dyo_v, ds_ref[...]))
        ds_ref[...] = grads[0]
        for ref, val in zip(d_rows, grads[1:1 + N_WKV_ROWS]):
            for b in range(nbat):
                for q in range(pp):
                    ref[b, :, q * LANES:(q + 1) * LANES] = val[b * pp + q].astype(ref.dtype)
        for ref, val in zip(d_pars, grads[1 + N_WKV_ROWS:]):
            per_pair = [functools.reduce(lambda s, t: s + t, [val[b * pp + q] for b in range(nbat)])
                        for q in range(pp)]
            tot = jnp.concatenate(per_pair, axis=1)

            @pl.when(ch == 0)
            def _(ref=ref, tot=tot):
                ref[...] = tot

            @pl.when(ch != 0)
            def _(ref=ref, tot=tot):
                ref[...] += tot

    tok = pl.BlockSpec((nbat, CHUNK, pp * LANES), lambda p, ch: (0, nch - 1 - ch, p))
    par = pl.BlockSpec((1, pp * LANES), lambda p, ch: (0, p))
    ck = pl.BlockSpec((nbat, pp, None, LANES, LANES), lambda p, ch: (0, p, nch - 1 - ch, 0, 0))
    res = pl.pallas_call(
        body, name="wkv_bwd",
        out_shape=[jax.ShapeDtypeStruct((nbat, seq, D), BF16)] * N_WKV_ROWS
        + [jax.ShapeDtypeStruct((1, D), F32)] * N_WKV_PAR,
        grid=(npair // pp, nch),
        in_specs=[tok] * N_WKV_ROWS + [par] * N_WKV_PAR + [ck, tok, tok],
        out_specs=[tok] * N_WKV_ROWS + [par] * N_WKV_PAR,
        scratch_shapes=[pltpu.VMEM((nbat * pp, LANES, LANES), F32)],
        compiler_params=_params(("parallel", "arbitrary")),
    )(*[t.reshape(nbat, seq, D) for t in rows], *pars, ckpt, u_all, dyo.reshape(nbat, seq, D))
    return [t.reshape(n, D) for t in res[:N_WKV_ROWS]] + list(res[N_WKV_ROWS:])


def _place():
    return lax.axis_index("x"), lax.axis_index("y"), lax.axis_index("c")


def _all_gather8(blk):
    m_per, n = blk.shape
    assert m_per % 8 == 0

    def body(x_ref, out_ref, send_sems, recv_sems, local_sem):
        x, y, c = _place()
        me, sibling = (x, y, c), (x, y, 1 - c)
        chips = [(1 - x, y), (x, 1 - y), (1 - x, 1 - y)]

        def rows(px, py, pc):
            return out_ref.at[pl.ds((4 * px + 2 * py + pc) * m_per, m_per), :]

        def copy(k, block, to, src=None):
            return pltpu.make_async_remote_copy(
                src_ref=rows(*block) if src is None else src, dst_ref=rows(*block),
                send_sem=send_sems.at[k], recv_sem=recv_sems.at[k],
                device_id=to, device_id_type=MESH)

        mine = pltpu.make_async_copy(x_ref, rows(*me), local_sem)
        mine.start()
        first = [copy(0, me, sibling, src=x_ref)]
        first += [copy(1 + j, me, (*chip, c), src=x_ref) for j, chip in enumerate(chips)]
        for cp in first:
            cp.start()
        passed = [copy(4 + j, (*chip, c), sibling) for j, chip in enumerate(chips)]
        for j, chip in enumerate(chips):
            copy(1 + j, (*chip, c), me).wait_recv()
            passed[j].start()
        copy(0, sibling, me).wait_recv()
        for j, chip in enumerate(chips):
            copy(4 + j, (*chip, 1 - c), me).wait_recv()
        for cp in first + passed:
            cp.wait_send()
        mine.wait()

    vm = pl.BlockSpec(memory_space=pltpu.VMEM)
    return pl.pallas_call(
        body, name="all_gather8_%dx%d" % (m_per, n),
        out_shape=jax.ShapeDtypeStruct((N_DEV * m_per, n), blk.dtype),
        in_specs=[vm], out_specs=vm,
        scratch_shapes=[pltpu.SemaphoreType.DMA((7,)), pltpu.SemaphoreType.DMA((7,)),
                        pltpu.SemaphoreType.DMA],
        compiler_params=_params(),
    )(blk)


def _own_slot(src, name):
    r, w = src.shape[-2:]
    tr = _tile(r, 1008)
    xi, yi, _ = _place()
    chip = jnp.reshape(2 * xi + yi, (1,)).astype(jnp.int32)

    def body(chip_ref, x_ref, o_ref):
        o_ref[...] = x_ref[...]

    if src.ndim == 2:
        in_spec = pl.BlockSpec((tr, w), lambda i, chip_ref: (i, 0))
    else:
        in_spec = pl.BlockSpec((None, tr, w), lambda i, chip_ref: (chip_ref[0], i, 0))
    return pl.pallas_call(
        body, name=name,
        out_shape=jax.ShapeDtypeStruct((N_CHIPS, r, w), src.dtype),
        grid_spec=pltpu.PrefetchScalarGridSpec(
            num_scalar_prefetch=1, grid=(r // tr,), in_specs=[in_spec],
            out_specs=pl.BlockSpec((None, tr, w), lambda i, chip_ref: (chip_ref[0], i, 0))),
        compiler_params=_params(("parallel",)),
    )(chip, src)


def _chip_all_gather(shard):
    r, w = shard.shape
    half = r // 2
    assert r % 2 == 0 and half % 16 == 0

    def body(x_ref, buf_ref, out_ref, send_sems, recv_sems):
        del buf_ref
        x, y, c = _place()
        sibling = (x, y, 1 - c)
        me_p = 2 * x + y
        chips = [(1 - x, y), (x, 1 - y), (1 - x, 1 - y)]

        def piece(p, h):
            return out_ref.at[p, pl.ds(h * half, half), :]

        def copy(k, p, h, to, src=None):
            return pltpu.make_async_remote_copy(
                src_ref=piece(p, h) if src is None else src, dst_ref=piece(p, h),
                send_sem=send_sems.at[k], recv_sem=recv_sems.at[k],
                device_id=to, device_id_type=MESH)

        my_half = x_ref.at[pl.ds(c * half, half), :]
        first = [copy(j, me_p, c, (*chip, c), src=my_half) for j, chip in enumerate(chips)]
        for cp in first:
            cp.start()
        passed = [copy(3 + j, 2 * chip[0] + chip[1], c, sibling) for j, chip in enumerate(chips)]
        for j, chip in enumerate(chips):
            copy(j, 2 * chip[0] + chip[1], c, sibling).wait_recv()
            passed[j].start()
        for j, chip in enumerate(chips):
            copy(3 + j, 2 * chip[0] + chip[1], 1 - c, sibling).wait_recv()
        for cp in first + passed:
            cp.wait_send()

    hbm = pl.BlockSpec(memory_space=pl.ANY)
    return pl.pallas_call(
        body, name="chip_all_gather",
        out_shape=jax.ShapeDtypeStruct((N_CHIPS, r, w), shard.dtype),
        in_specs=[hbm, hbm], out_specs=hbm, input_output_aliases={1: 0},
        scratch_shapes=[pltpu.SemaphoreType.DMA((6,)), pltpu.SemaphoreType.DMA((6,))],
        compiler_params=_params(),
    )(shard, _own_slot(shard, "gather_own_slot"))


def _sibling_swap_halves(g):
    _, r, w = g.shape
    half = r // 2

    def body(g_ref, t_ref, send_sem, recv_sem):
        x, y, c = _place()
        sibling = (x, y, 1 - c)
        cp = pltpu.make_async_remote_copy(
            src_ref=g_ref.at[:, pl.ds((1 - c) * half, half), :], dst_ref=t_ref,
            send_sem=send_sem, recv_sem=recv_sem, device_id=sibling, device_id_type=MESH)
        cp.start()
        cp.wait()

    hbm = pl.BlockSpec(memory_space=pl.ANY)
    return pl.pallas_call(
        body, name="rs_sibling_halves",
        out_shape=jax.ShapeDtypeStruct((N_CHIPS, half, w), g.dtype),
        in_specs=[hbm], out_specs=hbm,
        scratch_shapes=[pltpu.SemaphoreType.DMA, pltpu.SemaphoreType.DMA],
        compiler_params=_params(),
    )(g)


def _add_own_half(g, t):
    _, r, w = g.shape
    half = r // 2
    tr = 1008 if half % 1008 == 0 else 16
    assert half % tr == 0
    cidx = jnp.reshape(lax.axis_index("c"), (1,)).astype(jnp.int32)

    def body(c_ref, g_ref, t_ref, o_ref):
        o_ref[...] = (g_ref[...] + t_ref[...]).astype(o_ref.dtype)

    return pl.pallas_call(
        body, name="rs_add_halves",
        out_shape=jax.ShapeDtypeStruct((N_CHIPS, half, w), BF16),
        grid_spec=pltpu.PrefetchScalarGridSpec(
            num_scalar_prefetch=1, grid=(N_CHIPS, half // tr),
            in_specs=[pl.BlockSpec((None, None, tr, w), lambda p, i, c_ref: (p, c_ref[0], i, 0)),
                      pl.BlockSpec((None, tr, w), lambda p, i, c_ref: (p, i, 0))],
            out_specs=pl.BlockSpec((None, tr, w), lambda p, i, c_ref: (p, i, 0))),
        compiler_params=_params(("parallel", "parallel")),
    )(cidx, g.reshape(N_CHIPS, 2, half, w), t)


def _chip_exchange(h):
    _, hh, w = h.shape

    def body(h_ref, buf_ref, t_ref, send_sems, recv_sems):
        del buf_ref
        x, y, c = _place()
        me_p = 2 * x + y
        chips = [(1 - x, y), (x, 1 - y), (1 - x, 1 - y)]
        cps = []
        for j, chip in enumerate(chips):
            q = 2 * chip[0] + chip[1]
            cps.append(pltpu.make_async_remote_copy(
                src_ref=h_ref.at[q], dst_ref=t_ref.at[me_p],
                send_sem=send_sems.at[j], recv_sem=recv_sems.at[j],
                device_id=(*chip, c), device_id_type=MESH))
        for cp in cps:
            cp.start()
        for j, chip in enumerate(chips):
            q = 2 * chip[0] + chip[1]
            pltpu.make_async_remote_copy(
                src_ref=h_ref.at[q], dst_ref=t_ref.at[q],
                send_sem=send_sems.at[j], recv_sem=recv_sems.at[j],
                device_id=(*chip, c), device_id_type=MESH).wait_recv()
        for cp in cps:
            cp.wait_send()

    hbm = pl.BlockSpec(memory_space=pl.ANY)
    return pl.pallas_call(
        body, name="rs_chip_exchange",
        out_shape=jax.ShapeDtypeStruct(h.shape, h.dtype),
        in_specs=[hbm, hbm], out_specs=hbm, input_output_aliases={1: 0},
        scratch_shapes=[pltpu.SemaphoreType.DMA((3,)), pltpu.SemaphoreType.DMA((3,))],
        compiler_params=_params(),
    )(h, _own_slot(h, "rs_own_slot"))


def _sum_slots(t):
    _, hh, w = t.shape
    tr = 1008 if hh % 1008 == 0 else 16
    assert hh % tr == 0
    nblk = hh // tr
    cidx = jnp.reshape(lax.axis_index("c"), (1,)).astype(jnp.int32)

    def body(c_ref, t_ref, o_ref):
        s0, s1, s2, s3 = [t_ref[j].astype(F32) for j in range(N_CHIPS)]
        o_ref[...] = ((s0 + s1) + s2) + s3

    return pl.pallas_call(
        body, name="rs_sum_slots", out_shape=jax.ShapeDtypeStruct((2 * hh, w), F32),
        grid_spec=pltpu.PrefetchScalarGridSpec(
            num_scalar_prefetch=1, grid=(nblk,),
            in_specs=[pl.BlockSpec((N_CHIPS, tr, w), lambda i, c_ref: (0, i, 0))],
            out_specs=pl.BlockSpec((tr, w), lambda i, c_ref: (c_ref[0] * nblk + i, 0))),
        compiler_params=_params(("parallel",)),
    )(cidx, t)


def _sibling_join_halves(s):
    h2, w = s.shape
    hh = h2 // 2

    def body(s_ref, o_ref, send_sem, recv_sem):
        del s_ref
        x, y, c = _place()
        sibling = (x, y, 1 - c)
        cp = pltpu.make_async_remote_copy(
            src_ref=o_ref.at[pl.ds(c * hh, hh), :], dst_ref=o_ref.at[pl.ds(c * hh, hh), :],
            send_sem=send_sem, recv_sem=recv_sem, device_id=sibling, device_id_type=MESH)
        cp.start()
        pltpu.make_async_remote_copy(
            src_ref=o_ref.at[pl.ds((1 - c) * hh, hh), :], dst_ref=o_ref.at[pl.ds((1 - c) * hh, hh), :],
            send_sem=send_sem, recv_sem=recv_sem, device_id=sibling, device_id_type=MESH).wait_recv()
        cp.wait_send()

    hbm = pl.BlockSpec(memory_space=pl.ANY)
    return pl.pallas_call(
        body, name="rs_sibling_join",
        out_shape=jax.ShapeDtypeStruct(s.shape, s.dtype),
        in_specs=[hbm], out_specs=hbm, input_output_aliases={0: 0},
        scratch_shapes=[pltpu.SemaphoreType.DMA, pltpu.SemaphoreType.DMA],
        compiler_params=_params(),
    )(s)


def _reduce_scatter(g):
    h = _add_own_half(g, _sibling_swap_halves(g))
    return _sibling_join_halves(_sum_slots(_chip_exchange(h)))


def _unshard_cols(piece):
    p, k, n = piece.shape
    return jnp.transpose(piece, (1, 0, 2)).reshape(k, p * n)


def _shard_cols(full):
    k, n4 = full.shape
    return jnp.transpose(full.reshape(k, N_CHIPS, n4 // N_CHIPS), (1, 0, 2))


def _rows_of(piece):
    return piece.reshape(N_CHIPS, -1, ROW_W)


def _pad_rows(a, mult):
    pad = (-a.shape[-2]) % mult
    if pad == 0:
        return a
    widths = [(0, 0)] * (a.ndim - 2) + [(0, pad), (0, 0)]
    return jnp.pad(a, widths)


def _adamw(w, g, m, v):
    m2 = ADAM_B1 * m + (1.0 - ADAM_B1) * g
    v2 = ADAM_B2 * v + (1.0 - ADAM_B2) * (g * g)
    m_hat = m2 / (1.0 - ADAM_B1 ** ADAM_STEP)
    v_hat = v2 / (1.0 - ADAM_B2 ** ADAM_STEP)
    delta = -ADAM_LR * (m_hat / (jnp.sqrt(v_hat) + ADAM_EPS) + ADAM_WD * w)
    return delta, m2, v2


def _mlp_fwd(seq, tm, x_in, mix, mod, w1, w2, tag, residual=True):
    x1, h2 = _rw_fwd(_f_res_norm2, "res_norm2_" + tag, seq=seq, tm=tm, rows=[x_in, mix], bvecs=[mod],
                     outs=[(D, F32), (D, BF16)])
    def relu_and_square(acc):
        r = jnp.maximum(acc, 0.0)
        return r, r * r

    p, f = _matmul_ep(h2, w1, [], relu_and_square, [BF16, BF16], mode="nn", name="mlp_up_" + tag)
    ff = _matmul(f, w2, mode="nn", name="mlp_down_" + tag)
    x2 = None
    if residual:
        (x2,) = _rw_fwd(_f_res2, "res2_" + tag, seq=seq, tm=tm, rows=[x1, ff], bvecs=[mod], outs=[(D, F32)])
    return x2, (x1, h2, p, f, ff)


def _mlp_bwd(seq, tm, saved, x_in, mix, mod, w1, w2, dx2, tag, res2_grads=None):
    x1, h2, p, f, ff = saved
    if res2_grads is None:
        dff, dmod_a = _rw_bwd(_f_res2, "res2_bwd_" + tag, seq=seq, tm=tm, rows=[x1, ff], bvecs=[mod],
                              cts=[dx2], need_rows=[1], dtypes=[BF16])
    else:
        dff, dmod_a = res2_grads
    (dp,) = _matmul_ep(dff, w2, [p], lambda acc, pt: (2.0 * pt.astype(F32) * acc,), [BF16],
                       mode="nt", name="mlp_down_dx_" + tag)
    dw2 = _matmul(f, dff, mode="tn", name="mlp_down_dw_" + tag)
    dh2 = _matmul(dp, w1, mode="nt", name="mlp_up_dx_" + tag)
    dw1 = _matmul(h2, dp, mode="tn", name="mlp_up_dw_" + tag, out_shards=N_CHIPS)
    dx_in, dmix, dmod_b = _rw_bwd(_f_res_norm2, "res_norm2_bwd_" + tag, seq=seq, tm=tm, rows=[x_in, mix],
                                  bvecs=[mod], cts=[dx2, dh2], need_rows=[0, 1], dtypes=[F32, BF16])
    return dx_in, dmix, dmod_a + dmod_b, dw1, dw2


def kernel(x, c, ada_w, ada_b, mlp_w1, mlp_w2, a_w_in, a_ln_g, a_ln_b, a_w_s, a_b_s, a_w_out, b_mu, b_w_in, b_w0, b_w1, b_w2, b_a0, b_a1, b_a2, b_g1, b_g2, b_k_k, b_k_a, b_r_k, b_ln_g, b_ln_b, b_w_out, final_g, loss_target, m_ada_w, m_ada_b, m_mlp_w1, m_mlp_w2, m_a_w_in, m_a_ln_g, m_a_ln_b, m_a_w_s, m_a_b_s, m_a_w_out, m_b_mu, m_b_w_in, m_b_w0, m_b_w1, m_b_w2, m_b_a0, m_b_a1, m_b_a2, m_b_g1, m_b_g2, m_b_k_k, m_b_k_a, m_b_r_k, m_b_ln_g, m_b_ln_b, m_b_w_out, m_final_g, v_ada_w, v_ada_b, v_mlp_w1, v_mlp_w2, v_a_w_in, v_a_ln_g, v_a_ln_b, v_a_w_s, v_a_b_s, v_a_w_out, v_b_mu, v_b_w_in, v_b_w0, v_b_w1, v_b_w2, v_b_a0, v_b_a1, v_b_a2, v_b_g1, v_b_g2, v_b_k_k, v_b_k_a, v_b_r_k, v_b_ln_g, v_b_ln_b, v_b_w_out, v_final_g):
    weights = dict(ada_w=ada_w, ada_b=ada_b, mlp_w1=mlp_w1, mlp_w2=mlp_w2, a_w_in=a_w_in, a_ln_g=a_ln_g,
                   a_ln_b=a_ln_b, a_w_s=a_w_s, a_b_s=a_b_s, a_w_out=a_w_out, b_mu=b_mu, b_w_in=b_w_in,
                   b_w0=b_w0, b_w1=b_w1, b_w2=b_w2, b_a0=b_a0, b_a1=b_a1, b_a2=b_a2, b_g1=b_g1, b_g2=b_g2,
                   b_k_k=b_k_k, b_k_a=b_k_a, b_r_k=b_r_k, b_ln_g=b_ln_g, b_ln_b=b_ln_b, b_w_out=b_w_out,
                   final_g=final_g)
    moms = dict(ada_w=(m_ada_w, v_ada_w), ada_b=(m_ada_b, v_ada_b), mlp_w1=(m_mlp_w1, v_mlp_w1),
                mlp_w2=(m_mlp_w2, v_mlp_w2), a_w_in=(m_a_w_in, v_a_w_in), a_ln_g=(m_a_ln_g, v_a_ln_g),
                a_ln_b=(m_a_ln_b, v_a_ln_b), a_w_s=(m_a_w_s, v_a_w_s), a_b_s=(m_a_b_s, v_a_b_s),
                a_w_out=(m_a_w_out, v_a_w_out), b_mu=(m_b_mu, v_b_mu), b_w_in=(m_b_w_in, v_b_w_in),
                b_w0=(m_b_w0, v_b_w0), b_w1=(m_b_w1, v_b_w1), b_w2=(m_b_w2, v_b_w2), b_a0=(m_b_a0, v_b_a0),
                b_a1=(m_b_a1, v_b_a1), b_a2=(m_b_a2, v_b_a2), b_g1=(m_b_g1, v_b_g1), b_g2=(m_b_g2, v_b_g2),
                b_k_k=(m_b_k_k, v_b_k_k), b_k_a=(m_b_k_a, v_b_k_a), b_r_k=(m_b_r_k, v_b_r_k),
                b_ln_g=(m_b_ln_g, v_b_ln_g), b_ln_b=(m_b_ln_b, v_b_ln_b), b_w_out=(m_b_w_out, v_b_w_out),
                final_g=(m_final_g, v_final_g))
    order = list(weights)

    nbat, seq, _ = x.shape
    n = nbat * seq
    tm = 256
    xi, yi, ci = _place()
    chip = 2 * xi + yi
    dev = 2 * chip + ci
    x0 = x.reshape(n, D)
    tgt = loss_target.reshape(n, D)
    lora_w, lora_g = b_w1.shape[-1], b_g1.shape[-1]
    lora_wp, lora_gp = LANES, 2 * LANES

    (cond,) = _small(lambda cc: (cc / (1.0 + jnp.exp(-cc)),), "silu_c", [c], [c.shape])
    vec_names = ["b_w0", "b_a0", "b_k_k", "b_k_a", "b_ln_g", "b_ln_b"]
    vec_shard = jnp.concatenate([b_mu[0]] + [weights[k] for k in vec_names], axis=0)
    n_vec = vec_shard.shape[0]
    vec_rows = vec_shard.reshape(-1, ROW_W)
    blk = _pad_rows(jnp.concatenate([cond, vec_rows], axis=0), 8)
    assert blk.shape[0] == 8
    gathered = _all_gather8(blk).reshape(N_DEV, 8, D)
    cond_all = gathered[:, :nbat].reshape(N_DEV * nbat, D)
    vec_all = gathered[0::2, nbat:nbat + vec_rows.shape[0]].reshape(N_CHIPS, n_vec, D // N_CHIPS)
    vec_full = jnp.transpose(vec_all, (1, 0, 2)).reshape(n_vec, D)
    mu_full = vec_full[0:6]
    w0_f, a0_f, kk_f, ka_f, lng_f, lnb_f = [vec_full[6 + j:7 + j] for j in range(6)]
    rk_f = b_r_k.reshape(1, D)

    n_ada = ada_w.shape[-1]
    parts = jnp.concatenate(
        [_matmul(cond_all, ada_w[i], mode="nn", name="ada_fwd_%d" % i) for i in range(2)], axis=1)
    parts_all = _all_gather8(parts).reshape(N_DEV, N_DEV * nbat, 2, n_ada)[0::2]
    mine = lax.dynamic_slice_in_dim(parts_all, dev * nbat, nbat, axis=1)
    mods = []
    for i in range(2):
        full = jnp.transpose(mine[:, :, i], (1, 0, 2)).reshape(nbat, N_MOD * D) + ada_b[i]
        mods.append(full.reshape(nbat, N_MOD, D))

    big = [("mlp_w1_0", mlp_w1[0]), ("mlp_w1_1", mlp_w1[1]), ("mlp_w2_0", mlp_w2[0]), ("mlp_w2_1", mlp_w2[1]),
           ("a_w_in", a_w_in[0]), ("a_w_out", a_w_out[0]), ("b_w_in", b_w_in[0]), ("b_w_out", b_w_out[0]),
           ("b_w1", b_w1[0]), ("b_w2", b_w2[0]), ("b_a1", b_a1[0]), ("b_a2", b_a2[0]),
           ("b_g1", b_g1[0]), ("b_g2", b_g2[0])]
    offs, pos = {}, 0
    for name, arr in big:
        rows_k = arr.size // ROW_W
        offs[name] = (pos, rows_k, arr.shape)
        pos += rows_k
    n_big_rows = pos
    wflat = _pad_rows(jnp.concatenate([arr.astype(BF16).reshape(-1, ROW_W) for _, arr in big], axis=0), 32)
    wg = _chip_all_gather(wflat)

    def gathered_piece(name):
        start, rows_k, shape = offs[name]
        return wg[:, start:start + rows_k].reshape((N_CHIPS,) + shape)

    def col_w(name):
        return _unshard_cols(gathered_piece(name))

    def row_w(name):
        piece = gathered_piece(name)
        return piece.reshape(N_CHIPS * piece.shape[1], piece.shape[2])

    w1_l = [col_w("mlp_w1_0"), col_w("mlp_w1_1")]
    w2_l = [row_w("mlp_w2_0"), row_w("mlp_w2_1")]
    a_win, a_wout = col_w("a_w_in"), row_w("a_w_out")
    b_win, b_wout = col_w("b_w_in"), row_w("b_w_out")
    w_r, w_k, w_v = b_win[:, :D], b_win[:, D:2 * D], b_win[:, 2 * D:]
    w1p = jnp.pad(row_w("b_w1"), ((0, 0), (0, lora_wp - lora_w)))
    a1p = jnp.pad(row_w("b_a1"), ((0, 0), (0, lora_wp - lora_w)))
    g1p = jnp.pad(row_w("b_g1"), ((0, 0), (0, lora_gp - lora_g)))
    w2p = jnp.pad(col_w("b_w2"), ((0, lora_wp - lora_w), (0, 0)))
    a2p = jnp.pad(col_w("b_a2"), ((0, lora_wp - lora_w), (0, 0)))
    g2p = jnp.pad(col_w("b_g2"), ((0, lora_gp - lora_g), (0, 0)))

    mod0, mod1 = mods
    (h_a,) = _rw_fwd(_f_norm1, "norm1_a", seq=seq, tm=tm, rows=[x0], bvecs=[mod0], outs=[(D, BF16)])
    uvp = _matmul(h_a, a_win, mode="nn", name="sgu_in")
    ws = a_w_s[0]
    bias = jnp.broadcast_to(a_b_s[0][:, :, None], (GROUPS, CHUNK, LANES))
    z = _sgu_fwd(uvp, a_ln_g, a_ln_b, ws, bias)
    mix0 = _matmul(z, a_wout, mode="nn", name="sgu_out")
    x2, saved0 = _mlp_fwd(seq, tm, x0, mix0, mod0, w1_l[0], w2_l[0], "0")

    def shift_fwd(ctx, rv, pv, nv, bv, pa):
        _, _, mixes = _shift_mix(ctx, rv[0], pv[0], bv[0], pa[0])
        return mixes, [], []

    xr, xw, xk, xv, xa, xg = _rowwise(shift_fwd, "shift_mix", seq=seq, tm=tm, rows=[x2], prev8=[x2],
                                      bvecs=[mod1], params=[mu_full], out_rows=[(D, BF16)] * 6)
    r = _matmul(xr, w_r, mode="nn", name="rwkv_r")
    k = _matmul(xk, w_k, mode="nn", name="rwkv_k")
    v = _matmul(xv, w_v, mode="nn", name="rwkv_v")
    def act_tanh(t):
        return jnp.tanh(t)

    def act_sigmoid(t):
        return 1.0 / (1.0 + jnp.exp(-t))

    t1, th = _matmul_ep(xw, w1p, [], lambda acc: (acc, act_tanh(acc)), [F32, BF16], mode="nn", name="lora_w1")
    t2 = _matmul(xa, a1p, mode="nn", name="lora_a1", out_dtype=BF16)
    t3, sg = _matmul_ep(xg, g1p, [], lambda acc: (acc, act_sigmoid(acc)), [F32, BF16], mode="nn", name="lora_g1")
    wl = _matmul(th, w2p, mode="nn", name="lora_w2")
    al = _matmul(t2, a2p, mode="nn", name="lora_a2")
    g = _matmul(sg, g2p, mode="nn", name="lora_g2")
    wkv_rows = [r, k, v, wl, al, g]
    wkv_pars = [w0_f, a0_f, kk_f, ka_f, rk_f, lng_f, lnb_f]
    yo, ckpt, wkv_u = _wkv_fwd(seq, wkv_rows, wkv_pars)
    mix1 = _matmul(yo, b_wout, mode="nn", name="rwkv_out")
    _, saved1 = _mlp_fwd(seq, tm, x2, mix1, mod1, w1_l[1], w2_l[1], "1", residual=False)

    def loss_fn(ctx, rv, pv, nv, bv, pa):
        def head(x1, ff, mod, fg):
            return _f_loss(_f_res2(x1, ff, mod)[0], rv[2], fg)
        val, (dx, dff, dmod, dfg) = jax.value_and_grad(head, argnums=(0, 1, 2, 3))(rv[0], rv[1], bv[0], pa[0])
        return [dx, dff], [dmod], [dfg, jnp.full((1, LANES), val, F32)]

    dx4, dff1, dmod1_a, d_final_g, loss_acc = _rowwise(
        loss_fn, "loss_head", seq=seq, tm=tm, rows=[saved1[0], saved1[4], tgt], bvecs=[mod1],
        params=[final_g.reshape(1, D)], out_rows=[(D, F32), (D, BF16)], out_bacc=[(N_MOD, D)],
        out_pacc=[(1, D), (1, LANES)])
    loss = lax.psum(loss_acc[0, 0], AXES)

    dx2_a, dmix1, dmod1, dw1_1, dw2_1 = _mlp_bwd(seq, tm, saved1, x2, mix1, mod1, w1_l[1], w2_l[1], dx4, "1",
                                                 res2_grads=(dff1, dmod1_a))
    dyo = _matmul(dmix1, b_wout, mode="nt", name="rwkv_out_dx")
    d_b_wout = _matmul(yo, dmix1, mode="tn", name="rwkv_out_dw")
    wkv_grads = _wkv_bwd(seq, wkv_rows, wkv_pars, ckpt, wkv_u, dyo)
    dr, dk, dv, dwl, dal, dg = wkv_grads[:N_WKV_ROWS]
    d_w0, d_a0, d_kk, d_ka, d_rk, d_lng, d_lnb = wkv_grads[N_WKV_ROWS:]
    def through(act):
        return lambda acc, t: (jax.vjp(act, t)[1](acc)[0],)

    (dt1,) = _matmul_ep(dwl, w2p, [t1], through(act_tanh), [BF16], mode="nt", name="lora_w2_dx")
    d_w2p = _matmul(th, dwl, mode="tn", name="lora_w2_dw")
    dt2 = _matmul(dal, a2p, mode="nt", name="lora_a2_dx", out_dtype=BF16)
    d_a2p = _matmul(t2, dal, mode="tn", name="lora_a2_dw")
    (dt3,) = _matmul_ep(dg, g2p, [t3], through(act_sigmoid), [BF16], mode="nt", name="lora_g2_dx")
    d_g2p = _matmul(sg, dg, mode="tn", name="lora_g2_dw")
    dxw = _matmul(dt1, w1p, mode="nt", name="lora_w1_dx")
    d_w1p = _matmul(xw, dt1, mode="tn", name="lora_w1_dw")
    dxa = _matmul(dt2, a1p, mode="nt", name="lora_a1_dx")
    d_a1p = _matmul(xa, dt2, mode="tn", name="lora_a1_dw")
    dxg = _matmul(dt3, g1p, mode="nt", name="lora_g1_dx")
    d_g1p = _matmul(xg, dt3, mode="tn", name="lora_g1_dw")
    dxr = _matmul(dr, w_r, mode="nt", name="rwkv_r_dx")
    dxk = _matmul(dk, w_k, mode="nt", name="rwkv_k_dx")
    dxv = _matmul(dv, w_v, mode="nt", name="rwkv_v_dx")
    d_b_win = jnp.concatenate([_matmul(xr, dr, mode="tn", name="rwkv_r_dw"),
                               _matmul(xk, dk, mode="tn", name="rwkv_k_dw"),
                               _matmul(xv, dv, mode="tn", name="rwkv_v_dw")], axis=1)

    def shift_bwd(ctx, rv, pv, nv, bv, pa):
        xt, dres = rv[0], rv[1]
        dmix_in = rv[2:8]
        mod, mu = bv[0], pa[0]
        f_h = lambda xx_, mod_: _rmsmod(xx_, mod_[0:1], mod_[1:2])
        h, vjp = jax.vjp(f_h, xt, mod)
        hprev = f_h(pv[0], mod)[7:8]
        hprev = jnp.where(ctx.first, jnp.zeros_like(hprev), hprev)
        rowid = lax.broadcasted_iota(jnp.int32, h.shape, 0)
        xx = jnp.where(rowid == 0, hprev, pltpu.roll(h, 1, 0)) - h
        tot = dmix_in[0]
        wsum = dmix_in[0] * mu[0:1]
        for j in range(1, 6):
            tot = tot + dmix_in[j]
            wsum = wsum + dmix_in[j] * mu[j:j + 1]
        nxt = nv[0][0:1] * mu[0:1]
        for j in range(1, 6):
            nxt = nxt + nv[j][0:1] * mu[j:j + 1]
        nxt = jnp.where(ctx.last, jnp.zeros_like(nxt), nxt)
        tmr = h.shape[0]
        wshift = jnp.where(rowid == tmr - 1, nxt, pltpu.roll(wsum, tmr - 1, 0))
        dh = tot - wsum + wshift
        dx_, dmod_ = vjp(dh)
        dmu = jnp.concatenate([jnp.sum(dmix_in[j] * xx, axis=0, keepdims=True) for j in range(6)], axis=0)
        dx2_t = dx_ + dres
        ff_below, mod_below = rv[8], bv[1]
        dff_below = mod_below[5:6] * dx2_t
        dgate = jnp.sum(dx2_t * ff_below, axis=0, keepdims=True)
        gate_row = lax.broadcasted_iota(jnp.int32, (N_MOD, D), 0) == N_MOD - 1
        dmod_below = jnp.where(gate_row, jnp.broadcast_to(dgate, (N_MOD, D)), 0.0)
        return [dx2_t, dff_below], [dmod_, dmod_below], [dmu]

    dmix_list = [dxr, dxw, dxk, dxv, dxa, dxg]
    dx2, dff0, dmod1_c, dmod0_a, d_mu = _rowwise(
        shift_bwd, "shift_mix_bwd", seq=seq, tm=tm, rows=[x2, dx2_a] + dmix_list + [saved0[4]],
        prev8=[x2], next8=dmix_list, bvecs=[mod1, mod0], params=[mu_full],
        out_rows=[(D, F32), (D, BF16)], out_bacc=[(N_MOD, D), (N_MOD, D)], out_pacc=[(6, D)])
    dmod1 = dmod1 + dmod1_c

    dx0_a, dmix0, dmod0, dw1_0, dw2_0 = _mlp_bwd(seq, tm, saved0, x0, mix0, mod0, w1_l[0], w2_l[0], dx2, "0",
                                                 res2_grads=(dff0, dmod0_a))
    dz = _matmul(dmix0, a_wout, mode="nt", name="sgu_out_dx")
    d_a_wout = _matmul(z, dmix0, mode="tn", name="sgu_out_dw")
    duvp, d_a_lng, d_a_lnb, d_ws, d_bias = _sgu_bwd(uvp, a_ln_g, a_ln_b, ws, bias, dz)
    dh_a = _matmul(duvp, a_win, mode="nt", name="sgu_in_dx")
    d_a_win = _matmul(h_a, duvp, mode="tn", name="sgu_in_dw", out_shards=N_CHIPS)
    grad_x, dmod0_c = _rw_bwd(_f_norm1, "norm1_a_bwd", seq=seq, tm=tm, rows=[x0], bvecs=[mod0], cts=[dh_a],
                              need_rows=[0], extra=dx0_a)
    dmod0 = dmod0 + dmod0_c

    dmod_blk = _pad_rows(jnp.concatenate([dmod0.reshape(nbat, -1), dmod1.reshape(nbat, -1)], axis=1), 8)
    dmod_all = _all_gather8(dmod_blk).reshape(N_DEV, 8, 2, N_MOD * D)[:, :nbat].reshape(N_DEV * nbat, 2, N_MOD * D)
    g_ada_w, g_ada_b = [], []
    for i in range(2):
        cols = lax.dynamic_slice_in_dim(dmod_all[:, i], chip * n_ada, n_ada, axis=1)
        g_ada_w.append(_matmul(cond_all, cols, mode="tn", name="ada_dw_%d" % i))
    (g_ada_b_all,) = _small(lambda t: (jnp.sum(t, axis=0),), "ada_db", [dmod_all], [(2, N_MOD * D)])
    grads = {"ada_w": jnp.stack(g_ada_w), "ada_b": g_ada_b_all}

    rep = _pad_rows(jnp.concatenate([
        d_a_lng, d_a_lnb, jnp.sum(d_bias, axis=-1).reshape(1, D), d_rk, d_final_g,
        jnp.zeros((3, D), F32), d_ws.reshape(-1, D)], axis=0), 8)
    rep_rows = rep.shape[0]
    rep_all = _all_gather8(rep)
    (rep_sum,) = _small(lambda t: (functools.reduce(lambda p, q: p + q,
                                                     [t[j * rep_rows:(j + 1) * rep_rows] for j in range(N_DEV)]),),
                        "replicated_sum", [rep_all], [(rep_rows, D)])
    grads["a_ln_g"] = rep_sum[0:1]
    grads["a_ln_b"] = rep_sum[1:2]
    grads["a_b_s"] = rep_sum[2:3].reshape(a_b_s.shape)
    grads["b_r_k"] = rep_sum[3:4].reshape(b_r_k.shape)
    grads["final_g"] = rep_sum[4].reshape(final_g.shape)
    grads["a_w_s"] = rep_sum[8:8 + GROUPS * CHUNK * LANES // D].reshape(a_w_s.shape)

    vec_grads = jnp.concatenate([d_mu, d_w0, d_a0, d_kk, d_ka, d_lng, d_lnb], axis=0)
    packed = {
        "mlp_w1_0": dw1_0, "mlp_w1_1": dw1_1,
        "mlp_w2_0": dw2_0.reshape(N_CHIPS, -1, D), "mlp_w2_1": dw2_1.reshape(N_CHIPS, -1, D),
        "a_w_in": d_a_win, "a_w_out": d_a_wout.reshape(N_CHIPS, -1, D),
        "b_w_in": _shard_cols(d_b_win), "b_w_out": d_b_wout.reshape(N_CHIPS, -1, D),
        "b_w1": d_w1p[:, :lora_w].reshape(N_CHIPS, -1, lora_w), "b_w2": _shard_cols(d_w2p[:lora_w]),
        "b_a1": d_a1p[:, :lora_w].reshape(N_CHIPS, -1, lora_w), "b_a2": _shard_cols(d_a2p[:lora_w]),
        "b_g1": d_g1p[:, :lora_g].reshape(N_CHIPS, -1, lora_g), "b_g2": _shard_cols(d_g2p[:lora_g]),
    }
    pieces = [_rows_of(packed[name]) for name, _ in big] + [_pad_rows(_rows_of(_shard_cols(vec_grads)), 8)]
    used = sum(p.shape[1] for p in pieces)
    pieces.append(jnp.zeros((N_CHIPS, (-used) % 2016, ROW_W), F32))
    g_pack = jnp.concatenate(pieces, axis=1)
    g_red = _reduce_scatter(g_pack)
    for name, _ in big:
        start, rows_k, shape = offs[name]
        grads[name] = g_red[start:start + rows_k].reshape(shape)
    vec_red = g_red[n_big_rows:n_big_rows + vec_rows.shape[0]].reshape(n_vec, D // N_CHIPS)
    grads["b_mu"] = vec_red[0:6].reshape(b_mu.shape)
    for j, name in enumerate(vec_names):
        grads[name] = vec_red[6 + j:7 + j].reshape(weights[name].shape)
    for base in ("mlp_w1", "mlp_w2"):
        grads[base] = jnp.stack([grads.pop(base + "_0"), grads.pop(base + "_1")])
    for name in ("a_w_in", "a_w_out", "b_w_in", "b_w_out", "b_w1", "b_w2", "b_a1", "b_a2", "b_g1", "b_g2"):
        grads[name] = grads[name].reshape(weights[name].shape)

    deltas, new_m, new_v = {}, {}, {}
    for name in order:
        gr = grads[name].reshape(weights[name].shape)
        grads[name] = gr
        deltas[name], new_m[name], new_v[name] = _elementwise(
            _adamw, "adamw_" + name, [weights[name], gr, moms[name][0], moms[name][1]], 3)

    return (loss, grad_x.reshape(x.shape), *[grads[k] for k in order], *[deltas[k] for k in order],
            *[new_m[k] for k in order], *[new_v[k] for k in order])
```

```python
import functools

import jax
import jax.numpy as jnp
from jax import lax
from jax.experimental import pallas as pl
from jax.experimental.pallas import tpu as pltpu

F32 = jnp.float32
BF16 = jnp.bfloat16
MESH = pl.DeviceIdType.MESH
AXES = ("x", "y", "c")

D = 1024
N_MOD = 6
HEAD = 64
CHUNK = 128
GROUPS = 8
LANES = 128
ROW_W = 1024
N_CHIPS = 4
N_DEV = 8

RMS_EPS = 1e-6
LN_EPS = 1e-5
GN_EPS = HEAD * 1e-5
L2_EPS = 1e-12

ADAM_LR = 0.001
ADAM_B1 = 0.9
ADAM_B2 = 0.999
ADAM_EPS = 1e-08
ADAM_WD = 0.01
ADAM_STEP = 10

VMEM_LIMIT_V7X = 56 * 1024 * 1024
HIGHEST = lax.Precision.HIGHEST


def _params(sem=None):
    return pltpu.CompilerParams(dimension_semantics=sem, vmem_limit_bytes=VMEM_LIMIT_V7X)


def _tile(dim, target):
    if dim <= target:
        return dim
    for cand in range(target, 0, -LANES):
        if dim % cand == 0:
            return cand
    raise ValueError((dim, target))


def _matmul(a, b, *, mode, name, out_dtype=F32, out_shards=1, tm=1024, tn=1024, tk=4096):
    if mode == "nn":
        (m, k), (k2, n) = a.shape, b.shape
    elif mode == "nt":
        (m, k), (n, k2) = a.shape, b.shape
    else:
        (k, m), (k2, n) = a.shape, b.shape
    assert k == k2, (name, a.shape, b.shape)
    n_sh = n // out_shards
    tm, tn, tk = _tile(m, tm), _tile(n_sh, tn), _tile(k, tk)
    nk = k // tk
    nb = n_sh // tn
    use_scratch = nk > 1 and out_dtype != F32

    if mode == "tn":
        a_spec = pl.BlockSpec((tk, tm), lambda i, j, kk: (kk, i))
    else:
        a_spec = pl.BlockSpec((tm, tk), lambda i, j, kk: (i, kk))
    if mode == "nt":
        b_spec = pl.BlockSpec((tn, tk), lambda i, j, kk: (j, kk))
    else:
        b_spec = pl.BlockSpec((tk, tn), lambda i, j, kk: (kk, j))
    if out_shards == 1:
        out_shape = jax.ShapeDtypeStruct((m, n), out_dtype)
        o_spec = pl.BlockSpec((tm, tn), lambda i, j, kk: (i, j))
    else:
        out_shape = jax.ShapeDtypeStruct((out_shards, m, n_sh), out_dtype)
        o_spec = pl.BlockSpec((None, tm, tn), lambda i, j, kk: (j // nb, i, j % nb))

    def body(a_ref, b_ref, o_ref, *scratch):
        kk = pl.program_id(2)
        av = a_ref[...].astype(BF16)
        bv = b_ref[...].astype(BF16)
        if mode == "nn":
            dims = (((1,), (0,)), ((), ()))
        elif mode == "nt":
            dims = (((1,), (1,)), ((), ()))
        else:
            dims = (((0,), (0,)), ((), ()))
        part = lax.dot_general(av, bv, dims, preferred_element_type=F32)
        if nk == 1:
            o_ref[...] = part.astype(o_ref.dtype)
            return
        acc_ref = scratch[0] if use_scratch else o_ref

        @pl.when(kk == 0)
        def _():
            acc_ref[...] = part

        @pl.when(kk != 0)
        def _():
            acc_ref[...] += part

        if use_scratch:
            @pl.when(kk == nk - 1)
            def _():
                o_ref[...] = acc_ref[...].astype(o_ref.dtype)

    return pl.pallas_call(
        body, name=name, out_shape=out_shape,
        grid=(m // tm, n // tn, nk),
        in_specs=[a_spec, b_spec], out_specs=o_spec,
        scratch_shapes=[pltpu.VMEM((tm, tn), F32)] if use_scratch else [],
        compiler_params=_params(("parallel", "parallel", "arbitrary")),
    )(a, b)


def _matmul_ep(a, b, extras, epilogue, out_dtypes, *, mode, name, tm=1024, tn=1024, tk=2048):
    if mode == "nn":
        (m, k), (k2, n) = a.shape, b.shape
    else:
        (m, k), (n, k2) = a.shape, b.shape
    assert k == k2 and mode in ("nn", "nt"), (name, a.shape, b.shape)
    tm, tn, tk = _tile(m, tm), _tile(n, tn), _tile(k, tk)
    nk = k // tk
    n_ex, n_out = len(extras), len(out_dtypes)

    def body(a_ref, b_ref, *rest):
        extra_refs, out_refs = rest[:n_ex], rest[n_ex:n_ex + n_out]
        kk = pl.program_id(2)
        dims = (((1,), (0,)), ((), ())) if mode == "nn" else (((1,), (1,)), ((), ()))
        part = lax.dot_general(a_ref[...].astype(BF16), b_ref[...].astype(BF16), dims,
                               preferred_element_type=F32)

        def finish(acc):
            for ref, val in zip(out_refs, epilogue(acc, *[r[...] for r in extra_refs])):
                ref[...] = val.astype(ref.dtype)

        if nk == 1:
            finish(part)
            return
        acc_ref = rest[-1]

        @pl.when(kk == 0)
        def _():
            acc_ref[...] = part

        @pl.when(kk != 0)
        def _():
            acc_ref[...] += part

        @pl.when(kk == nk - 1)
        def _():
            finish(acc_ref[...])

    a_spec = pl.BlockSpec((tm, tk), lambda i, j, kk: (i, kk))
    b_spec = (pl.BlockSpec((tk, tn), lambda i, j, kk: (kk, j)) if mode == "nn"
              else pl.BlockSpec((tn, tk), lambda i, j, kk: (j, kk)))
    o_spec = pl.BlockSpec((tm, tn), lambda i, j, kk: (i, j))
    res = pl.pallas_call(
        body, name=name, out_shape=[jax.ShapeDtypeStruct((m, n), dt) for dt in out_dtypes],
        grid=(m // tm, n // tn, nk),
        in_specs=[a_spec, b_spec] + [o_spec] * n_ex, out_specs=[o_spec] * n_out,
        scratch_shapes=[pltpu.VMEM((tm, tn), F32)] if nk > 1 else [],
        compiler_params=_params(("parallel", "parallel", "arbitrary")),
    )(a, b, *extras)
    return list(res)


class _Ctx:
    def __init__(self, first, last):
        self.first = first
        self.last = last


def _rowwise(fn, name, *, seq, tm, rows=(), prev8=(), next8=(), bvecs=(), params=(),
             out_rows=(), out_bacc=(), out_pacc=()):
    n = rows[0].shape[0]
    tm = min(tm, seq)
    assert n % seq == 0 and seq % tm == 0 and tm % 8 == 0
    tpb = seq // tm
    nt = n // tm
    nbat = n // seq
    r8 = tm // 8
    counts = [len(rows), len(prev8), len(next8), len(bvecs), len(params)]
    n_in = sum(counts)

    def body(*refs):
        i = pl.program_id(0)
        first = (i % tpb) == 0
        last = (i % tpb) == (tpb - 1)
        vals = [r[...] for r in refs[:n_in]]
        groups, pos = [], 0
        for cnt in counts:
            groups.append(vals[pos:pos + cnt])
            pos += cnt
        ro, bo, po = fn(_Ctx(first, last), *groups)
        outs = refs[n_in:]
        assert len(ro) == len(out_rows) and len(bo) == len(out_bacc) and len(po) == len(out_pacc)
        for ref, val in zip(outs[:len(ro)], ro):
            ref[...] = val.astype(ref.dtype)
        for ref, val in zip(outs[len(ro):len(ro) + len(bo)], bo):
            @pl.when(first)
            def _(ref=ref, val=val):
                ref[...] = val

            @pl.when(jnp.logical_not(first))
            def _(ref=ref, val=val):
                ref[...] += val
        for ref, val in zip(outs[len(ro) + len(bo):], po):
            @pl.when(i == 0)
            def _(ref=ref, val=val):
                ref[...] = val

            @pl.when(i != 0)
            def _(ref=ref, val=val):
                ref[...] += val

    in_specs = []
    for arr in rows:
        in_specs.append(pl.BlockSpec((tm, arr.shape[1]), lambda i: (i, 0)))
    for arr in prev8:
        in_specs.append(pl.BlockSpec((8, arr.shape[1]), lambda i: (jnp.maximum(i * r8 - 1, 0), 0)))
    for arr in next8:
        in_specs.append(pl.BlockSpec((8, arr.shape[1]), lambda i: (jnp.minimum((i + 1) * r8, n // 8 - 1), 0)))
    for arr in bvecs:
        in_specs.append(pl.BlockSpec((None,) + arr.shape[1:], lambda i: (i // tpb, 0, 0)))
    for arr in params:
        in_specs.append(pl.BlockSpec(arr.shape, lambda i: (0, 0)))
    out_shape, out_specs = [], []
    for d, dt in out_rows:
        out_shape.append(jax.ShapeDtypeStruct((n, d), dt))
        out_specs.append(pl.BlockSpec((tm, d), lambda i: (i, 0)))
    for r, d in out_bacc:
        out_shape.append(jax.ShapeDtypeStruct((nbat, r, d), F32))
        out_specs.append(pl.BlockSpec((None, r, d), lambda i: (i // tpb, 0, 0)))
    for r, d in out_pacc:
        out_shape.append(jax.ShapeDtypeStruct((r, d), F32))
        out_specs.append(pl.BlockSpec((r, d), lambda i: (0, 0)))
    res = pl.pallas_call(
        body, name=name, out_shape=out_shape, grid=(nt,),
        in_specs=in_specs, out_specs=out_specs,
        compiler_params=_params(("arbitrary",)),
    )(*rows, *prev8, *next8, *bvecs, *params)
    return list(res)


def _rw_fwd(f, name, *, seq, tm, rows, bvecs=(), params=(), outs):
    def fn(ctx, rv, pv, nv, bv, pa):
        res = f(*[v.astype(F32) for v in rv], *bv, *pa)
        return list(res), [], []
    return _rowwise(fn, name, seq=seq, tm=tm, rows=rows, bvecs=bvecs, params=params, out_rows=outs)


def _rw_bwd(f, name, *, seq, tm, rows, bvecs=(), params=(), cts, need_rows, extra=None, dtypes=None):
    nr, nb, npar = len(rows), len(bvecs), len(params)
    all_rows = list(rows) + list(cts) + ([extra] if extra is not None else [])

    def fn(ctx, rv, pv, nv, bv, pa):
        prim = [v.astype(F32) for v in rv[:nr]]
        ct = tuple(v.astype(F32) for v in rv[nr:nr + len(cts)])
        _, vjp = jax.vjp(f, *prim, *bv, *pa)
        g = vjp(ct)
        d_rows = [g[j] for j in need_rows]
        if extra is not None:
            d_rows[0] = d_rows[0] + rv[-1].astype(F32)
        return d_rows, list(g[nr:nr + nb]), list(g[nr + nb:])

    return _rowwise(
        fn, name, seq=seq, tm=tm, rows=all_rows, bvecs=bvecs, params=params,
        out_rows=[(rows[j].shape[1], F32 if dtypes is None else dtypes[i]) for i, j in enumerate(need_rows)],
        out_bacc=[b.shape[1:] for b in bvecs], out_pacc=[p.shape for p in params])


def _small(fn, name, arrays, out_shapes):
    def body(*refs):
        res = fn(*[r[...] for r in refs[:len(arrays)]])
        for ref, val in zip(refs[len(arrays):], res):
            ref[...] = val.astype(ref.dtype)

    vm = pl.BlockSpec(memory_space=pltpu.VMEM)
    res = pl.pallas_call(
        body, name=name,
        out_shape=[jax.ShapeDtypeStruct(s, F32) for s in out_shapes],
        in_specs=[vm] * len(arrays), out_specs=[vm] * len(out_shapes),
        compiler_params=_params(),
    )(*arrays)
    return list(res)


def _elementwise(fn, name, arrays, n_out):
    shape = arrays[0].shape
    size = arrays[0].size
    if len(shape) >= 2 and shape[-1] % LANES == 0 and (size // shape[-1]) % 8 == 0:
        view = (size // shape[-1], shape[-1])
    elif size % ROW_W == 0 and (size // ROW_W) % 8 == 0:
        view = (size // ROW_W, ROW_W)
    else:
        view = (1, size) if len(shape) < 2 else (size // shape[-1], shape[-1])
    rows = view[0]
    tr = rows
    for cand in (256, 128, 64, 32, 16, 8):
        if rows > cand and rows % cand == 0:
            tr = cand
            break

    def body(*refs):
        res = fn(*[r[...] for r in refs[:len(arrays)]])
        for ref, val in zip(refs[len(arrays):], res):
            ref[...] = val

    spec = pl.BlockSpec((tr, view[1]), lambda i: (i, 0))
    res = pl.pallas_call(
        body, name=name,
        out_shape=[jax.ShapeDtypeStruct(view, F32)] * n_out,
        grid=(rows // tr,), in_specs=[spec] * len(arrays), out_specs=[spec] * n_out,
        compiler_params=_params(("parallel",)),
    )(*[a.reshape(view) for a in arrays])
    return [r.reshape(shape) for r in res]


def _rms(x):
    return x * lax.rsqrt(jnp.mean(x * x, axis=-1, keepdims=True) + RMS_EPS)


def _rmsmod(x, sh, sc):
    return _rms(x) * (1.0 + sc) + sh


def _f_norm1(x, mod):
    return (_rmsmod(x, mod[0:1], mod[1:2]),)


def _f_sgu_pre(uvp, ln_g, ln_b):
    uv = 0.5 * uvp * (1.0 + lax.erf(uvp * (2.0 ** -0.5)))
    u = uv[:, :D]
    v = uv[:, D:]
    mu = jnp.mean(v, axis=-1, keepdims=True)
    vc = v - mu
    var = jnp.mean(vc * vc, axis=-1, keepdims=True)
    return u, vc * lax.rsqrt(var + LN_EPS) * ln_g + ln_b


def _f_res_norm2(x, mix, mod):
    x1 = x + mod[2:3] * mix
    return x1, _rmsmod(x1, mod[3:4], mod[4:5])


def _f_res2(x1, ff, mod):
    return (x1 + mod[5:6] * ff,)


def _f_loss(x, tgt, fg):
    err = _rms(x) * fg - tgt
    return 0.5 * jnp.sum(jnp.mean(err * err, axis=-1))


def _shift_mix(ctx, x, xprev8, mod, mu):
    h = _rmsmod(x, mod[0:1], mod[1:2])
    hprev = _rmsmod(xprev8, mod[0:1], mod[1:2])[7:8]
    hprev = jnp.where(ctx.first, jnp.zeros_like(hprev), hprev)
    rowid = lax.broadcasted_iota(jnp.int32, h.shape, 0)
    hp = jnp.where(rowid == 0, hprev, pltpu.roll(h, 1, 0))
    xx = hp - h
    return h, xx, [h + xx * mu[j:j + 1] for j in range(6)]


def _split_bf16(t, parts):
    out, rest = [], t.astype(F32)
    for _ in range(parts):
        piece = rest.astype(BF16)
        out.append(piece)
        rest = rest - piece.astype(F32)
    return out


def _make_mm(na, nb, ct_pieces=1, saved_pieces=1):
    def raw(a, b, pa, pb):
        if pa == 0:
            return jnp.dot(a, b, precision=HIGHEST, preferred_element_type=F32)
        acc = None
        bs = _split_bf16(b, pb)
        for i, ai in enumerate(_split_bf16(a, pa)):
            for j, bj in enumerate(bs):
                if i + j < max(pa, pb):
                    term = jnp.dot(ai, bj, preferred_element_type=F32)
                    acc = term if acc is None else acc + term
        return acc

    @jax.custom_vjp
    def mm(a, b):
        return raw(a, b, na, nb)

    def fwd(a, b):
        return raw(a, b, na, nb), (a, b)

    def bwd(res, ct):
        a, b = res
        if na == 0:
            return raw(ct, b.T, 0, 0), raw(a.T, ct, 0, 0)
        return raw(ct, b.T, ct_pieces, saved_pieces), raw(a.T, ct, saved_pieces, ct_pieces)

    mm.defvjp(fwd, bwd)
    return mm


class _WkvMms:
    def __init__(self, head_sum, cum, score, square, apply, out, state):
        self.head_sum, self.cum, self.score = head_sum, cum, score
        self.square, self.apply, self.out, self.state = square, apply, out, state


def _wkv_mms(cfg):
    table = {"x": (0, 0), "1": (1, 1), "2": (2, 2), "3": (3, 3), "a": (2, 1), "b": (1, 2)}
    hs, cu, sc_, sq, ap, ou, st = [table[ch] for ch in cfg]
    return _WkvMms(_make_mm(hs[0], 1) if hs[0] else _make_mm(0, 0),
                   _make_mm(1, cu[1], ct_pieces=2) if cu[0] else _make_mm(0, 0),
                   _make_mm(*sc_, saved_pieces=2), _make_mm(*sq), _make_mm(*ap, saved_pieces=2),
                   _make_mm(*ou), _make_mm(*st))


WKV_PRECISION = "1221b11"


SGU_CHUNKS_PER_STEP = 4


def _sgu_tile(mm, u, vn, ws, bias):
    row = lax.broadcasted_iota(jnp.int32, (CHUNK, CHUNK), 0)
    col = lax.broadcasted_iota(jnp.int32, (CHUNK, CHUNK), 1)
    wm = [jnp.where(col <= row, w, 0.0) for w in ws]
    out_rows = []
    for ch in range(u.shape[0] // CHUNK):
        rs = slice(ch * CHUNK, (ch + 1) * CHUNK)
        out_rows.append(jnp.concatenate(
            [mm(wm[g], vn[rs, g * LANES:(g + 1) * LANES]) + bias[g] for g in range(GROUPS)], axis=1))
    return u * jnp.concatenate(out_rows, axis=0)


def _sgu_mixer_tile(mm, uvp, ln_g, ln_b, ws, bias):
    u, vn = _f_sgu_pre(uvp, ln_g, ln_b)
    return _sgu_tile(mm, u, vn, ws, bias)


def _sgu_fwd(uvp, ln_g, ln_b, ws, bias):
    n = uvp.shape[0]
    rows = CHUNK * SGU_CHUNKS_PER_STEP
    mm = _make_mm(1, 1)

    def body(x_ref, g_ref, b2_ref, w_ref, b_ref, z_ref):
        ws_l = [w_ref[g] for g in range(GROUPS)]
        bias_l = [b_ref[g] for g in range(GROUPS)]
        z_ref[...] = _sgu_mixer_tile(mm, x_ref[...], g_ref[...], b2_ref[...], ws_l, bias_l).astype(z_ref.dtype)

    tok_in = pl.BlockSpec((rows, 2 * D), lambda i: (i, 0))
    tok = pl.BlockSpec((rows, D), lambda i: (i, 0))
    vec = pl.BlockSpec((1, D), lambda i: (0, 0))
    grp = pl.BlockSpec((GROUPS, CHUNK, LANES), lambda i: (0, 0, 0))
    return pl.pallas_call(
        body, name="sgu_fwd", out_shape=jax.ShapeDtypeStruct((n, D), BF16),
        grid=(n // rows,), in_specs=[tok_in, vec, vec, grp, grp], out_specs=tok,
        compiler_params=_params(("parallel",)),
    )(uvp, ln_g, ln_b, ws, bias)


def _sgu_bwd(uvp, ln_g, ln_b, ws, bias, dz):
    n = uvp.shape[0]
    rows = CHUNK * SGU_CHUNKS_PER_STEP
    mm = _make_mm(1, 1)

    def body(x_ref, g_ref, b2_ref, w_ref, b_ref, dz_ref, dx_ref, dg_ref, db2_ref, dw_ref, db_ref):
        i = pl.program_id(0)
        ws_l = [w_ref[g] for g in range(GROUPS)]
        bias_l = [b_ref[g] for g in range(GROUPS)]
        _, vjp = jax.vjp(functools.partial(_sgu_mixer_tile, mm), x_ref[...], g_ref[...], b2_ref[...], ws_l, bias_l)
        dx, dg, db2, dw, db = vjp(dz_ref[...].astype(F32))
        dx_ref[...] = dx.astype(dx_ref.dtype)

        @pl.when(i == 0)
        def _():
            dg_ref[...] = dg
            db2_ref[...] = db2
            for g in range(GROUPS):
                dw_ref[g] = dw[g]
                db_ref[g] = db[g]

        @pl.when(i != 0)
        def _():
            dg_ref[...] += dg
            db2_ref[...] += db2
            for g in range(GROUPS):
                dw_ref[g] += dw[g]
                db_ref[g] += db[g]

    tok_in = pl.BlockSpec((rows, 2 * D), lambda i: (i, 0))
    tok = pl.BlockSpec((rows, D), lambda i: (i, 0))
    vec = pl.BlockSpec((1, D), lambda i: (0, 0))
    grp = pl.BlockSpec((GROUPS, CHUNK, LANES), lambda i: (0, 0, 0))
    return pl.pallas_call(
        body, name="sgu_bwd",
        out_shape=[jax.ShapeDtypeStruct((n, 2 * D), BF16), jax.ShapeDtypeStruct((1, D), F32),
                   jax.ShapeDtypeStruct((1, D), F32),
                   jax.ShapeDtypeStruct((GROUPS, CHUNK, LANES), F32),
                   jax.ShapeDtypeStruct((GROUPS, CHUNK, LANES), F32)],
        grid=(n // rows,), in_specs=[tok_in, vec, vec, grp, grp, tok],
        out_specs=[tok_in, vec, vec, grp, grp],
        compiler_params=_params(("arbitrary",)),
    )(uvp, ln_g, ln_b, ws, bias, dz)


def _chains(t):
    return [t[i] for i in range(t.shape[0])] if t.ndim == 3 else [t]


def _bmm(mm, a, b):
    if a.ndim == 2 and b.ndim == 2:
        return mm(a, b)
    ca, cb = _chains(a), _chains(b)
    n = max(len(ca), len(cb))
    return jnp.stack([mm(ca[i % len(ca)], cb[i % len(cb)]) for i in range(n)])


def _bt(a):
    return a.T if a.ndim == 2 else jnp.stack([t.T for t in _chains(a)])


def _make_solver(mms, given):
    def masks():
        lane = lax.broadcasted_iota(jnp.int32, (1, LANES), 1)
        m_a = (lane < HEAD).astype(F32)
        return m_a, 1.0 - m_a

    def doubling(pa, pb, x):
        xa = x + _bmm(mms.apply, pa, x)
        xb = x + _bmm(mms.apply, pb, x)
        for _ in range(6):
            pa = _bmm(mms.square, pa, pa)
            pb = _bmm(mms.square, pb, pb)
            xa = xa + _bmm(mms.apply, pa, xa)
            xb = xb + _bmm(mms.apply, pb, xb)
        m_a, m_b = masks()
        return m_a * xa + m_b * xb

    @jax.custom_vjp
    def solve(ab_a, ab_b, rhs, hint):
        return hint if given else doubling(ab_a, ab_b, rhs)

    def fwd(ab_a, ab_b, rhs, hint):
        u = hint if given else doubling(ab_a, ab_b, rhs)
        return u, (ab_a, ab_b, u)

    def bwd(res, g):
        ab_a, ab_b, u = res
        m_a, m_b = masks()
        h = doubling(_bt(ab_a), _bt(ab_b), g)
        ut = _bt(u)
        return _bmm(mms.apply, h * m_a, ut), _bmm(mms.apply, h * m_b, ut), h, jnp.zeros_like(u)

    solve.defvjp(fwd, bwd)
    return solve


def _wkv_chunk(mms, s0, r, k, v, wl, al, g, w0, a0, k_k, k_a, r_k, ln_g, ln_b, u_hint=None):
    ln = CHUNK
    row = lax.broadcasted_iota(jnp.int32, (ln, ln), 0)
    col = lax.broadcasted_iota(jnp.int32, (ln, ln), 1)
    incl = (col <= row).astype(F32)
    strict = (col < row).astype(F32)
    same_head = ((row // HEAD) == (col // HEAD)).astype(F32)
    lane = lax.broadcasted_iota(jnp.int32, (1, LANES), 1)
    m_a = (lane < HEAD).astype(F32)
    m_b = 1.0 - m_a
    rowid = lax.broadcasted_iota(jnp.int32, (ln, LANES), 0)
    cat = jnp.concatenate

    def hsum(t):
        return _bmm(mms.head_sum, t, same_head)

    def pick_row(t, j):
        return jnp.sum(jnp.where(rowid == j, t, 0.0), axis=-2, keepdims=True)

    z = w0 + wl
    softplus_neg = jnp.maximum(-z, 0.0) + jnp.log(1.0 + jnp.exp(-jnp.abs(z)))
    lw = -jnp.exp(-softplus_neg - 0.5)
    a = 1.0 / (1.0 + jnp.exp(-(a0 + al)))
    kx = k * k_k
    kkn = kx / jnp.maximum(jnp.sqrt(hsum(kx * kx)), L2_EPS)
    kp = k * (1.0 + (a - 1.0) * k_a)
    aa = -kkn
    bb = kkn * a

    c = _bmm(mms.cum, incl, lw)
    c_mid = pick_row(c, ln // 2 - 1)
    ce = c - c_mid
    e_pos = jnp.exp(ce)
    e_neg = jnp.exp(-ce)
    at = aa * jnp.exp(ce - lw)
    bt = bb * e_neg
    kt = kp * e_neg
    rt = r * e_pos
    s0p = s0 * jnp.exp(c_mid)

    bk = cat([bt, kt], axis=-2)
    sc = _bmm(mms.score, cat([at * m_a, at * m_b, rt * m_a, rt * m_b], axis=-2), _bt(bk))
    ab_a, ak_a = sc[..., 0:ln, 0:ln] * strict, sc[..., 0:ln, ln:] * strict
    ab_b, ak_b = sc[..., ln:2 * ln, 0:ln] * strict, sc[..., ln:2 * ln, ln:] * strict
    incl2 = cat([incl, incl], axis=1)
    p_a = sc[..., 2 * ln:3 * ln, :] * incl2
    p_b = sc[..., 3 * ln:, :] * incl2

    base = _bmm(mms.score, cat([at, rt], axis=-2), _bt(s0p))
    rhs = base[..., :ln, :] + m_a * _bmm(mms.out, ak_a, v) + m_b * _bmm(mms.out, ak_b, v)

    u = _make_solver(mms, u_hint is not None)(ab_a, ab_b, rhs, rhs if u_hint is None else u_hint)
    uv = cat([u, v], axis=-2)
    y = base[..., ln:, :] + m_a * _bmm(mms.out, p_a, uv) + m_b * _bmm(mms.out, p_b, uv)
    s_new = (s0p + _bmm(mms.state, _bt(uv), bk)) * same_head * jnp.exp(pick_row(ce, ln - 1))

    mean = hsum(y) * (1.0 / HEAD)
    yc = y - mean
    var = hsum(yc * yc) * (1.0 / HEAD)
    yn = yc * lax.rsqrt(var + GN_EPS) * ln_g + ln_b
    bonus = hsum(r * kp * r_k) * v
    return ((yn + bonus) * g, s_new), u


N_WKV_ROWS = 6
N_WKV_PAR = 7


WKV_PAIRS_PER_STEP = 4


def _to_chains(val, nbat, pp):
    if val.ndim == 2:
        return jnp.stack([val[:, q * LANES:(q + 1) * LANES] for _ in range(nbat) for q in range(pp)])
    return jnp.stack([val[b, :, q * LANES:(q + 1) * LANES] for b in range(nbat) for q in range(pp)])


def _wkv_fwd(seq, rows, pars):
    n = rows[0].shape[0]
    nbat, nch, npair, pp = n // seq, seq // CHUNK, D // LANES, WKV_PAIRS_PER_STEP
    chunk_fn = functools.partial(_wkv_chunk, _wkv_mms(WKV_PRECISION))

    def body(*refs):
        row_vals = [_to_chains(r[...], nbat, pp) for r in refs[:N_WKV_ROWS]]
        par_vals = [_to_chains(r[...], nbat, pp) for r in refs[N_WKV_ROWS:N_WKV_ROWS + N_WKV_PAR]]
        yo_ref, ck_ref, u_ref, s_ref = refs[N_WKV_ROWS + N_WKV_PAR:]
        ch = pl.program_id(1)

        @pl.when(ch == 0)
        def _():
            s_ref[...] = jnp.zeros_like(s_ref)

        s0 = s_ref[...]
        (yo, s_new), u = chunk_fn(s0, *row_vals, *par_vals)
        s_ref[...] = s_new
        for b in range(nbat):
            for q in range(pp):
                ck_ref[b, q] = s0[b * pp + q]
                yo_ref[b, :, q * LANES:(q + 1) * LANES] = yo[b * pp + q].astype(yo_ref.dtype)
                u_ref[b, :, q * LANES:(q + 1) * LANES] = u[b * pp + q]

    tok = pl.BlockSpec((nbat, CHUNK, pp * LANES), lambda p, ch: (0, ch, p))
    par = pl.BlockSpec((1, pp * LANES), lambda p, ch: (0, p))
    ck = pl.BlockSpec((nbat, pp, None, LANES, LANES), lambda p, ch: (0, p, ch, 0, 0))
    yo, ckpt, u_all = pl.pallas_call(
        body, name="wkv_fwd",
        out_shape=[jax.ShapeDtypeStruct((nbat, seq, D), BF16),
                   jax.ShapeDtypeStruct((nbat, npair, nch, LANES, LANES), F32),
                   jax.ShapeDtypeStruct((nbat, seq, D), F32)],
        grid=(npair // pp, nch),
        in_specs=[tok] * N_WKV_ROWS + [par] * N_WKV_PAR, out_specs=[tok, ck, tok],
        scratch_shapes=[pltpu.VMEM((nbat * pp, LANES, LANES), F32)],
        compiler_params=_params(("parallel", "arbitrary")),
    )(*[t.reshape(nbat, seq, D) for t in rows], *pars)
    return yo.reshape(n, D), ckpt, u_all


def _wkv_bwd(seq, rows, pars, ckpt, u_all, dyo):
    n = rows[0].shape[0]
    nbat, nch, npair, pp = n // seq, seq // CHUNK, D // LANES, WKV_PAIRS_PER_STEP
    chunk_fn = functools.partial(_wkv_chunk, _wkv_mms(WKV_PRECISION))
    n_in = N_WKV_ROWS + N_WKV_PAR

    def body(*refs):
        row_vals = [_to_chains(r[...], nbat, pp) for r in refs[:N_WKV_ROWS]]
        par_vals = [_to_chains(r[...], nbat, pp) for r in refs[N_WKV_ROWS:n_in]]
        ck_ref, u_ref, dyo_ref = refs[n_in:n_in + 3]
        d_rows = refs[n_in + 3:n_in + 3 + N_WKV_ROWS]
        d_pars = refs[n_in + 3 + N_WKV_ROWS:n_in + 3 + N_WKV_ROWS + N_WKV_PAR]
        ds_ref = refs[-1]
        ch = pl.program_id(1)

        @pl.when(ch == 0)
        def _():
            ds_ref[...] = jnp.zeros_like(ds_ref)

        s0 = jnp.stack([ck_ref[b, q] for b in range(nbat) for q in range(pp)])
        dyo_v = _to_chains(dyo_ref[...].astype(F32), nbat, pp)
        u_hint = _to_chains(u_ref[...], nbat, pp)
        _, vjp, _ = jax.vjp(functools.partial(chunk_fn, u_hint=u_hint), s0, *row_vals, *par_vals, has_aux=True)
        grads = vjp((dyo_v, ds_ref[...]))
        ds_ref[...] = grads[0]
        for ref, val in zip(d_rows, grads[1:1 + N_WKV_ROWS]):
            for b in range(nbat):
                for q in range(pp):
                    ref[b, :, q * LANES:(q + 1) * LANES] = val[b * pp + q].astype(ref.dtype)
        for ref, val in zip(d_pars, grads[1 + N_WKV_ROWS:]):
            per_pair = [functools.reduce(lambda s, t: s + t, [val[b * pp + q] for b in range(nbat)])
                        for q in range(pp)]
            tot = jnp.concatenate(per_pair, axis=1)

            @pl.when(ch == 0)
            def _(ref=ref, tot=tot):
                ref[...] = tot

            @pl.when(ch != 0)
            def _(ref=ref, tot=tot):
                ref[...] += tot

    tok = pl.BlockSpec((nbat, CHUNK, pp * LANES), lambda p, ch: (0, nch - 1 - ch, p))
    par = pl.BlockSpec((1, pp * LANES), lambda p, ch: (0, p))
    ck = pl.BlockSpec((nbat, pp, None, LANES, LANES), lambda p, ch: (0, p, nch - 1 - ch, 0, 0))
    res = pl.pallas_call(
        body, name="wkv_bwd",
        out_shape=[jax.ShapeDtypeStruct((nbat, seq, D), BF16)] * N_WKV_ROWS
        + [jax.ShapeDtypeStruct((1, D), F32)] * N_WKV_PAR,
        grid=(npair // pp, nch),
        in_specs=[tok] * N_WKV_ROWS + [par] * N_WKV_PAR + [ck, tok, tok],
        out_specs=[tok] * N_WKV_ROWS + [par] * N_WKV_PAR,
        scratch_shapes=[pltpu.VMEM((nbat * pp, LANES, LANES), F32)],
        compiler_params=_params(("parallel", "arbitrary")),
    )(*[t.reshape(nbat, seq, D) for t in rows], *pars, ckpt, u_all, dyo.reshape(nbat, seq, D))
    return [t.reshape(n, D) for t in res[:N_WKV_ROWS]] + list(res[N_WKV_ROWS:])


def _place():
    return lax.axis_index("x"), lax.axis_index("y"), lax.axis_index("c")


def _all_gather8(blk):
    m_per, n = blk.shape
    assert m_per % 8 == 0

    def body(x_ref, out_ref, send_sems, recv_sems, local_sem):
        x, y, c = _place()
        me, sibling = (x, y, c), (x, y, 1 - c)
        chips = [(1 - x, y), (x, 1 - y), (1 - x, 1 - y)]

        def rows(px, py, pc):
            return out_ref.at[pl.ds((4 * px + 2 * py + pc) * m_per, m_per), :]

        def copy(k, block, to, src=None):
            return pltpu.make_async_remote_copy(
                src_ref=rows(*block) if src is None else src, dst_ref=rows(*block),
                send_sem=send_sems.at[k], recv_sem=recv_sems.at[k],
                device_id=to, device_id_type=MESH)

        mine = pltpu.make_async_copy(x_ref, rows(*me), local_sem)
        mine.start()
        first = [copy(0, me, sibling, src=x_ref)]
        first += [copy(1 + j, me, (*chip, c), src=x_ref) for j, chip in enumerate(chips)]
        for cp in first:
            cp.start()
        passed = [copy(4 + j, (*chip, c), sibling) for j, chip in enumerate(chips)]
        for j, chip in enumerate(chips):
            copy(1 + j, (*chip, c), me).wait_recv()
            passed[j].start()
        copy(0, sibling, me).wait_recv()
        for j, chip in enumerate(chips):
            copy(4 + j, (*chip, 1 - c), me).wait_recv()
        for cp in first + passed:
            cp.wait_send()
        mine.wait()

    vm = pl.BlockSpec(memory_space=pltpu.VMEM)
    return pl.pallas_call(
        body, name="all_gather8_%dx%d" % (m_per, n),
        out_shape=jax.ShapeDtypeStruct((N_DEV * m_per, n), blk.dtype),
        in_specs=[vm], out_specs=vm,
        scratch_shapes=[pltpu.SemaphoreType.DMA((7,)), pltpu.SemaphoreType.DMA((7,)),
                        pltpu.SemaphoreType.DMA],
        compiler_params=_params(),
    )(blk)


def _own_slot(src, name):
    r, w = src.shape[-2:]
    tr = _tile(r, 1008)
    xi, yi, _ = _place()
    chip = jnp.reshape(2 * xi + yi, (1,)).astype(jnp.int32)

    def body(chip_ref, x_ref, o_ref):
        o_ref[...] = x_ref[...]

    if src.ndim == 2:
        in_spec = pl.BlockSpec((tr, w), lambda i, chip_ref: (i, 0))
    else:
        in_spec = pl.BlockSpec((None, tr, w), lambda i, chip_ref: (chip_ref[0], i, 0))
    return pl.pallas_call(
        body, name=name,
        out_shape=jax.ShapeDtypeStruct((N_CHIPS, r, w), src.dtype),
        grid_spec=pltpu.PrefetchScalarGridSpec(
            num_scalar_prefetch=1, grid=(r // tr,), in_specs=[in_spec],
            out_specs=pl.BlockSpec((None, tr, w), lambda i, chip_ref: (chip_ref[0], i, 0))),
        compiler_params=_params(("parallel",)),
    )(chip, src)


def _chip_all_gather(shard):
    r, w = shard.shape
    half = r // 2
    assert r % 2 == 0 and half % 16 == 0

    def body(x_ref, buf_ref, out_ref, send_sems, recv_sems):
        del buf_ref
        x, y, c = _place()
        sibling = (x, y, 1 - c)
        me_p = 2 * x + y
        chips = [(1 - x, y), (x, 1 - y), (1 - x, 1 - y)]

        def piece(p, h):
            return out_ref.at[p, pl.ds(h * half, half), :]

        def copy(k, p, h, to, src=None):
            return pltpu.make_async_remote_copy(
                src_ref=piece(p, h) if src is None else src, dst_ref=piece(p, h),
                send_sem=send_sems.at[k], recv_sem=recv_sems.at[k],
                device_id=to, device_id_type=MESH)

        my_half = x_ref.at[pl.ds(c * half, half), :]
        first = [copy(j, me_p, c, (*chip, c), src=my_half) for j, chip in enumerate(chips)]
        for cp in first:
            cp.start()
        passed = [copy(3 + j, 2 * chip[0] + chip[1], c, sibling) for j, chip in enumerate(chips)]
        for j, chip in enumerate(chips):
            copy(j, 2 * chip[0] + chip[1], c, sibling).wait_recv()
            passed[j].start()
        for j, chip in enumerate(chips):
            copy(3 + j, 2 * chip[0] + chip[1], 1 - c, sibling).wait_recv()
        for cp in first + passed:
            cp.wait_send()

    hbm = pl.BlockSpec(memory_space=pl.ANY)
    return pl.pallas_call(
        body, name="chip_all_gather",
        out_shape=jax.ShapeDtypeStruct((N_CHIPS, r, w), shard.dtype),
        in_specs=[hbm, hbm], out_specs=hbm, input_output_aliases={1: 0},
        scratch_shapes=[pltpu.SemaphoreType.DMA((6,)), pltpu.SemaphoreType.DMA((6,))],
        compiler_params=_params(),
    )(shard, _own_slot(shard, "gather_own_slot"))


def _sibling_swap_halves(g):
    _, r, w = g.shape
    half = r // 2

    def body(g_ref, t_ref, send_sem, recv_sem):
        x, y, c = _place()
        sibling = (x, y, 1 - c)
        cp = pltpu.make_async_remote_copy(
            src_ref=g_ref.at[:, pl.ds((1 - c) * half, half), :], dst_ref=t_ref,
            send_sem=send_sem, recv_sem=recv_sem, device_id=sibling, device_id_type=MESH)
        cp.start()
        cp.wait()

    hbm = pl.BlockSpec(memory_space=pl.ANY)
    return pl.pallas_call(
        body, name="rs_sibling_halves",
        out_shape=jax.ShapeDtypeStruct((N_CHIPS, half, w), g.dtype),
        in_specs=[hbm], out_specs=hbm,
        scratch_shapes=[pltpu.SemaphoreType.DMA, pltpu.SemaphoreType.DMA],
        compiler_params=_params(),
    )(g)


def _add_own_half(g, t):
    _, r, w = g.shape
    half = r // 2
    tr = 1008 if half % 1008 == 0 else 16
    assert half % tr == 0
    cidx = jnp.reshape(lax.axis_index("c"), (1,)).astype(jnp.int32)

    def body(c_ref, g_ref, t_ref, o_ref):
        o_ref[...] = (g_ref[...] + t_ref[...]).astype(o_ref.dtype)

    return pl.pallas_call(
        body, name="rs_add_halves",
        out_shape=jax.ShapeDtypeStruct((N_CHIPS, half, w), BF16),
        grid_spec=pltpu.PrefetchScalarGridSpec(
            num_scalar_prefetch=1, grid=(N_CHIPS, half // tr),
            in_specs=[pl.BlockSpec((None, None, tr, w), lambda p, i, c_ref: (p, c_ref[0], i, 0)),
                      pl.BlockSpec((None, tr, w), lambda p, i, c_ref: (p, i, 0))],
            out_specs=pl.BlockSpec((None, tr, w), lambda p, i, c_ref: (p, i, 0))),
        compiler_params=_params(("parallel", "parallel")),
    )(cidx, g.reshape(N_CHIPS, 2, half, w), t)


def _chip_exchange(h):
    _, hh, w = h.shape

    def body(h_ref, buf_ref, t_ref, send_sems, recv_sems):
        del buf_ref
        x, y, c = _place()
        me_p = 2 * x + y
        chips = [(1 - x, y), (x, 1 - y), (1 - x, 1 - y)]
        cps = []
        for j, chip in enumerate(chips):
            q = 2 * chip[0] + chip[1]
            cps.append(pltpu.make_async_remote_copy(
                src_ref=h_ref.at[q], dst_ref=t_ref.at[me_p],
                send_sem=send_sems.at[j], recv_sem=recv_sems.at[j],
                device_id=(*chip, c), device_id_type=MESH))
        for cp in cps:
            cp.start()
        for j, chip in enumerate(chips):
            q = 2 * chip[0] + chip[1]
            pltpu.make_async_remote_copy(
                src_ref=h_ref.at[q], dst_ref=t_ref.at[q],
                send_sem=send_sems.at[j], recv_sem=recv_sems.at[j],
                device_id=(*chip, c), device_id_type=MESH).wait_recv()
        for cp in cps:
            cp.wait_send()

    hbm = pl.BlockSpec(memory_space=pl.ANY)
    return pl.pallas_call(
        body, name="rs_chip_exchange",
        out_shape=jax.ShapeDtypeStruct(h.shape, h.dtype),
        in_specs=[hbm, hbm], out_specs=hbm, input_output_aliases={1: 0},
        scratch_shapes=[pltpu.SemaphoreType.DMA((3,)), pltpu.SemaphoreType.DMA((3,))],
        compiler_params=_params(),
    )(h, _own_slot(h, "rs_own_slot"))


def _sum_slots(t):
    _, hh, w = t.shape
    tr = 1008 if hh % 1008 == 0 else 16
    assert hh % tr == 0
    nblk = hh // tr
    cidx = jnp.reshape(lax.axis_index("c"), (1,)).astype(jnp.int32)

    def body(c_ref, t_ref, o_ref):
        s0, s1, s2, s3 = [t_ref[j].astype(F32) for j in range(N_CHIPS)]
        o_ref[...] = ((s0 + s1) + s2) + s3

    return pl.pallas_call(
        body, name="rs_sum_slots", out_shape=jax.ShapeDtypeStruct((2 * hh, w), F32),
        grid_spec=pltpu.PrefetchScalarGridSpec(
            num_scalar_prefetch=1, grid=(nblk,),
            in_specs=[pl.BlockSpec((N_CHIPS, tr, w), lambda i, c_ref: (0, i, 0))],
            out_specs=pl.BlockSpec((tr, w), lambda i, c_ref: (c_ref[0] * nblk + i, 0))),
        compiler_params=_params(("parallel",)),
    )(cidx, t)


def _sibling_join_halves(s):
    h2, w = s.shape
    hh = h2 // 2

    def body(s_ref, o_ref, send_sem, recv_sem):
        del s_ref
        x, y, c = _place()
        sibling = (x, y, 1 - c)
        cp = pltpu.make_async_remote_copy(
            src_ref=o_ref.at[pl.ds(c * hh, hh), :], dst_ref=o_ref.at[pl.ds(c * hh, hh), :],
            send_sem=send_sem, recv_sem=recv_sem, device_id=sibling, device_id_type=MESH)
        cp.start()
        pltpu.make_async_remote_copy(
            src_ref=o_ref.at[pl.ds((1 - c) * hh, hh), :], dst_ref=o_ref.at[pl.ds((1 - c) * hh, hh), :],
            send_sem=send_sem, recv_sem=recv_sem, device_id=sibling, device_id_type=MESH).wait_recv()
        cp.wait_send()

    hbm = pl.BlockSpec(memory_space=pl.ANY)
    return pl.pallas_call(
        body, name="rs_sibling_join",
        out_shape=jax.ShapeDtypeStruct(s.shape, s.dtype),
        in_specs=[hbm], out_specs=hbm, input_output_aliases={0: 0},
        scratch_shapes=[pltpu.SemaphoreType.DMA, pltpu.SemaphoreType.DMA],
        compiler_params=_params(),
    )(s)


def _reduce_scatter(g):
    h = _add_own_half(g, _sibling_swap_halves(g))
    return _sibling_join_halves(_sum_slots(_chip_exchange(h)))


def _unshard_cols(piece):
    p, k, n = piece.shape
    return jnp.transpose(piece, (1, 0, 2)).reshape(k, p * n)


def _shard_cols(full):
    k, n4 = full.shape
    return jnp.transpose(full.reshape(k, N_CHIPS, n4 // N_CHIPS), (1, 0, 2))


def _rows_of(piece):
    return piece.reshape(N_CHIPS, -1, ROW_W)


def _pad_rows(a, mult):
    pad = (-a.shape[-2]) % mult
    if pad == 0:
        return a
    widths = [(0, 0)] * (a.ndim - 2) + [(0, pad), (0, 0)]
    return jnp.pad(a, widths)


def _adamw(w, g, m, v):
    m2 = ADAM_B1 * m + (1.0 - ADAM_B1) * g
    v2 = ADAM_B2 * v + (1.0 - ADAM_B2) * (g * g)
    m_hat = m2 / (1.0 - ADAM_B1 ** ADAM_STEP)
    v_hat = v2 / (1.0 - ADAM_B2 ** ADAM_STEP)
    delta = -ADAM_LR * (m_hat / (jnp.sqrt(v_hat) + ADAM_EPS) + ADAM_WD * w)
    return delta, m2, v2


def _mlp_fwd(seq, tm, x_in, mix, mod, w1, w2, tag, residual=True):
    x1, h2 = _rw_fwd(_f_res_norm2, "res_norm2_" + tag, seq=seq, tm=tm, rows=[x_in, mix], bvecs=[mod],
                     outs=[(D, F32), (D, BF16)])
    def relu_and_square(acc):
        r = jnp.maximum(acc, 0.0)
        return r, r * r

    p, f = _matmul_ep(h2, w1, [], relu_and_square, [BF16, BF16], mode="nn", name="mlp_up_" + tag)
    ff = _matmul(f, w2, mode="nn", name="mlp_down_" + tag)
    x2 = None
    if residual:
        (x2,) = _rw_fwd(_f_res2, "res2_" + tag, seq=seq, tm=tm, rows=[x1, ff], bvecs=[mod], outs=[(D, F32)])
    return x2, (x1, h2, p, f, ff)


def _mlp_bwd(seq, tm, saved, x_in, mix, mod, w1, w2, dx2, tag, res2_grads=None):
    x1, h2, p, f, ff = saved
    if res2_grads is None:
        dff, dmod_a = _rw_bwd(_f_res2, "res2_bwd_" + tag, seq=seq, tm=tm, rows=[x1, ff], bvecs=[mod],
                              cts=[dx2], need_rows=[1], dtypes=[BF16])
    else:
        dff, dmod_a = res2_grads
    (dp,) = _matmul_ep(dff, w2, [p], lambda acc, pt: (2.0 * pt.astype(F32) * acc,), [BF16],
                       mode="nt", name="mlp_down_dx_" + tag)
    dw2 = _matmul(f, dff, mode="tn", name="mlp_down_dw_" + tag)
    dh2 = _matmul(dp, w1, mode="nt", name="mlp_up_dx_" + tag)
    dw1 = _matmul(h2, dp, mode="tn", name="mlp_up_dw_" + tag, out_shards=N_CHIPS)
    dx_in, dmix, dmod_b = _rw_bwd(_f_res_norm2, "res_norm2_bwd_" + tag, seq=seq, tm=tm, rows=[x_in, mix],
                                  bvecs=[mod], cts=[dx2, dh2], need_rows=[0, 1], dtypes=[F32, BF16])
    return dx_in, dmix, dmod_a + dmod_b, dw1, dw2


def kernel(x, c, ada_w, ada_b, mlp_w1, mlp_w2, a_w_in, a_ln_g, a_ln_b, a_w_s, a_b_s, a_w_out, b_mu, b_w_in, b_w0, b_w1, b_w2, b_a0, b_a1, b_a2, b_g1, b_g2, b_k_k, b_k_a, b_r_k, b_ln_g, b_ln_b, b_w_out, final_g, loss_target, m_ada_w, m_ada_b, m_mlp_w1, m_mlp_w2, m_a_w_in, m_a_ln_g, m_a_ln_b, m_a_w_s, m_a_b_s, m_a_w_out, m_b_mu, m_b_w_in, m_b_w0, m_b_w1, m_b_w2, m_b_a0, m_b_a1, m_b_a2, m_b_g1, m_b_g2, m_b_k_k, m_b_k_a, m_b_r_k, m_b_ln_g, m_b_ln_b, m_b_w_out, m_final_g, v_ada_w, v_ada_b, v_mlp_w1, v_mlp_w2, v_a_w_in, v_a_ln_g, v_a_ln_b, v_a_w_s, v_a_b_s, v_a_w_out, v_b_mu, v_b_w_in, v_b_w0, v_b_w1, v_b_w2, v_b_a0, v_b_a1, v_b_a2, v_b_g1, v_b_g2, v_b_k_k, v_b_k_a, v_b_r_k, v_b_ln_g, v_b_ln_b, v_b_w_out, v_final_g):
    weights = dict(ada_w=ada_w, ada_b=ada_b, mlp_w1=mlp_w1, mlp_w2=mlp_w2, a_w_in=a_w_in, a_ln_g=a_ln_g,
                   a_ln_b=a_ln_b, a_w_s=a_w_s, a_b_s=a_b_s, a_w_out=a_w_out, b_mu=b_mu, b_w_in=b_w_in,
                   b_w0=b_w0, b_w1=b_w1, b_w2=b_w2, b_a0=b_a0, b_a1=b_a1, b_a2=b_a2, b_g1=b_g1, b_g2=b_g2,
                   b_k_k=b_k_k, b_k_a=b_k_a, b_r_k=b_r_k, b_ln_g=b_ln_g, b_ln_b=b_ln_b, b_w_out=b_w_out,
                   final_g=final_g)
    moms = dict(ada_w=(m_ada_w, v_ada_w), ada_b=(m_ada_b, v_ada_b), mlp_w1=(m_mlp_w1, v_mlp_w1),
                mlp_w2=(m_mlp_w2, v_mlp_w2), a_w_in=(m_a_w_in, v_a_w_in), a_ln_g=(m_a_ln_g, v_a_ln_g),
                a_ln_b=(m_a_ln_b, v_a_ln_b), a_w_s=(m_a_w_s, v_a_w_s), a_b_s=(m_a_b_s, v_a_b_s),
                a_w_out=(m_a_w_out, v_a_w_out), b_mu=(m_b_mu, v_b_mu), b_w_in=(m_b_w_in, v_b_w_in),
                b_w0=(m_b_w0, v_b_w0), b_w1=(m_b_w1, v_b_w1), b_w2=(m_b_w2, v_b_w2), b_a0=(m_b_a0, v_b_a0),
                b_a1=(m_b_a1, v_b_a1), b_a2=(m_b_a2, v_b_a2), b_g1=(m_b_g1, v_b_g1), b_g2=(m_b_g2, v_b_g2),
                b_k_k=(m_b_k_k, v_b_k_k), b_k_a=(m_b_k_a, v_b_k_a), b_r_k=(m_b_r_k, v_b_r_k),
                b_ln_g=(m_b_ln_g, v_b_ln_g), b_ln_b=(m_b_ln_b, v_b_ln_b), b_w_out=(m_b_w_out, v_b_w_out),
                final_g=(m_final_g, v_final_g))
    order = list(weights)

    nbat, seq, _ = x.shape
    n = nbat * seq
    tm = 512
    xi, yi, ci = _place()
    chip = 2 * xi + yi
    dev = 2 * chip + ci
    x0 = x.reshape(n, D)
    tgt = loss_target.reshape(n, D)
    lora_w, lora_g = b_w1.shape[-1], b_g1.shape[-1]
    lora_wp, lora_gp = LANES, 2 * LANES

    (cond,) = _small(lambda cc: (cc / (1.0 + jnp.exp(-cc)),), "silu_c", [c], [c.shape])
    vec_names = ["b_w0", "b_a0", "b_k_k", "b_k_a", "b_ln_g", "b_ln_b"]
    vec_shard = jnp.concatenate([b_mu[0]] + [weights[k] for k in vec_names], axis=0)
    n_vec = vec_shard.shape[0]
    vec_rows = vec_shard.reshape(-1, ROW_W)
    blk = _pad_rows(jnp.concatenate([cond, vec_rows], axis=0), 8)
    assert blk.shape[0] == 8
    gathered = _all_gather8(blk).reshape(N_DEV, 8, D)
    cond_all = gathered[:, :nbat].reshape(N_DEV * nbat, D)
    vec_all = gathered[0::2, nbat:nbat + vec_rows.shape[0]].reshape(N_CHIPS, n_vec, D // N_CHIPS)
    vec_full = jnp.transpose(vec_all, (1, 0, 2)).reshape(n_vec, D)
    mu_full = vec_full[0:6]
    w0_f, a0_f, kk_f, ka_f, lng_f, lnb_f = [vec_full[6 + j:7 + j] for j in range(6)]
    rk_f = b_r_k.reshape(1, D)

    n_ada = ada_w.shape[-1]
    parts = jnp.concatenate(
        [_matmul(cond_all, ada_w[i], mode="nn", name="ada_fwd_%d" % i) for i in range(2)], axis=1)
    parts_all = _all_gather8(parts).reshape(N_DEV, N_DEV * nbat, 2, n_ada)[0::2]
    mine = lax.dynamic_slice_in_dim(parts_all, dev * nbat, nbat, axis=1)
    mods = []
    for i in range(2):
        full = jnp.transpose(mine[:, :, i], (1, 0, 2)).reshape(nbat, N_MOD * D) + ada_b[i]
        mods.append(full.reshape(nbat, N_MOD, D))

    big = [("mlp_w1_0", mlp_w1[0]), ("mlp_w1_1", mlp_w1[1]), ("mlp_w2_0", mlp_w2[0]), ("mlp_w2_1", mlp_w2[1]),
           ("a_w_in", a_w_in[0]), ("a_w_out", a_w_out[0]), ("b_w_in", b_w_in[0]), ("b_w_out", b_w_out[0]),
           ("b_w1", b_w1[0]), ("b_w2", b_w2[0]), ("b_a1", b_a1[0]), ("b_a2", b_a2[0]),
           ("b_g1", b_g1[0]), ("b_g2", b_g2[0])]
    offs, pos = {}, 0
    for name, arr in big:
        rows_k = arr.size // ROW_W
        offs[name] = (pos, rows_k, arr.shape)
        pos += rows_k
    n_big_rows = pos
    wflat = _pad_rows(jnp.concatenate([arr.astype(BF16).reshape(-1, ROW_W) for _, arr in big], axis=0), 32)
    wg = _chip_all_gather(wflat)

    def gathered_piece(name):
        start, rows_k, shape = offs[name]
        return wg[:, start:start + rows_k].reshape((N_CHIPS,) + shape)

    def col_w(name):
        return _unshard_cols(gathered_piece(name))

    def row_w(name):
        piece = gathered_piece(name)
        return piece.reshape(N_CHIPS * piece.shape[1], piece.shape[2])

    w1_l = [col_w("mlp_w1_0"), col_w("mlp_w1_1")]
    w2_l = [row_w("mlp_w2_0"), row_w("mlp_w2_1")]
    a_win, a_wout = col_w("a_w_in"), row_w("a_w_out")
    b_win, b_wout = col_w("b_w_in"), row_w("b_w_out")
    w_r, w_k, w_v = b_win[:, :D], b_win[:, D:2 * D], b_win[:, 2 * D:]
    w1p = jnp.pad(row_w("b_w1"), ((0, 0), (0, lora_wp - lora_w)))
    a1p = jnp.pad(row_w("b_a1"), ((0, 0), (0, lora_wp - lora_w)))
    g1p = jnp.pad(row_w("b_g1"), ((0, 0), (0, lora_gp - lora_g)))
    w2p = jnp.pad(col_w("b_w2"), ((0, lora_wp - lora_w), (0, 0)))
    a2p = jnp.pad(col_w("b_a2"), ((0, lora_wp - lora_w), (0, 0)))
    g2p = jnp.pad(col_w("b_g2"), ((0, lora_gp - lora_g), (0, 0)))

    mod0, mod1 = mods
    (h_a,) = _rw_fwd(_f_norm1, "norm1_a", seq=seq, tm=tm, rows=[x0], bvecs=[mod0], outs=[(D, BF16)])
    uvp = _matmul(h_a, a_win, mode="nn", name="sgu_in")
    ws = a_w_s[0]
    bias = jnp.broadcast_to(a_b_s[0][:, :, None], (GROUPS, CHUNK, LANES))
    z = _sgu_fwd(uvp, a_ln_g, a_ln_b, ws, bias)
    mix0 = _matmul(z, a_wout, mode="nn", name="sgu_out")
    x2, saved0 = _mlp_fwd(seq, tm, x0, mix0, mod0, w1_l[0], w2_l[0], "0")

    def shift_fwd(ctx, rv, pv, nv, bv, pa):
        _, _, mixes = _shift_mix(ctx, rv[0], pv[0], bv[0], pa[0])
        return mixes, [], []

    xr, xw, xk, xv, xa, xg = _rowwise(shift_fwd, "shift_mix", seq=seq, tm=tm, rows=[x2], prev8=[x2],
                                      bvecs=[mod1], params=[mu_full], out_rows=[(D, BF16)] * 6)
    r = _matmul(xr, w_r, mode="nn", name="rwkv_r")
    k = _matmul(xk, w_k, mode="nn", name="rwkv_k")
    v = _matmul(xv, w_v, mode="nn", name="rwkv_v")
    def act_tanh(t):
        return jnp.tanh(t)

    def act_sigmoid(t):
        return 1.0 / (1.0 + jnp.exp(-t))

    t1, th = _matmul_ep(xw, w1p, [], lambda acc: (acc, act_tanh(acc)), [F32, BF16], mode="nn", name="lora_w1")
    t2 = _matmul(xa, a1p, mode="nn", name="lora_a1", out_dtype=BF16)
    t3, sg = _matmul_ep(xg, g1p, [], lambda acc: (acc, act_sigmoid(acc)), [F32, BF16], mode="nn", name="lora_g1")
    wl = _matmul(th, w2p, mode="nn", name="lora_w2")
    al = _matmul(t2, a2p, mode="nn", name="lora_a2")
    g = _matmul(sg, g2p, mode="nn", name="lora_g2")
    wkv_rows = [r, k, v, wl, al, g]
    wkv_pars = [w0_f, a0_f, kk_f, ka_f, rk_f, lng_f, lnb_f]
    yo, ckpt, wkv_u = _wkv_fwd(seq, wkv_rows, wkv_pars)
    mix1 = _matmul(yo, b_wout, mode="nn", name="rwkv_out")
    _, saved1 = _mlp_fwd(seq, tm, x2, mix1, mod1, w1_l[1], w2_l[1], "1", residual=False)

    def loss_fn(ctx, rv, pv, nv, bv, pa):
        def head(x1, ff, mod, fg):
            return _f_loss(_f_res2(x1, ff, mod)[0], rv[2], fg)
        val, (dx, dff, dmod, dfg) = jax.value_and_grad(head, argnums=(0, 1, 2, 3))(rv[0], rv[1], bv[0], pa[0])
        return [dx, dff], [dmod], [dfg, jnp.full((1, LANES), val, F32)]

    dx4, dff1, dmod1_a, d_final_g, loss_acc = _rowwise(
        loss_fn, "loss_head", seq=seq, tm=tm, rows=[saved1[0], saved1[4], tgt], bvecs=[mod1],
        params=[final_g.reshape(1, D)], out_rows=[(D, F32), (D, BF16)], out_bacc=[(N_MOD, D)],
        out_pacc=[(1, D), (1, LANES)])
    loss = lax.psum(loss_acc[0, 0], AXES)

    dx2_a, dmix1, dmod1, dw1_1, dw2_1 = _mlp_bwd(seq, tm, saved1, x2, mix1, mod1, w1_l[1], w2_l[1], dx4, "1",
                                                 res2_grads=(dff1, dmod1_a))
    dyo = _matmul(dmix1, b_wout, mode="nt", name="rwkv_out_dx")
    d_b_wout = _matmul(yo, dmix1, mode="tn", name="rwkv_out_dw")
    wkv_grads = _wkv_bwd(seq, wkv_rows, wkv_pars, ckpt, wkv_u, dyo)
    dr, dk, dv, dwl, dal, dg = wkv_grads[:N_WKV_ROWS]
    d_w0, d_a0, d_kk, d_ka, d_rk, d_lng, d_lnb = wkv_grads[N_WKV_ROWS:]
    def through(act):
        return lambda acc, t: (jax.vjp(act, t)[1](acc)[0],)

    (dt1,) = _matmul_ep(dwl, w2p, [t1], through(act_tanh), [BF16], mode="nt", name="lora_w2_dx")
    d_w2p = _matmul(th, dwl, mode="tn", name="lora_w2_dw")
    dt2 = _matmul(dal, a2p, mode="nt", name="lora_a2_dx", out_dtype=BF16)
    d_a2p = _matmul(t2, dal, mode="tn", name="lora_a2_dw")
    (dt3,) = _matmul_ep(dg, g2p, [t3], through(act_sigmoid), [BF16], mode="nt", name="lora_g2_dx")
    d_g2p = _matmul(sg, dg, mode="tn", name="lora_g2_dw")
    dxw = _matmul(dt1, w1p, mode="nt", name="lora_w1_dx")
    d_w1p = _matmul(xw, dt1, mode="tn", name="lora_w1_dw")
    dxa = _matmul(dt2, a1p, mode="nt", name="lora_a1_dx")
    d_a1p = _matmul(xa, dt2, mode="tn", name="lora_a1_dw")
    dxg = _matmul(dt3, g1p, mode="nt", name="lora_g1_dx")
    d_g1p = _matmul(xg, dt3, mode="tn", name="lora_g1_dw")
    dxr = _matmul(dr, w_r, mode="nt", name="rwkv_r_dx")
    dxk = _matmul(dk, w_k, mode="nt", name="rwkv_k_dx")
    dxv = _matmul(dv, w_v, mode="nt", name="rwkv_v_dx")
    d_b_win = jnp.concatenate([_matmul(xr, dr, mode="tn", name="rwkv_r_dw"),
                               _matmul(xk, dk, mode="tn", name="rwkv_k_dw"),
                               _matmul(xv, dv, mode="tn", name="rwkv_v_dw")], axis=1)

    def shift_bwd(ctx, rv, pv, nv, bv, pa):
        xt, dres = rv[0], rv[1]
        dmix_in = rv[2:8]
        mod, mu = bv[0], pa[0]
        f_h = lambda xx_, mod_: _rmsmod(xx_, mod_[0:1], mod_[1:2])
        h, vjp = jax.vjp(f_h, xt, mod)
        hprev = f_h(pv[0], mod)[7:8]
        hprev = jnp.where(ctx.first, jnp.zeros_like(hprev), hprev)
        rowid = lax.broadcasted_iota(jnp.int32, h.shape, 0)
        xx = jnp.where(rowid == 0, hprev, pltpu.roll(h, 1, 0)) - h
        tot = dmix_in[0]
        wsum = dmix_in[0] * mu[0:1]
        for j in range(1, 6):
            tot = tot + dmix_in[j]
            wsum = wsum + dmix_in[j] * mu[j:j + 1]
        nxt = nv[0][0:1] * mu[0:1]
        for j in range(1, 6):
            nxt = nxt + nv[j][0:1] * mu[j:j + 1]
        nxt = jnp.where(ctx.last, jnp.zeros_like(nxt), nxt)
        tmr = h.shape[0]
        wshift = jnp.where(rowid == tmr - 1, nxt, pltpu.roll(wsum, tmr - 1, 0))
        dh = tot - wsum + wshift
        dx_, dmod_ = vjp(dh)
        dmu = jnp.concatenate([jnp.sum(dmix_in[j] * xx, axis=0, keepdims=True) for j in range(6)], axis=0)
        dx2_t = dx_ + dres
        ff_below, mod_below = rv[8], bv[1]
        dff_below = mod_below[5:6] * dx2_t
        dgate = jnp.sum(dx2_t * ff_below, axis=0, keepdims=True)
        gate_row = lax.broadcasted_iota(jnp.int32, (N_MOD, D), 0) == N_MOD - 1
        dmod_below = jnp.where(gate_row, jnp.broadcast_to(dgate, (N_MOD, D)), 0.0)
        return [dx2_t, dff_below], [dmod_, dmod_below], [dmu]

    dmix_list = [dxr, dxw, dxk, dxv, dxa, dxg]
    dx2, dff0, dmod1_c, dmod0_a, d_mu = _rowwise(
        shift_bwd, "shift_mix_bwd", seq=seq, tm=tm // 2, rows=[x2, dx2_a] + dmix_list + [saved0[4]],
        prev8=[x2], next8=dmix_list, bvecs=[mod1, mod0], params=[mu_full],
        out_rows=[(D, F32), (D, BF16)], out_bacc=[(N_MOD, D), (N_MOD, D)], out_pacc=[(6, D)])
    dmod1 = dmod1 + dmod1_c

    dx0_a, dmix0, dmod0, dw1_0, dw2_0 = _mlp_bwd(seq, tm, saved0, x0, mix0, mod0, w1_l[0], w2_l[0], dx2, "0",
                                                 res2_grads=(dff0, dmod0_a))
    dz = _matmul(dmix0, a_wout, mode="nt", name="sgu_out_dx")
    d_a_wout = _matmul(z, dmix0, mode="tn", name="sgu_out_dw")
    duvp, d_a_lng, d_a_lnb, d_ws, d_bias = _sgu_bwd(uvp, a_ln_g, a_ln_b, ws, bias, dz)
    dh_a = _matmul(duvp, a_win, mode="nt", name="sgu_in_dx")
    d_a_win = _matmul(h_a, duvp, mode="tn", name="sgu_in_dw", out_shards=N_CHIPS)
    grad_x, dmod0_c = _rw_bwd(_f_norm1, "norm1_a_bwd", seq=seq, tm=tm, rows=[x0], bvecs=[mod0], cts=[dh_a],
                              need_rows=[0], extra=dx0_a)
    dmod0 = dmod0 + dmod0_c

    dmod_blk = _pad_rows(jnp.concatenate([dmod0.reshape(nbat, -1), dmod1.reshape(nbat, -1)], axis=1), 8)
    dmod_all = _all_gather8(dmod_blk).reshape(N_DEV, 8, 2, N_MOD * D)[:, :nbat].reshape(N_DEV * nbat, 2, N_MOD * D)
    g_ada_w, g_ada_b = [], []
    for i in range(2):
        cols = lax.dynamic_slice_in_dim(dmod_all[:, i], chip * n_ada, n_ada, axis=1)
        g_ada_w.append(_matmul(cond_all, cols, mode="tn", name="ada_dw_%d" % i))
    (g_ada_b_all,) = _small(lambda t: (jnp.sum(t, axis=0),), "ada_db", [dmod_all], [(2, N_MOD * D)])
    grads = {"ada_w": jnp.stack(g_ada_w), "ada_b": g_ada_b_all}

    rep = _pad_rows(jnp.concatenate([
        d_a_lng, d_a_lnb, jnp.sum(d_bias, axis=-1).reshape(1, D), d_rk, d_final_g,
        jnp.zeros((3, D), F32), d_ws.reshape(-1, D)], axis=0), 8)
    rep_rows = rep.shape[0]
    rep_all = _all_gather8(rep)
    (rep_sum,) = _small(lambda t: (functools.reduce(lambda p, q: p + q,
                                                     [t[j * rep_rows:(j + 1) * rep_rows] for j in range(N_DEV)]),),
                        "replicated_sum", [rep_all], [(rep_rows, D)])
    grads["a_ln_g"] = rep_sum[0:1]
    grads["a_ln_b"] = rep_sum[1:2]
    grads["a_b_s"] = rep_sum[2:3].reshape(a_b_s.shape)
    grads["b_r_k"] = rep_sum[3:4].reshape(b_r_k.shape)
    grads["final_g"] = rep_sum[4].reshape(final_g.shape)
    grads["a_w_s"] = rep_sum[8:8 + GROUPS * CHUNK * LANES // D].reshape(a_w_s.shape)

    vec_grads = jnp.concatenate([d_mu, d_w0, d_a0, d_kk, d_ka, d_lng, d_lnb], axis=0)
    packed = {
        "mlp_w1_0": dw1_0, "mlp_w1_1": dw1_1,
        "mlp_w2_0": dw2_0.reshape(N_CHIPS, -1, D), "mlp_w2_1": dw2_1.reshape(N_CHIPS, -1, D),
        "a_w_in": d_a_win, "a_w_out": d_a_wout.reshape(N_CHIPS, -1, D),
        "b_w_in": _shard_cols(d_b_win), "b_w_out": d_b_wout.reshape(N_CHIPS, -1, D),
        "b_w1": d_w1p[:, :lora_w].reshape(N_CHIPS, -1, lora_w), "b_w2": _shard_cols(d_w2p[:lora_w]),
        "b_a1": d_a1p[:, :lora_w].reshape(N_CHIPS, -1, lora_w), "b_a2": _shard_cols(d_a2p[:lora_w]),
        "b_g1": d_g1p[:, :lora_g].reshape(N_CHIPS, -1, lora_g), "b_g2": _shard_cols(d_g2p[:lora_g]),
    }
    pieces = [_rows_of(packed[name]) for name, _ in big] + [_pad_rows(_rows_of(_shard_cols(vec_grads)), 8)]
    used = sum(p.shape[1] for p in pieces)
    pieces.append(jnp.zeros((N_CHIPS, (-used) % 2016, ROW_W), F32))
    g_pack = jnp.concatenate(pieces, axis=1)
    g_red = _reduce_scatter(g_pack)
    for name, _ in big:
        start, rows_k, shape = offs[name]
        grads[name] = g_red[start:start + rows_k].reshape(shape)
    vec_red = g_red[n_big_rows:n_big_rows + vec_rows.shape[0]].reshape(n_vec, D // N_CHIPS)
    grads["b_mu"] = vec_red[0:6].reshape(b_mu.shape)
    for j, name in enumerate(vec_names):
        grads[name] = vec_red[6 + j:7 + j].reshape(weights[name].shape)
    for base in ("mlp_w1", "mlp_w2"):
        grads[base] = jnp.stack([grads.pop(base + "_0"), grads.pop(base + "_1")])
    for name in ("a_w_in", "a_w_out", "b_w_in", "b_w_out", "b_w1", "b_w2", "b_a1", "b_a2", "b_g1", "b_g2"):
        grads[name] = grads[name].reshape(weights[name].shape)

    deltas, new_m, new_v = {}, {}, {}
    for name in order:
        gr = grads[name].reshape(weights[name].shape)
        grads[name] = gr
        deltas[name], new_m[name], new_v[name] = _elementwise(
            _adamw, "adamw_" + name, [weights[name], gr, moms[name][0], moms[name][1]], 3)

    return (loss, grad_x.reshape(x.shape), *[grads[k] for k in order], *[deltas[k] for k in order],
            *[new_m[k] for k in order], *[new_v[k] for k in order])
```

```python
import functools

import jax
import jax.numpy as jnp
from jax import lax
from jax.experimental import pallas as pl
from jax.experimental.pallas import tpu as pltpu

F32 = jnp.float32
BF16 = jnp.bfloat16
MESH = pl.DeviceIdType.MESH
AXES = ("x", "y", "c")

D = 1024
N_MOD = 6
HEAD = 64
CHUNK = 128
GROUPS = 8
LANES = 128
ROW_W = 1024
N_CHIPS = 4
N_DEV = 8

RMS_EPS = 1e-6
LN_EPS = 1e-5
GN_EPS = HEAD * 1e-5
L2_EPS = 1e-12

ADAM_LR = 0.001
ADAM_B1 = 0.9
ADAM_B2 = 0.999
ADAM_EPS = 1e-08
ADAM_WD = 0.01
ADAM_STEP = 10

VMEM_LIMIT_V7X = 56 * 1024 * 1024
HIGHEST = lax.Precision.HIGHEST


def _params(sem=None):
    return pltpu.CompilerParams(dimension_semantics=sem, vmem_limit_bytes=VMEM_LIMIT_V7X)


def _tile(dim, target):
    if dim <= target:
        return dim
    for cand in range(target, 0, -LANES):
        if dim % cand == 0:
            return cand
    raise ValueError((dim, target))


def _matmul(a, b, *, mode, name, out_dtype=F32, out_shards=1, tm=1024, tn=1024, tk=4096):
    if mode == "nn":
        (m, k), (k2, n) = a.shape, b.shape
    elif mode == "nt":
        (m, k), (n, k2) = a.shape, b.shape
    else:
        (k, m), (k2, n) = a.shape, b.shape
    assert k == k2, (name, a.shape, b.shape)
    n_sh = n // out_shards
    tm, tn, tk = _tile(m, tm), _tile(n_sh, tn), _tile(k, tk)
    nk = k // tk
    nb = n_sh // tn
    use_scratch = nk > 1 and out_dtype != F32

    if mode == "tn":
        a_spec = pl.BlockSpec((tk, tm), lambda i, j, kk: (kk, i))
    else:
        a_spec = pl.BlockSpec((tm, tk), lambda i, j, kk: (i, kk))
    if mode == "nt":
        b_spec = pl.BlockSpec((tn, tk), lambda i, j, kk: (j, kk))
    else:
        b_spec = pl.BlockSpec((tk, tn), lambda i, j, kk: (kk, j))
    if out_shards == 1:
        out_shape = jax.ShapeDtypeStruct((m, n), out_dtype)
        o_spec = pl.BlockSpec((tm, tn), lambda i, j, kk: (i, j))
    else:
        out_shape = jax.ShapeDtypeStruct((out_shards, m, n_sh), out_dtype)
        o_spec = pl.BlockSpec((None, tm, tn), lambda i, j, kk: (j // nb, i, j % nb))

    def body(a_ref, b_ref, o_ref, *scratch):
        kk = pl.program_id(2)
        av = a_ref[...].astype(BF16)
        bv = b_ref[...].astype(BF16)
        if mode == "nn":
            dims = (((1,), (0,)), ((), ()))
        elif mode == "nt":
            dims = (((1,), (1,)), ((), ()))
        else:
            dims = (((0,), (0,)), ((), ()))
        part = lax.dot_general(av, bv, dims, preferred_element_type=F32)
        if nk == 1:
            o_ref[...] = part.astype(o_ref.dtype)
            return
        acc_ref = scratch[0] if use_scratch else o_ref

        @pl.when(kk == 0)
        def _():
            acc_ref[...] = part

        @pl.when(kk != 0)
        def _():
            acc_ref[...] += part

        if use_scratch:
            @pl.when(kk == nk - 1)
            def _():
                o_ref[...] = acc_ref[...].astype(o_ref.dtype)

    return pl.pallas_call(
        body, name=name, out_shape=out_shape,
        grid=(m // tm, n // tn, nk),
        in_specs=[a_spec, b_spec], out_specs=o_spec,
        scratch_shapes=[pltpu.VMEM((tm, tn), F32)] if use_scratch else [],
        compiler_params=_params(("parallel", "parallel", "arbitrary")),
    )(a, b)


def _matmul_ep(a, b, extras, epilogue, out_dtypes, *, mode, name, tm=1024, tn=1024, tk=2048):
    if mode == "nn":
        (m, k), (k2, n) = a.shape, b.shape
    else:
        (m, k), (n, k2) = a.shape, b.shape
    assert k == k2 and mode in ("nn", "nt"), (name, a.shape, b.shape)
    tm, tn, tk = _tile(m, tm), _tile(n, tn), _tile(k, tk)
    nk = k // tk
    n_ex, n_out = len(extras), len(out_dtypes)

    def body(a_ref, b_ref, *rest):
        extra_refs, out_refs = rest[:n_ex], rest[n_ex:n_ex + n_out]
        kk = pl.program_id(2)
        dims = (((1,), (0,)), ((), ())) if mode == "nn" else (((1,), (1,)), ((), ()))
        part = lax.dot_general(a_ref[...].astype(BF16), b_ref[...].astype(BF16), dims,
                               preferred_element_type=F32)

        def finish(acc):
            for ref, val in zip(out_refs, epilogue(acc, *[r[...] for r in extra_refs])):
                ref[...] = val.astype(ref.dtype)

        if nk == 1:
            finish(part)
            return
        acc_ref = rest[-1]

        @pl.when(kk == 0)
        def _():
            acc_ref[...] = part

        @pl.when(kk != 0)
        def _():
            acc_ref[...] += part

        @pl.when(kk == nk - 1)
        def _():
            finish(acc_ref[...])

    a_spec = pl.BlockSpec((tm, tk), lambda i, j, kk: (i, kk))
    b_spec = (pl.BlockSpec((tk, tn), lambda i, j, kk: (kk, j)) if mode == "nn"
              else pl.BlockSpec((tn, tk), lambda i, j, kk: (j, kk)))
    o_spec = pl.BlockSpec((tm, tn), lambda i, j, kk: (i, j))
    res = pl.pallas_call(
        body, name=name, out_shape=[jax.ShapeDtypeStruct((m, n), dt) for dt in out_dtypes],
        grid=(m // tm, n // tn, nk),
        in_specs=[a_spec, b_spec] + [o_spec] * n_ex, out_specs=[o_spec] * n_out,
        scratch_shapes=[pltpu.VMEM((tm, tn), F32)] if nk > 1 else [],
        compiler_params=_params(("parallel", "parallel", "arbitrary")),
    )(a, b, *extras)
    return list(res)


class _Ctx:
    def __init__(self, first, last):
        self.first = first
        self.last = last


def _rowwise(fn, name, *, seq, tm, rows=(), prev8=(), next8=(), bvecs=(), params=(),
             out_rows=(), out_bacc=(), out_pacc=()):
    n = rows[0].shape[0]
    tm = min(tm, seq)
    assert n % seq == 0 and seq % tm == 0 and tm % 8 == 0
    tpb = seq // tm
    nt = n // tm
    nbat = n // seq
    r8 = tm // 8
    counts = [len(rows), len(prev8), len(next8), len(bvecs), len(params)]
    n_in = sum(counts)

    def body(*refs):
        i = pl.program_id(0)
        first = (i % tpb) == 0
        last = (i % tpb) == (tpb - 1)
        vals = [r[...] for r in refs[:n_in]]
        groups, pos = [], 0
        for cnt in counts:
            groups.append(vals[pos:pos + cnt])
            pos += cnt
        ro, bo, po = fn(_Ctx(first, last), *groups)
        outs = refs[n_in:]
        assert len(ro) == len(out_rows) and len(bo) == len(out_bacc) and len(po) == len(out_pacc)
        for ref, val in zip(outs[:len(ro)], ro):
            ref[...] = val.astype(ref.dtype)
        for ref, val in zip(outs[len(ro):len(ro) + len(bo)], bo):
            @pl.when(first)
            def _(ref=ref, val=val):
                ref[...] = val

            @pl.when(jnp.logical_not(first))
            def _(ref=ref, val=val):
                ref[...] += val
        for ref, val in zip(outs[len(ro) + len(bo):], po):
            @pl.when(i == 0)
            def _(ref=ref, val=val):
                ref[...] = val

            @pl.when(i != 0)
            def _(ref=ref, val=val):
                ref[...] += val

    in_specs = []
    for arr in rows:
        in_specs.append(pl.BlockSpec((tm, arr.shape[1]), lambda i: (i, 0)))
    for arr in prev8:
        in_specs.append(pl.BlockSpec((8, arr.shape[1]), lambda i: (jnp.maximum(i * r8 - 1, 0), 0)))
    for arr in next8:
        in_specs.append(pl.BlockSpec((8, arr.shape[1]), lambda i: (jnp.minimum((i + 1) * r8, n // 8 - 1), 0)))
    for arr in bvecs:
        in_specs.append(pl.BlockSpec((None,) + arr.shape[1:], lambda i: (i // tpb, 0, 0)))
    for arr in params:
        in_specs.append(pl.BlockSpec(arr.shape, lambda i: (0, 0)))
    out_shape, out_specs = [], []
    for d, dt in out_rows:
        out_shape.append(jax.ShapeDtypeStruct((n, d), dt))
        out_specs.append(pl.BlockSpec((tm, d), lambda i: (i, 0)))
    for r, d in out_bacc:
        out_shape.append(jax.ShapeDtypeStruct((nbat, r, d), F32))
        out_specs.append(pl.BlockSpec((None, r, d), lambda i: (i // tpb, 0, 0)))
    for r, d in out_pacc:
        out_shape.append(jax.ShapeDtypeStruct((r, d), F32))
        out_specs.append(pl.BlockSpec((r, d), lambda i: (0, 0)))
    res = pl.pallas_call(
        body, name=name, out_shape=out_shape, grid=(nt,),
        in_specs=in_specs, out_specs=out_specs,
        compiler_params=_params(("arbitrary",)),
    )(*rows, *prev8, *next8, *bvecs, *params)
    return list(res)


def _rw_fwd(f, name, *, seq, tm, rows, bvecs=(), params=(), outs):
    def fn(ctx, rv, pv, nv, bv, pa):
        res = f(*[v.astype(F32) for v in rv], *bv, *pa)
        return list(res), [], []
    return _rowwise(fn, name, seq=seq, tm=tm, rows=rows, bvecs=bvecs, params=params, out_rows=outs)


def _rw_bwd(f, name, *, seq, tm, rows, bvecs=(), params=(), cts, need_rows, extra=None, dtypes=None):
    nr, nb, npar = len(rows), len(bvecs), len(params)
    all_rows = list(rows) + list(cts) + ([extra] if extra is not None else [])

    def fn(ctx, rv, pv, nv, bv, pa):
        prim = [v.astype(F32) for v in rv[:nr]]
        ct = tuple(v.astype(F32) for v in rv[nr:nr + len(cts)])
        _, vjp = jax.vjp(f, *prim, *bv, *pa)
        g = vjp(ct)
        d_rows = [g[j] for j in need_rows]
        if extra is not None:
            d_rows[0] = d_rows[0] + rv[-1].astype(F32)
        return d_rows, list(g[nr:nr + nb]), list(g[nr + nb:])

    return _rowwise(
        fn, name, seq=seq, tm=tm, rows=all_rows, bvecs=bvecs, params=params,
        out_rows=[(rows[j].shape[1], F32 if dtypes is None else dtypes[i]) for i, j in enumerate(need_rows)],
        out_bacc=[b.shape[1:] for b in bvecs], out_pacc=[p.shape for p in params])


def _small(fn, name, arrays, out_shapes):
    def body(*refs):
        res = fn(*[r[...] for r in refs[:len(arrays)]])
        for ref, val in zip(refs[len(arrays):], res):
            ref[...] = val.astype(ref.dtype)

    vm = pl.BlockSpec(memory_space=pltpu.VMEM)
    res = pl.pallas_call(
        body, name=name,
        out_shape=[jax.ShapeDtypeStruct(s, F32) for s in out_shapes],
        in_specs=[vm] * len(arrays), out_specs=[vm] * len(out_shapes),
        compiler_params=_params(),
    )(*arrays)
    return list(res)


def _elementwise(fn, name, arrays, n_out):
    shape = arrays[0].shape
    size = arrays[0].size
    if len(shape) >= 2 and shape[-1] % LANES == 0 and (size // shape[-1]) % 8 == 0:
        view = (size // shape[-1], shape[-1])
    elif size % ROW_W == 0 and (size // ROW_W) % 8 == 0:
        view = (size // ROW_W, ROW_W)
    else:
        view = (1, size) if len(shape) < 2 else (size // shape[-1], shape[-1])
    rows = view[0]
    tr = rows
    for cand in (256, 128, 64, 32, 16, 8):
        if rows > cand and rows % cand == 0:
            tr = cand
            break

    def body(*refs):
        res = fn(*[r[...] for r in refs[:len(arrays)]])
        for ref, val in zip(refs[len(arrays):], res):
            ref[...] = val

    spec = pl.BlockSpec((tr, view[1]), lambda i: (i, 0))
    res = pl.pallas_call(
        body, name=name,
        out_shape=[jax.ShapeDtypeStruct(view, F32)] * n_out,
        grid=(rows // tr,), in_specs=[spec] * len(arrays), out_specs=[spec] * n_out,
        compiler_params=_params(("parallel",)),
    )(*[a.reshape(view) for a in arrays])
    return [r.reshape(shape) for r in res]


def _rms(x):
    return x * lax.rsqrt(jnp.mean(x * x, axis=-1, keepdims=True) + RMS_EPS)


def _rmsmod(x, sh, sc):
    return _rms(x) * (1.0 + sc) + sh


def _f_norm1(x, mod):
    return (_rmsmod(x, mod[0:1], mod[1:2]),)


def _f_sgu_pre(uvp, ln_g, ln_b):
    uv = 0.5 * uvp * (1.0 + lax.erf(uvp * (2.0 ** -0.5)))
    u = uv[:, :D]
    v = uv[:, D:]
    mu = jnp.mean(v, axis=-1, keepdims=True)
    vc = v - mu
    var = jnp.mean(vc * vc, axis=-1, keepdims=True)
    return u, vc * lax.rsqrt(var + LN_EPS) * ln_g + ln_b


def _f_res_norm2(x, mix, mod):
    x1 = x + mod[2:3] * mix
    return x1, _rmsmod(x1, mod[3:4], mod[4:5])


def _f_res2(x1, ff, mod):
    return (x1 + mod[5:6] * ff,)


def _f_loss(x, tgt, fg):
    err = _rms(x) * fg - tgt
    return 0.5 * jnp.sum(jnp.mean(err * err, axis=-1))


def _shift_mix(ctx, x, xprev8, mod, mu):
    h = _rmsmod(x, mod[0:1], mod[1:2])
    hprev = _rmsmod(xprev8, mod[0:1], mod[1:2])[7:8]
    hprev = jnp.where(ctx.first, jnp.zeros_like(hprev), hprev)
    rowid = lax.broadcasted_iota(jnp.int32, h.shape, 0)
    hp = jnp.where(rowid == 0, hprev, pltpu.roll(h, 1, 0))
    xx = hp - h
    return h, xx, [h + xx * mu[j:j + 1] for j in range(6)]


def _split_bf16(t, parts):
    out, rest = [], t.astype(F32)
    for _ in range(parts):
        piece = rest.astype(BF16)
        out.append(piece)
        rest = rest - piece.astype(F32)
    return out


def _make_mm(na, nb, ct_pieces=1, saved_pieces=1):
    def raw(a, b, pa, pb):
        if pa == 0:
            return jnp.dot(a, b, precision=HIGHEST, preferred_element_type=F32)
        acc = None
        bs = _split_bf16(b, pb)
        for i, ai in enumerate(_split_bf16(a, pa)):
            for j, bj in enumerate(bs):
                if i + j < max(pa, pb):
                    term = jnp.dot(ai, bj, preferred_element_type=F32)
                    acc = term if acc is None else acc + term
        return acc

    @jax.custom_vjp
    def mm(a, b):
        return raw(a, b, na, nb)

    def fwd(a, b):
        return raw(a, b, na, nb), (a, b)

    def bwd(res, ct):
        a, b = res
        if na == 0:
            return raw(ct, b.T, 0, 0), raw(a.T, ct, 0, 0)
        return raw(ct, b.T, ct_pieces, saved_pieces), raw(a.T, ct, saved_pieces, ct_pieces)

    mm.defvjp(fwd, bwd)
    return mm


class _WkvMms:
    def __init__(self, head_sum, cum, score, square, apply, out, state):
        self.head_sum, self.cum, self.score = head_sum, cum, score
        self.square, self.apply, self.out, self.state = square, apply, out, state


def _wkv_mms(cfg):
    table = {"x": (0, 0), "1": (1, 1), "2": (2, 2), "3": (3, 3), "a": (2, 1), "b": (1, 2)}
    hs, cu, sc_, sq, ap, ou, st = [table[ch] for ch in cfg]
    return _WkvMms(_make_mm(hs[0], 1) if hs[0] else _make_mm(0, 0),
                   _make_mm(1, cu[1], ct_pieces=2) if cu[0] else _make_mm(0, 0),
                   _make_mm(*sc_, saved_pieces=2), _make_mm(*sq), _make_mm(*ap, saved_pieces=2),
                   _make_mm(*ou), _make_mm(*st))


WKV_PRECISION = "1221b11"


SGU_CHUNKS_PER_STEP = 4


def _sgu_tile(mm, u, vn, ws, bias):
    row = lax.broadcasted_iota(jnp.int32, (CHUNK, CHUNK), 0)
    col = lax.broadcasted_iota(jnp.int32, (CHUNK, CHUNK), 1)
    wm = [jnp.where(col <= row, w, 0.0) for w in ws]
    out_rows = []
    for ch in range(u.shape[0] // CHUNK):
        rs = slice(ch * CHUNK, (ch + 1) * CHUNK)
        out_rows.append(jnp.concatenate(
            [mm(wm[g], vn[rs, g * LANES:(g + 1) * LANES]) + bias[g] for g in range(GROUPS)], axis=1))
    return u * jnp.concatenate(out_rows, axis=0)


def _sgu_mixer_tile(mm, uvp, ln_g, ln_b, ws, bias):
    u, vn = _f_sgu_pre(uvp, ln_g, ln_b)
    return _sgu_tile(mm, u, vn, ws, bias)


def _sgu_fwd(uvp, ln_g, ln_b, ws, bias):
    n = uvp.shape[0]
    rows = CHUNK * SGU_CHUNKS_PER_STEP
    mm = _make_mm(1, 1)

    def body(x_ref, g_ref, b2_ref, w_ref, b_ref, z_ref):
        ws_l = [w_ref[g] for g in range(GROUPS)]
        bias_l = [b_ref[g] for g in range(GROUPS)]
        z_ref[...] = _sgu_mixer_tile(mm, x_ref[...], g_ref[...], b2_ref[...], ws_l, bias_l).astype(z_ref.dtype)

    tok_in = pl.BlockSpec((rows, 2 * D), lambda i: (i, 0))
    tok = pl.BlockSpec((rows, D), lambda i: (i, 0))
    vec = pl.BlockSpec((1, D), lambda i: (0, 0))
    grp = pl.BlockSpec((GROUPS, CHUNK, LANES), lambda i: (0, 0, 0))
    return pl.pallas_call(
        body, name="sgu_fwd", out_shape=jax.ShapeDtypeStruct((n, D), BF16),
        grid=(n // rows,), in_specs=[tok_in, vec, vec, grp, grp], out_specs=tok,
        compiler_params=_params(("parallel",)),
    )(uvp, ln_g, ln_b, ws, bias)


def _sgu_bwd(uvp, ln_g, ln_b, ws, bias, dz):
    n = uvp.shape[0]
    rows = CHUNK * SGU_CHUNKS_PER_STEP
    mm = _make_mm(1, 1)

    def body(x_ref, g_ref, b2_ref, w_ref, b_ref, dz_ref, dx_ref, dg_ref, db2_ref, dw_ref, db_ref):
        i = pl.program_id(0)
        ws_l = [w_ref[g] for g in range(GROUPS)]
        bias_l = [b_ref[g] for g in range(GROUPS)]
        _, vjp = jax.vjp(functools.partial(_sgu_mixer_tile, mm), x_ref[...], g_ref[...], b2_ref[...], ws_l, bias_l)
        dx, dg, db2, dw, db = vjp(dz_ref[...].astype(F32))
        dx_ref[...] = dx.astype(dx_ref.dtype)

        @pl.when(i == 0)
        def _():
            dg_ref[...] = dg
            db2_ref[...] = db2
            for g in range(GROUPS):
                dw_ref[g] = dw[g]
                db_ref[g] = db[g]

        @pl.when(i != 0)
        def _():
            dg_ref[...] += dg
            db2_ref[...] += db2
            for g in range(GROUPS):
                dw_ref[g] += dw[g]
                db_ref[g] += db[g]

    tok_in = pl.BlockSpec((rows, 2 * D), lambda i: (i, 0))
    tok = pl.BlockSpec((rows, D), lambda i: (i, 0))
    vec = pl.BlockSpec((1, D), lambda i: (0, 0))
    grp = pl.BlockSpec((GROUPS, CHUNK, LANES), lambda i: (0, 0, 0))
    return pl.pallas_call(
        body, name="sgu_bwd",
        out_shape=[jax.ShapeDtypeStruct((n, 2 * D), BF16), jax.ShapeDtypeStruct((1, D), F32),
                   jax.ShapeDtypeStruct((1, D), F32),
                   jax.ShapeDtypeStruct((GROUPS, CHUNK, LANES), F32),
                   jax.ShapeDtypeStruct((GROUPS, CHUNK, LANES), F32)],
        grid=(n // rows,), in_specs=[tok_in, vec, vec, grp, grp, tok],
        out_specs=[tok_in, vec, vec, grp, grp],
        compiler_params=_params(("arbitrary",)),
    )(uvp, ln_g, ln_b, ws, bias, dz)


def _chains(t):
    return [t[i] for i in range(t.shape[0])] if t.ndim == 3 else [t]


def _bmm(mm, a, b):
    if a.ndim == 2 and b.ndim == 2:
        return mm(a, b)
    ca, cb = _chains(a), _chains(b)
    n = max(len(ca), len(cb))
    return jnp.stack([mm(ca[i % len(ca)], cb[i % len(cb)]) for i in range(n)])


def _bt(a):
    return a.T if a.ndim == 2 else jnp.stack([t.T for t in _chains(a)])


def _make_solver(mms, given):
    def masks():
        lane = lax.broadcasted_iota(jnp.int32, (1, LANES), 1)
        m_a = (lane < HEAD).astype(F32)
        return m_a, 1.0 - m_a

    def doubling(pa, pb, x):
        xa = x + _bmm(mms.apply, pa, x)
        xb = x + _bmm(mms.apply, pb, x)
        for _ in range(6):
            pa = _bmm(mms.square, pa, pa)
            pb = _bmm(mms.square, pb, pb)
            xa = xa + _bmm(mms.apply, pa, xa)
            xb = xb + _bmm(mms.apply, pb, xb)
        m_a, m_b = masks()
        return m_a * xa + m_b * xb

    @jax.custom_vjp
    def solve(ab_a, ab_b, rhs, hint):
        return hint if given else doubling(ab_a, ab_b, rhs)

    def fwd(ab_a, ab_b, rhs, hint):
        u = hint if given else doubling(ab_a, ab_b, rhs)
        return u, (ab_a, ab_b, u)

    def bwd(res, g):
        ab_a, ab_b, u = res
        m_a, m_b = masks()
        h = doubling(_bt(ab_a), _bt(ab_b), g)
        ut = _bt(u)
        return _bmm(mms.apply, h * m_a, ut), _bmm(mms.apply, h * m_b, ut), h, jnp.zeros_like(u)

    solve.defvjp(fwd, bwd)
    return solve


def _wkv_chunk(mms, s0, r, k, v, wl, al, g, w0, a0, k_k, k_a, r_k, ln_g, ln_b, u_hint=None):
    ln = CHUNK
    row = lax.broadcasted_iota(jnp.int32, (ln, ln), 0)
    col = lax.broadcasted_iota(jnp.int32, (ln, ln), 1)
    incl = (col <= row).astype(F32)
    strict = (col < row).astype(F32)
    same_head = ((row // HEAD) == (col // HEAD)).astype(F32)
    lane = lax.broadcasted_iota(jnp.int32, (1, LANES), 1)
    m_a = (lane < HEAD).astype(F32)
    m_b = 1.0 - m_a
    rowid = lax.broadcasted_iota(jnp.int32, (ln, LANES), 0)
    cat = jnp.concatenate

    def hsum(t):
        return _bmm(mms.head_sum, t, same_head)

    def pick_row(t, j):
        return jnp.sum(jnp.where(rowid == j, t, 0.0), axis=-2, keepdims=True)

    z = w0 + wl
    softplus_neg = jnp.maximum(-z, 0.0) + jnp.log(1.0 + jnp.exp(-jnp.abs(z)))
    lw = -jnp.exp(-softplus_neg - 0.5)
    a = 1.0 / (1.0 + jnp.exp(-(a0 + al)))
    kx = k * k_k
    kkn = kx / jnp.maximum(jnp.sqrt(hsum(kx * kx)), L2_EPS)
    kp = k * (1.0 + (a - 1.0) * k_a)
    aa = -kkn
    bb = kkn * a

    c = _bmm(mms.cum, incl, lw)
    c_mid = pick_row(c, ln // 2 - 1)
    ce = c - c_mid
    e_pos = jnp.exp(ce)
    e_neg = jnp.exp(-ce)
    at = aa * jnp.exp(ce - lw)
    bt = bb * e_neg
    kt = kp * e_neg
    rt = r * e_pos
    s0p = s0 * jnp.exp(c_mid)

    bk = cat([bt, kt], axis=-2)
    sc = _bmm(mms.score, cat([at * m_a, at * m_b, rt * m_a, rt * m_b], axis=-2), _bt(bk))
    ab_a, ak_a = sc[..., 0:ln, 0:ln] * strict, sc[..., 0:ln, ln:] * strict
    ab_b, ak_b = sc[..., ln:2 * ln, 0:ln] * strict, sc[..., ln:2 * ln, ln:] * strict
    incl2 = cat([incl, incl], axis=1)
    p_a = sc[..., 2 * ln:3 * ln, :] * incl2
    p_b = sc[..., 3 * ln:, :] * incl2

    base = _bmm(mms.score, cat([at, rt], axis=-2), _bt(s0p))
    rhs = base[..., :ln, :] + m_a * _bmm(mms.out, ak_a, v) + m_b * _bmm(mms.out, ak_b, v)

    u = _make_solver(mms, u_hint is not None)(ab_a, ab_b, rhs, rhs if u_hint is None else u_hint)
    uv = cat([u, v], axis=-2)
    y = base[..., ln:, :] + m_a * _bmm(mms.out, p_a, uv) + m_b * _bmm(mms.out, p_b, uv)
    s_new = (s0p + _bmm(mms.state, _bt(uv), bk)) * same_head * jnp.exp(pick_row(ce, ln - 1))

    mean = hsum(y) * (1.0 / HEAD)
    yc = y - mean
    var = hsum(yc * yc) * (1.0 / HEAD)
    yn = yc * lax.rsqrt(var + GN_EPS) * ln_g + ln_b
    bonus = hsum(r * kp * r_k) * v
    return ((yn + bonus) * g, s_new), u


N_WKV_ROWS = 6
N_WKV_PAR = 7


WKV_PAIRS_PER_STEP = 4


def _to_chains(val, nbat, pp):
    if val.ndim == 2:
        return jnp.stack([val[:, q * LANES:(q + 1) * LANES] for _ in range(nbat) for q in range(pp)])
    return jnp.stack([val[b, :, q * LANES:(q + 1) * LANES] for b in range(nbat) for q in range(pp)])


def _wkv_fwd(seq, rows, pars):
    n = rows[0].shape[0]
    nbat, nch, npair, pp = n // seq, seq // CHUNK, D // LANES, WKV_PAIRS_PER_STEP
    chunk_fn = functools.partial(_wkv_chunk, _wkv_mms(WKV_PRECISION))

    def body(*refs):
        row_vals = [_to_chains(r[...], nbat, pp) for r in refs[:N_WKV_ROWS]]
        par_vals = [_to_chains(r[...], nbat, pp) for r in refs[N_WKV_ROWS:N_WKV_ROWS + N_WKV_PAR]]
        yo_ref, ck_ref, u_ref, s_ref = refs[N_WKV_ROWS + N_WKV_PAR:]
        ch = pl.program_id(1)

        @pl.when(ch == 0)
        def _():
            s_ref[...] = jnp.zeros_like(s_ref)

        s0 = s_ref[...]
        (yo, s_new), u = chunk_fn(s0, *row_vals, *par_vals)
        s_ref[...] = s_new
        for b in range(nbat):
            for q in range(pp):
                ck_ref[b, q] = s0[b * pp + q]
                yo_ref[b, :, q * LANES:(q + 1) * LANES] = yo[b * pp + q].astype(yo_ref.dtype)
                u_ref[b, :, q * LANES:(q + 1) * LANES] = u[b * pp + q]

    tok = pl.BlockSpec((nbat, CHUNK, pp * LANES), lambda p, ch: (0, ch, p))
    par = pl.BlockSpec((1, pp * LANES), lambda p, ch: (0, p))
    ck = pl.BlockSpec((nbat, pp, None, LANES, LANES), lambda p, ch: (0, p, ch, 0, 0))
    yo, ckpt, u_all = pl.pallas_call(
        body, name="wkv_fwd",
        out_shape=[jax.ShapeDtypeStruct((nbat, seq, D), BF16),
                   jax.ShapeDtypeStruct((nbat, npair, nch, LANES, LANES), F32),
                   jax.ShapeDtypeStruct((nbat, seq, D), F32)],
        grid=(npair // pp, nch),
        in_specs=[tok] * N_WKV_ROWS + [par] * N_WKV_PAR, out_specs=[tok, ck, tok],
        scratch_shapes=[pltpu.VMEM((nbat * pp, LANES, LANES), F32)],
        compiler_params=_params(("parallel", "arbitrary")),
    )(*[t.reshape(nbat, seq, D) for t in rows], *pars)
    return yo.reshape(n, D), ckpt, u_all


def _wkv_bwd(seq, rows, pars, ckpt, u_all, dyo):
    n = rows[0].shape[0]
    nbat, nch, npair, pp = n // seq, seq // CHUNK, D // LANES, WKV_PAIRS_PER_STEP
    chunk_fn = functools.partial(_wkv_chunk, _wkv_mms(WKV_PRECISION))
    n_in = N_WKV_ROWS + N_WKV_PAR

    def body(*refs):
        row_vals = [_to_chains(r[...], nbat, pp) for r in refs[:N_WKV_ROWS]]
        par_vals = [_to_chains(r[...], nbat, pp) for r in refs[N_WKV_ROWS:n_in]]
        ck_ref, u_ref, dyo_ref = refs[n_in:n_in + 3]
        d_rows = refs[n_in + 3:n_in + 3 + N_WKV_ROWS]
        d_pars = refs[n_in + 3 + N_WKV_ROWS:n_in + 3 + N_WKV_ROWS + N_WKV_PAR]
        ds_ref = refs[-1]
        ch = pl.program_id(1)

        @pl.when(ch == 0)
        def _():
            ds_ref[...] = jnp.zeros_like(ds_ref)

        s0 = jnp.stack([ck_ref[b, q] for b in range(nbat) for q in range(pp)])
        dyo_v = _to_chains(dyo_ref[...].astype(F32), nbat, pp)
        u_hint = _to_chains(u_ref[...], nbat, pp)
        _, vjp, _ = jax.vjp(functools.partial(chunk_fn, u_hint=u_hint), s0, *row_vals, *par_vals, has_aux=True)
        grads = vjp((dyo_v, ds_ref[...]))
        ds_ref[...] = grads[0]
        for ref, val in zip(d_rows, grads[1:1 + N_WKV_ROWS]):
            for b in range(nbat):
                for q in range(pp):
                    ref[b, :, q * LANES:(q + 1) * LANES] = val[b * pp + q].astype(ref.dtype)
        for ref, val in zip(d_pars, grads[1 + N_WKV_ROWS:]):
            per_pair = [functools.reduce(lambda s, t: s + t, [val[b * pp + q] for b in range(nbat)])
                        for q in range(pp)]
            tot = jnp.concatenate(per_pair, axis=1)

            @pl.when(ch == 0)
            def _(ref=ref, tot=tot):
                ref[...] = tot

            @pl.when(ch != 0)
            def _(ref=ref, tot=tot):
                ref[...] += tot

    tok = pl.BlockSpec((nbat, CHUNK, pp * LANES), lambda p, ch: (0, nch - 1 - ch, p))
    par = pl.BlockSpec((1, pp * LANES), lambda p, ch: (0, p))
    ck = pl.BlockSpec((nbat, pp, None, LANES, LANES), lambda p, ch: (0, p, nch - 1 - ch, 0, 0))
    res = pl.pallas_call(
        body, name="wkv_bwd",
        out_shape=[jax.ShapeDtypeStruct((nbat, seq, D), BF16)] * N_WKV_ROWS
        + [jax.ShapeDtypeStruct((1, D), F32)] * N_WKV_PAR,
        grid=(npair // pp, nch),
        in_specs=[tok] * N_WKV_ROWS + [par] * N_WKV_PAR + [ck, tok, tok],
        out_specs=[tok] * N_WKV_ROWS + [par] * N_WKV_PAR,
        scratch_shapes=[pltpu.VMEM((nbat * pp, LANES, LANES), F32)],
        compiler_params=_params(("parallel", "arbitrary")),
    )(*[t.reshape(nbat, seq, D) for t in rows], *pars, ckpt, u_all, dyo.reshape(nbat, seq, D))
    return [t.reshape(n, D) for t in res[:N_WKV_ROWS]] + list(res[N_WKV_ROWS:])


def _place():
    return lax.axis_index("x"), lax.axis_index("y"), lax.axis_index("c")


def _all_gather8(blk):
    m_per, n = blk.shape
    assert m_per % 8 == 0

    def body(x_ref, out_ref, send_sems, recv_sems, local_sem):
        x, y, c = _place()
        me, sibling = (x, y, c), (x, y, 1 - c)
        chips = [(1 - x, y), (x, 1 - y), (1 - x, 1 - y)]

        def rows(px, py, pc):
            return out_ref.at[pl.ds((4 * px + 2 * py + pc) * m_per, m_per), :]

        def copy(k, block, to, src=None):
            return pltpu.make_async_remote_copy(
                src_ref=rows(*block) if src is None else src, dst_ref=rows(*block),
                send_sem=send_sems.at[k], recv_sem=recv_sems.at[k],
                device_id=to, device_id_type=MESH)

        mine = pltpu.make_async_copy(x_ref, rows(*me), local_sem)
        mine.start()
        first = [copy(0, me, sibling, src=x_ref)]
        first += [copy(1 + j, me, (*chip, c), src=x_ref) for j, chip in enumerate(chips)]
        for cp in first:
            cp.start()
        passed = [copy(4 + j, (*chip, c), sibling) for j, chip in enumerate(chips)]
        for j, chip in enumerate(chips):
            copy(1 + j, (*chip, c), me).wait_recv()
            passed[j].start()
        copy(0, sibling, me).wait_recv()
        for j, chip in enumerate(chips):
            copy(4 + j, (*chip, 1 - c), me).wait_recv()
        for cp in first + passed:
            cp.wait_send()
        mine.wait()

    vm = pl.BlockSpec(memory_space=pltpu.VMEM)
    return pl.pallas_call(
        body, name="all_gather8_%dx%d" % (m_per, n),
        out_shape=jax.ShapeDtypeStruct((N_DEV * m_per, n), blk.dtype),
        in_specs=[vm], out_specs=vm,
        scratch_shapes=[pltpu.SemaphoreType.DMA((7,)), pltpu.SemaphoreType.DMA((7,)),
                        pltpu.SemaphoreType.DMA],
        compiler_params=_params(),
    )(blk)


def _own_slot(src, name):
    r, w = src.shape[-2:]
    tr = _tile(r, 1008)
    xi, yi, _ = _place()
    chip = jnp.reshape(2 * xi + yi, (1,)).astype(jnp.int32)

    def body(chip_ref, x_ref, o_ref):
        o_ref[...] = x_ref[...]

    if src.ndim == 2:
        in_spec = pl.BlockSpec((tr, w), lambda i, chip_ref: (i, 0))
    else:
        in_spec = pl.BlockSpec((None, tr, w), lambda i, chip_ref: (chip_ref[0], i, 0))
    return pl.pallas_call(
        body, name=name,
        out_shape=jax.ShapeDtypeStruct((N_CHIPS, r, w), src.dtype),
        grid_spec=pltpu.PrefetchScalarGridSpec(
            num_scalar_prefetch=1, grid=(r // tr,), in_specs=[in_spec],
            out_specs=pl.BlockSpec((None, tr, w), lambda i, chip_ref: (chip_ref[0], i, 0))),
        compiler_params=_params(("parallel",)),
    )(chip, src)


def _chip_all_gather(shard):
    r, w = shard.shape
    half = r // 2
    assert r % 2 == 0 and half % 16 == 0

    def body(x_ref, buf_ref, out_ref, send_sems, recv_sems):
        del buf_ref
        x, y, c = _place()
        sibling = (x, y, 1 - c)
        me_p = 2 * x + y
        chips = [(1 - x, y), (x, 1 - y), (1 - x, 1 - y)]

        def piece(p, h):
            return out_ref.at[p, pl.ds(h * half, half), :]

        def copy(k, p, h, to, src=None):
            return pltpu.make_async_remote_copy(
                src_ref=piece(p, h) if src is None else src, dst_ref=piece(p, h),
                send_sem=send_sems.at[k], recv_sem=recv_sems.at[k],
                device_id=to, device_id_type=MESH)

        my_half = x_ref.at[pl.ds(c * half, half), :]
        first = [copy(j, me_p, c, (*chip, c), src=my_half) for j, chip in enumerate(chips)]
        for cp in first:
            cp.start()
        passed = [copy(3 + j, 2 * chip[0] + chip[1], c, sibling) for j, chip in enumerate(chips)]
        for j, chip in enumerate(chips):
            copy(j, 2 * chip[0] + chip[1], c, sibling).wait_recv()
            passed[j].start()
        for j, chip in enumerate(chips):
            copy(3 + j, 2 * chip[0] + chip[1], 1 - c, sibling).wait_recv()
        for cp in first + passed:
            cp.wait_send()

    hbm = pl.BlockSpec(memory_space=pl.ANY)
    return pl.pallas_call(
        body, name="chip_all_gather",
        out_shape=jax.ShapeDtypeStruct((N_CHIPS, r, w), shard.dtype),
        in_specs=[hbm, hbm], out_specs=hbm, input_output_aliases={1: 0},
        scratch_shapes=[pltpu.SemaphoreType.DMA((6,)), pltpu.SemaphoreType.DMA((6,))],
        compiler_params=_params(),
    )(shard, _own_slot(shard, "gather_own_slot"))


def _sibling_swap_halves(g):
    _, r, w = g.shape
    half = r // 2

    def body(g_ref, t_ref, send_sem, recv_sem):
        x, y, c = _place()
        sibling = (x, y, 1 - c)
        cp = pltpu.make_async_remote_copy(
            src_ref=g_ref.at[:, pl.ds((1 - c) * half, half), :], dst_ref=t_ref,
            send_sem=send_sem, recv_sem=recv_sem, device_id=sibling, device_id_type=MESH)
        cp.start()
        cp.wait()

    hbm = pl.BlockSpec(memory_space=pl.ANY)
    return pl.pallas_call(
        body, name="rs_sibling_halves",
        out_shape=jax.ShapeDtypeStruct((N_CHIPS, half, w), g.dtype),
        in_specs=[hbm], out_specs=hbm,
        scratch_shapes=[pltpu.SemaphoreType.DMA, pltpu.SemaphoreType.DMA],
        compiler_params=_params(),
    )(g)


def _add_own_half(g, t):
    _, r, w = g.shape
    half = r // 2
    tr = 1008 if half % 1008 == 0 else 16
    assert half % tr == 0
    cidx = jnp.reshape(lax.axis_index("c"), (1,)).astype(jnp.int32)

    def body(c_ref, g_ref, t_ref, o_ref):
        o_ref[...] = (g_ref[...] + t_ref[...]).astype(o_ref.dtype)

    return pl.pallas_call(
        body, name="rs_add_halves",
        out_shape=jax.ShapeDtypeStruct((N_CHIPS, half, w), BF16),
        grid_spec=pltpu.PrefetchScalarGridSpec(
            num_scalar_prefetch=1, grid=(N_CHIPS, half // tr),
            in_specs=[pl.BlockSpec((None, None, tr, w), lambda p, i, c_ref: (p, c_ref[0], i, 0)),
                      pl.BlockSpec((None, tr, w), lambda p, i, c_ref: (p, i, 0))],
            out_specs=pl.BlockSpec((None, tr, w), lambda p, i, c_ref: (p, i, 0))),
        compiler_params=_params(("parallel", "parallel")),
    )(cidx, g.reshape(N_CHIPS, 2, half, w), t)


def _chip_exchange(h):
    _, hh, w = h.shape

    def body(h_ref, buf_ref, t_ref, send_sems, recv_sems):
        del buf_ref
        x, y, c = _place()
        me_p = 2 * x + y
        chips = [(1 - x, y), (x, 1 - y), (1 - x, 1 - y)]
        cps = []
        for j, chip in enumerate(chips):
            q = 2 * chip[0] + chip[1]
            cps.append(pltpu.make_async_remote_copy(
                src_ref=h_ref.at[q], dst_ref=t_ref.at[me_p],
                send_sem=send_sems.at[j], recv_sem=recv_sems.at[j],
                device_id=(*chip, c), device_id_type=MESH))
        for cp in cps:
            cp.start()
        for j, chip in enumerate(chips):
            q = 2 * chip[0] + chip[1]
            pltpu.make_async_remote_copy(
                src_ref=h_ref.at[q], dst_ref=t_ref.at[q],
                send_sem=send_sems.at[j], recv_sem=recv_sems.at[j],
                device_id=(*chip, c), device_id_type=MESH).wait_recv()
        for cp in cps:
            cp.wait_send()

    hbm = pl.BlockSpec(memory_space=pl.ANY)
    return pl.pallas_call(
        body, name="rs_chip_exchange",
        out_shape=jax.ShapeDtypeStruct(h.shape, h.dtype),
        in_specs=[hbm, hbm], out_specs=hbm, input_output_aliases={1: 0},
        scratch_shapes=[pltpu.SemaphoreType.DMA((3,)), pltpu.SemaphoreType.DMA((3,))],
        compiler_params=_params(),
    )(h, _own_slot(h, "rs_own_slot"))


def _sum_slots(t):
    _, hh, w = t.shape
    tr = 1008 if hh % 1008 == 0 else 16
    assert hh % tr == 0
    nblk = hh // tr
    cidx = jnp.reshape(lax.axis_index("c"), (1,)).astype(jnp.int32)

    def body(c_ref, t_ref, o_ref):
        s0, s1, s2, s3 = [t_ref[j].astype(F32) for j in range(N_CHIPS)]
        o_ref[...] = ((s0 + s1) + s2) + s3

    return pl.pallas_call(
        body, name="rs_sum_slots", out_shape=jax.ShapeDtypeStruct((2 * hh, w), F32),
        grid_spec=pltpu.PrefetchScalarGridSpec(
            num_scalar_prefetch=1, grid=(nblk,),
            in_specs=[pl.BlockSpec((N_CHIPS, tr, w), lambda i, c_ref: (0, i, 0))],
            out_specs=pl.BlockSpec((tr, w), lambda i, c_ref: (c_ref[0] * nblk + i, 0))),
        compiler_params=_params(("parallel",)),
    )(cidx, t)


def _sibling_join_halves(s):
    h2, w = s.shape
    hh = h2 // 2

    def body(s_ref, o_ref, send_sem, recv_sem):
        del s_ref
        x, y, c = _place()
        sibling = (x, y, 1 - c)
        cp = pltpu.make_async_remote_copy(
            src_ref=o_ref.at[pl.ds(c * hh, hh), :], dst_ref=o_ref.at[pl.ds(c * hh, hh), :],
            send_sem=send_sem, recv_sem=recv_sem, device_id=sibling, device_id_type=MESH)
        cp.start()
        pltpu.make_async_remote_copy(
            src_ref=o_ref.at[pl.ds((1 - c) * hh, hh), :], dst_ref=o_ref.at[pl.ds((1 - c) * hh, hh), :],
            send_sem=send_sem, recv_sem=recv_sem, device_id=sibling, device_id_type=MESH).wait_recv()
        cp.wait_send()

    hbm = pl.BlockSpec(memory_space=pl.ANY)
    return pl.pallas_call(
        body, name="rs_sibling_join",
        out_shape=jax.ShapeDtypeStruct(s.shape, s.dtype),
        in_specs=[hbm], out_specs=hbm, input_output_aliases={0: 0},
        scratch_shapes=[pltpu.SemaphoreType.DMA, pltpu.SemaphoreType.DMA],
        compiler_params=_params(),
    )(s)


def _reduce_scatter(g):
    h = _add_own_half(g, _sibling_swap_halves(g))
    return _sibling_join_halves(_sum_slots(_chip_exchange(h)))


def _unshard_cols(piece):
    p, k, n = piece.shape
    return jnp.transpose(piece, (1, 0, 2)).reshape(k, p * n)


def _shard_cols(full):
    k, n4 = full.shape
    return jnp.transpose(full.reshape(k, N_CHIPS, n4 // N_CHIPS), (1, 0, 2))


def _rows_of(piece):
    return piece.reshape(N_CHIPS, -1, ROW_W)


def _pad_rows(a, mult):
    pad = (-a.shape[-2]) % mult
    if pad == 0:
        return a
    widths = [(0, 0)] * (a.ndim - 2) + [(0, pad), (0, 0)]
    return jnp.pad(a, widths)


def _adamw(w, g, m, v):
    m2 = ADAM_B1 * m + (1.0 - ADAM_B1) * g
    v2 = ADAM_B2 * v + (1.0 - ADAM_B2) * (g * g)
    m_hat = m2 / (1.0 - ADAM_B1 ** ADAM_STEP)
    v_hat = v2 / (1.0 - ADAM_B2 ** ADAM_STEP)
    delta = -ADAM_LR * (m_hat / (jnp.sqrt(v_hat) + ADAM_EPS) + ADAM_WD * w)
    return delta, m2, v2


def _mlp_fwd(seq, tm, x_in, mix, mod, w1, w2, tag, residual=True):
    x1, h2 = _rw_fwd(_f_res_norm2, "res_norm2_" + tag, seq=seq, tm=tm, rows=[x_in, mix], bvecs=[mod],
                     outs=[(D, F32), (D, BF16)])
    def relu_and_square(acc):
        r = jnp.maximum(acc, 0.0)
        return r, r * r

    p, f = _matmul_ep(h2, w1, [], relu_and_square, [BF16, BF16], mode="nn", name="mlp_up_" + tag)
    ff = _matmul(f, w2, mode="nn", name="mlp_down_" + tag)
    x2 = None
    if residual:
        (x2,) = _rw_fwd(_f_res2, "res2_" + tag, seq=seq, tm=tm, rows=[x1, ff], bvecs=[mod], outs=[(D, F32)])
    return x2, (x1, h2, p, f, ff)


def _mlp_bwd(seq, tm, saved, x_in, mix, mod, w1, w2, dx2, tag, res2_grads=None):
    x1, h2, p, f, ff = saved
    if res2_grads is None:
        dff, dmod_a = _rw_bwd(_f_res2, "res2_bwd_" + tag, seq=seq, tm=tm, rows=[x1, ff], bvecs=[mod],
                              cts=[dx2], need_rows=[1], dtypes=[BF16])
    else:
        dff, dmod_a = res2_grads
    (dp,) = _matmul_ep(dff, w2, [p], lambda acc, pt: (2.0 * pt.astype(F32) * acc,), [BF16],
                       mode="nt", name="mlp_down_dx_" + tag)
    dw2 = _matmul(f, dff, mode="tn", name="mlp_down_dw_" + tag)
    dh2 = _matmul(dp, w1, mode="nt", name="mlp_up_dx_" + tag)
    dw1 = _matmul(h2, dp, mode="tn", name="mlp_up_dw_" + tag, out_shards=N_CHIPS)
    dx_in, dmix, dmod_b = _rw_bwd(_f_res_norm2, "res_norm2_bwd_" + tag, seq=seq, tm=tm, rows=[x_in, mix],
                                  bvecs=[mod], cts=[dx2, dh2], need_rows=[0, 1], dtypes=[F32, BF16])
    return dx_in, dmix, dmod_a + dmod_b, dw1, dw2


def kernel(x, c, ada_w, ada_b, mlp_w1, mlp_w2, a_w_in, a_ln_g, a_ln_b, a_w_s, a_b_s, a_w_out, b_mu, b_w_in, b_w0, b_w1, b_w2, b_a0, b_a1, b_a2, b_g1, b_g2, b_k_k, b_k_a, b_r_k, b_ln_g, b_ln_b, b_w_out, final_g, loss_target, m_ada_w, m_ada_b, m_mlp_w1, m_mlp_w2, m_a_w_in, m_a_ln_g, m_a_ln_b, m_a_w_s, m_a_b_s, m_a_w_out, m_b_mu, m_b_w_in, m_b_w0, m_b_w1, m_b_w2, m_b_a0, m_b_a1, m_b_a2, m_b_g1, m_b_g2, m_b_k_k, m_b_k_a, m_b_r_k, m_b_ln_g, m_b_ln_b, m_b_w_out, m_final_g, v_ada_w, v_ada_b, v_mlp_w1, v_mlp_w2, v_a_w_in, v_a_ln_g, v_a_ln_b, v_a_w_s, v_a_b_s, v_a_w_out, v_b_mu, v_b_w_in, v_b_w0, v_b_w1, v_b_w2, v_b_a0, v_b_a1, v_b_a2, v_b_g1, v_b_g2, v_b_k_k, v_b_k_a, v_b_r_k, v_b_ln_g, v_b_ln_b, v_b_w_out, v_final_g):
    weights = dict(ada_w=ada_w, ada_b=ada_b, mlp_w1=mlp_w1, mlp_w2=mlp_w2, a_w_in=a_w_in, a_ln_g=a_ln_g,
                   a_ln_b=a_ln_b, a_w_s=a_w_s, a_b_s=a_b_s, a_w_out=a_w_out, b_mu=b_mu, b_w_in=b_w_in,
                   b_w0=b_w0, b_w1=b_w1, b_w2=b_w2, b_a0=b_a0, b_a1=b_a1, b_a2=b_a2, b_g1=b_g1, b_g2=b_g2,
                   b_k_k=b_k_k, b_k_a=b_k_a, b_r_k=b_r_k, b_ln_g=b_ln_g, b_ln_b=b_ln_b, b_w_out=b_w_out,
                   final_g=final_g)
    moms = dict(ada_w=(m_ada_w, v_ada_w), ada_b=(m_ada_b, v_ada_b), mlp_w1=(m_mlp_w1, v_mlp_w1),
                mlp_w2=(m_mlp_w2, v_mlp_w2), a_w_in=(m_a_w_in, v_a_w_in), a_ln_g=(m_a_ln_g, v_a_ln_g),
                a_ln_b=(m_a_ln_b, v_a_ln_b), a_w_s=(m_a_w_s, v_a_w_s), a_b_s=(m_a_b_s, v_a_b_s),
                a_w_out=(m_a_w_out, v_a_w_out), b_mu=(m_b_mu, v_b_mu), b_w_in=(m_b_w_in, v_b_w_in),
                b_w0=(m_b_w0, v_b_w0), b_w1=(m_b_w1, v_b_w1), b_w2=(m_b_w2, v_b_w2), b_a0=(m_b_a0, v_b_a0),
                b_a1=(m_b_a1, v_b_a1), b_a2=(m_b_a2, v_b_a2), b_g1=(m_b_g1, v_b_g1), b_g2=(m_b_g2, v_b_g2),
                b_k_k=(m_b_k_k, v_b_k_k), b_k_a=(m_b_k_a, v_b_k_a), b_r_k=(m_b_r_k, v_b_r_k),
                b_ln_g=(m_b_ln_g, v_b_ln_g), b_ln_b=(m_b_ln_b, v_b_ln_b), b_w_out=(m_b_w_out, v_b_w_out),
                final_g=(m_final_g, v_final_g))
    order = list(weights)

    nbat, seq, _ = x.shape
    n = nbat * seq
    tm = 512
    xi, yi, ci = _place()
    chip = 2 * xi + yi
    dev = 2 * chip + ci
    x0 = x.reshape(n, D)
    tgt = loss_target.reshape(n, D)
    lora_w, lora_g = b_w1.shape[-1], b_g1.shape[-1]
    lora_wp, lora_gp = LANES, 2 * LANES

    (cond,) = _small(lambda cc: (cc / (1.0 + jnp.exp(-cc)),), "silu_c", [c], [c.shape])
    vec_names = ["b_w0", "b_a0", "b_k_k", "b_k_a", "b_ln_g", "b_ln_b"]
    vec_shard = jnp.concatenate([b_mu[0]] + [weights[k] for k in vec_names], axis=0)
    n_vec = vec_shard.shape[0]
    vec_rows = vec_shard.reshape(-1, ROW_W)
    blk = _pad_rows(jnp.concatenate([cond, vec_rows], axis=0), 8)
    assert blk.shape[0] == 8
    gathered = _all_gather8(blk).reshape(N_DEV, 8, D)
    cond_all = gathered[:, :nbat].reshape(N_DEV * nbat, D)
    vec_all = gathered[0::2, nbat:nbat + vec_rows.shape[0]].reshape(N_CHIPS, n_vec, D // N_CHIPS)
    vec_full = jnp.transpose(vec_all, (1, 0, 2)).reshape(n_vec, D)
    mu_full = vec_full[0:6]
    w0_f, a0_f, kk_f, ka_f, lng_f, lnb_f = [vec_full[6 + j:7 + j] for j in range(6)]
    rk_f = b_r_k.reshape(1, D)

    n_ada = ada_w.shape[-1]
    parts = jnp.concatenate(
        [_matmul(cond_all, ada_w[i], mode="nn", name="ada_fwd_%d" % i) for i in range(2)], axis=1)
    parts_all = _all_gather8(parts).reshape(N_DEV, N_DEV * nbat, 2, n_ada)[0::2]
    mine = lax.dynamic_slice_in_dim(parts_all, dev * nbat, nbat, axis=1)
    mods = []
    for i in range(2):
        full = jnp.transpose(mine[:, :, i], (1, 0, 2)).reshape(nbat, N_MOD * D) + ada_b[i]
        mods.append(full.reshape(nbat, N_MOD, D))

    big = [("mlp_w1_0", mlp_w1[0]), ("mlp_w1_1", mlp_w1[1]), ("mlp_w2_0", mlp_w2[0]), ("mlp_w2_1", mlp_w2[1]),
           ("a_w_in", a_w_in[0]), ("a_w_out", a_w_out[0]), ("b_w_in", b_w_in[0]), ("b_w_out", b_w_out[0]),
           ("b_w1", b_w1[0]), ("b_w2", b_w2[0]), ("b_a1", b_a1[0]), ("b_a2", b_a2[0]),
           ("b_g1", b_g1[0]), ("b_g2", b_g2[0])]
    offs, pos = {}, 0
    for name, arr in big:
        rows_k = arr.size // ROW_W
        offs[name] = (pos, rows_k, arr.shape)
        pos += rows_k
    n_big_rows = pos
    wflat = _pad_rows(jnp.concatenate([arr.astype(BF16).reshape(-1, ROW_W) for _, arr in big], axis=0), 32)
    wg = _chip_all_gather(wflat)

    def gathered_piece(name):
        start, rows_k, shape = offs[name]
        return wg[:, start:start + rows_k].reshape((N_CHIPS,) + shape)

    def col_w(name):
        return _unshard_cols(gathered_piece(name))

    def row_w(name):
        piece = gathered_piece(name)
        return piece.reshape(N_CHIPS * piece.shape[1], piece.shape[2])

    w1_l = [col_w("mlp_w1_0"), col_w("mlp_w1_1")]
    w2_l = [row_w("mlp_w2_0"), row_w("mlp_w2_1")]
    a_win, a_wout = col_w("a_w_in"), row_w("a_w_out")
    b_win, b_wout = col_w("b_w_in"), row_w("b_w_out")
    w_r, w_k, w_v = b_win[:, :D], b_win[:, D:2 * D], b_win[:, 2 * D:]
    w1p = jnp.pad(row_w("b_w1"), ((0, 0), (0, lora_wp - lora_w)))
    a1p = jnp.pad(row_w("b_a1"), ((0, 0), (0, lora_wp - lora_w)))
    g1p = jnp.pad(row_w("b_g1"), ((0, 0), (0, lora_gp - lora_g)))
    w2p = jnp.pad(col_w("b_w2"), ((0, lora_wp - lora_w), (0, 0)))
    a2p = jnp.pad(col_w("b_a2"), ((0, lora_wp - lora_w), (0, 0)))
    g2p = jnp.pad(col_w("b_g2"), ((0, lora_gp - lora_g), (0, 0)))

    mod0, mod1 = mods
    (h_a,) = _rw_fwd(_f_norm1, "norm1_a", seq=seq, tm=tm, rows=[x0], bvecs=[mod0], outs=[(D, BF16)])
    uvp = _matmul(h_a, a_win, mode="nn", name="sgu_in")
    ws = a_w_s[0]
    bias = jnp.broadcast_to(a_b_s[0][:, :, None], (GROUPS, CHUNK, LANES))
    z = _sgu_fwd(uvp, a_ln_g, a_ln_b, ws, bias)
    mix0 = _matmul(z, a_wout, mode="nn", name="sgu_out")
    x2, saved0 = _mlp_fwd(seq, tm, x0, mix0, mod0, w1_l[0], w2_l[0], "0")

    def shift_fwd(ctx, rv, pv, nv, bv, pa):
        _, _, mixes = _shift_mix(ctx, rv[0], pv[0], bv[0], pa[0])
        return mixes, [], []

    xr, xw, xk, xv, xa, xg = _rowwise(shift_fwd, "shift_mix", seq=seq, tm=tm, rows=[x2], prev8=[x2],
                                      bvecs=[mod1], params=[mu_full], out_rows=[(D, BF16)] * 6)
    r = _matmul(xr, w_r, mode="nn", name="rwkv_r")
    k = _matmul(xk, w_k, mode="nn", name="rwkv_k")
    v = _matmul(xv, w_v, mode="nn", name="rwkv_v")
    def act_tanh(t):
        return jnp.tanh(t)

    def act_sigmoid(t):
        return 1.0 / (1.0 + jnp.exp(-t))

    t1, th = _matmul_ep(xw, w1p, [], lambda acc: (acc, act_tanh(acc)), [F32, BF16], mode="nn", name="lora_w1")
    t2 = _matmul(xa, a1p, mode="nn", name="lora_a1", out_dtype=BF16)
    t3, sg = _matmul_ep(xg, g1p, [], lambda acc: (acc, act_sigmoid(acc)), [F32, BF16], mode="nn", name="lora_g1")
    wl = _matmul(th, w2p, mode="nn", name="lora_w2")
    al = _matmul(t2, a2p, mode="nn", name="lora_a2")
    g = _matmul(sg, g2p, mode="nn", name="lora_g2")
    wkv_rows = [r, k, v, wl, al, g]
    wkv_pars = [w0_f, a0_f, kk_f, ka_f, rk_f, lng_f, lnb_f]
    yo, ckpt, wkv_u = _wkv_fwd(seq, wkv_rows, wkv_pars)
    mix1 = _matmul(yo, b_wout, mode="nn", name="rwkv_out")
    _, saved1 = _mlp_fwd(seq, tm, x2, mix1, mod1, w1_l[1], w2_l[1], "1", residual=False)

    def loss_fn(ctx, rv, pv, nv, bv, pa):
        def head(x1, ff, mod, fg):
            return _f_loss(_f_res2(x1, ff, mod)[0], rv[2], fg)
        val, (dx, dff, dmod, dfg) = jax.value_and_grad(head, argnums=(0, 1, 2, 3))(rv[0], rv[1], bv[0], pa[0])
        return [dx, dff], [dmod], [dfg, jnp.full((1, LANES), val, F32)]

    dx4, dff1, dmod1_a, d_final_g, loss_acc = _rowwise(
        loss_fn, "loss_head", seq=seq, tm=tm, rows=[saved1[0], saved1[4], tgt], bvecs=[mod1],
        params=[final_g.reshape(1, D)], out_rows=[(D, F32), (D, BF16)], out_bacc=[(N_MOD, D)],
        out_pacc=[(1, D), (1, LANES)])
    loss = lax.psum(loss_acc[0, 0], AXES)

    dx2_a, dmix1, dmod1, dw1_1, dw2_1 = _mlp_bwd(seq, tm, saved1, x2, mix1, mod1, w1_l[1], w2_l[1], dx4, "1",
                                                 res2_grads=(dff1, dmod1_a))
    dyo = _matmul(dmix1, b_wout, mode="nt", name="rwkv_out_dx")
    d_b_wout = _matmul(yo, dmix1, mode="tn", name="rwkv_out_dw")
    wkv_grads = _wkv_bwd(seq, wkv_rows, wkv_pars, ckpt, wkv_u, dyo)
    dr, dk, dv, dwl, dal, dg = wkv_grads[:N_WKV_ROWS]
    d_w0, d_a0, d_kk, d_ka, d_rk, d_lng, d_lnb = wkv_grads[N_WKV_ROWS:]
    def through(act):
        return lambda acc, t: (jax.vjp(act, t)[1](acc)[0],)

    (dt1,) = _matmul_ep(dwl, w2p, [t1], through(act_tanh), [BF16], mode="nt", name="lora_w2_dx")
    d_w2p = _matmul(th, dwl, mode="tn", name="lora_w2_dw")
    dt2 = _matmul(dal, a2p, mode="nt", name="lora_a2_dx", out_dtype=BF16)
    d_a2p = _matmul(t2, dal, mode="tn", name="lora_a2_dw")
    (dt3,) = _matmul_ep(dg, g2p, [t3], through(act_sigmoid), [BF16], mode="nt", name="lora_g2_dx")
    d_g2p = _matmul(sg, dg, mode="tn", name="lora_g2_dw")
    dxw = _matmul(dt1, w1p, mode="nt", name="lora_w1_dx")
    d_w1p = _matmul(xw, dt1, mode="tn", name="lora_w1_dw")
    dxa = _matmul(dt2, a1p, mode="nt", name="lora_a1_dx")
    d_a1p = _matmul(xa, dt2, mode="tn", name="lora_a1_dw")
    dxg = _matmul(dt3, g1p, mode="nt", name="lora_g1_dx")
    d_g1p = _matmul(xg, dt3, mode="tn", name="lora_g1_dw")
    dxr = _matmul(dr, w_r, mode="nt", name="rwkv_r_dx")
    dxk = _matmul(dk, w_k, mode="nt", name="rwkv_k_dx")
    dxv = _matmul(dv, w_v, mode="nt", name="rwkv_v_dx")
    d_b_win = jnp.concatenate([_matmul(xr, dr, mode="tn", name="rwkv_r_dw"),
                               _matmul(xk, dk, mode="tn", name="rwkv_k_dw"),
                               _matmul(xv, dv, mode="tn", name="rwkv_v_dw")], axis=1)

    def shift_bwd(ctx, rv, pv, nv, bv, pa):
        xt, dres = rv[0], rv[1]
        dmix_in = rv[2:8]
        mod, mu = bv[0], pa[0]
        f_h = lambda xx_, mod_: _rmsmod(xx_, mod_[0:1], mod_[1:2])
        h, vjp = jax.vjp(f_h, xt, mod)
        hprev = f_h(pv[0], mod)[7:8]
        hprev = jnp.where(ctx.first, jnp.zeros_like(hprev), hprev)
        rowid = lax.broadcasted_iota(jnp.int32, h.shape, 0)
        xx = jnp.where(rowid == 0, hprev, pltpu.roll(h, 1, 0)) - h
        tot = dmix_in[0]
        wsum = dmix_in[0] * mu[0:1]
        for j in range(1, 6):
            tot = tot + dmix_in[j]
            wsum = wsum + dmix_in[j] * mu[j:j + 1]
        nxt = nv[0][0:1] * mu[0:1]
        for j in range(1, 6):
            nxt = nxt + nv[j][0:1] * mu[j:j + 1]
        nxt = jnp.where(ctx.last, jnp.zeros_like(nxt), nxt)
        tmr = h.shape[0]
        wshift = jnp.where(rowid == tmr - 1, nxt, pltpu.roll(wsum, tmr - 1, 0))
        dh = tot - wsum + wshift
        dx_, dmod_ = vjp(dh)
        dmu = jnp.concatenate([jnp.sum(dmix_in[j] * xx, axis=0, keepdims=True) for j in range(6)], axis=0)
        dx2_t = dx_ + dres
        ff_below, mod_below = rv[8], bv[1]
        dff_below = mod_below[5:6] * dx2_t
        dgate = jnp.sum(dx2_t * ff_below, axis=0, keepdims=True)
        gate_row = lax.broadcasted_iota(jnp.int32, (N_MOD, D), 0) == N_MOD - 1
        dmod_below = jnp.where(gate_row, jnp.broadcast_to(dgate, (N_MOD, D)), 0.0)
        return [dx2_t, dff_below], [dmod_, dmod_below], [dmu]

    dmix_list = [dxr, dxw, dxk, dxv, dxa, dxg]
    dx2, dff0, dmod1_c, dmod0_a, d_mu = _rowwise(
        shift_bwd, "shift_mix_bwd", seq=seq, tm=tm // 2, rows=[x2, dx2_a] + dmix_list + [saved0[4]],
        prev8=[x2], next8=dmix_list, bvecs=[mod1, mod0], params=[mu_full],
        out_rows=[(D, F32), (D, BF16)], out_bacc=[(N_MOD, D), (N_MOD, D)], out_pacc=[(6, D)])
    dmod1 = dmod1 + dmod1_c

    dx0_a, dmix0, dmod0, dw1_0, dw2_0 = _mlp_bwd(seq, tm, saved0, x0, mix0, mod0, w1_l[0], w2_l[0], dx2, "0",
                                                 res2_grads=(dff0, dmod0_a))
    dz = _matmul(dmix0, a_wout, mode="nt", name="sgu_out_dx")
    d_a_wout = _matmul(z, dmix0, mode="tn", name="sgu_out_dw")
    duvp, d_a_lng, d_a_lnb, d_ws, d_bias = _sgu_bwd(uvp, a_ln_g, a_ln_b, ws, bias, dz)
    dh_a = _matmul(duvp, a_win, mode="nt", name="sgu_in_dx")
    d_a_win = _matmul(h_a, duvp, mode="tn", name="sgu_in_dw", out_shards=N_CHIPS)
    grad_x, dmod0_c = _rw_bwd(_f_norm1, "norm1_a_bwd", seq=seq, tm=tm, rows=[x0], bvecs=[mod0], cts=[dh_a],
                              need_rows=[0], extra=dx0_a)
    dmod0 = dmod0 + dmod0_c

    dmod_blk = _pad_rows(jnp.concatenate([dmod0.reshape(nbat, -1), dmod1.reshape(nbat, -1)], axis=1), 8)
    dmod_rows = dmod_blk.reshape(-1, D)
    rep = _pad_rows(jnp.concatenate([
        d_a_lng, d_a_lnb, jnp.sum(d_bias, axis=-1).reshape(1, D), d_rk, d_final_g,
        jnp.zeros((3, D), F32), d_ws.reshape(-1, D)], axis=0), 8)
    rep_rows = rep.shape[0]
    tail = _all_gather8(jnp.concatenate([dmod_rows, rep], axis=0)).reshape(N_DEV, -1, D)
    dmod_all = tail[:, :dmod_rows.shape[0]].reshape(N_DEV, 8, 2, N_MOD * D)[:, :nbat]
    dmod_all = dmod_all.reshape(N_DEV * nbat, 2, N_MOD * D)
    rep_all = tail[:, dmod_rows.shape[0]:].reshape(N_DEV * rep_rows, D)
    g_ada_w, g_ada_b = [], []
    for i in range(2):
        cols = lax.dynamic_slice_in_dim(dmod_all[:, i], chip * n_ada, n_ada, axis=1)
        g_ada_w.append(_matmul(cond_all, cols, mode="tn", name="ada_dw_%d" % i))
    (g_ada_b_all,) = _small(lambda t: (jnp.sum(t, axis=0),), "ada_db", [dmod_all], [(2, N_MOD * D)])
    grads = {"ada_w": jnp.stack(g_ada_w), "ada_b": g_ada_b_all}

    (rep_sum,) = _small(lambda t: (functools.reduce(lambda p, q: p + q,
                                                     [t[j * rep_rows:(j + 1) * rep_rows] for j in range(N_DEV)]),),
                        "replicated_sum", [rep_all], [(rep_rows, D)])
    grads["a_ln_g"] = rep_sum[0:1]
    grads["a_ln_b"] = rep_sum[1:2]
    grads["a_b_s"] = rep_sum[2:3].reshape(a_b_s.shape)
    grads["b_r_k"] = rep_sum[3:4].reshape(b_r_k.shape)
    grads["final_g"] = rep_sum[4].reshape(final_g.shape)
    grads["a_w_s"] = rep_sum[8:8 + GROUPS * CHUNK * LANES // D].reshape(a_w_s.shape)

    vec_grads = jnp.concatenate([d_mu, d_w0, d_a0, d_kk, d_ka, d_lng, d_lnb], axis=0)
    packed = {
        "mlp_w1_0": dw1_0, "mlp_w1_1": dw1_1,
        "mlp_w2_0": dw2_0.reshape(N_CHIPS, -1, D), "mlp_w2_1": dw2_1.reshape(N_CHIPS, -1, D),
        "a_w_in": d_a_win, "a_w_out": d_a_wout.reshape(N_CHIPS, -1, D),
        "b_w_in": _shard_cols(d_b_win), "b_w_out": d_b_wout.reshape(N_CHIPS, -1, D),
        "b_w1": d_w1p[:, :lora_w].reshape(N_CHIPS, -1, lora_w), "b_w2": _shard_cols(d_w2p[:lora_w]),
        "b_a1": d_a1p[:, :lora_w].reshape(N_CHIPS, -1, lora_w), "b_a2": _shard_cols(d_a2p[:lora_w]),
        "b_g1": d_g1p[:, :lora_g].reshape(N_CHIPS, -1, lora_g), "b_g2": _shard_cols(d_g2p[:lora_g]),
    }
    pieces = [_rows_of(packed[name]) for name, _ in big] + [_pad_rows(_rows_of(_shard_cols(vec_grads)), 8)]
    used = sum(p.shape[1] for p in pieces)
    pieces.append(jnp.zeros((N_CHIPS, (-used) % 2016, ROW_W), F32))
    g_pack = jnp.concatenate(pieces, axis=1)
    g_red = _reduce_scatter(g_pack)
    for name, _ in big:
        start, rows_k, shape = offs[name]
        grads[name] = g_red[start:start + rows_k].reshape(shape)
    vec_red = g_red[n_big_rows:n_big_rows + vec_rows.shape[0]].reshape(n_vec, D // N_CHIPS)
    grads["b_mu"] = vec_red[0:6].reshape(b_mu.shape)
    for j, name in enumerate(vec_names):
        grads[name] = vec_red[6 + j:7 + j].reshape(weights[name].shape)
    for base in ("mlp_w1", "mlp_w2"):
        grads[base] = jnp.stack([grads.pop(base + "_0"), grads.pop(base + "_1")])
    for name in ("a_w_in", "a_w_out", "b_w_in", "b_w_out", "b_w1", "b_w2", "b_a1", "b_a2", "b_g1", "b_g2"):
        grads[name] = grads[name].reshape(weights[name].shape)

    deltas, new_m, new_v = {}, {}, {}
    for name in order:
        gr = grads[name].reshape(weights[name].shape)
        grads[name] = gr
        deltas[name], new_m[name], new_v[name] = _elementwise(
            _adamw, "adamw_" + name, [weights[name], gr, moms[name][0], moms[name][1]], 3)

    return (loss, grad_x.reshape(x.shape), *[grads[k] for k in order], *[deltas[k] for k in order],
            *[new_m[k] for k in order], *[new_v[k] for k in order])
```
